```python
import jax, jax.numpy as jnp
from jax import lax
import numpy as np

D_MODEL = 1024
BATCH = 8
SEQ = 2048
DEPTH = 1
DEC_BATCH = 128
DEC_SEQ = 8
PAST_LEN = 16384
PAGE_SIZE = 128

D_MIX = D_MODEL
D_A = D_MIX // 2
D_B = D_MIX - D_A
HEAD_A = 128
H_A = D_A // HEAD_A
DK = HEAD_A
DV = HEAD_A
HEAD_B = 128
H_B = D_B // HEAD_B
CHUNK_A = 64
CHUNK_B = 128
D_IN = 4 * D_A + 3 * D_B
EPS = 1e-6

kernel_name = "hymba_hgrn2_chunkmlp_step"


def _rms_norm(x, g):
    xf = x.astype(jnp.float32)
    y = xf * lax.rsqrt(jnp.mean(xf * xf, axis=-1, keepdims=True) + EPS)
    return (y * g.astype(jnp.float32)).astype(x.dtype)


def _layer_norm(x, g, b):
    xf = x.astype(jnp.float32)
    mu = jnp.mean(xf, axis=-1, keepdims=True)
    var = jnp.mean(jnp.square(xf - mu), axis=-1, keepdims=True)
    y = (xf - mu) * lax.rsqrt(var + EPS)
    return (y * g.astype(jnp.float32) + b.astype(jnp.float32)).astype(x.dtype)


def _hgrn2_chunked(q, logf, k, v, s0):
    B, L, H, _ = q.shape
    C = min(CHUNK_A, L)
    n_chunks = -(-L // C)
    pad = n_chunks * C - L

    def to_chunks(a):
        a = jnp.pad(a, ((0, 0), (0, pad), (0, 0), (0, 0)))
        return a.reshape(B, n_chunks, C, H, a.shape[-1]).transpose(1, 0, 2, 3, 4)

    mask = jnp.tril(jnp.ones((C, C), dtype=bool))[None, :, :, None, None]

    def step(S, inp):
        qc, lfc, kc, vc = inp
        b = jnp.cumsum(lfc, axis=1)
        o_inter = jnp.einsum('bthk,bhkv->bthv', qc * jnp.exp(b), S)
        diff = b[:, :, None] - b[:, None, :]
        decay = jnp.where(mask, jnp.exp(jnp.where(mask, diff, 0.0)), 0.0)
        scores = jnp.einsum('bthk,btshk,bshk->bhts', qc, decay, kc)
        o_intra = jnp.einsum('bhts,bshv->bthv', scores, vc)
        b_last = b[:, -1]
        S_new = jnp.exp(b_last)[..., None] * S + jnp.einsum(
            'bshk,bshv->bhkv', kc * jnp.exp(b_last[:, None] - b), vc)
        return S_new, o_inter + o_intra

    s_final, o = lax.scan(step, s0, (to_chunks(q), to_chunks(logf), to_chunks(k), to_chunks(v)))
    o = o.transpose(1, 0, 2, 3, 4).reshape(B, n_chunks * C, H, v.shape[-1])[:, :L]
    return o, s_final


def _chunk_spatial_mix(v, w_s, b_s):
    B, L, H, E = v.shape
    n_chunks = -(-L // CHUNK_B)
    pad = n_chunks * CHUNK_B - L
    vc = jnp.pad(v, ((0, 0), (0, pad), (0, 0), (0, 0))).reshape(B, n_chunks, CHUNK_B, H, E)
    w_causal = jnp.where(jnp.tril(jnp.ones((CHUNK_B, CHUNK_B), dtype=bool))[None], w_s, 0.0)
    out = jnp.einsum('hts,bnshe->bnthe', w_causal, vc) + b_s.T[None, None, :, :, None]
    return out.reshape(B, n_chunks * CHUNK_B, H, E)[:, :L]


def _mixer_layer(x, s0, lb, norm_g, w_in, hgrn_norm_g, sgu_ln_g, sgu_ln_b, w_s, b_s, w_out):
    B, L, _ = x.shape
    h = _rms_norm(x, norm_g)
    proj = jnp.einsum('bld,de->ble', h, w_in)
    splits = [D_A, 2 * D_A, 3 * D_A, 4 * D_A, 4 * D_A + D_B, 4 * D_A + 2 * D_B]
    q, f, i, z_a, u, v, z_b = jnp.split(proj, splits, axis=-1)

    qf = jax.nn.silu(q.astype(jnp.float32)).reshape(B, L, H_A, DK)
    lbr = lb.reshape(H_A, DK)
    forget = lbr + (1.0 - lbr) * jax.nn.sigmoid(f.astype(jnp.float32)).reshape(B, L, H_A, DK)
    logf = jnp.log(forget)
    k = 1.0 - forget
    vi = i.astype(jnp.float32).reshape(B, L, H_A, DV)
    o_a, s_new = _hgrn2_chunked(qf, logf, k, vi, s0.astype(jnp.float32))
    o_a = _rms_norm(o_a, hgrn_norm_g.reshape(H_A, DV)).reshape(B, L, D_A)
    o_a = o_a * jax.nn.silu(z_a.astype(jnp.float32))

    u = jax.nn.gelu(u)
    v_n = _layer_norm(jax.nn.gelu(v), sgu_ln_g, sgu_ln_b).reshape(B, L, H_B, HEAD_B)
    sgu = u.reshape(B, L, H_B, HEAD_B) * _chunk_spatial_mix(v_n, w_s, b_s)
    o_b = sgu.reshape(B, L, D_B) * jax.nn.silu(z_b)

    mixed = jnp.concatenate([o_a.astype(x.dtype), o_b.astype(x.dtype)], axis=-1)
    out = x + jnp.einsum('ble,ed->bld', mixed, w_out)
    n_open = L - ((L - 1) // CHUNK_B) * CHUNK_B
    v_open = v_n[:, L - n_open:]
    return out, s_new.astype(s0.dtype), v_open


def setup_inputs(seed: int = 0) -> dict:
    key = jax.random.key(seed)
    ks = jax.random.split(key, 14)
    f32 = jnp.float32
    x_prompt = jax.random.normal(ks[0], (BATCH, SEQ, D_MODEL), f32)
    x_sample = jax.random.normal(ks[1], (DEC_BATCH, DEC_SEQ, D_MODEL), f32)
    state_hgrn = 0.3 * jax.random.normal(ks[2], (DEPTH, DEC_BATCH, H_A, DK, DV), f32)
    norm_g = 1.0 + 0.02 * jax.random.normal(ks[3], (DEPTH, D_MODEL), f32)
    w_in = jax.random.normal(ks[4], (DEPTH, D_MODEL, D_IN), f32) * D_MODEL ** -0.5
    lb_logits = 0.5 * jax.random.normal(ks[5], (DEPTH + 1, D_A), f32)
    hgrn_norm_g = 1.0 + 0.02 * jax.random.normal(ks[6], (DEPTH, D_A), f32)
    sgu_ln_g = 1.0 + 0.02 * jax.random.normal(ks[7], (DEPTH, D_B), f32)
    sgu_ln_b = 0.02 * jax.random.normal(ks[8], (DEPTH, D_B), f32)
    w_s = jax.random.normal(ks[9], (DEPTH, H_B, CHUNK_B, CHUNK_B), f32) * CHUNK_B ** -0.5
    b_s = 1.0 + 0.1 * jax.random.normal(ks[10], (DEPTH, H_B, CHUNK_B), f32)
    w_out = jax.random.normal(ks[11], (DEPTH, D_MIX, D_MODEL), f32) * D_MIX ** -0.5
    final_norm_g = 1.0 + 0.02 * jax.random.normal(ks[12], (D_MODEL,), f32)
    return {"x_prompt": x_prompt, "x_sample": x_sample, "state_hgrn": state_hgrn,
            "norm_g": norm_g, "w_in": w_in, "lb_logits": lb_logits,
            "hgrn_norm_g": hgrn_norm_g, "sgu_ln_g": sgu_ln_g, "sgu_ln_b": sgu_ln_b,
            "w_s": w_s, "b_s": b_s, "w_out": w_out, "final_norm_g": final_norm_g}


def reference(x_prompt, x_sample, state_hgrn, norm_g, w_in, lb_logits, hgrn_norm_g,
              sgu_ln_g, sgu_ln_b, w_s, b_s, w_out, final_norm_g):
    lower_bounds = jnp.cumsum(jax.nn.softmax(lb_logits.astype(jnp.float32), axis=0), axis=0)
    B = x_prompt.shape[0]
    xp, xs = x_prompt, x_sample
    sp_list, ss_list, vp_list, vs_list = [], [], [], []
    for l in range(DEPTH):
        lw = (lower_bounds[l], norm_g[l], w_in[l], hgrn_norm_g[l], sgu_ln_g[l], sgu_ln_b[l],
              w_s[l], b_s[l], w_out[l])
        s0_p = jnp.zeros((B, H_A, DK, DV), state_hgrn.dtype)
        xp, sp, vp = _mixer_layer(xp, s0_p, *lw)
        xs, ss, vs = _mixer_layer(xs, state_hgrn[l], *lw)
        sp_list.append(sp); ss_list.append(ss); vp_list.append(vp); vs_list.append(vs)
    y_prompt = _rms_norm(xp, final_norm_g)
    y_sample = _rms_norm(xs, final_norm_g)
    state_hgrn_prompt = jnp.stack(sp_list, axis=0)
    state_hgrn_sample = jnp.stack(ss_list, axis=0)
    v_chunk_prompt = jnp.stack(vp_list, axis=0)
    v_chunk_sample = jnp.stack(vs_list, axis=0)
    return (y_prompt, y_sample, state_hgrn_prompt, state_hgrn_sample, v_chunk_prompt, v_chunk_sample)
```

```python
import functools

import jax
import jax.numpy as jnp
from jax import lax
from jax.experimental import pallas as pl
from jax.experimental.pallas import tpu as pltpu

F32 = jnp.float32
BF16 = jnp.bfloat16

D_MODEL = 1024
D_A = 512
D_B = 512
HEAD = 128
N_HEADS = 4
D_IN = 4 * D_A + 3 * D_B
ROWS = 128
EPS = 1e-6
PROMPT_TILE = 256
DEC_SEQ = 8
SEQS_PER_TILE = ROWS // DEC_SEQ
VMEM_LIMIT_BYTES = 48 * 1024 * 1024

_NT = (((1,), (1,)), ((), ()))


def _silu(x):
    return x * jax.nn.sigmoid(x)


def _rms_norm(x, g):
    return x * lax.rsqrt(jnp.mean(x * x, axis=-1, keepdims=True) + EPS) * g


def _layer_norm(x, g, b):
    mu = jnp.mean(x, axis=-1, keepdims=True)
    xc = x - mu
    var = jnp.mean(xc * xc, axis=-1, keepdims=True)
    return xc * lax.rsqrt(var + EPS) * g + b


def _lower_bound(lbl):
    rows = [lbl[r:r + 1, :] for r in range(lbl.shape[0])]
    mx = functools.reduce(jnp.maximum, rows)
    es = [jnp.exp(r - mx) for r in rows]
    return es[0] / functools.reduce(lambda a, c: a + c, es)


def _split3(x):
    hi = x.astype(BF16)
    r1 = x - hi.astype(F32)
    mid = r1.astype(BF16)
    lo = (r1 - mid.astype(F32)).astype(BF16)
    return hi, mid, lo


def _chunk_constants(seg_len):
    t = lax.broadcasted_iota(jnp.int32, (ROWS, ROWS), 0)
    s = lax.broadcasted_iota(jnp.int32, (ROWS, ROWS), 1)
    n_levels = seg_len.bit_length() - 1
    assert seg_len == 1 << n_levels
    same_seq = (t >> n_levels) == (s >> n_levels)
    causal = (t >= s) & same_seq
    x = t ^ s
    lvl = jnp.full((ROWS, ROWS), -1, jnp.int32)
    for j in range(n_levels):
        lvl = jnp.where(((x >> j) == 1) & (((t >> j) & 1) == 1), j, lvl)
    rowi = lax.broadcasted_iota(jnp.int32, (ROWS, HEAD), 0)
    return causal, lvl, rowi, n_levels


def _hgrn_inputs(pq, pf, pi, lb, causal):
    q = _silu(pq)
    forget = lb + (1.0 - lb) * jax.nn.sigmoid(pf)
    logf = jnp.log(forget)
    kk = 1.0 - forget
    cm = jnp.where(causal, 1.0, 0.0).astype(BF16)
    hi, mid, lo = _split3(logf)
    b = (jnp.dot(cm, hi, preferred_element_type=F32)
         + jnp.dot(cm, mid, preferred_element_type=F32)
         + jnp.dot(cm, lo, preferred_element_type=F32))
    return q, kk, pi, b


def _hgrn_head_intra(q, kk, vi, b, lvl, rowi, n_levels):
    e = b
    sc = jnp.zeros((ROWS, ROWS), F32)
    for j in range(n_levels):
        m = 1 << j
        xk = e - b
        xq = jnp.minimum(b - pltpu.roll(e, m, 0), 0.0)
        kt = (kk * jnp.exp(xk)).astype(BF16)
        qt = (q * jnp.exp(xq)).astype(BF16)
        p = lax.dot_general(qt, kt, _NT, preferred_element_type=F32)
        sc = jnp.where(lvl == j, p, sc)
        upper = ((rowi >> j) & 1) == 1
        e = jnp.where(upper, e, pltpu.roll(e, ROWS - m, 0))
    diag = jnp.sum(q * kk, axis=-1, keepdims=True)
    o = jnp.dot(sc.astype(BF16), vi.astype(BF16), preferred_element_type=F32) + diag * vi
    return o, e


def _hgrn_finish(o, pz, g):
    return (_rms_norm(o, g) * _silu(pz)).astype(BF16)


def _chunk_mlp(pu, pv, pzb, lng, lnb, ws_ref, bst, causal):
    u = jax.nn.gelu(pu)
    v_n = _layer_norm(jax.nn.gelu(pv), lng, lnb)
    gate = _silu(pzb)
    outs = []
    for h in range(N_HEADS):
        hs = slice(h * HEAD, (h + 1) * HEAD)
        w = jnp.where(causal, ws_ref[h], 0.0).astype(BF16)
        mix = jnp.dot(w, v_n[:, hs].astype(BF16), preferred_element_type=F32) + bst[:, h:h + 1]
        outs.append((u[:, hs] * mix * gate[:, hs]).astype(BF16))
    return jnp.concatenate(outs, axis=-1), v_n


def _prompt_kernel(x_ref, ng_ref, win_ref, lbl_ref, hg_ref, lng_ref, lnb_ref, ws_ref, bst_ref,
                   wout_ref, fg_ref, y_ref, st_ref, vch_ref, proj_ref, mixed_ref):
    @pl.when(pl.program_id(1) == 0)
    def _():
        st_ref[...] = jnp.zeros_like(st_ref)

    x = x_ref[0]
    h = _rms_norm(x, ng_ref[...]).astype(BF16)
    proj_ref[...] = jnp.dot(h, win_ref[...], preferred_element_type=F32)

    lb = _lower_bound(lbl_ref[...])
    causal, lvl, rowi, n_levels = _chunk_constants(ROWS)
    hg = hg_ref[...]
    lng = lng_ref[...]
    lnb = lnb_ref[...]
    bst = bst_ref[...]

    def chunk_body(c, carry):
        rows = pl.ds(pl.multiple_of(c * ROWS, ROWS), ROWS)

        def col(k, width=D_A):
            return proj_ref[rows, k * D_A:k * D_A + width]

        q, kk, vi, b = _hgrn_inputs(col(0), col(1), col(2), lb, causal)
        pza = col(3)
        for hd in range(N_HEADS):
            hs = slice(hd * HEAD, (hd + 1) * HEAD)
            qh, kh, vh, bh = q[:, hs], kk[:, hs], vi[:, hs], b[:, hs]
            o, e = _hgrn_head_intra(qh, kh, vh, bh, lvl, rowi, n_levels)
            state = st_ref[0, 0, hd]
            qin = (qh * jnp.exp(bh)).astype(BF16)
            o = o + jnp.dot(qin, state.astype(BF16), preferred_element_type=F32)
            khat = kh * jnp.exp(e - bh)
            st_ref[0, 0, hd] = jnp.exp(e).T * state + jnp.dot(
                khat.T.astype(BF16), vh.astype(BF16), preferred_element_type=F32)
            mixed_ref[rows, hs] = _hgrn_finish(o, pza[:, hs], hg[:, hs])

        o_b, v_n = _chunk_mlp(col(4), col(5), col(6), lng, lnb, ws_ref, bst, causal)
        mixed_ref[rows, D_A:D_A + D_B] = o_b
        vch_ref[0, 0] = v_n
        return carry

    lax.fori_loop(0, PROMPT_TILE // ROWS, chunk_body, 0)

    out = x + jnp.dot(mixed_ref[...], wout_ref[...], preferred_element_type=F32)
    y_ref[0] = _rms_norm(out, fg_ref[...])


def _sample_kernel(x_ref, ng_ref, win_ref, lbl_ref, hg_ref, lng_ref, lnb_ref, ws_ref, bst_ref,
                   wout_ref, fg_ref, stin_ref, y_ref, stout_ref, vch_ref,
                   qin_ref, kht_ref, dlt_ref, oint_ref):
    x = x_ref[...]
    h = _rms_norm(x, ng_ref[...]).astype(BF16)
    proj = jnp.dot(h, win_ref[...], preferred_element_type=F32)

    def col(k):
        return proj[:, k * D_A:(k + 1) * D_A]

    lb = _lower_bound(lbl_ref[...])
    causal, lvl, rowi, n_levels = _chunk_constants(DEC_SEQ)
    q, kk, vi, b = _hgrn_inputs(col(0), col(1), col(2), lb, causal)
    vi16 = vi.astype(BF16)

    o_intra = []
    for hd in range(N_HEADS):
        hs = slice(hd * HEAD, (hd + 1) * HEAD)
        qh, kh, bh = q[:, hs], kk[:, hs], b[:, hs]
        o, e = _hgrn_head_intra(qh, kh, vi[:, hs], bh, lvl, rowi, n_levels)
        o_intra.append(o)
        qin_ref[:, hs] = qh * jnp.exp(bh)
        kht_ref[hd] = (kh * jnp.exp(e - bh)).T
        dlt_ref[hd] = jnp.exp(e).T

    lane = lax.broadcasted_iota(jnp.int32, (HEAD, ROWS), 1)

    def seq_body(i, carry):
        r0 = pl.multiple_of(i * DEC_SEQ, DEC_SEQ)
        in_seq = (lane >> (DEC_SEQ.bit_length() - 1)) == i
        first = lane == r0
        for hd in range(N_HEADS):
            hs = slice(hd * HEAD, (hd + 1) * HEAD)
            state = stin_ref[0, i, hd]
            qi = qin_ref[pl.ds(r0, DEC_SEQ), hs].astype(BF16)
            oint_ref[pl.ds(r0, DEC_SEQ), hs] = jnp.dot(
                qi, state.astype(BF16), preferred_element_type=F32)
            khm = jnp.where(in_seq, kht_ref[hd], 0.0).astype(BF16)
            decay = jnp.sum(jnp.where(first, dlt_ref[hd], 0.0), axis=1, keepdims=True)
            stout_ref[0, i, hd] = decay * state + jnp.dot(
                khm, vi16[:, hs], preferred_element_type=F32)
        return carry

    lax.fori_loop(0, SEQS_PER_TILE, seq_body, 0)

    hg = hg_ref[...]
    pza = col(3)
    o_a = []
    for hd in range(N_HEADS):
        hs = slice(hd * HEAD, (hd + 1) * HEAD)
        o_a.append(_hgrn_finish(o_intra[hd] + oint_ref[:, hs], pza[:, hs], hg[:, hs]))

    o_b, v_n = _chunk_mlp(col(4), col(5), col(6), lng_ref[...], lnb_ref[...], ws_ref,
                          bst_ref[...], causal)
    vch_ref[...] = v_n
    mixed = jnp.concatenate(o_a + [o_b], axis=-1)
    out = x + jnp.dot(mixed, wout_ref[...], preferred_element_type=F32)
    y_ref[...] = _rms_norm(out, fg_ref[...])


def _const_spec(shape):
    return pl.BlockSpec(shape, lambda *_: (0,) * len(shape))


def _weight_specs(bst_rows):
    return [
        _const_spec((1, D_MODEL)),
        _const_spec((D_MODEL, D_IN)),
        _const_spec((2, D_A)),
        _const_spec((1, D_A)),
        _const_spec((1, D_B)),
        _const_spec((1, D_B)),
        _const_spec((N_HEADS, ROWS, ROWS)),
        _const_spec((bst_rows, N_HEADS)),
        _const_spec((D_MODEL, D_MODEL)),
        _const_spec((1, D_MODEL)),
    ]


def kernel(x_prompt, x_sample, state_hgrn, norm_g, w_in, lb_logits, hgrn_norm_g, sgu_ln_g,
           sgu_ln_b, w_s, b_s, w_out, final_norm_g):
    depth = norm_g.shape[0]
    assert depth == 1 and lb_logits.shape == (2, D_A)
    batch, seq, _ = x_prompt.shape
    dec_batch, dec_seq, _ = x_sample.shape
    assert seq % PROMPT_TILE == 0 and dec_seq == DEC_SEQ and dec_batch % SEQS_PER_TILE == 0

    weights = (norm_g, w_in[0].astype(BF16), lb_logits, hgrn_norm_g, sgu_ln_g, sgu_ln_b)
    tail = (w_out[0].astype(BF16), final_norm_g.reshape(1, D_MODEL))
    params = pltpu.CompilerParams(dimension_semantics=("parallel", "arbitrary"),
                                  vmem_limit_bytes=VMEM_LIMIT_BYTES)

    n_tiles = seq // PROMPT_TILE
    y_p, st_p, vch_p = pl.pallas_call(
        _prompt_kernel,
        grid=(batch, n_tiles),
        in_specs=[pl.BlockSpec((1, PROMPT_TILE, D_MODEL), lambda bi, si: (bi, si, 0))]
        + _weight_specs(ROWS),
        out_specs=[
            pl.BlockSpec((1, PROMPT_TILE, D_MODEL), lambda bi, si: (bi, si, 0)),
            pl.BlockSpec((1, 1, N_HEADS, HEAD, HEAD), lambda bi, si: (0, bi, 0, 0, 0)),
            pl.BlockSpec((1, 1, ROWS, D_B), lambda bi, si: (0, bi, 0, 0)),
        ],
        out_shape=[
            jax.ShapeDtypeStruct((batch, seq, D_MODEL), F32),
            jax.ShapeDtypeStruct((1, batch, N_HEADS, HEAD, HEAD), F32),
            jax.ShapeDtypeStruct((1, batch, ROWS, D_B), F32),
        ],
        scratch_shapes=[pltpu.VMEM((PROMPT_TILE, D_IN), F32),
                        pltpu.VMEM((PROMPT_TILE, D_MODEL), BF16)],
        compiler_params=params,
        name="prompt_layer",
    )(x_prompt, *weights, w_s[0], b_s[0].T, *tail)

    ws_dec = jnp.tile(w_s[0, :, :DEC_SEQ, :DEC_SEQ], (1, SEQS_PER_TILE, SEQS_PER_TILE))
    bst_dec = jnp.tile(b_s[0, :, :DEC_SEQ].T, (SEQS_PER_TILE, 1))
    n_rows = dec_batch * DEC_SEQ
    y_s, st_s, vch_s = pl.pallas_call(
        _sample_kernel,
        grid=(n_rows // ROWS,),
        in_specs=[pl.BlockSpec((ROWS, D_MODEL), lambda i: (i, 0))]
        + _weight_specs(ROWS)
        + [pl.BlockSpec((1, SEQS_PER_TILE, N_HEADS, HEAD, HEAD), lambda i: (0, i, 0, 0, 0))],
        out_specs=[
            pl.BlockSpec((ROWS, D_MODEL), lambda i: (i, 0)),
            pl.BlockSpec((1, SEQS_PER_TILE, N_HEADS, HEAD, HEAD), lambda i: (0, i, 0, 0, 0)),
            pl.BlockSpec((ROWS, D_B), lambda i: (i, 0)),
        ],
        out_shape=[
            jax.ShapeDtypeStruct((n_rows, D_MODEL), F32),
            jax.ShapeDtypeStruct((1, dec_batch, N_HEADS, HEAD, HEAD), F32),
            jax.ShapeDtypeStruct((n_rows, D_B), F32),
        ],
        scratch_shapes=[pltpu.VMEM((ROWS, D_A), F32),
                        pltpu.VMEM((N_HEADS, HEAD, ROWS), F32),
                        pltpu.VMEM((N_HEADS, HEAD, ROWS), F32),
                        pltpu.VMEM((ROWS, D_A), F32)],
        compiler_params=pltpu.CompilerParams(dimension_semantics=("arbitrary",),
                                             vmem_limit_bytes=VMEM_LIMIT_BYTES),
        name="decode_layer",
    )(x_sample.reshape(n_rows, D_MODEL), *weights, ws_dec, bst_dec, *tail, state_hgrn)

    return (y_p,
            y_s.reshape(dec_batch, DEC_SEQ, D_MODEL),
            st_p,
            st_s,
            vch_p.reshape(1, batch, ROWS, N_HEADS, HEAD),
            vch_s.reshape(1, dec_batch, DEC_SEQ, N_HEADS, HEAD))
```

```python
import functools

import jax
import jax.numpy as jnp
from jax import lax
from jax.experimental import pallas as pl
from jax.experimental.pallas import tpu as pltpu

F32 = jnp.float32
BF16 = jnp.bfloat16

D_MODEL = 1024
D_A = 512
D_B = 512
HEAD = 128
N_HEADS = 4
D_IN = 4 * D_A + 3 * D_B
ROWS = 128
SUBLANES = 8
N_TILES = ROWS // SUBLANES
SUBLANE_LEVELS = 3
EPS = 1e-6
PROMPT_TILE = 256
DEC_SEQ = 8
SEQS_PER_TILE = ROWS // DEC_SEQ
VMEM_LIMIT_BYTES = 48 * 1024 * 1024

_NT = (((1,), (1,)), ((), ()))


def _silu(x):
    return x * jax.nn.sigmoid(x)


def _rms_norm(x, g):
    return x * lax.rsqrt(jnp.mean(x * x, axis=-1, keepdims=True) + EPS) * g


def _layer_norm(x, g, b):
    mu = jnp.mean(x, axis=-1, keepdims=True)
    xc = x - mu
    var = jnp.mean(xc * xc, axis=-1, keepdims=True)
    return xc * lax.rsqrt(var + EPS) * g + b


def _lower_bound(lbl):
    rows = [lbl[r:r + 1, :] for r in range(lbl.shape[0])]
    mx = functools.reduce(jnp.maximum, rows)
    es = [jnp.exp(r - mx) for r in rows]
    return es[0] / functools.reduce(lambda a, c: a + c, es)


def _chunk_constants(seg_len):
    t = lax.broadcasted_iota(jnp.int32, (ROWS, ROWS), 0)
    s = lax.broadcasted_iota(jnp.int32, (ROWS, ROWS), 1)
    n_levels = seg_len.bit_length() - 1
    assert seg_len == 1 << n_levels and n_levels >= SUBLANE_LEVELS
    same_seq = (t >> n_levels) == (s >> n_levels)
    causal = (t >= s) & same_seq
    x = t ^ s
    lvl = jnp.full((ROWS, ROWS), -1, jnp.int32)
    for j in range(n_levels):
        lvl = jnp.where(((x >> j) == 1) & (((t >> j) & 1) == 1), j, lvl)
    rowi = lax.broadcasted_iota(jnp.int32, (ROWS, HEAD), 0)
    bits = [((rowi >> j) & 1) == 1 for j in range(SUBLANE_LEVELS)]
    return causal, lvl, bits, n_levels


def _sibling(x, m, bit):
    x3 = x.reshape(N_TILES, SUBLANES, HEAD)
    up = pltpu.roll(x3, m, 1)
    if 2 * m == SUBLANES:
        return up.reshape(ROWS, HEAD)
    down = pltpu.roll(x3, SUBLANES - m, 1)
    return jnp.where(bit, up.reshape(ROWS, HEAD), down.reshape(ROWS, HEAD))


def _tiles(x):
    return [x[i * SUBLANES:(i + 1) * SUBLANES, :] for i in range(N_TILES)]


def _hgrn_head(q, kk, f, vi, lvl, bits, n_levels):
    one = jnp.ones((ROWS, HEAD), F32)
    pre, suf, blk = f, one, f
    sc = jnp.zeros((ROWS, ROWS), F32)
    for j in range(SUBLANE_LEVELS):
        z = jnp.where(bits[j], q * pre, kk * suf).astype(BF16)
        p = lax.dot_general(z, z, _NT, preferred_element_type=F32)
        sc = jnp.where(lvl == j, p, sc)
        sib = _sibling(blk, 1 << j, bits[j])
        pre = pre * jnp.where(bits[j], sib, one)
        suf = suf * jnp.where(bits[j], one, sib)
        blk = blk * sib

    q_t, k_t, pre_t, suf_t = _tiles(q), _tiles(kk), _tiles(pre), _tiles(suf)
    blk_b = _tiles(blk)
    lvl_t = _tiles(lvl)
    sc_t = _tiles(sc)
    for j in range(SUBLANE_LEVELS, n_levels):
        span = 1 << (j - SUBLANE_LEVELS)
        upper = [(i // span) & 1 == 1 for i in range(N_TILES)]
        z = jnp.concatenate(
            [q_t[i] * pre_t[i] if upper[i] else k_t[i] * suf_t[i] for i in range(N_TILES)],
            axis=0).astype(BF16)
        p = lax.dot_general(z, z, _NT, preferred_element_type=F32)
        for i in range(N_TILES):
            sib = blk_b[(i // span) ^ 1]
            if upper[i]:
                sc_t[i] = jnp.where(lvl_t[i] == j, p[i * SUBLANES:(i + 1) * SUBLANES, :], sc_t[i])
                pre_t[i] = pre_t[i] * sib
            else:
                suf_t[i] = suf_t[i] * sib
        blk_b = [blk_b[2 * n] * blk_b[2 * n + 1] for n in range(len(blk_b) // 2)]

    sc = jnp.concatenate(sc_t, axis=0)
    diag = jnp.sum(q * kk, axis=-1, keepdims=True)
    o = jnp.dot(sc.astype(BF16), vi.astype(BF16), preferred_element_type=F32) + diag * vi
    q_in = jnp.concatenate([q_t[i] * pre_t[i] for i in range(N_TILES)], axis=0)
    k_out = jnp.concatenate([k_t[i] * suf_t[i] for i in range(N_TILES)], axis=0)
    per_tile = N_TILES // len(blk_b)
    total = jnp.concatenate([blk_b[i // per_tile] for i in range(N_TILES)], axis=0)
    return o, q_in, k_out, total


def _hgrn_gates(pq, pf, lb):
    q = _silu(pq)
    forget = lb + (1.0 - lb) * jax.nn.sigmoid(pf)
    return q, 1.0 - forget, forget


def _hgrn_finish(o, pz, g):
    return (_rms_norm(o, g) * _silu(pz)).astype(BF16)


def _chunk_mlp(pu, pv, pzb, lng, lnb, ws_ref, bst, causal):
    u = jax.nn.gelu(pu)
    v_n = _layer_norm(jax.nn.gelu(pv), lng, lnb)
    gate = _silu(pzb)
    outs = []
    for h in range(N_HEADS):
        hs = slice(h * HEAD, (h + 1) * HEAD)
        w = jnp.where(causal, ws_ref[h], 0.0).astype(BF16)
        mix = jnp.dot(w, v_n[:, hs].astype(BF16), preferred_element_type=F32) + bst[:, h:h + 1]
        outs.append((u[:, hs] * mix * gate[:, hs]).astype(BF16))
    return jnp.concatenate(outs, axis=-1), v_n


def _prompt_kernel(x_ref, ng_ref, win_ref, lbl_ref, hg_ref, lng_ref, lnb_ref, ws_ref, bst_ref,
                   wout_ref, fg_ref, y_ref, st_ref, vch_ref, proj_ref, mixed_ref):
    @pl.when(pl.program_id(1) == 0)
    def _():
        st_ref[...] = jnp.zeros_like(st_ref)

    x = x_ref[0]
    h = _rms_norm(x, ng_ref[...]).astype(BF16)
    proj_ref[...] = jnp.dot(h, win_ref[...], preferred_element_type=F32)

    lb = _lower_bound(lbl_ref[...])
    causal, lvl, bits, n_levels = _chunk_constants(ROWS)
    hg = hg_ref[...]
    lng = lng_ref[...]
    lnb = lnb_ref[...]
    bst = bst_ref[...]

    def chunk_body(c, carry):
        rows = pl.ds(pl.multiple_of(c * ROWS, ROWS), ROWS)

        def col(k):
            return proj_ref[rows, k * D_A:(k + 1) * D_A]

        q, kk, f = _hgrn_gates(col(0), col(1), lb)
        vi = col(2)
        pza = col(3)
        for hd in range(N_HEADS):
            hs = slice(hd * HEAD, (hd + 1) * HEAD)
            vh = vi[:, hs]
            o, q_in, k_out, total = _hgrn_head(q[:, hs], kk[:, hs], f[:, hs], vh, lvl, bits,
                                                n_levels)
            state = st_ref[0, 0, hd]
            o = o + jnp.dot(q_in.astype(BF16), state.astype(BF16), preferred_element_type=F32)
            st_ref[0, 0, hd] = total.T * state + jnp.dot(
                k_out.T.astype(BF16), vh.astype(BF16), preferred_element_type=F32)
            mixed_ref[rows, hs] = _hgrn_finish(o, pza[:, hs], hg[:, hs])

        o_b, v_n = _chunk_mlp(col(4), col(5), col(6), lng, lnb, ws_ref, bst, causal)
        mixed_ref[rows, D_A:D_A + D_B] = o_b
        vch_ref[0, 0] = v_n
        return carry

    lax.fori_loop(0, PROMPT_TILE // ROWS, chunk_body, 0)

    out = x + jnp.dot(mixed_ref[...], wout_ref[...], preferred_element_type=F32)
    y_ref[0] = _rms_norm(out, fg_ref[...])


def _sample_kernel(x_ref, ng_ref, win_ref, lbl_ref, hg_ref, lng_ref, lnb_ref, ws_ref, bst_ref,
                   wout_ref, fg_ref, stin_ref, y_ref, stout_ref, vch_ref,
                   qin_ref, kot_ref, tot_ref, oint_ref):
    x = x_ref[...]
    h = _rms_norm(x, ng_ref[...]).astype(BF16)
    proj = jnp.dot(h, win_ref[...], preferred_element_type=F32)

    def col(k):
        return proj[:, k * D_A:(k + 1) * D_A]

    lb = _lower_bound(lbl_ref[...])
    causal, lvl, bits, n_levels = _chunk_constants(DEC_SEQ)
    q, kk, f = _hgrn_gates(col(0), col(1), lb)
    vi = col(2)
    vi16 = vi.astype(BF16)

    o_intra = []
    for hd in range(N_HEADS):
        hs = slice(hd * HEAD, (hd + 1) * HEAD)
        o, q_in, k_out, total = _hgrn_head(q[:, hs], kk[:, hs], f[:, hs], vi[:, hs], lvl, bits,
                                            n_levels)
        o_intra.append(o)
        qin_ref[:, hs] = q_in
        kot_ref[hd] = k_out.T
        tot_ref[hd] = total.T

    lane = lax.broadcasted_iota(jnp.int32, (HEAD, ROWS), 1)

    def seq_body(i, carry):
        r0 = pl.multiple_of(i * DEC_SEQ, DEC_SEQ)
        in_seq = (lane >> (DEC_SEQ.bit_length() - 1)) == i
        first = lane == r0
        for hd in range(N_HEADS):
            hs = slice(hd * HEAD, (hd + 1) * HEAD)
            state = stin_ref[0, i, hd]
            qi = qin_ref[pl.ds(r0, DEC_SEQ), hs].astype(BF16)
            oint_ref[pl.ds(r0, DEC_SEQ), hs] = jnp.dot(
                qi, state.astype(BF16), preferred_element_type=F32)
            k_seq = jnp.where(in_seq, kot_ref[hd], 0.0).astype(BF16)
            decay = jnp.sum(jnp.where(first, tot_ref[hd], 0.0), axis=1, keepdims=True)
            stout_ref[0, i, hd] = decay * state + jnp.dot(
                k_seq, vi16[:, hs], preferred_element_type=F32)
        return carry

    lax.fori_loop(0, SEQS_PER_TILE, seq_body, 0)

    hg = hg_ref[...]
    pza = col(3)
    o_a = []
    for hd in range(N_HEADS):
        hs = slice(hd * HEAD, (hd + 1) * HEAD)
        o_a.append(_hgrn_finish(o_intra[hd] + oint_ref[:, hs], pza[:, hs], hg[:, hs]))

    o_b, v_n = _chunk_mlp(col(4), col(5), col(6), lng_ref[...], lnb_ref[...], ws_ref,
                          bst_ref[...], causal)
    vch_ref[...] = v_n
    mixed = jnp.concatenate(o_a + [o_b], axis=-1)
    out = x + jnp.dot(mixed, wout_ref[...], preferred_element_type=F32)
    y_ref[...] = _rms_norm(out, fg_ref[...])


def _const_spec(shape):
    return pl.BlockSpec(shape, lambda *_: (0,) * len(shape))


def _weight_specs(bst_rows):
    return [
        _const_spec((1, D_MODEL)),
        _const_spec((D_MODEL, D_IN)),
        _const_spec((2, D_A)),
        _const_spec((1, D_A)),
        _const_spec((1, D_B)),
        _const_spec((1, D_B)),
        _const_spec((N_HEADS, ROWS, ROWS)),
        _const_spec((bst_rows, N_HEADS)),
        _const_spec((D_MODEL, D_MODEL)),
        _const_spec((1, D_MODEL)),
    ]


def kernel(x_prompt, x_sample, state_hgrn, norm_g, w_in, lb_logits, hgrn_norm_g, sgu_ln_g,
           sgu_ln_b, w_s, b_s, w_out, final_norm_g):
    depth = norm_g.shape[0]
    assert depth == 1 and lb_logits.shape == (2, D_A)
    batch, seq, _ = x_prompt.shape
    dec_batch, dec_seq, _ = x_sample.shape
    assert seq % PROMPT_TILE == 0 and dec_seq == DEC_SEQ and dec_batch % SEQS_PER_TILE == 0

    weights = (norm_g, w_in[0].astype(BF16), lb_logits, hgrn_norm_g, sgu_ln_g, sgu_ln_b)
    tail = (w_out[0].astype(BF16), final_norm_g.reshape(1, D_MODEL))
    params = pltpu.CompilerParams(dimension_semantics=("parallel", "arbitrary"),
                                  vmem_limit_bytes=VMEM_LIMIT_BYTES)

    n_tiles = seq // PROMPT_TILE
    y_p, st_p, vch_p = pl.pallas_call(
        _prompt_kernel,
        grid=(batch, n_tiles),
        in_specs=[pl.BlockSpec((1, PROMPT_TILE, D_MODEL), lambda bi, si: (bi, si, 0))]
        + _weight_specs(ROWS),
        out_specs=[
            pl.BlockSpec((1, PROMPT_TILE, D_MODEL), lambda bi, si: (bi, si, 0)),
            pl.BlockSpec((1, 1, N_HEADS, HEAD, HEAD), lambda bi, si: (0, bi, 0, 0, 0)),
            pl.BlockSpec((1, 1, ROWS, D_B), lambda bi, si: (0, bi, 0, 0)),
        ],
        out_shape=[
            jax.ShapeDtypeStruct((batch, seq, D_MODEL), F32),
            jax.ShapeDtypeStruct((1, batch, N_HEADS, HEAD, HEAD), F32),
            jax.ShapeDtypeStruct((1, batch, ROWS, D_B), F32),
        ],
        scratch_shapes=[pltpu.VMEM((PROMPT_TILE, D_IN), F32),
                        pltpu.VMEM((PROMPT_TILE, D_MODEL), BF16)],
        compiler_params=params,
        name="prompt_layer",
    )(x_prompt, *weights, w_s[0], b_s[0].T, *tail)

    ws_dec = jnp.tile(w_s[0, :, :DEC_SEQ, :DEC_SEQ], (1, SEQS_PER_TILE, SEQS_PER_TILE))
    bst_dec = jnp.tile(b_s[0, :, :DEC_SEQ].T, (SEQS_PER_TILE, 1))
    n_rows = dec_batch * DEC_SEQ
    y_s, st_s, vch_s = pl.pallas_call(
        _sample_kernel,
        grid=(n_rows // ROWS,),
        in_specs=[pl.BlockSpec((ROWS, D_MODEL), lambda i: (i, 0))]
        + _weight_specs(ROWS)
        + [pl.BlockSpec((1, SEQS_PER_TILE, N_HEADS, HEAD, HEAD), lambda i: (0, i, 0, 0, 0))],
        out_specs=[
            pl.BlockSpec((ROWS, D_MODEL), lambda i: (i, 0)),
            pl.BlockSpec((1, SEQS_PER_TILE, N_HEADS, HEAD, HEAD), lambda i: (0, i, 0, 0, 0)),
            pl.BlockSpec((ROWS, D_B), lambda i: (i, 0)),
        ],
        out_shape=[
            jax.ShapeDtypeStruct((n_rows, D_MODEL), F32),
            jax.ShapeDtypeStruct((1, dec_batch, N_HEADS, HEAD, HEAD), F32),
            jax.ShapeDtypeStruct((n_rows, D_B), F32),
        ],
        scratch_shapes=[pltpu.VMEM((ROWS, D_A), F32),
                        pltpu.VMEM((N_HEADS, HEAD, ROWS), F32),
                        pltpu.VMEM((N_HEADS, HEAD, ROWS), F32),
                        pltpu.VMEM((ROWS, D_A), F32)],
        compiler_params=pltpu.CompilerParams(dimension_semantics=("arbitrary",),
                                             vmem_limit_bytes=VMEM_LIMIT_BYTES),
        name="decode_layer",
    )(x_sample.reshape(n_rows, D_MODEL), *weights, ws_dec, bst_dec, *tail, state_hgrn)

    return (y_p,
            y_s.reshape(dec_batch, DEC_SEQ, D_MODEL),
            st_p,
            st_s,
            vch_p.reshape(1, batch, ROWS, N_HEADS, HEAD),
            vch_s.reshape(1, dec_batch, DEC_SEQ, N_HEADS, HEAD))
```

```python
import functools

import jax
import jax.numpy as jnp
from jax import lax
from jax.experimental import pallas as pl
from jax.experimental.pallas import tpu as pltpu

F32 = jnp.float32
BF16 = jnp.bfloat16

D_MODEL = 1024
D_A = 512
D_B = 512
HEAD = 128
N_HEADS = 4
D_IN = 4 * D_A + 3 * D_B
ROWS = 128
SUBLANES = 8
N_TILES = ROWS // SUBLANES
SUBLANE_LEVELS = 3
EPS = 1e-6
PROMPT_TILE = 256
DEC_SEQ = 8
SEQS_PER_TILE = ROWS // DEC_SEQ
VMEM_LIMIT_BYTES = 48 * 1024 * 1024

_NT = (((1,), (1,)), ((), ()))


def _silu(x):
    return x * jax.nn.sigmoid(x)


def _rms_norm(x, g):
    return x * lax.rsqrt(jnp.mean(x * x, axis=-1, keepdims=True) + EPS) * g


def _layer_norm(x, g, b):
    mu = jnp.mean(x, axis=-1, keepdims=True)
    xc = x - mu
    var = jnp.mean(xc * xc, axis=-1, keepdims=True)
    return xc * lax.rsqrt(var + EPS) * g + b


def _lower_bound(lbl):
    rows = [lbl[r:r + 1, :] for r in range(lbl.shape[0])]
    mx = functools.reduce(jnp.maximum, rows)
    es = [jnp.exp(r - mx) for r in rows]
    return es[0] / functools.reduce(lambda a, c: a + c, es)


def _chunk_constants(seg_len):
    t = lax.broadcasted_iota(jnp.int32, (ROWS, ROWS), 0)
    s = lax.broadcasted_iota(jnp.int32, (ROWS, ROWS), 1)
    n_levels = seg_len.bit_length() - 1
    assert seg_len == 1 << n_levels and n_levels >= SUBLANE_LEVELS
    same_seq = (t >> n_levels) == (s >> n_levels)
    causal = (t >= s) & same_seq
    x = t ^ s
    lvl = jnp.full((ROWS, ROWS), -1, jnp.int32)
    for j in range(n_levels):
        lvl = jnp.where(((x >> j) == 1) & (((t >> j) & 1) == 1), j, lvl)
    rowi = lax.broadcasted_iota(jnp.int32, (ROWS, HEAD), 0)
    bits = [((rowi >> j) & 1) == 1 for j in range(SUBLANE_LEVELS)]
    return causal, lvl, bits, n_levels


def _sibling(x, m, bit):
    x3 = x.reshape(N_TILES, SUBLANES, HEAD)
    up = pltpu.roll(x3, m, 1)
    if 2 * m == SUBLANES:
        return up.reshape(ROWS, HEAD)
    down = pltpu.roll(x3, SUBLANES - m, 1)
    return jnp.where(bit, up.reshape(ROWS, HEAD), down.reshape(ROWS, HEAD))


def _tiles(x):
    return [x[i * SUBLANES:(i + 1) * SUBLANES, :] for i in range(N_TILES)]


def _hgrn_head(q, kk, f, vi, lvl, bits, n_levels):
    one = jnp.ones((ROWS, HEAD), F32)
    pre, suf, blk = f, one, f
    sc = jnp.zeros((ROWS, ROWS), F32)
    for j in range(SUBLANE_LEVELS):
        z = jnp.where(bits[j], q * pre, kk * suf).astype(BF16)
        p = lax.dot_general(z, z, _NT, preferred_element_type=F32)
        sc = jnp.where(lvl == j, p, sc)
        sib = _sibling(blk, 1 << j, bits[j])
        pre = pre * jnp.where(bits[j], sib, one)
        suf = suf * jnp.where(bits[j], one, sib)
        blk = blk * sib
        yield

    q_t, k_t, pre_t, suf_t = _tiles(q), _tiles(kk), _tiles(pre), _tiles(suf)
    blk_b = _tiles(blk)
    lvl_t = _tiles(lvl)
    sc_t = _tiles(sc)
    for j in range(SUBLANE_LEVELS, n_levels):
        span = 1 << (j - SUBLANE_LEVELS)
        upper = [(i // span) & 1 == 1 for i in range(N_TILES)]
        z = jnp.concatenate(
            [q_t[i] * pre_t[i] if upper[i] else k_t[i] * suf_t[i] for i in range(N_TILES)],
            axis=0).astype(BF16)
        p = lax.dot_general(z, z, _NT, preferred_element_type=F32)
        for i in range(N_TILES):
            sib = blk_b[(i // span) ^ 1]
            if upper[i]:
                sc_t[i] = jnp.where(lvl_t[i] == j, p[i * SUBLANES:(i + 1) * SUBLANES, :], sc_t[i])
                pre_t[i] = pre_t[i] * sib
            else:
                suf_t[i] = suf_t[i] * sib
        blk_b = [blk_b[2 * n] * blk_b[2 * n + 1] for n in range(len(blk_b) // 2)]
        yield

    sc = jnp.concatenate(sc_t, axis=0)
    diag = jnp.sum(q * kk, axis=-1, keepdims=True)
    o = jnp.dot(sc.astype(BF16), vi.astype(BF16), preferred_element_type=F32) + diag * vi
    q_in = jnp.concatenate([q_t[i] * pre_t[i] for i in range(N_TILES)], axis=0)
    k_out = jnp.concatenate([k_t[i] * suf_t[i] for i in range(N_TILES)], axis=0)
    per_tile = N_TILES // len(blk_b)
    total = jnp.concatenate([blk_b[i // per_tile] for i in range(N_TILES)], axis=0)
    return o, q_in, k_out, total


def _lockstep(generators):
    results = [None] * len(generators)
    active = list(enumerate(generators))
    while active:
        still = []
        for idx, gen in active:
            try:
                next(gen)
                still.append((idx, gen))
            except StopIteration as done:
                results[idx] = done.value
        active = still
    return results


def _hgrn_gates(pq, pf, lb):
    q = _silu(pq)
    forget = lb + (1.0 - lb) * jax.nn.sigmoid(pf)
    return q, 1.0 - forget, forget


def _hgrn_finish(o, pz, g):
    return (_rms_norm(o, g) * _silu(pz)).astype(BF16)


def _chunk_mlp(pu, pv, pzb, lng, lnb, ws_ref, bst, causal):
    u = jax.nn.gelu(pu)
    v_n = _layer_norm(jax.nn.gelu(pv), lng, lnb)
    gate = _silu(pzb)
    outs = []
    for h in range(N_HEADS):
        hs = slice(h * HEAD, (h + 1) * HEAD)
        w = jnp.where(causal, ws_ref[h], 0.0).astype(BF16)
        mix = jnp.dot(w, v_n[:, hs].astype(BF16), preferred_element_type=F32) + bst[:, h:h + 1]
        outs.append((u[:, hs] * mix * gate[:, hs]).astype(BF16))
    return jnp.concatenate(outs, axis=-1), v_n


def _prompt_kernel(x_ref, ng_ref, win_ref, lbl_ref, hg_ref, lng_ref, lnb_ref, ws_ref, bst_ref,
                   wout_ref, fg_ref, y_ref, st_ref, vch_ref, proj_ref, mixed_ref):
    @pl.when(pl.program_id(1) == 0)
    def _():
        st_ref[...] = jnp.zeros_like(st_ref)

    x = x_ref[0]
    h = _rms_norm(x, ng_ref[...]).astype(BF16)
    proj_ref[...] = jnp.dot(h, win_ref[...], preferred_element_type=F32)

    lb = _lower_bound(lbl_ref[...])
    causal, lvl, bits, n_levels = _chunk_constants(ROWS)
    hg = hg_ref[...]
    lng = lng_ref[...]
    lnb = lnb_ref[...]
    bst = bst_ref[...]

    def chunk_body(c, carry):
        rows = pl.ds(pl.multiple_of(c * ROWS, ROWS), ROWS)

        def col(k):
            return proj_ref[rows, k * D_A:(k + 1) * D_A]

        q, kk, f = _hgrn_gates(col(0), col(1), lb)
        vi = col(2)
        pza = col(3)
        def head(hd):
            hs = slice(hd * HEAD, (hd + 1) * HEAD)
            vh = vi[:, hs]
            o, q_in, k_out, total = yield from _hgrn_head(q[:, hs], kk[:, hs], f[:, hs], vh, lvl,
                                                          bits, n_levels)
            state = st_ref[0, 0, hd]
            o = o + jnp.dot(q_in.astype(BF16), state.astype(BF16), preferred_element_type=F32)
            yield
            st_ref[0, 0, hd] = total.T * state + jnp.dot(
                k_out.T.astype(BF16), vh.astype(BF16), preferred_element_type=F32)
            yield
            mixed_ref[rows, hs] = _hgrn_finish(o, pza[:, hs], hg[:, hs])

        _lockstep([head(hd) for hd in range(N_HEADS)])

        o_b, v_n = _chunk_mlp(col(4), col(5), col(6), lng, lnb, ws_ref, bst, causal)
        mixed_ref[rows, D_A:D_A + D_B] = o_b
        vch_ref[0, 0] = v_n
        return carry

    lax.fori_loop(0, PROMPT_TILE // ROWS, chunk_body, 0)

    out = x + jnp.dot(mixed_ref[...], wout_ref[...], preferred_element_type=F32)
    y_ref[0] = _rms_norm(out, fg_ref[...])


def _sample_kernel(x_ref, ng_ref, win_ref, lbl_ref, hg_ref, lng_ref, lnb_ref, ws_ref, bst_ref,
                   wout_ref, fg_ref, stin_ref, y_ref, stout_ref, vch_ref,
                   qin_ref, kot_ref, tot_ref, oint_ref):
    x = x_ref[...]
    h = _rms_norm(x, ng_ref[...]).astype(BF16)
    proj = jnp.dot(h, win_ref[...], preferred_element_type=F32)

    def col(k):
        return proj[:, k * D_A:(k + 1) * D_A]

    lb = _lower_bound(lbl_ref[...])
    causal, lvl, bits, n_levels = _chunk_constants(DEC_SEQ)
    q, kk, f = _hgrn_gates(col(0), col(1), lb)
    vi = col(2)
    vi16 = vi.astype(BF16)

    def head(hd):
        hs = slice(hd * HEAD, (hd + 1) * HEAD)
        o, q_in, k_out, total = yield from _hgrn_head(q[:, hs], kk[:, hs], f[:, hs], vi[:, hs], lvl,
                                                      bits, n_levels)
        qin_ref[:, hs] = q_in
        kot_ref[hd] = k_out.T
        tot_ref[hd] = total.T
        return o

    o_intra = _lockstep([head(hd) for hd in range(N_HEADS)])

    lane = lax.broadcasted_iota(jnp.int32, (HEAD, ROWS), 1)

    def seq_body(i, carry):
        r0 = pl.multiple_of(i * DEC_SEQ, DEC_SEQ)
        in_seq = (lane >> (DEC_SEQ.bit_length() - 1)) == i
        first = lane == r0
        for hd in range(N_HEADS):
            hs = slice(hd * HEAD, (hd + 1) * HEAD)
            state = stin_ref[0, i, hd]
            qi = qin_ref[pl.ds(r0, DEC_SEQ), hs].astype(BF16)
            oint_ref[pl.ds(r0, DEC_SEQ), hs] = jnp.dot(
                qi, state.astype(BF16), preferred_element_type=F32)
            k_seq = jnp.where(in_seq, kot_ref[hd], 0.0).astype(BF16)
            decay = jnp.sum(jnp.where(first, tot_ref[hd], 0.0), axis=1, keepdims=True)
            stout_ref[0, i, hd] = decay * state + jnp.dot(
                k_seq, vi16[:, hs], preferred_element_type=F32)
        return carry

    lax.fori_loop(0, SEQS_PER_TILE, seq_body, 0)

    hg = hg_ref[...]
    pza = col(3)
    o_a = []
    for hd in range(N_HEADS):
        hs = slice(hd * HEAD, (hd + 1) * HEAD)
        o_a.append(_hgrn_finish(o_intra[hd] + oint_ref[:, hs], pza[:, hs], hg[:, hs]))

    o_b, v_n = _chunk_mlp(col(4), col(5), col(6), lng_ref[...], lnb_ref[...], ws_ref,
                          bst_ref[...], causal)
    vch_ref[...] = v_n
    mixed = jnp.concatenate(o_a + [o_b], axis=-1)
    out = x + jnp.dot(mixed, wout_ref[...], preferred_element_type=F32)
    y_ref[...] = _rms_norm(out, fg_ref[...])


def _const_spec(shape):
    return pl.BlockSpec(shape, lambda *_: (0,) * len(shape))


def _weight_specs(bst_rows):
    return [
        _const_spec((1, D_MODEL)),
        _const_spec((D_MODEL, D_IN)),
        _const_spec((2, D_A)),
        _const_spec((1, D_A)),
        _const_spec((1, D_B)),
        _const_spec((1, D_B)),
        _const_spec((N_HEADS, ROWS, ROWS)),
        _const_spec((bst_rows, N_HEADS)),
        _const_spec((D_MODEL, D_MODEL)),
        _const_spec((1, D_MODEL)),
    ]


def kernel(x_prompt, x_sample, state_hgrn, norm_g, w_in, lb_logits, hgrn_norm_g, sgu_ln_g,
           sgu_ln_b, w_s, b_s, w_out, final_norm_g):
    depth = norm_g.shape[0]
    assert depth == 1 and lb_logits.shape == (2, D_A)
    batch, seq, _ = x_prompt.shape
    dec_batch, dec_seq, _ = x_sample.shape
    assert seq % PROMPT_TILE == 0 and dec_seq == DEC_SEQ and dec_batch % SEQS_PER_TILE == 0

    weights = (norm_g, w_in[0].astype(BF16), lb_logits, hgrn_norm_g, sgu_ln_g, sgu_ln_b)
    tail = (w_out[0].astype(BF16), final_norm_g.reshape(1, D_MODEL))
    params = pltpu.CompilerParams(dimension_semantics=("parallel", "arbitrary"),
                                  vmem_limit_bytes=VMEM_LIMIT_BYTES)

    n_tiles = seq // PROMPT_TILE
    y_p, st_p, vch_p = pl.pallas_call(
        _prompt_kernel,
        grid=(batch, n_tiles),
        in_specs=[pl.BlockSpec((1, PROMPT_TILE, D_MODEL), lambda bi, si: (bi, si, 0))]
        + _weight_specs(ROWS),
        out_specs=[
            pl.BlockSpec((1, PROMPT_TILE, D_MODEL), lambda bi, si: (bi, si, 0)),
            pl.BlockSpec((1, 1, N_HEADS, HEAD, HEAD), lambda bi, si: (0, bi, 0, 0, 0)),
            pl.BlockSpec((1, 1, ROWS, D_B), lambda bi, si: (0, bi, 0, 0)),
        ],
        out_shape=[
            jax.ShapeDtypeStruct((batch, seq, D_MODEL), F32),
            jax.ShapeDtypeStruct((1, batch, N_HEADS, HEAD, HEAD), F32),
            jax.ShapeDtypeStruct((1, batch, ROWS, D_B), F32),
        ],
        scratch_shapes=[pltpu.VMEM((PROMPT_TILE, D_IN), F32),
                        pltpu.VMEM((PROMPT_TILE, D_MODEL), BF16)],
        compiler_params=params,
        name="prompt_layer",
    )(x_prompt, *weights, w_s[0], b_s[0].T, *tail)

    ws_dec = jnp.tile(w_s[0, :, :DEC_SEQ, :DEC_SEQ], (1, SEQS_PER_TILE, SEQS_PER_TILE))
    bst_dec = jnp.tile(b_s[0, :, :DEC_SEQ].T, (SEQS_PER_TILE, 1))
    n_rows = dec_batch * DEC_SEQ
    y_s, st_s, vch_s = pl.pallas_call(
        _sample_kernel,
        grid=(n_rows // ROWS,),
        in_specs=[pl.BlockSpec((ROWS, D_MODEL), lambda i: (i, 0))]
        + _weight_specs(ROWS)
        + [pl.BlockSpec((1, SEQS_PER_TILE, N_HEADS, HEAD, HEAD), lambda i: (0, i, 0, 0, 0))],
        out_specs=[
            pl.BlockSpec((ROWS, D_MODEL), lambda i: (i, 0)),
            pl.BlockSpec((1, SEQS_PER_TILE, N_HEADS, HEAD, HEAD), lambda i: (0, i, 0, 0, 0)),
            pl.BlockSpec((ROWS, D_B), lambda i: (i, 0)),
        ],
        out_shape=[
            jax.ShapeDtypeStruct((n_rows, D_MODEL), F32),
            jax.ShapeDtypeStruct((1, dec_batch, N_HEADS, HEAD, HEAD), F32),
            jax.ShapeDtypeStruct((n_rows, D_B), F32),
        ],
        scratch_shapes=[pltpu.VMEM((ROWS, D_A), F32),
                        pltpu.VMEM((N_HEADS, HEAD, ROWS), F32),
                        pltpu.VMEM((N_HEADS, HEAD, ROWS), F32),
                        pltpu.VMEM((ROWS, D_A), F32)],
        compiler_params=pltpu.CompilerParams(dimension_semantics=("arbitrary",),
                                             vmem_limit_bytes=VMEM_LIMIT_BYTES),
        name="decode_layer",
    )(x_sample.reshape(n_rows, D_MODEL), *weights, ws_dec, bst_dec, *tail, state_hgrn)

    return (y_p,
            y_s.reshape(dec_batch, DEC_SEQ, D_MODEL),
            st_p,
            st_s,
            vch_p.reshape(1, batch, ROWS, N_HEADS, HEAD),
            vch_s.reshape(1, dec_batch, DEC_SEQ, N_HEADS, HEAD))
```

```python
import functools

import jax
import jax.numpy as jnp
from jax import lax
from jax.experimental import pallas as pl
from jax.experimental.pallas import tpu as pltpu

F32 = jnp.float32
BF16 = jnp.bfloat16

D_MODEL = 1024
D_A = 512
D_B = 512
HEAD = 128
N_HEADS = 4
D_IN = 4 * D_A + 3 * D_B
ROWS = 128
SUBLANES = 8
N_TILES = ROWS // SUBLANES
SUBLANE_LEVELS = 3
EPS = 1e-6
PROMPT_TILE = 256
PROJ_BLOCK = 256
DEC_SEQ = 8
SEQS_PER_TILE = ROWS // DEC_SEQ
VMEM_LIMIT_BYTES = 48 * 1024 * 1024

_NT = (((1,), (1,)), ((), ()))


def _silu(x):
    return x * jax.nn.sigmoid(x)


def _rms_norm(x, g):
    return x * lax.rsqrt(jnp.mean(x * x, axis=-1, keepdims=True) + EPS) * g


def _layer_norm(x, g, b):
    mu = jnp.mean(x, axis=-1, keepdims=True)
    xc = x - mu
    var = jnp.mean(xc * xc, axis=-1, keepdims=True)
    return xc * lax.rsqrt(var + EPS) * g + b


def _lower_bound(lbl):
    rows = [lbl[r:r + 1, :] for r in range(lbl.shape[0])]
    mx = functools.reduce(jnp.maximum, rows)
    es = [jnp.exp(r - mx) for r in rows]
    return es[0] / functools.reduce(lambda a, c: a + c, es)


def _chunk_constants(seg_len):
    t = lax.broadcasted_iota(jnp.int32, (ROWS, ROWS), 0)
    s = lax.broadcasted_iota(jnp.int32, (ROWS, ROWS), 1)
    n_levels = seg_len.bit_length() - 1
    assert seg_len == 1 << n_levels and n_levels >= SUBLANE_LEVELS
    same_seq = (t >> n_levels) == (s >> n_levels)
    causal = (t >= s) & same_seq
    x = t ^ s
    lvl = jnp.full((ROWS, ROWS), -1, jnp.int32)
    for j in range(n_levels):
        lvl = jnp.where(((x >> j) == 1) & (((t >> j) & 1) == 1), j, lvl)
    rowi = lax.broadcasted_iota(jnp.int32, (ROWS, HEAD), 0)
    bits = [((rowi >> j) & 1) == 1 for j in range(SUBLANE_LEVELS)]
    return causal, lvl, bits, n_levels


def _sibling(x, m, bit):
    x3 = x.reshape(N_TILES, SUBLANES, HEAD)
    up = pltpu.roll(x3, m, 1)
    if 2 * m == SUBLANES:
        return up.reshape(ROWS, HEAD)
    down = pltpu.roll(x3, SUBLANES - m, 1)
    return jnp.where(bit, up.reshape(ROWS, HEAD), down.reshape(ROWS, HEAD))


def _tiles(x):
    return [x[i * SUBLANES:(i + 1) * SUBLANES, :] for i in range(N_TILES)]


def _hgrn_head(q, kk, f, vi, lvl, bits, n_levels):
    one = jnp.ones((ROWS, HEAD), F32)
    pre, suf, blk = f, one, f
    sc = jnp.zeros((ROWS, ROWS), F32)
    for j in range(SUBLANE_LEVELS):
        z = jnp.where(bits[j], q * pre, kk * suf).astype(BF16)
        p = lax.dot_general(z, z, _NT, preferred_element_type=F32)
        sc = jnp.where(lvl == j, p, sc)
        sib = _sibling(blk, 1 << j, bits[j])
        pre = pre * jnp.where(bits[j], sib, one)
        suf = suf * jnp.where(bits[j], one, sib)
        blk = blk * sib
        yield

    q_t, k_t, pre_t, suf_t = _tiles(q), _tiles(kk), _tiles(pre), _tiles(suf)
    blk_b = _tiles(blk)
    lvl_t = _tiles(lvl)
    sc_t = _tiles(sc)
    for j in range(SUBLANE_LEVELS, n_levels):
        span = 1 << (j - SUBLANE_LEVELS)
        upper = [(i // span) & 1 == 1 for i in range(N_TILES)]
        z = jnp.concatenate(
            [q_t[i] * pre_t[i] if upper[i] else k_t[i] * suf_t[i] for i in range(N_TILES)],
            axis=0).astype(BF16)
        p = lax.dot_general(z, z, _NT, preferred_element_type=F32)
        for i in range(N_TILES):
            sib = blk_b[(i // span) ^ 1]
            if upper[i]:
                sc_t[i] = jnp.where(lvl_t[i] == j, p[i * SUBLANES:(i + 1) * SUBLANES, :], sc_t[i])
                pre_t[i] = pre_t[i] * sib
            else:
                suf_t[i] = suf_t[i] * sib
        blk_b = [blk_b[2 * n] * blk_b[2 * n + 1] for n in range(len(blk_b) // 2)]
        yield

    sc = jnp.concatenate(sc_t, axis=0)
    diag = jnp.sum(q * kk, axis=-1, keepdims=True)
    o = jnp.dot(sc.astype(BF16), vi.astype(BF16), preferred_element_type=F32) + diag * vi
    q_in = jnp.concatenate([q_t[i] * pre_t[i] for i in range(N_TILES)], axis=0)
    k_out = jnp.concatenate([k_t[i] * suf_t[i] for i in range(N_TILES)], axis=0)
    per_tile = N_TILES // len(blk_b)
    total = jnp.concatenate([blk_b[i // per_tile] for i in range(N_TILES)], axis=0)
    return o, q_in, k_out, total


def _lockstep(generators):
    results = [None] * len(generators)
    active = list(enumerate(generators))
    while active:
        still = []
        for idx, gen in active:
            try:
                next(gen)
                still.append((idx, gen))
            except StopIteration as done:
                results[idx] = done.value
        active = still
    return results


def _hgrn_gates(pq, pf, lb):
    q = _silu(pq)
    forget = lb + (1.0 - lb) * jax.nn.sigmoid(pf)
    return q, 1.0 - forget, forget


def _hgrn_finish(o, pz, g):
    return (_rms_norm(o, g) * _silu(pz)).astype(BF16)


def _chunk_mlp(pu, pv, pzb, lng, lnb, ws_ref, bst, causal):
    u = jax.nn.gelu(pu)
    v_n = _layer_norm(jax.nn.gelu(pv), lng, lnb)
    gate = _silu(pzb)
    outs = []
    for h in range(N_HEADS):
        hs = slice(h * HEAD, (h + 1) * HEAD)
        w = jnp.where(causal, ws_ref[h], 0.0).astype(BF16)
        mix = jnp.dot(w, v_n[:, hs].astype(BF16), preferred_element_type=F32) + bst[:, h:h + 1]
        outs.append((u[:, hs] * mix * gate[:, hs]).astype(BF16))
    return jnp.concatenate(outs, axis=-1), v_n


def _prompt_kernel(x_ref, xprev_ref, ng_ref, win_ref, lbl_ref, hg_ref, lng_ref, lnb_ref, ws_ref,
                   bst_ref, wout_ref, fg_ref, y_ref, st_ref, vch_ref, proj_a, proj_b, h_ref,
                   mixed_ref, *, tiles_per_seq):
    g = pl.program_id(0)

    @pl.when(g == 0)
    def _():
        proj_b[...] = jnp.zeros_like(proj_b)

    @pl.when(lax.rem(jnp.maximum(g - 1, 0), tiles_per_seq) == 0)
    def _():
        st_ref[...] = jnp.zeros_like(st_ref)

    def step(proj_next, proj_prev):
        h_ref[...] = _rms_norm(x_ref[0], ng_ref[...]).astype(BF16)
        lb = _lower_bound(lbl_ref[...])
        causal, lvl, bits, n_levels = _chunk_constants(ROWS)
        hg = hg_ref[...]
        lng = lng_ref[...]
        lnb = lnb_ref[...]
        bst = bst_ref[...]
        n_chunks = PROMPT_TILE // ROWS
        blocks_per_chunk = D_IN // PROJ_BLOCK // n_chunks

        def projection(first_block):
            for blk in range(first_block, first_block + blocks_per_chunk):
                cols = slice(blk * PROJ_BLOCK, (blk + 1) * PROJ_BLOCK)
                proj_next[:, cols] = jnp.dot(h_ref[...], win_ref[:, cols],
                                             preferred_element_type=F32)
                yield

        for c in range(n_chunks):
            rows = slice(c * ROWS, (c + 1) * ROWS)

            def col(k, hs=slice(0, D_A)):
                return proj_prev[rows, k * D_A + hs.start:k * D_A + hs.stop]

            def head(hd):
                hs = slice(hd * HEAD, (hd + 1) * HEAD)
                q, kk, f = _hgrn_gates(col(0, hs), col(1, hs), lb[:, hs])
                vh = col(2, hs)
                yield
                o, q_in, k_out, total = yield from _hgrn_head(q, kk, f, vh, lvl, bits, n_levels)
                state = st_ref[0, 0, hd]
                o = o + jnp.dot(q_in.astype(BF16), state.astype(BF16),
                                preferred_element_type=F32)
                yield
                st_ref[0, 0, hd] = total.T * state + jnp.dot(
                    k_out.T.astype(BF16), vh.astype(BF16), preferred_element_type=F32)
                yield
                mixed_ref[rows, hs] = _hgrn_finish(o, col(3, hs), hg[:, hs])

            def mlp():
                u = jax.nn.gelu(col(4))
                yield
                v_n = _layer_norm(jax.nn.gelu(col(5)), lng, lnb)
                if c == n_chunks - 1:
                    vch_ref[0, 0] = v_n
                yield
                gate = _silu(col(6))
                yield
                for hd in range(N_HEADS):
                    hs = slice(hd * HEAD, (hd + 1) * HEAD)
                    w = jnp.where(causal, ws_ref[hd], 0.0).astype(BF16)
                    mix = jnp.dot(w, v_n[:, hs].astype(BF16),
                                  preferred_element_type=F32) + bst[:, hd:hd + 1]
                    mixed_ref[rows, D_A + hd * HEAD:D_A + (hd + 1) * HEAD] = (
                        u[:, hs] * mix * gate[:, hs]).astype(BF16)
                    yield

            _lockstep([head(hd) for hd in range(N_HEADS)]
                      + [mlp(), projection(c * blocks_per_chunk)])

        out = xprev_ref[0] + jnp.dot(mixed_ref[...], wout_ref[...], preferred_element_type=F32)
        y_ref[0] = _rms_norm(out, fg_ref[...])

    parity = lax.rem(g, 2)

    @pl.when(parity == 0)
    def _():
        step(proj_a, proj_b)

    @pl.when(parity == 1)
    def _():
        step(proj_b, proj_a)


def _sample_kernel(x_ref, ng_ref, win_ref, lbl_ref, hg_ref, lng_ref, lnb_ref, ws_ref, bst_ref,
                   wout_ref, fg_ref, stin_ref, y_ref, stout_ref, vch_ref,
                   qin_ref, kot_ref, tot_ref, oint_ref):
    x = x_ref[...]
    h = _rms_norm(x, ng_ref[...]).astype(BF16)
    proj = jnp.dot(h, win_ref[...], preferred_element_type=F32)

    def col(k):
        return proj[:, k * D_A:(k + 1) * D_A]

    lb = _lower_bound(lbl_ref[...])
    causal, lvl, bits, n_levels = _chunk_constants(DEC_SEQ)
    q, kk, f = _hgrn_gates(col(0), col(1), lb)
    vi = col(2)
    vi16 = vi.astype(BF16)

    def head(hd):
        hs = slice(hd * HEAD, (hd + 1) * HEAD)
        o, q_in, k_out, total = yield from _hgrn_head(q[:, hs], kk[:, hs], f[:, hs], vi[:, hs], lvl,
                                                      bits, n_levels)
        qin_ref[:, hs] = q_in
        kot_ref[hd] = k_out.T
        tot_ref[hd] = total.T
        return o

    o_intra = _lockstep([head(hd) for hd in range(N_HEADS)])

    lane = lax.broadcasted_iota(jnp.int32, (HEAD, ROWS), 1)

    def seq_body(i, carry):
        r0 = pl.multiple_of(i * DEC_SEQ, DEC_SEQ)
        in_seq = (lane >> (DEC_SEQ.bit_length() - 1)) == i
        first = lane == r0
        for hd in range(N_HEADS):
            hs = slice(hd * HEAD, (hd + 1) * HEAD)
            state = stin_ref[0, i, hd]
            qi = qin_ref[pl.ds(r0, DEC_SEQ), hs].astype(BF16)
            oint_ref[pl.ds(r0, DEC_SEQ), hs] = jnp.dot(
                qi, state.astype(BF16), preferred_element_type=F32)
            k_seq = jnp.where(in_seq, kot_ref[hd], 0.0).astype(BF16)
            decay = jnp.sum(jnp.where(first, tot_ref[hd], 0.0), axis=1, keepdims=True)
            stout_ref[0, i, hd] = decay * state + jnp.dot(
                k_seq, vi16[:, hs], preferred_element_type=F32)
        return carry

    lax.fori_loop(0, SEQS_PER_TILE, seq_body, 0)

    hg = hg_ref[...]
    pza = col(3)
    o_a = []
    for hd in range(N_HEADS):
        hs = slice(hd * HEAD, (hd + 1) * HEAD)
        o_a.append(_hgrn_finish(o_intra[hd] + oint_ref[:, hs], pza[:, hs], hg[:, hs]))

    o_b, v_n = _chunk_mlp(col(4), col(5), col(6), lng_ref[...], lnb_ref[...], ws_ref,
                          bst_ref[...], causal)
    vch_ref[...] = v_n
    mixed = jnp.concatenate(o_a + [o_b], axis=-1)
    out = x + jnp.dot(mixed, wout_ref[...], preferred_element_type=F32)
    y_ref[...] = _rms_norm(out, fg_ref[...])


def _const_spec(shape):
    return pl.BlockSpec(shape, lambda *_: (0,) * len(shape))


def _weight_specs(bst_rows):
    return [
        _const_spec((1, D_MODEL)),
        _const_spec((D_MODEL, D_IN)),
        _const_spec((2, D_A)),
        _const_spec((1, D_A)),
        _const_spec((1, D_B)),
        _const_spec((1, D_B)),
        _const_spec((N_HEADS, ROWS, ROWS)),
        _const_spec((bst_rows, N_HEADS)),
        _const_spec((D_MODEL, D_MODEL)),
        _const_spec((1, D_MODEL)),
    ]


def kernel(x_prompt, x_sample, state_hgrn, norm_g, w_in, lb_logits, hgrn_norm_g, sgu_ln_g,
           sgu_ln_b, w_s, b_s, w_out, final_norm_g):
    depth = norm_g.shape[0]
    assert depth == 1 and lb_logits.shape == (2, D_A)
    batch, seq, _ = x_prompt.shape
    dec_batch, dec_seq, _ = x_sample.shape
    assert seq % PROMPT_TILE == 0 and dec_seq == DEC_SEQ and dec_batch % SEQS_PER_TILE == 0

    weights = (norm_g, w_in[0].astype(BF16), lb_logits, hgrn_norm_g, sgu_ln_g, sgu_ln_b)
    tail = (w_out[0].astype(BF16), final_norm_g.reshape(1, D_MODEL))
    params = pltpu.CompilerParams(dimension_semantics=("arbitrary",),
                                  vmem_limit_bytes=VMEM_LIMIT_BYTES)

    tiles_per_seq = seq // PROMPT_TILE
    n_tiles = batch * tiles_per_seq

    def cur_tile(g):
        t = jnp.minimum(g, n_tiles - 1)
        return t // tiles_per_seq, t % tiles_per_seq

    def prev_tile(g):
        t = jnp.maximum(g - 1, 0)
        return t // tiles_per_seq, t % tiles_per_seq

    y_p, st_p, vch_p = pl.pallas_call(
        functools.partial(_prompt_kernel, tiles_per_seq=tiles_per_seq),
        grid=(n_tiles + 1,),
        in_specs=[pl.BlockSpec((1, PROMPT_TILE, D_MODEL), lambda g: (*cur_tile(g), 0)),
                  pl.BlockSpec((1, PROMPT_TILE, D_MODEL), lambda g: (*prev_tile(g), 0))]
        + _weight_specs(ROWS),
        out_specs=[
            pl.BlockSpec((1, PROMPT_TILE, D_MODEL), lambda g: (*prev_tile(g), 0)),
            pl.BlockSpec((1, 1, N_HEADS, HEAD, HEAD), lambda g: (0, prev_tile(g)[0], 0, 0, 0)),
            pl.BlockSpec((1, 1, ROWS, D_B), lambda g: (0, prev_tile(g)[0], 0, 0)),
        ],
        out_shape=[
            jax.ShapeDtypeStruct((batch, seq, D_MODEL), F32),
            jax.ShapeDtypeStruct((1, batch, N_HEADS, HEAD, HEAD), F32),
            jax.ShapeDtypeStruct((1, batch, ROWS, D_B), F32),
        ],
        scratch_shapes=[pltpu.VMEM((PROMPT_TILE, D_IN), F32),
                        pltpu.VMEM((PROMPT_TILE, D_IN), F32),
                        pltpu.VMEM((PROMPT_TILE, D_MODEL), BF16),
                        pltpu.VMEM((PROMPT_TILE, D_MODEL), BF16)],
        compiler_params=params,
        name="prompt_layer",
    )(x_prompt, x_prompt, *weights, w_s[0], b_s[0].T, *tail)

    ws_dec = jnp.tile(w_s[0, :, :DEC_SEQ, :DEC_SEQ], (1, SEQS_PER_TILE, SEQS_PER_TILE))
    bst_dec = jnp.tile(b_s[0, :, :DEC_SEQ].T, (SEQS_PER_TILE, 1))
    n_rows = dec_batch * DEC_SEQ
    y_s, st_s, vch_s = pl.pallas_call(
        _sample_kernel,
        grid=(n_rows // ROWS,),
        in_specs=[pl.BlockSpec((ROWS, D_MODEL), lambda i: (i, 0))]
        + _weight_specs(ROWS)
        + [pl.BlockSpec((1, SEQS_PER_TILE, N_HEADS, HEAD, HEAD), lambda i: (0, i, 0, 0, 0))],
        out_specs=[
            pl.BlockSpec((ROWS, D_MODEL), lambda i: (i, 0)),
            pl.BlockSpec((1, SEQS_PER_TILE, N_HEADS, HEAD, HEAD), lambda i: (0, i, 0, 0, 0)),
            pl.BlockSpec((ROWS, D_B), lambda i: (i, 0)),
        ],
        out_shape=[
            jax.ShapeDtypeStruct((n_rows, D_MODEL), F32),
            jax.ShapeDtypeStruct((1, dec_batch, N_HEADS, HEAD, HEAD), F32),
            jax.ShapeDtypeStruct((n_rows, D_B), F32),
        ],
        scratch_shapes=[pltpu.VMEM((ROWS, D_A), F32),
                        pltpu.VMEM((N_HEADS, HEAD, ROWS), F32),
                        pltpu.VMEM((N_HEADS, HEAD, ROWS), F32),
                        pltpu.VMEM((ROWS, D_A), F32)],
        compiler_params=pltpu.CompilerParams(dimension_semantics=("arbitrary",),
                                             vmem_limit_bytes=VMEM_LIMIT_BYTES),
        name="decode_layer",
    )(x_sample.reshape(n_rows, D_MODEL), *weights, ws_dec, bst_dec, *tail, state_hgrn)

    return (y_p,
            y_s.reshape(dec_batch, DEC_SEQ, D_MODEL),
            st_p,
            st_s,
            vch_p.reshape(1, batch, ROWS, N_HEADS, HEAD),
            vch_s.reshape(1, dec_batch, DEC_SEQ, N_HEADS, HEAD))
```

```python
import functools

import jax
import jax.numpy as jnp
from jax import lax
from jax.experimental import pallas as pl
from jax.experimental.pallas import tpu as pltpu

F32 = jnp.float32
BF16 = jnp.bfloat16

D_MODEL = 1024
D_A = 512
D_B = 512
HEAD = 128
N_HEADS = 4
D_IN = 4 * D_A + 3 * D_B
ROWS = 128
SUBLANES = 8
N_TILES = ROWS // SUBLANES
SUBLANE_LEVELS = 3
EPS = 1e-6
PROMPT_TILE = 512
PROJ_BLOCK = 256
DEC_SEQ = 8
SEQS_PER_TILE = ROWS // DEC_SEQ
VMEM_LIMIT_BYTES = 56 * 1024 * 1024

_NT = (((1,), (1,)), ((), ()))


def _silu(x):
    return x * jax.nn.sigmoid(x)


def _rms_norm(x, g):
    return x * lax.rsqrt(jnp.mean(x * x, axis=-1, keepdims=True) + EPS) * g


def _layer_norm(x, g, b):
    mu = jnp.mean(x, axis=-1, keepdims=True)
    xc = x - mu
    var = jnp.mean(xc * xc, axis=-1, keepdims=True)
    return xc * lax.rsqrt(var + EPS) * g + b


def _lower_bound(lbl):
    rows = [lbl[r:r + 1, :] for r in range(lbl.shape[0])]
    mx = functools.reduce(jnp.maximum, rows)
    es = [jnp.exp(r - mx) for r in rows]
    return es[0] / functools.reduce(lambda a, c: a + c, es)


def _chunk_constants(seg_len):
    t = lax.broadcasted_iota(jnp.int32, (ROWS, ROWS), 0)
    s = lax.broadcasted_iota(jnp.int32, (ROWS, ROWS), 1)
    n_levels = seg_len.bit_length() - 1
    assert seg_len == 1 << n_levels and n_levels >= SUBLANE_LEVELS
    same_seq = (t >> n_levels) == (s >> n_levels)
    causal = (t >= s) & same_seq
    x = t ^ s
    lvl = jnp.full((ROWS, ROWS), -1, jnp.int32)
    for j in range(n_levels):
        lvl = jnp.where(((x >> j) == 1) & (((t >> j) & 1) == 1), j, lvl)
    rowi = lax.broadcasted_iota(jnp.int32, (ROWS, HEAD), 0)
    bits = [((rowi >> j) & 1) == 1 for j in range(SUBLANE_LEVELS)]
    return causal, lvl, bits, n_levels


def _sibling(x, m, bit):
    x3 = x.reshape(N_TILES, SUBLANES, HEAD)
    up = pltpu.roll(x3, m, 1)
    if 2 * m == SUBLANES:
        return up.reshape(ROWS, HEAD)
    down = pltpu.roll(x3, SUBLANES - m, 1)
    return jnp.where(bit, up.reshape(ROWS, HEAD), down.reshape(ROWS, HEAD))


def _tiles(x):
    return [x[i * SUBLANES:(i + 1) * SUBLANES, :] for i in range(N_TILES)]


def _hgrn_head(q, kk, f, vi, lvl, bits, n_levels, state=None):
    one = jnp.ones((ROWS, HEAD), F32)
    pre, suf, blk = f, one, f
    sc = jnp.zeros((ROWS, ROWS), F32)
    for j in range(SUBLANE_LEVELS):
        z = jnp.where(bits[j], q * pre, kk * suf).astype(BF16)
        p = lax.dot_general(z, z, _NT, preferred_element_type=F32)
        sc = jnp.where(lvl == j, p, sc)
        sib = _sibling(blk, 1 << j, bits[j])
        pre = pre * jnp.where(bits[j], sib, one)
        suf = suf * jnp.where(bits[j], one, sib)
        blk = blk * sib
        yield

    q_t, k_t, pre_t, suf_t = _tiles(q), _tiles(kk), _tiles(pre), _tiles(suf)
    blk_b = _tiles(blk)
    lvl_t = _tiles(lvl)
    sc_t = _tiles(sc)
    for j in range(SUBLANE_LEVELS, n_levels):
        span = 1 << (j - SUBLANE_LEVELS)
        upper = [(i // span) & 1 == 1 for i in range(N_TILES)]
        z = jnp.concatenate(
            [q_t[i] * pre_t[i] if upper[i] else k_t[i] * suf_t[i] for i in range(N_TILES)],
            axis=0).astype(BF16)
        p = lax.dot_general(z, z, _NT, preferred_element_type=F32)
        for i in range(N_TILES):
            sib = blk_b[(i // span) ^ 1]
            if upper[i]:
                sc_t[i] = jnp.where(lvl_t[i] == j, p[i * SUBLANES:(i + 1) * SUBLANES, :], sc_t[i])
                pre_t[i] = pre_t[i] * sib
            else:
                suf_t[i] = suf_t[i] * sib
        blk_b = [blk_b[2 * n] * blk_b[2 * n + 1] for n in range(len(blk_b) // 2)]
        yield

    sc = jnp.concatenate(sc_t, axis=0)
    diag = jnp.sum(q * kk, axis=-1, keepdims=True)
    q_in = jnp.concatenate([q_t[i] * pre_t[i] for i in range(N_TILES)], axis=0)
    k_out = jnp.concatenate([k_t[i] * suf_t[i] for i in range(N_TILES)], axis=0)
    if state is None:
        o = jnp.dot(sc.astype(BF16), vi.astype(BF16), preferred_element_type=F32) + diag * vi
    else:
        lhs = jnp.concatenate([sc.astype(BF16), q_in.astype(BF16)], axis=1)
        rhs = jnp.concatenate([vi.astype(BF16), state.astype(BF16)], axis=0)
        o = jnp.dot(lhs, rhs, preferred_element_type=F32) + diag * vi
    per_tile = N_TILES // len(blk_b)
    total = jnp.concatenate([blk_b[i // per_tile] for i in range(N_TILES)], axis=0)
    return o, q_in, k_out, total


def _lockstep(generators):
    results = [None] * len(generators)
    active = list(enumerate(generators))
    while active:
        still = []
        for idx, gen in active:
            try:
                next(gen)
                still.append((idx, gen))
            except StopIteration as done:
                results[idx] = done.value
        active = still
    return results


def _hgrn_gates(pq, pf, lb):
    q = _silu(pq)
    forget = lb + (1.0 - lb) * jax.nn.sigmoid(pf)
    return q, 1.0 - forget, forget


def _hgrn_finish(o, pz, g):
    return (_rms_norm(o, g) * _silu(pz)).astype(BF16)


def _chunk_mlp(pu, pv, pzb, lng, lnb, ws_ref, bst, causal):
    u = jax.nn.gelu(pu)
    v_n = _layer_norm(jax.nn.gelu(pv), lng, lnb)
    gate = _silu(pzb)
    outs = []
    for h in range(N_HEADS):
        hs = slice(h * HEAD, (h + 1) * HEAD)
        w = jnp.where(causal, ws_ref[h], 0.0).astype(BF16)
        mix = jnp.dot(w, v_n[:, hs].astype(BF16), preferred_element_type=F32) + bst[:, h:h + 1]
        outs.append((u[:, hs] * mix * gate[:, hs]).astype(BF16))
    return jnp.concatenate(outs, axis=-1), v_n


def _prompt_kernel(x_ref, xprev_ref, ng_ref, win_ref, lbl_ref, hg_ref, lng_ref, lnb_ref, ws_ref,
                   bst_ref, wout_ref, fg_ref, y_ref, st_ref, vch_ref, proj_a, proj_b, h_ref,
                   mixed_ref, *, tiles_per_seq):
    g = pl.program_id(0)

    @pl.when(g == 0)
    def _():
        proj_b[...] = jnp.zeros_like(proj_b)

    @pl.when(lax.rem(jnp.maximum(g - 1, 0), tiles_per_seq) == 0)
    def _():
        st_ref[...] = jnp.zeros_like(st_ref)

    def step(proj_next, proj_prev):
        h_ref[...] = _rms_norm(x_ref[0], ng_ref[...]).astype(BF16)
        lb = _lower_bound(lbl_ref[...])
        causal, lvl, bits, n_levels = _chunk_constants(ROWS)
        hg = hg_ref[...]
        lng = lng_ref[...]
        lnb = lnb_ref[...]
        bst = bst_ref[...]
        n_chunks = PROMPT_TILE // ROWS
        n_blocks = D_IN // PROJ_BLOCK

        def projection(c):
            for blk in range(c * n_blocks // n_chunks, (c + 1) * n_blocks // n_chunks):
                cols = slice(blk * PROJ_BLOCK, (blk + 1) * PROJ_BLOCK)
                proj_next[:, cols] = jnp.dot(h_ref[...], win_ref[:, cols],
                                             preferred_element_type=F32)
                yield

        for c in range(n_chunks):
            rows = slice(c * ROWS, (c + 1) * ROWS)

            def col(k, hs=slice(0, D_A)):
                return proj_prev[rows, k * D_A + hs.start:k * D_A + hs.stop]

            def head(hd):
                hs = slice(hd * HEAD, (hd + 1) * HEAD)
                q, kk, f = _hgrn_gates(col(0, hs), col(1, hs), lb[:, hs])
                vh = col(2, hs)
                yield
                state = st_ref[0, 0, hd]
                o, _, k_out, total = yield from _hgrn_head(q, kk, f, vh, lvl, bits, n_levels,
                                                           state)
                yield
                st_ref[0, 0, hd] = total.T * state + jnp.dot(
                    k_out.T.astype(BF16), vh.astype(BF16), preferred_element_type=F32)
                yield
                mixed_ref[rows, hs] = _hgrn_finish(o, col(3, hs), hg[:, hs])

            def mlp():
                u = jax.nn.gelu(col(4))
                yield
                v_n = _layer_norm(jax.nn.gelu(col(5)), lng, lnb)
                if c == n_chunks - 1:
                    vch_ref[0, 0] = v_n
                yield
                gate = _silu(col(6))
                yield
                for hd in range(N_HEADS):
                    hs = slice(hd * HEAD, (hd + 1) * HEAD)
                    w = jnp.where(causal, ws_ref[hd], 0.0).astype(BF16)
                    mix = jnp.dot(w, v_n[:, hs].astype(BF16),
                                  preferred_element_type=F32) + bst[:, hd:hd + 1]
                    mixed_ref[rows, D_A + hd * HEAD:D_A + (hd + 1) * HEAD] = (
                        u[:, hs] * mix * gate[:, hs]).astype(BF16)
                    yield

            _lockstep([head(hd) for hd in range(N_HEADS)]
                      + [mlp(), projection(c)])

        out = xprev_ref[0] + jnp.dot(mixed_ref[...], wout_ref[...], preferred_element_type=F32)
        y_ref[0] = _rms_norm(out, fg_ref[...])

    parity = lax.rem(g, 2)

    @pl.when(parity == 0)
    def _():
        step(proj_a, proj_b)

    @pl.when(parity == 1)
    def _():
        step(proj_b, proj_a)


def _sample_kernel(x_ref, ng_ref, win_ref, lbl_ref, hg_ref, lng_ref, lnb_ref, ws_ref, bst_ref,
                   wout_ref, fg_ref, stin_ref, y_ref, stout_ref, vch_ref,
                   qin_ref, kot_ref, tot_ref, oint_ref):
    x = x_ref[...]
    h = _rms_norm(x, ng_ref[...]).astype(BF16)
    proj = jnp.dot(h, win_ref[...], preferred_element_type=F32)

    def col(k):
        return proj[:, k * D_A:(k + 1) * D_A]

    lb = _lower_bound(lbl_ref[...])
    causal, lvl, bits, n_levels = _chunk_constants(DEC_SEQ)
    q, kk, f = _hgrn_gates(col(0), col(1), lb)
    vi = col(2)
    vi16 = vi.astype(BF16)

    def head(hd):
        hs = slice(hd * HEAD, (hd + 1) * HEAD)
        o, q_in, k_out, total = yield from _hgrn_head(q[:, hs], kk[:, hs], f[:, hs], vi[:, hs], lvl,
                                                      bits, n_levels)
        qin_ref[:, hs] = q_in
        kot_ref[hd] = k_out.T
        tot_ref[hd] = total.T
        return o

    o_intra = _lockstep([head(hd) for hd in range(N_HEADS)])

    lane = lax.broadcasted_iota(jnp.int32, (HEAD, ROWS), 1)

    def seq_body(i, carry):
        r0 = pl.multiple_of(i * DEC_SEQ, DEC_SEQ)
        in_seq = (lane >> (DEC_SEQ.bit_length() - 1)) == i
        first = lane == r0
        for hd in range(N_HEADS):
            hs = slice(hd * HEAD, (hd + 1) * HEAD)
            state = stin_ref[0, i, hd]
            qi = qin_ref[pl.ds(r0, DEC_SEQ), hs].astype(BF16)
            oint_ref[pl.ds(r0, DEC_SEQ), hs] = jnp.dot(
                qi, state.astype(BF16), preferred_element_type=F32)
            k_seq = jnp.where(in_seq, kot_ref[hd], 0.0).astype(BF16)
            decay = jnp.sum(jnp.where(first, tot_ref[hd], 0.0), axis=1, keepdims=True)
            stout_ref[0, i, hd] = decay * state + jnp.dot(
                k_seq, vi16[:, hs], preferred_element_type=F32)
        return carry

    lax.fori_loop(0, SEQS_PER_TILE, seq_body, 0)

    hg = hg_ref[...]
    pza = col(3)
    o_a = []
    for hd in range(N_HEADS):
        hs = slice(hd * HEAD, (hd + 1) * HEAD)
        o_a.append(_hgrn_finish(o_intra[hd] + oint_ref[:, hs], pza[:, hs], hg[:, hs]))

    o_b, v_n = _chunk_mlp(col(4), col(5), col(6), lng_ref[...], lnb_ref[...], ws_ref,
                          bst_ref[...], causal)
    vch_ref[...] = v_n
    mixed = jnp.concatenate(o_a + [o_b], axis=-1)
    out = x + jnp.dot(mixed, wout_ref[...], preferred_element_type=F32)
    y_ref[...] = _rms_norm(out, fg_ref[...])


def _const_spec(shape):
    return pl.BlockSpec(shape, lambda *_: (0,) * len(shape), pipeline_mode=pl.Buffered(1))


def _weight_specs(bst_rows):
    return [
        _const_spec((1, D_MODEL)),
        _const_spec((D_MODEL, D_IN)),
        _const_spec((2, D_A)),
        _const_spec((1, D_A)),
        _const_spec((1, D_B)),
        _const_spec((1, D_B)),
        _const_spec((N_HEADS, ROWS, ROWS)),
        _const_spec((bst_rows, N_HEADS)),
        _const_spec((D_MODEL, D_MODEL)),
        _const_spec((1, D_MODEL)),
    ]


def kernel(x_prompt, x_sample, state_hgrn, norm_g, w_in, lb_logits, hgrn_norm_g, sgu_ln_g,
           sgu_ln_b, w_s, b_s, w_out, final_norm_g):
    depth = norm_g.shape[0]
    assert depth == 1 and lb_logits.shape == (2, D_A)
    batch, seq, _ = x_prompt.shape
    dec_batch, dec_seq, _ = x_sample.shape
    assert seq % PROMPT_TILE == 0 and dec_seq == DEC_SEQ and dec_batch % SEQS_PER_TILE == 0

    weights = (norm_g, w_in[0].astype(BF16), lb_logits, hgrn_norm_g, sgu_ln_g, sgu_ln_b)
    tail = (w_out[0].astype(BF16), final_norm_g.reshape(1, D_MODEL))
    params = pltpu.CompilerParams(dimension_semantics=("arbitrary",),
                                  vmem_limit_bytes=VMEM_LIMIT_BYTES)

    tiles_per_seq = seq // PROMPT_TILE
    n_tiles = batch * tiles_per_seq

    def cur_tile(g):
        t = jnp.minimum(g, n_tiles - 1)
        return t // tiles_per_seq, t % tiles_per_seq

    def prev_tile(g):
        t = jnp.maximum(g - 1, 0)
        return t // tiles_per_seq, t % tiles_per_seq

    y_p, st_p, vch_p = pl.pallas_call(
        functools.partial(_prompt_kernel, tiles_per_seq=tiles_per_seq),
        grid=(n_tiles + 1,),
        in_specs=[pl.BlockSpec((1, PROMPT_TILE, D_MODEL), lambda g: (*cur_tile(g), 0)),
                  pl.BlockSpec((1, PROMPT_TILE, D_MODEL), lambda g: (*prev_tile(g), 0))]
        + _weight_specs(ROWS),
        out_specs=[
            pl.BlockSpec((1, PROMPT_TILE, D_MODEL), lambda g: (*prev_tile(g), 0)),
            pl.BlockSpec((1, 1, N_HEADS, HEAD, HEAD), lambda g: (0, prev_tile(g)[0], 0, 0, 0)),
            pl.BlockSpec((1, 1, ROWS, D_B), lambda g: (0, prev_tile(g)[0], 0, 0)),
        ],
        out_shape=[
            jax.ShapeDtypeStruct((batch, seq, D_MODEL), F32),
            jax.ShapeDtypeStruct((1, batch, N_HEADS, HEAD, HEAD), F32),
            jax.ShapeDtypeStruct((1, batch, ROWS, D_B), F32),
        ],
        scratch_shapes=[pltpu.VMEM((PROMPT_TILE, D_IN), F32),
                        pltpu.VMEM((PROMPT_TILE, D_IN), F32),
                        pltpu.VMEM((PROMPT_TILE, D_MODEL), BF16),
                        pltpu.VMEM((PROMPT_TILE, D_MODEL), BF16)],
        compiler_params=params,
        name="prompt_layer",
    )(x_prompt, x_prompt, *weights, w_s[0], b_s[0].T, *tail)

    ws_dec = jnp.tile(w_s[0, :, :DEC_SEQ, :DEC_SEQ], (1, SEQS_PER_TILE, SEQS_PER_TILE))
    bst_dec = jnp.tile(b_s[0, :, :DEC_SEQ].T, (SEQS_PER_TILE, 1))
    n_rows = dec_batch * DEC_SEQ
    y_s, st_s, vch_s = pl.pallas_call(
        _sample_kernel,
        grid=(n_rows // ROWS,),
        in_specs=[pl.BlockSpec((ROWS, D_MODEL), lambda i: (i, 0))]
        + _weight_specs(ROWS)
        + [pl.BlockSpec((1, SEQS_PER_TILE, N_HEADS, HEAD, HEAD), lambda i: (0, i, 0, 0, 0))],
        out_specs=[
            pl.BlockSpec((ROWS, D_MODEL), lambda i: (i, 0)),
            pl.BlockSpec((1, SEQS_PER_TILE, N_HEADS, HEAD, HEAD), lambda i: (0, i, 0, 0, 0)),
            pl.BlockSpec((ROWS, D_B), lambda i: (i, 0)),
        ],
        out_shape=[
            jax.ShapeDtypeStruct((n_rows, D_MODEL), F32),
            jax.ShapeDtypeStruct((1, dec_batch, N_HEADS, HEAD, HEAD), F32),
            jax.ShapeDtypeStruct((n_rows, D_B), F32),
        ],
        scratch_shapes=[pltpu.VMEM((ROWS, D_A), F32),
                        pltpu.VMEM((N_HEADS, HEAD, ROWS), F32),
                        pltpu.VMEM((N_HEADS, HEAD, ROWS), F32),
                        pltpu.VMEM((ROWS, D_A), F32)],
        compiler_params=pltpu.CompilerParams(dimension_semantics=("arbitrary",),
                                             vmem_limit_bytes=VMEM_LIMIT_BYTES),
        name="decode_layer",
    )(x_sample.reshape(n_rows, D_MODEL), *weights, ws_dec, bst_dec, *tail, state_hgrn)

    return (y_p,
            y_s.reshape(dec_batch, DEC_SEQ, D_MODEL),
            st_p,
            st_s,
            vch_p.reshape(1, batch, ROWS, N_HEADS, HEAD),
            vch_s.reshape(1, dec_batch, DEC_SEQ, N_HEADS, HEAD))
```

```python
import functools

import jax
import jax.numpy as jnp
from jax import lax
from jax.experimental import pallas as pl
from jax.experimental.pallas import tpu as pltpu

F32 = jnp.float32
BF16 = jnp.bfloat16

D_MODEL = 1024
D_A = 512
D_B = 512
HEAD = 128
N_HEADS = 4
D_IN = 4 * D_A + 3 * D_B
ROWS = 128
SUBLANES = 8
N_TILES = ROWS // SUBLANES
SUBLANE_LEVELS = 3
EPS = 1e-6
PROMPT_TILE = 256
PROJ_BLOCK = 256
HEAD_ROUNDS = 12
DEC_SEQ = 8
SEQS_PER_TILE = ROWS // DEC_SEQ
SEQ_GROUP = 4
VMEM_LIMIT_BYTES = 56 * 1024 * 1024

_NT = (((1,), (1,)), ((), ()))


def _silu(x):
    return x * jax.nn.sigmoid(x)


def _rms_norm(x, g):
    return x * lax.rsqrt(jnp.mean(x * x, axis=-1, keepdims=True) + EPS) * g


def _layer_norm(x, g, b):
    mu = jnp.mean(x, axis=-1, keepdims=True)
    xc = x - mu
    var = jnp.mean(xc * xc, axis=-1, keepdims=True)
    return xc * lax.rsqrt(var + EPS) * g + b


def _lower_bound(lbl):
    rows = [lbl[r:r + 1, :] for r in range(lbl.shape[0])]
    mx = functools.reduce(jnp.maximum, rows)
    es = [jnp.exp(r - mx) for r in rows]
    return es[0] / functools.reduce(lambda a, c: a + c, es)


def _chunk_constants(seg_len):
    t = lax.broadcasted_iota(jnp.int32, (ROWS, ROWS), 0)
    s = lax.broadcasted_iota(jnp.int32, (ROWS, ROWS), 1)
    n_levels = seg_len.bit_length() - 1
    assert seg_len == 1 << n_levels and n_levels >= SUBLANE_LEVELS
    same_seq = (t >> n_levels) == (s >> n_levels)
    causal = (t >= s) & same_seq
    x = t ^ s
    lvl = jnp.full((ROWS, ROWS), -1, jnp.int32)
    for j in range(n_levels):
        lvl = jnp.where(((x >> j) == 1) & (((t >> j) & 1) == 1), j, lvl)
    rowi = lax.broadcasted_iota(jnp.int32, (ROWS, HEAD), 0)
    bits = [((rowi >> j) & 1) == 1 for j in range(SUBLANE_LEVELS)]
    return causal, lvl, bits, n_levels


def _sibling(x, m, bit):
    x3 = x.reshape(N_TILES, SUBLANES, HEAD)
    up = pltpu.roll(x3, m, 1)
    if 2 * m == SUBLANES:
        return up.reshape(ROWS, HEAD)
    down = pltpu.roll(x3, SUBLANES - m, 1)
    return jnp.where(bit, up.reshape(ROWS, HEAD), down.reshape(ROWS, HEAD))


def _tiles(x):
    return [x[i * SUBLANES:(i + 1) * SUBLANES, :] for i in range(N_TILES)]


def _hgrn_head(q, kk, f, vi, lvl, bits, n_levels, state=None):
    one = jnp.ones((ROWS, HEAD), F32)
    pre, suf, blk = f, one, f
    sc = jnp.zeros((ROWS, ROWS), F32)
    for j in range(SUBLANE_LEVELS):
        z = jnp.where(bits[j], q * pre, kk * suf).astype(BF16)
        p = lax.dot_general(z, z, _NT, preferred_element_type=F32)
        sc = jnp.where(lvl == j, p, sc)
        sib = _sibling(blk, 1 << j, bits[j])
        pre = pre * jnp.where(bits[j], sib, one)
        suf = suf * jnp.where(bits[j], one, sib)
        blk = blk * sib
        yield

    q_t, k_t, pre_t, suf_t = _tiles(q), _tiles(kk), _tiles(pre), _tiles(suf)
    blk_b = _tiles(blk)
    lvl_t = _tiles(lvl)
    sc_t = _tiles(sc)
    for j in range(SUBLANE_LEVELS, n_levels):
        span = 1 << (j - SUBLANE_LEVELS)
        upper = [(i // span) & 1 == 1 for i in range(N_TILES)]
        z = jnp.concatenate(
            [q_t[i] * pre_t[i] if upper[i] else k_t[i] * suf_t[i] for i in range(N_TILES)],
            axis=0).astype(BF16)
        p = lax.dot_general(z, z, _NT, preferred_element_type=F32)
        for i in range(N_TILES):
            sib = blk_b[(i // span) ^ 1]
            if upper[i]:
                sc_t[i] = jnp.where(lvl_t[i] == j, p[i * SUBLANES:(i + 1) * SUBLANES, :], sc_t[i])
                pre_t[i] = pre_t[i] * sib
            else:
                suf_t[i] = suf_t[i] * sib
        blk_b = [blk_b[2 * n] * blk_b[2 * n + 1] for n in range(len(blk_b) // 2)]
        yield

    sc = jnp.concatenate(sc_t, axis=0)
    diag = jnp.sum(q * kk, axis=-1, keepdims=True)
    q_in = jnp.concatenate([q_t[i] * pre_t[i] for i in range(N_TILES)], axis=0)
    k_out = jnp.concatenate([k_t[i] * suf_t[i] for i in range(N_TILES)], axis=0)
    if state is None:
        o = jnp.dot(sc.astype(BF16), vi.astype(BF16), preferred_element_type=F32) + diag * vi
    else:
        lhs = jnp.concatenate([sc.astype(BF16), q_in.astype(BF16)], axis=1)
        rhs = jnp.concatenate([vi.astype(BF16), state.astype(BF16)], axis=0)
        o = jnp.dot(lhs, rhs, preferred_element_type=F32) + diag * vi
    per_tile = N_TILES // len(blk_b)
    total = jnp.concatenate([blk_b[i // per_tile] for i in range(N_TILES)], axis=0)
    return o, q_in, k_out, total


def _lockstep(generators, starts=None):
    starts = starts or [0] * len(generators)
    results = [None] * len(generators)
    waiting = sorted(range(len(generators)), key=lambda i: starts[i])
    active = []
    rnd = 0
    while waiting or active:
        while waiting and starts[waiting[0]] <= rnd:
            active.append(waiting.pop(0))
        still = []
        for idx in active:
            try:
                next(generators[idx])
                still.append(idx)
            except StopIteration as done:
                results[idx] = done.value
        active = still
        rnd += 1
    return results


def _hgrn_gates(pq, pf, lb):
    q = _silu(pq)
    forget = lb + (1.0 - lb) * jax.nn.sigmoid(pf)
    return q, 1.0 - forget, forget


def _hgrn_finish(o, pz, g):
    return (_rms_norm(o, g) * _silu(pz)).astype(BF16)


def _chunk_mlp(pu, pv, pzb, lng, lnb, ws_ref, bst, causal):
    u = jax.nn.gelu(pu)
    v_n = _layer_norm(jax.nn.gelu(pv), lng, lnb)
    gate = _silu(pzb)
    outs = []
    for h in range(N_HEADS):
        hs = slice(h * HEAD, (h + 1) * HEAD)
        w = jnp.where(causal, ws_ref[h], 0.0).astype(BF16)
        mix = jnp.dot(w, v_n[:, hs].astype(BF16), preferred_element_type=F32) + bst[:, h:h + 1]
        outs.append((u[:, hs] * mix * gate[:, hs]).astype(BF16))
    return jnp.concatenate(outs, axis=-1), v_n


def _prompt_kernel(x_ref, xprev_ref, ng_ref, win_ref, lbl_ref, hg_ref, lng_ref, lnb_ref, ws_ref,
                   bst_ref, wout_ref, fg_ref, y_ref, st_ref, vch_ref, proj_a, proj_b, h_ref,
                   mixed_ref, *, tiles_per_seq):
    g = pl.program_id(0)

    @pl.when(g == 0)
    def _():
        proj_b[...] = jnp.zeros_like(proj_b)

    @pl.when(lax.rem(jnp.maximum(g - 1, 0), tiles_per_seq) == 0)
    def _():
        st_ref[...] = jnp.zeros_like(st_ref)

    def step(proj_next, proj_prev):
        h_ref[...] = _rms_norm(x_ref[0], ng_ref[...]).astype(BF16)
        lb = _lower_bound(lbl_ref[...])
        causal, lvl, bits, n_levels = _chunk_constants(ROWS)
        hg = hg_ref[...]
        lng = lng_ref[...]
        lnb = lnb_ref[...]
        bst = bst_ref[...]
        n_chunks = PROMPT_TILE // ROWS
        n_blocks = D_IN // PROJ_BLOCK

        def col(c, k, hs=slice(0, D_A)):
            return proj_prev[c * ROWS:(c + 1) * ROWS, k * D_A + hs.start:k * D_A + hs.stop]

        def head(c, hd):
            hs = slice(hd * HEAD, (hd + 1) * HEAD)
            q, kk, f = _hgrn_gates(col(c, 0, hs), col(c, 1, hs), lb[:, hs])
            vh = col(c, 2, hs)
            yield
            state = st_ref[0, 0, hd]
            o, _, k_out, total = yield from _hgrn_head(q, kk, f, vh, lvl, bits, n_levels, state)
            yield
            st_ref[0, 0, hd] = total.T * state + jnp.dot(
                k_out.T.astype(BF16), vh.astype(BF16), preferred_element_type=F32)
            yield
            mixed_ref[c * ROWS:(c + 1) * ROWS, hs] = _hgrn_finish(o, col(c, 3, hs), hg[:, hs])

        def mlp(c):
            u = jax.nn.gelu(col(c, 4))
            yield
            v_n = _layer_norm(jax.nn.gelu(col(c, 5)), lng, lnb)
            if c == n_chunks - 1:
                vch_ref[0, 0] = v_n
            yield
            gate = _silu(col(c, 6))
            yield
            for hd in range(N_HEADS):
                hs = slice(hd * HEAD, (hd + 1) * HEAD)
                w = jnp.where(causal, ws_ref[hd], 0.0).astype(BF16)
                mix = jnp.dot(w, v_n[:, hs].astype(BF16),
                              preferred_element_type=F32) + bst[:, hd:hd + 1]
                mixed_ref[c * ROWS:(c + 1) * ROWS, D_A + hd * HEAD:D_A + (hd + 1) * HEAD] = (
                    u[:, hs] * mix * gate[:, hs]).astype(BF16)
                yield

        def projection(rounds_per_block):
            for blk in range(n_blocks):
                cols = slice(blk * PROJ_BLOCK, (blk + 1) * PROJ_BLOCK)
                proj_next[:, cols] = jnp.dot(h_ref[...], win_ref[:, cols],
                                             preferred_element_type=F32)
                for _ in range(rounds_per_block):
                    yield

        items = [gen for c in range(n_chunks)
                 for gen in [head(c, hd) for hd in range(N_HEADS)] + [mlp(c)]]
        starts = [c * HEAD_ROUNDS for c in range(n_chunks) for _ in range(N_HEADS + 1)]
        rounds = n_chunks * HEAD_ROUNDS
        _lockstep(items + [projection(max(1, rounds // n_blocks))], starts + [0])

        out = xprev_ref[0] + jnp.dot(mixed_ref[...], wout_ref[...], preferred_element_type=F32)
        y_ref[0] = _rms_norm(out, fg_ref[...])

    parity = lax.rem(g, 2)

    @pl.when(parity == 0)
    def _():
        step(proj_a, proj_b)

    @pl.when(parity == 1)
    def _():
        step(proj_b, proj_a)


def _sample_kernel(x_ref, ng_ref, win_ref, lbl_ref, hg_ref, lng_ref, lnb_ref, ws_ref, bst_ref,
                   wout_ref, fg_ref, stin_ref, y_ref, stout_ref, vch_ref,
                   qin_ref, kot_ref, tot_ref, oint_ref):
    x = x_ref[...]
    h = _rms_norm(x, ng_ref[...]).astype(BF16)
    proj = jnp.dot(h, win_ref[...], preferred_element_type=F32)

    def col(k):
        return proj[:, k * D_A:(k + 1) * D_A]

    lb = _lower_bound(lbl_ref[...])
    causal, lvl, bits, n_levels = _chunk_constants(DEC_SEQ)
    q, kk, f = _hgrn_gates(col(0), col(1), lb)
    vi = col(2)
    vi16 = vi.astype(BF16)

    def head(hd):
        hs = slice(hd * HEAD, (hd + 1) * HEAD)
        o, q_in, k_out, total = yield from _hgrn_head(q[:, hs], kk[:, hs], f[:, hs], vi[:, hs], lvl,
                                                      bits, n_levels)
        qin_ref[:, hs] = q_in
        kot_ref[hd] = k_out.T
        tot_ref[hd] = total.T
        return o

    o_intra = _lockstep([head(hd) for hd in range(N_HEADS)])

    lane = lax.broadcasted_iota(jnp.int32, (HEAD, ROWS), 1)

    def seq_head(i, hd):
        hs = slice(hd * HEAD, (hd + 1) * HEAD)
        seq_rows = slice(i * DEC_SEQ, (i + 1) * DEC_SEQ)
        state = stin_ref[0, i, hd]
        qi = qin_ref[seq_rows, hs].astype(BF16)
        oint_ref[seq_rows, hs] = jnp.dot(qi, state.astype(BF16), preferred_element_type=F32)
        yield
        in_seq = (lane >> (DEC_SEQ.bit_length() - 1)) == i
        k_seq = jnp.where(in_seq, kot_ref[hd], 0.0).astype(BF16)
        decay = tot_ref[hd, :, i * DEC_SEQ:i * DEC_SEQ + 1]
        yield
        stout_ref[0, i, hd] = decay * state + jnp.dot(
            k_seq, vi16[:, hs], preferred_element_type=F32)

    for grp in range(SEQS_PER_TILE // SEQ_GROUP):
        _lockstep([seq_head(grp * SEQ_GROUP + s, hd)
                   for s in range(SEQ_GROUP) for hd in range(N_HEADS)])

    hg = hg_ref[...]
    pza = col(3)
    o_a = []
    for hd in range(N_HEADS):
        hs = slice(hd * HEAD, (hd + 1) * HEAD)
        o_a.append(_hgrn_finish(o_intra[hd] + oint_ref[:, hs], pza[:, hs], hg[:, hs]))

    o_b, v_n = _chunk_mlp(col(4), col(5), col(6), lng_ref[...], lnb_ref[...], ws_ref,
                          bst_ref[...], causal)
    vch_ref[...] = v_n
    mixed = jnp.concatenate(o_a + [o_b], axis=-1)
    out = x + jnp.dot(mixed, wout_ref[...], preferred_element_type=F32)
    y_ref[...] = _rms_norm(out, fg_ref[...])


def _const_spec(shape):
    return pl.BlockSpec(shape, lambda *_: (0,) * len(shape), pipeline_mode=pl.Buffered(1))


def _weight_specs(bst_rows):
    return [
        _const_spec((1, D_MODEL)),
        _const_spec((D_MODEL, D_IN)),
        _const_spec((2, D_A)),
        _const_spec((1, D_A)),
        _const_spec((1, D_B)),
        _const_spec((1, D_B)),
        _const_spec((N_HEADS, ROWS, ROWS)),
        _const_spec((bst_rows, N_HEADS)),
        _const_spec((D_MODEL, D_MODEL)),
        _const_spec((1, D_MODEL)),
    ]


def kernel(x_prompt, x_sample, state_hgrn, norm_g, w_in, lb_logits, hgrn_norm_g, sgu_ln_g,
           sgu_ln_b, w_s, b_s, w_out, final_norm_g):
    depth = norm_g.shape[0]
    assert depth == 1 and lb_logits.shape == (2, D_A)
    batch, seq, _ = x_prompt.shape
    dec_batch, dec_seq, _ = x_sample.shape
    assert seq % PROMPT_TILE == 0 and dec_seq == DEC_SEQ and dec_batch % SEQS_PER_TILE == 0

    weights = (norm_g, w_in[0].astype(BF16), lb_logits, hgrn_norm_g, sgu_ln_g, sgu_ln_b)
    tail = (w_out[0].astype(BF16), final_norm_g.reshape(1, D_MODEL))
    params = pltpu.CompilerParams(dimension_semantics=("arbitrary",),
                                  vmem_limit_bytes=VMEM_LIMIT_BYTES)

    tiles_per_seq = seq // PROMPT_TILE
    n_tiles = batch * tiles_per_seq

    def cur_tile(g):
        t = jnp.minimum(g, n_tiles - 1)
        return t // tiles_per_seq, t % tiles_per_seq

    def prev_tile(g):
        t = jnp.maximum(g - 1, 0)
        return t // tiles_per_seq, t % tiles_per_seq

    y_p, st_p, vch_p = pl.pallas_call(
        functools.partial(_prompt_kernel, tiles_per_seq=tiles_per_seq),
        grid=(n_tiles + 1,),
        in_specs=[pl.BlockSpec((1, PROMPT_TILE, D_MODEL), lambda g: (*cur_tile(g), 0)),
                  pl.BlockSpec((1, PROMPT_TILE, D_MODEL), lambda g: (*prev_tile(g), 0))]
        + _weight_specs(ROWS),
        out_specs=[
            pl.BlockSpec((1, PROMPT_TILE, D_MODEL), lambda g: (*prev_tile(g), 0)),
            pl.BlockSpec((1, 1, N_HEADS, HEAD, HEAD), lambda g: (0, prev_tile(g)[0], 0, 0, 0)),
            pl.BlockSpec((1, 1, ROWS, D_B), lambda g: (0, prev_tile(g)[0], 0, 0)),
        ],
        out_shape=[
            jax.ShapeDtypeStruct((batch, seq, D_MODEL), F32),
            jax.ShapeDtypeStruct((1, batch, N_HEADS, HEAD, HEAD), F32),
            jax.ShapeDtypeStruct((1, batch, ROWS, D_B), F32),
        ],
        scratch_shapes=[pltpu.VMEM((PROMPT_TILE, D_IN), F32),
                        pltpu.VMEM((PROMPT_TILE, D_IN), F32),
                        pltpu.VMEM((PROMPT_TILE, D_MODEL), BF16),
                        pltpu.VMEM((PROMPT_TILE, D_MODEL), BF16)],
        compiler_params=params,
        name="prompt_layer",
    )(x_prompt, x_prompt, *weights, w_s[0], b_s[0].T, *tail)

    ws_dec = jnp.tile(w_s[0, :, :DEC_SEQ, :DEC_SEQ], (1, SEQS_PER_TILE, SEQS_PER_TILE))
    bst_dec = jnp.tile(b_s[0, :, :DEC_SEQ].T, (SEQS_PER_TILE, 1))
    n_rows = dec_batch * DEC_SEQ
    y_s, st_s, vch_s = pl.pallas_call(
        _sample_kernel,
        grid=(n_rows // ROWS,),
        in_specs=[pl.BlockSpec((ROWS, D_MODEL), lambda i: (i, 0))]
        + _weight_specs(ROWS)
        + [pl.BlockSpec((1, SEQS_PER_TILE, N_HEADS, HEAD, HEAD), lambda i: (0, i, 0, 0, 0))],
        out_specs=[
            pl.BlockSpec((ROWS, D_MODEL), lambda i: (i, 0)),
            pl.BlockSpec((1, SEQS_PER_TILE, N_HEADS, HEAD, HEAD), lambda i: (0, i, 0, 0, 0)),
            pl.BlockSpec((ROWS, D_B), lambda i: (i, 0)),
        ],
        out_shape=[
            jax.ShapeDtypeStruct((n_rows, D_MODEL), F32),
            jax.ShapeDtypeStruct((1, dec_batch, N_HEADS, HEAD, HEAD), F32),
            jax.ShapeDtypeStruct((n_rows, D_B), F32),
        ],
        scratch_shapes=[pltpu.VMEM((ROWS, D_A), F32),
                        pltpu.VMEM((N_HEADS, HEAD, ROWS), F32),
                        pltpu.VMEM((N_HEADS, HEAD, ROWS), F32),
                        pltpu.VMEM((ROWS, D_A), F32)],
        compiler_params=pltpu.CompilerParams(dimension_semantics=("arbitrary",),
                                             vmem_limit_bytes=VMEM_LIMIT_BYTES),
        name="decode_layer",
    )(x_sample.reshape(n_rows, D_MODEL), *weights, ws_dec, bst_dec, *tail, state_hgrn)

    return (y_p,
            y_s.reshape(dec_batch, DEC_SEQ, D_MODEL),
            st_p,
            st_s,
            vch_p.reshape(1, batch, ROWS, N_HEADS, HEAD),
            vch_s.reshape(1, dec_batch, DEC_SEQ, N_HEADS, HEAD))
```

```python
import functools

import jax
import jax.numpy as jnp
from jax import lax
from jax.experimental import pallas as pl
from jax.experimental.pallas import tpu as pltpu

F32 = jnp.float32
BF16 = jnp.bfloat16

D_MODEL = 1024
D_A = 512
D_B = 512
HEAD = 128
N_HEADS = 4
D_IN = 4 * D_A + 3 * D_B
ROWS = 128
SUBLANES = 8
N_TILES = ROWS // SUBLANES
SUBLANE_LEVELS = 3
EPS = 1e-6
PROMPT_TILE = 512
PROJ_BLOCK = 256
CHUNK_GROUP = 2
HEAD_ROUNDS = 12
DEC_SEQ = 8
SEQS_PER_TILE = ROWS // DEC_SEQ
SEQ_GROUP = 4
VMEM_LIMIT_BYTES = 56 * 1024 * 1024

_NT = (((1,), (1,)), ((), ()))


def _silu(x):
    return x * jax.nn.sigmoid(x)


def _rms_norm(x, g):
    return x * lax.rsqrt(jnp.mean(x * x, axis=-1, keepdims=True) + EPS) * g


def _layer_norm(x, g, b):
    mu = jnp.mean(x, axis=-1, keepdims=True)
    xc = x - mu
    var = jnp.mean(xc * xc, axis=-1, keepdims=True)
    return xc * lax.rsqrt(var + EPS) * g + b


def _lower_bound(lbl):
    rows = [lbl[r:r + 1, :] for r in range(lbl.shape[0])]
    mx = functools.reduce(jnp.maximum, rows)
    es = [jnp.exp(r - mx) for r in rows]
    return es[0] / functools.reduce(lambda a, c: a + c, es)


def _chunk_constants(seg_len):
    t = lax.broadcasted_iota(jnp.int32, (ROWS, ROWS), 0)
    s = lax.broadcasted_iota(jnp.int32, (ROWS, ROWS), 1)
    n_levels = seg_len.bit_length() - 1
    assert seg_len == 1 << n_levels and n_levels >= SUBLANE_LEVELS
    same_seq = (t >> n_levels) == (s >> n_levels)
    causal = (t >= s) & same_seq
    x = t ^ s
    lvl = jnp.full((ROWS, ROWS), -1, jnp.int32)
    for j in range(n_levels):
        lvl = jnp.where(((x >> j) == 1) & (((t >> j) & 1) == 1), j, lvl)
    rowi = lax.broadcasted_iota(jnp.int32, (ROWS, HEAD), 0)
    bits = [((rowi >> j) & 1) == 1 for j in range(SUBLANE_LEVELS)]
    return causal, lvl, bits, n_levels


def _sibling(x, m, bit):
    x3 = x.reshape(N_TILES, SUBLANES, HEAD)
    up = pltpu.roll(x3, m, 1)
    if 2 * m == SUBLANES:
        return up.reshape(ROWS, HEAD)
    down = pltpu.roll(x3, SUBLANES - m, 1)
    return jnp.where(bit, up.reshape(ROWS, HEAD), down.reshape(ROWS, HEAD))


def _tiles(x):
    return [x[i * SUBLANES:(i + 1) * SUBLANES, :] for i in range(N_TILES)]


def _hgrn_head(q, kk, f, vi, lvl, bits, n_levels, state=None):
    one = jnp.ones((ROWS, HEAD), F32)
    pre, suf, blk = f, one, f
    sc = jnp.zeros((ROWS, ROWS), F32)
    for j in range(SUBLANE_LEVELS):
        z = jnp.where(bits[j], q * pre, kk * suf).astype(BF16)
        p = lax.dot_general(z, z, _NT, preferred_element_type=F32)
        sc = jnp.where(lvl == j, p, sc)
        sib = _sibling(blk, 1 << j, bits[j])
        pre = pre * jnp.where(bits[j], sib, one)
        suf = suf * jnp.where(bits[j], one, sib)
        blk = blk * sib
        yield

    q_t, k_t, pre_t, suf_t = _tiles(q), _tiles(kk), _tiles(pre), _tiles(suf)
    blk_b = _tiles(blk)
    lvl_t = _tiles(lvl)
    sc_t = _tiles(sc)
    for j in range(SUBLANE_LEVELS, n_levels):
        span = 1 << (j - SUBLANE_LEVELS)
        upper = [(i // span) & 1 == 1 for i in range(N_TILES)]
        z = jnp.concatenate(
            [q_t[i] * pre_t[i] if upper[i] else k_t[i] * suf_t[i] for i in range(N_TILES)],
            axis=0).astype(BF16)
        p = lax.dot_general(z, z, _NT, preferred_element_type=F32)
        for i in range(N_TILES):
            sib = blk_b[(i // span) ^ 1]
            if upper[i]:
                sc_t[i] = jnp.where(lvl_t[i] == j, p[i * SUBLANES:(i + 1) * SUBLANES, :], sc_t[i])
                pre_t[i] = pre_t[i] * sib
            else:
                suf_t[i] = suf_t[i] * sib
        blk_b = [blk_b[2 * n] * blk_b[2 * n + 1] for n in range(len(blk_b) // 2)]
        yield

    sc = jnp.concatenate(sc_t, axis=0)
    diag = jnp.sum(q * kk, axis=-1, keepdims=True)
    q_in = jnp.concatenate([q_t[i] * pre_t[i] for i in range(N_TILES)], axis=0)
    k_out = jnp.concatenate([k_t[i] * suf_t[i] for i in range(N_TILES)], axis=0)
    if state is None:
        o = jnp.dot(sc.astype(BF16), vi.astype(BF16), preferred_element_type=F32) + diag * vi
    else:
        lhs = jnp.concatenate([sc.astype(BF16), q_in.astype(BF16)], axis=1)
        rhs = jnp.concatenate([vi.astype(BF16), state.astype(BF16)], axis=0)
        o = jnp.dot(lhs, rhs, preferred_element_type=F32) + diag * vi
    per_tile = N_TILES // len(blk_b)
    total = jnp.concatenate([blk_b[i // per_tile] for i in range(N_TILES)], axis=0)
    return o, q_in, k_out, total


def _lockstep(generators, starts=None):
    starts = starts or [0] * len(generators)
    results = [None] * len(generators)
    waiting = sorted(range(len(generators)), key=lambda i: starts[i])
    active = []
    rnd = 0
    while waiting or active:
        while waiting and starts[waiting[0]] <= rnd:
            active.append(waiting.pop(0))
        still = []
        for idx in active:
            try:
                next(generators[idx])
                still.append(idx)
            except StopIteration as done:
                results[idx] = done.value
        active = still
        rnd += 1
    return results


def _hgrn_gates(pq, pf, lb):
    q = _silu(pq)
    forget = lb + (1.0 - lb) * jax.nn.sigmoid(pf)
    return q, 1.0 - forget, forget


def _hgrn_finish(o, pz, g):
    return (_rms_norm(o, g) * _silu(pz)).astype(BF16)


def _chunk_mlp(pu, pv, pzb, lng, lnb, ws_ref, bst, causal):
    u = jax.nn.gelu(pu)
    v_n = _layer_norm(jax.nn.gelu(pv), lng, lnb)
    gate = _silu(pzb)
    outs = []
    for h in range(N_HEADS):
        hs = slice(h * HEAD, (h + 1) * HEAD)
        w = jnp.where(causal, ws_ref[h], 0.0).astype(BF16)
        mix = jnp.dot(w, v_n[:, hs].astype(BF16), preferred_element_type=F32) + bst[:, h:h + 1]
        outs.append((u[:, hs] * mix * gate[:, hs]).astype(BF16))
    return jnp.concatenate(outs, axis=-1), v_n


def _prompt_kernel(x_ref, xprev_ref, ng_ref, win_ref, lbl_ref, hg_ref, lng_ref, lnb_ref, ws_ref,
                   bst_ref, wout_ref, fg_ref, y_ref, st_ref, vch_ref, proj_a, proj_b, h_ref,
                   mixed_ref, *, tiles_per_seq):
    g = pl.program_id(0)

    @pl.when(g == 0)
    def _():
        proj_b[...] = jnp.zeros_like(proj_b)

    @pl.when(lax.rem(jnp.maximum(g - 1, 0), tiles_per_seq) == 0)
    def _():
        st_ref[...] = jnp.zeros_like(st_ref)

    def step(proj_next, proj_prev):
        h_ref[...] = _rms_norm(x_ref[0], ng_ref[...]).astype(BF16)
        lb = _lower_bound(lbl_ref[...])
        causal, lvl, bits, n_levels = _chunk_constants(ROWS)
        hg = hg_ref[...]
        lng = lng_ref[...]
        lnb = lnb_ref[...]
        bst = bst_ref[...]
        n_groups = PROMPT_TILE // ROWS // CHUNK_GROUP
        blocks_per_group = D_IN // PROJ_BLOCK // n_groups

        def col(rows, k, hs=slice(0, D_A)):
            lo = k * D_A + hs.start
            parts = [proj_prev[blk, rows, max(lo, blk * PROJ_BLOCK) - blk * PROJ_BLOCK:
                               min(lo + hs.stop - hs.start, (blk + 1) * PROJ_BLOCK)
                               - blk * PROJ_BLOCK]
                     for blk in range(lo // PROJ_BLOCK,
                                      (lo + hs.stop - hs.start - 1) // PROJ_BLOCK + 1)]
            return parts[0] if len(parts) == 1 else jnp.concatenate(parts, axis=1)

        def head(rows, hd):
            hs = slice(hd * HEAD, (hd + 1) * HEAD)
            q, kk, f = _hgrn_gates(col(rows, 0, hs), col(rows, 1, hs), lb[:, hs])
            vh = col(rows, 2, hs)
            yield
            state = st_ref[0, 0, hd]
            o, _, k_out, total = yield from _hgrn_head(q, kk, f, vh, lvl, bits, n_levels, state)
            yield
            st_ref[0, 0, hd] = total.T * state + jnp.dot(
                k_out.T.astype(BF16), vh.astype(BF16), preferred_element_type=F32)
            yield
            mixed_ref[rows, hs] = _hgrn_finish(o, col(rows, 3, hs), hg[:, hs])

        def mlp(rows):
            u = jax.nn.gelu(col(rows, 4))
            yield
            v_n = _layer_norm(jax.nn.gelu(col(rows, 5)), lng, lnb)
            vch_ref[0, 0] = v_n
            yield
            gate = _silu(col(rows, 6))
            yield
            for hd in range(N_HEADS):
                hs = slice(hd * HEAD, (hd + 1) * HEAD)
                w = jnp.where(causal, ws_ref[hd], 0.0).astype(BF16)
                mix = jnp.dot(w, v_n[:, hs].astype(BF16),
                              preferred_element_type=F32) + bst[:, hd:hd + 1]
                mixed_ref[rows, D_A + hd * HEAD:D_A + (hd + 1) * HEAD] = (
                    u[:, hs] * mix * gate[:, hs]).astype(BF16)
                yield

        def projection(first_block, rounds_per_block):
            for k in range(blocks_per_group):
                proj_next[first_block + k] = jnp.dot(h_ref[...], win_ref[first_block + k],
                                                     preferred_element_type=F32)
                for _ in range(rounds_per_block):
                    yield

        def group_body(i, carry):
            items, starts = [], []
            for sub in range(CHUNK_GROUP):
                rows = pl.ds(pl.multiple_of((i * CHUNK_GROUP + sub) * ROWS, ROWS), ROWS)
                items += [head(rows, hd) for hd in range(N_HEADS)] + [mlp(rows)]
                starts += [sub * HEAD_ROUNDS] * (N_HEADS + 1)
            spread = max(1, CHUNK_GROUP * HEAD_ROUNDS // blocks_per_group)
            _lockstep(items + [projection(i * blocks_per_group, spread)], starts + [0])
            return carry

        lax.fori_loop(0, n_groups, group_body, 0)

        out = xprev_ref[0] + jnp.dot(mixed_ref[...], wout_ref[...], preferred_element_type=F32)
        y_ref[0] = _rms_norm(out, fg_ref[...])

    parity = lax.rem(g, 2)

    @pl.when(parity == 0)
    def _():
        step(proj_a, proj_b)

    @pl.when(parity == 1)
    def _():
        step(proj_b, proj_a)


def _sample_kernel(x_ref, ng_ref, win_ref, lbl_ref, hg_ref, lng_ref, lnb_ref, ws_ref, bst_ref,
                   wout_ref, fg_ref, stin_ref, y_ref, stout_ref, vch_ref,
                   qin_ref, kot_ref, tot_ref, oint_ref):
    x = x_ref[...]
    h = _rms_norm(x, ng_ref[...]).astype(BF16)
    proj = jnp.concatenate([jnp.dot(h, win_ref[blk], preferred_element_type=F32)
                            for blk in range(D_IN // PROJ_BLOCK)], axis=1)

    def col(k):
        return proj[:, k * D_A:(k + 1) * D_A]

    lb = _lower_bound(lbl_ref[...])
    causal, lvl, bits, n_levels = _chunk_constants(DEC_SEQ)
    q, kk, f = _hgrn_gates(col(0), col(1), lb)
    vi = col(2)
    vi16 = vi.astype(BF16)

    def head(hd):
        hs = slice(hd * HEAD, (hd + 1) * HEAD)
        o, q_in, k_out, total = yield from _hgrn_head(q[:, hs], kk[:, hs], f[:, hs], vi[:, hs], lvl,
                                                      bits, n_levels)
        qin_ref[:, hs] = q_in
        kot_ref[hd] = k_out.T
        tot_ref[hd] = total.T
        return o

    o_intra = _lockstep([head(hd) for hd in range(N_HEADS)])

    lane = lax.broadcasted_iota(jnp.int32, (HEAD, ROWS), 1)

    def seq_head(i, hd):
        hs = slice(hd * HEAD, (hd + 1) * HEAD)
        seq_rows = slice(i * DEC_SEQ, (i + 1) * DEC_SEQ)
        state = stin_ref[0, i, hd]
        qi = qin_ref[seq_rows, hs].astype(BF16)
        oint_ref[seq_rows, hs] = jnp.dot(qi, state.astype(BF16), preferred_element_type=F32)
        yield
        in_seq = (lane >> (DEC_SEQ.bit_length() - 1)) == i
        k_seq = jnp.where(in_seq, kot_ref[hd], 0.0).astype(BF16)
        decay = tot_ref[hd, :, i * DEC_SEQ:i * DEC_SEQ + 1]
        yield
        stout_ref[0, i, hd] = decay * state + jnp.dot(
            k_seq, vi16[:, hs], preferred_element_type=F32)

    for grp in range(SEQS_PER_TILE // SEQ_GROUP):
        _lockstep([seq_head(grp * SEQ_GROUP + s, hd)
                   for s in range(SEQ_GROUP) for hd in range(N_HEADS)])

    hg = hg_ref[...]
    pza = col(3)
    o_a = []
    for hd in range(N_HEADS):
        hs = slice(hd * HEAD, (hd + 1) * HEAD)
        o_a.append(_hgrn_finish(o_intra[hd] + oint_ref[:, hs], pza[:, hs], hg[:, hs]))

    o_b, v_n = _chunk_mlp(col(4), col(5), col(6), lng_ref[...], lnb_ref[...], ws_ref,
                          bst_ref[...], causal)
    vch_ref[...] = v_n
    mixed = jnp.concatenate(o_a + [o_b], axis=-1)
    out = x + jnp.dot(mixed, wout_ref[...], preferred_element_type=F32)
    y_ref[...] = _rms_norm(out, fg_ref[...])


def _const_spec(shape):
    return pl.BlockSpec(shape, lambda *_: (0,) * len(shape), pipeline_mode=pl.Buffered(1))


def _weight_specs(bst_rows):
    return [
        _const_spec((1, D_MODEL)),
        _const_spec((D_IN // PROJ_BLOCK, D_MODEL, PROJ_BLOCK)),
        _const_spec((2, D_A)),
        _const_spec((1, D_A)),
        _const_spec((1, D_B)),
        _const_spec((1, D_B)),
        _const_spec((N_HEADS, ROWS, ROWS)),
        _const_spec((bst_rows, N_HEADS)),
        _const_spec((D_MODEL, D_MODEL)),
        _const_spec((1, D_MODEL)),
    ]


def kernel(x_prompt, x_sample, state_hgrn, norm_g, w_in, lb_logits, hgrn_norm_g, sgu_ln_g,
           sgu_ln_b, w_s, b_s, w_out, final_norm_g):
    depth = norm_g.shape[0]
    assert depth == 1 and lb_logits.shape == (2, D_A)
    batch, seq, _ = x_prompt.shape
    dec_batch, dec_seq, _ = x_sample.shape
    assert seq % PROMPT_TILE == 0 and dec_seq == DEC_SEQ and dec_batch % SEQS_PER_TILE == 0

    w_in_blocks = w_in[0].astype(BF16).reshape(D_MODEL, D_IN // PROJ_BLOCK, PROJ_BLOCK)
    weights = (norm_g, w_in_blocks.transpose(1, 0, 2), lb_logits, hgrn_norm_g, sgu_ln_g, sgu_ln_b)
    tail = (w_out[0].astype(BF16), final_norm_g.reshape(1, D_MODEL))
    params = pltpu.CompilerParams(dimension_semantics=("arbitrary",),
                                  vmem_limit_bytes=VMEM_LIMIT_BYTES)

    tiles_per_seq = seq // PROMPT_TILE
    n_tiles = batch * tiles_per_seq

    def cur_tile(g):
        t = jnp.minimum(g, n_tiles - 1)
        return t // tiles_per_seq, t % tiles_per_seq

    def prev_tile(g):
        t = jnp.maximum(g - 1, 0)
        return t // tiles_per_seq, t % tiles_per_seq

    y_p, st_p, vch_p = pl.pallas_call(
        functools.partial(_prompt_kernel, tiles_per_seq=tiles_per_seq),
        grid=(n_tiles + 1,),
        in_specs=[pl.BlockSpec((1, PROMPT_TILE, D_MODEL), lambda g: (*cur_tile(g), 0)),
                  pl.BlockSpec((1, PROMPT_TILE, D_MODEL), lambda g: (*prev_tile(g), 0))]
        + _weight_specs(ROWS),
        out_specs=[
            pl.BlockSpec((1, PROMPT_TILE, D_MODEL), lambda g: (*prev_tile(g), 0)),
            pl.BlockSpec((1, 1, N_HEADS, HEAD, HEAD), lambda g: (0, prev_tile(g)[0], 0, 0, 0)),
            pl.BlockSpec((1, 1, ROWS, D_B), lambda g: (0, prev_tile(g)[0], 0, 0)),
        ],
        out_shape=[
            jax.ShapeDtypeStruct((batch, seq, D_MODEL), F32),
            jax.ShapeDtypeStruct((1, batch, N_HEADS, HEAD, HEAD), F32),
            jax.ShapeDtypeStruct((1, batch, ROWS, D_B), F32),
        ],
        scratch_shapes=[pltpu.VMEM((D_IN // PROJ_BLOCK, PROMPT_TILE, PROJ_BLOCK), F32),
                        pltpu.VMEM((D_IN // PROJ_BLOCK, PROMPT_TILE, PROJ_BLOCK), F32),
                        pltpu.VMEM((PROMPT_TILE, D_MODEL), BF16),
                        pltpu.VMEM((PROMPT_TILE, D_MODEL), BF16)],
        compiler_params=params,
        name="prompt_layer",
    )(x_prompt, x_prompt, *weights, w_s[0], b_s[0].T, *tail)

    ws_dec = jnp.tile(w_s[0, :, :DEC_SEQ, :DEC_SEQ], (1, SEQS_PER_TILE, SEQS_PER_TILE))
    bst_dec = jnp.tile(b_s[0, :, :DEC_SEQ].T, (SEQS_PER_TILE, 1))
    n_rows = dec_batch * DEC_SEQ
    y_s, st_s, vch_s = pl.pallas_call(
        _sample_kernel,
        grid=(n_rows // ROWS,),
        in_specs=[pl.BlockSpec((ROWS, D_MODEL), lambda i: (i, 0))]
        + _weight_specs(ROWS)
        + [pl.BlockSpec((1, SEQS_PER_TILE, N_HEADS, HEAD, HEAD), lambda i: (0, i, 0, 0, 0))],
        out_specs=[
            pl.BlockSpec((ROWS, D_MODEL), lambda i: (i, 0)),
            pl.BlockSpec((1, SEQS_PER_TILE, N_HEADS, HEAD, HEAD), lambda i: (0, i, 0, 0, 0)),
            pl.BlockSpec((ROWS, D_B), lambda i: (i, 0)),
        ],
        out_shape=[
            jax.ShapeDtypeStruct((n_rows, D_MODEL), F32),
            jax.ShapeDtypeStruct((1, dec_batch, N_HEADS, HEAD, HEAD), F32),
            jax.ShapeDtypeStruct((n_rows, D_B), F32),
        ],
        scratch_shapes=[pltpu.VMEM((ROWS, D_A), F32),
                        pltpu.VMEM((N_HEADS, HEAD, ROWS), F32),
                        pltpu.VMEM((N_HEADS, HEAD, ROWS), F32),
                        pltpu.VMEM((ROWS, D_A), F32)],
        compiler_params=pltpu.CompilerParams(dimension_semantics=("arbitrary",),
                                             vmem_limit_bytes=VMEM_LIMIT_BYTES),
        name="decode_layer",
    )(x_sample.reshape(n_rows, D_MODEL), *weights, ws_dec, bst_dec, *tail, state_hgrn)

    return (y_p,
            y_s.reshape(dec_batch, DEC_SEQ, D_MODEL),
            st_p,
            st_s,
            vch_p.reshape(1, batch, ROWS, N_HEADS, HEAD),
            vch_s.reshape(1, dec_batch, DEC_SEQ, N_HEADS, HEAD))
```

```python
import functools

import jax
import jax.numpy as jnp
from jax import lax
from jax.experimental import pallas as pl
from jax.experimental.pallas import tpu as pltpu

F32 = jnp.float32
BF16 = jnp.bfloat16

D_MODEL = 1024
D_A = 512
D_B = 512
HEAD = 128
N_HEADS = 4
D_IN = 4 * D_A + 3 * D_B
ROWS = 128
SUBLANES = 8
N_TILES = ROWS // SUBLANES
SUBLANE_LEVELS = 3
EPS = 1e-6
PROMPT_TILE = 256
PROJ_BLOCK = 256
HEAD_ROUNDS = 12
DEC_SEQ = 8
SEQS_PER_TILE = ROWS // DEC_SEQ
SEQ_GROUP = 4
VMEM_LIMIT_BYTES = 56 * 1024 * 1024

_NT = (((1,), (1,)), ((), ()))


def _silu(x):
    return x * jax.nn.sigmoid(x)


def _rms_norm(x, g):
    return x * lax.rsqrt(jnp.mean(x * x, axis=-1, keepdims=True) + EPS) * g


def _layer_norm(x, g, b):
    mu = jnp.mean(x, axis=-1, keepdims=True)
    xc = x - mu
    var = jnp.mean(xc * xc, axis=-1, keepdims=True)
    return xc * lax.rsqrt(var + EPS) * g + b


def _lower_bound(lbl):
    rows = [lbl[r:r + 1, :] for r in range(lbl.shape[0])]
    mx = functools.reduce(jnp.maximum, rows)
    es = [jnp.exp(r - mx) for r in rows]
    return es[0] / functools.reduce(lambda a, c: a + c, es)


def _chunk_constants(seg_len):
    t = lax.broadcasted_iota(jnp.int32, (ROWS, ROWS), 0)
    s = lax.broadcasted_iota(jnp.int32, (ROWS, ROWS), 1)
    n_levels = seg_len.bit_length() - 1
    assert seg_len == 1 << n_levels and n_levels >= SUBLANE_LEVELS
    same_seq = (t >> n_levels) == (s >> n_levels)
    causal = (t >= s) & same_seq
    x = t ^ s
    lvl = jnp.full((ROWS, ROWS), -1, jnp.int32)
    for j in range(n_levels):
        lvl = jnp.where(((x >> j) == 1) & (((t >> j) & 1) == 1), j, lvl)
    rowi = lax.broadcasted_iota(jnp.int32, (ROWS, HEAD), 0)
    bits = [((rowi >> j) & 1) == 1 for j in range(SUBLANE_LEVELS)]
    return causal, lvl, bits, n_levels


def _sibling(x, m, bit):
    x3 = x.reshape(N_TILES, SUBLANES, HEAD)
    up = pltpu.roll(x3, m, 1)
    if 2 * m == SUBLANES:
        return up.reshape(ROWS, HEAD)
    down = pltpu.roll(x3, SUBLANES - m, 1)
    return jnp.where(bit, up.reshape(ROWS, HEAD), down.reshape(ROWS, HEAD))


def _tiles(x):
    return [x[i * SUBLANES:(i + 1) * SUBLANES, :] for i in range(N_TILES)]


def _hgrn_head(q, kk, f, lvl, bits, n_levels):
    one = jnp.ones((ROWS, HEAD), F32)
    diag = jnp.sum(q * kk, axis=-1, keepdims=True)
    qp, ks, blk = q * f, kk, f
    sc = jnp.zeros((ROWS, ROWS), F32)
    for j in range(SUBLANE_LEVELS):
        z = jnp.where(bits[j], qp, ks).astype(BF16)
        p = lax.dot_general(z, z, _NT, preferred_element_type=F32)
        sc = jnp.where(lvl == j, p, sc)
        sib = _sibling(blk, 1 << j, bits[j])
        qp = qp * jnp.where(bits[j], sib, one)
        ks = ks * jnp.where(bits[j], one, sib)
        blk = blk * sib
        yield

    qp_t, ks_t = _tiles(qp), _tiles(ks)
    blk_b = _tiles(blk)
    lvl_t = _tiles(lvl)
    sc_t = _tiles(sc)
    for j in range(SUBLANE_LEVELS, n_levels):
        span = 1 << (j - SUBLANE_LEVELS)
        upper = [(i // span) & 1 == 1 for i in range(N_TILES)]
        z = jnp.concatenate([qp_t[i] if upper[i] else ks_t[i] for i in range(N_TILES)],
                            axis=0).astype(BF16)
        p = lax.dot_general(z, z, _NT, preferred_element_type=F32)
        for i in range(N_TILES):
            sib = blk_b[(i // span) ^ 1]
            if upper[i]:
                sc_t[i] = jnp.where(lvl_t[i] == j, p[i * SUBLANES:(i + 1) * SUBLANES, :], sc_t[i])
                qp_t[i] = qp_t[i] * sib
            else:
                ks_t[i] = ks_t[i] * sib
        blk_b = [blk_b[2 * n] * blk_b[2 * n + 1] for n in range(len(blk_b) // 2)]
        yield

    sc = jnp.concatenate(sc_t, axis=0).astype(BF16)
    q_in = jnp.concatenate(qp_t, axis=0)
    k_out = jnp.concatenate(ks_t, axis=0)
    per_tile = N_TILES // len(blk_b)
    total = jnp.concatenate([blk_b[i // per_tile] for i in range(N_TILES)], axis=0)
    return sc, diag, q_in, k_out, total


def _lockstep(generators, starts=None):
    starts = starts or [0] * len(generators)
    results = [None] * len(generators)
    waiting = sorted(range(len(generators)), key=lambda i: starts[i])
    active = []
    rnd = 0
    while waiting or active:
        while waiting and starts[waiting[0]] <= rnd:
            active.append(waiting.pop(0))
        still = []
        for idx in active:
            try:
                next(generators[idx])
                still.append(idx)
            except StopIteration as done:
                results[idx] = done.value
        active = still
        rnd += 1
    return results


def _hgrn_gates(pq, pf, lb):
    q = _silu(pq)
    forget = lb + (1.0 - lb) * jax.nn.sigmoid(pf)
    return q, 1.0 - forget, forget


def _hgrn_finish(o, pz, g):
    return (_rms_norm(o, g) * _silu(pz)).astype(BF16)


def _mlp_head(hd, u, v_n16, gate, ws_ref, bst, causal):
    w = jnp.where(causal, ws_ref[hd], 0.0).astype(BF16)
    mix = jnp.dot(w, v_n16, preferred_element_type=F32) + bst[:, hd:hd + 1]
    return (jax.nn.gelu(u) * mix * _silu(gate)).astype(BF16)


def _prompt_kernel(x_ref, xprev_ref, ng_ref, win_ref, lbl_ref, hg_ref, lng_ref, lnb_ref, ws_ref,
                   bst_ref, wout_ref, fg_ref, y_ref, st_ref, vch_ref, proj_a, proj_b, h_ref,
                   mixed_ref, *, tiles_per_seq):
    g = pl.program_id(0)

    @pl.when(g == 0)
    def _():
        proj_b[...] = jnp.zeros_like(proj_b)

    @pl.when(lax.rem(jnp.maximum(g - 1, 0), tiles_per_seq) == 0)
    def _():
        st_ref[...] = jnp.zeros_like(st_ref)

    def step(proj_next, proj_prev):
        h_ref[...] = _rms_norm(x_ref[0], ng_ref[...]).astype(BF16)
        lb = _lower_bound(lbl_ref[...])
        causal, lvl, bits, n_levels = _chunk_constants(ROWS)
        hg = hg_ref[...]
        lng = lng_ref[...]
        lnb = lnb_ref[...]
        bst = bst_ref[...]
        n_chunks = PROMPT_TILE // ROWS
        n_blocks = D_IN // PROJ_BLOCK

        def col(c, k, hs=slice(0, D_A)):
            return proj_prev[c * ROWS:(c + 1) * ROWS, k * D_A + hs.start:k * D_A + hs.stop]

        def head(c, hd):
            hs = slice(hd * HEAD, (hd + 1) * HEAD)
            q, kk, f = _hgrn_gates(col(c, 0, hs), col(c, 1, hs), lb[:, hs])
            yield
            sc, diag, q_in, k_out, total = yield from _hgrn_head(q, kk, f, lvl, bits, n_levels)
            state = st_ref[0, 0, hd]
            vh = col(c, 2, hs)
            o = jnp.dot(jnp.concatenate([sc, q_in.astype(BF16)], axis=1),
                        jnp.concatenate([vh.astype(BF16), state.astype(BF16)], axis=0),
                        preferred_element_type=F32) + diag * vh
            yield
            st_ref[0, 0, hd] = total.T * state + jnp.dot(
                k_out.T.astype(BF16), vh.astype(BF16), preferred_element_type=F32)
            yield
            mixed_ref[c * ROWS:(c + 1) * ROWS, hs] = _hgrn_finish(o, col(c, 3, hs), hg[:, hs])

        def mlp(c):
            v_n = _layer_norm(jax.nn.gelu(col(c, 5)), lng, lnb)
            if c == n_chunks - 1:
                vch_ref[0, 0] = v_n
            v_n16 = v_n.astype(BF16)
            yield
            for hd in range(N_HEADS):
                hs = slice(hd * HEAD, (hd + 1) * HEAD)
                mixed_ref[c * ROWS:(c + 1) * ROWS, D_A + hd * HEAD:D_A + (hd + 1) * HEAD] = (
                    _mlp_head(hd, col(c, 4, hs), v_n16[:, hs], col(c, 6, hs), ws_ref, bst, causal))
                yield
                yield

        def projection(rounds_per_block):
            for blk in range(n_blocks):
                cols = slice(blk * PROJ_BLOCK, (blk + 1) * PROJ_BLOCK)
                proj_next[:, cols] = jnp.dot(h_ref[...], win_ref[:, cols],
                                             preferred_element_type=F32)
                for _ in range(rounds_per_block):
                    yield

        items = [gen for c in range(n_chunks)
                 for gen in [head(c, hd) for hd in range(N_HEADS)] + [mlp(c)]]
        starts = [c * HEAD_ROUNDS for c in range(n_chunks) for _ in range(N_HEADS + 1)]
        rounds = n_chunks * HEAD_ROUNDS
        _lockstep(items + [projection(max(1, rounds // n_blocks))], starts + [0])

        out = xprev_ref[0] + jnp.dot(mixed_ref[...], wout_ref[...], preferred_element_type=F32)
        y_ref[0] = _rms_norm(out, fg_ref[...])

    parity = lax.rem(g, 2)

    @pl.when(parity == 0)
    def _():
        step(proj_a, proj_b)

    @pl.when(parity == 1)
    def _():
        step(proj_b, proj_a)


def _sample_kernel(x_ref, ng_ref, win_ref, lbl_ref, hg_ref, lng_ref, lnb_ref, ws_ref, bst_ref,
                   wout_ref, fg_ref, stin_ref, y_ref, stout_ref, vch_ref,
                   qin_ref, kot_ref, tot_ref, oint_ref):
    x = x_ref[...]
    h = _rms_norm(x, ng_ref[...]).astype(BF16)
    proj = jnp.dot(h, win_ref[...], preferred_element_type=F32)

    def col(k, hs=slice(0, D_A)):
        return proj[:, k * D_A + hs.start:k * D_A + hs.stop]

    lb = _lower_bound(lbl_ref[...])
    causal, lvl, bits, n_levels = _chunk_constants(DEC_SEQ)
    vi16 = col(2).astype(BF16)

    def head(hd):
        hs = slice(hd * HEAD, (hd + 1) * HEAD)
        q, kk, f = _hgrn_gates(col(0, hs), col(1, hs), lb[:, hs])
        sc, diag, q_in, k_out, total = yield from _hgrn_head(q, kk, f, lvl, bits, n_levels)
        qin_ref[:, hs] = q_in
        kot_ref[hd] = k_out.T
        tot_ref[hd] = total.T
        return jnp.dot(sc, vi16[:, hs], preferred_element_type=F32) + diag * col(2, hs)

    o_intra = _lockstep([head(hd) for hd in range(N_HEADS)])

    lane = lax.broadcasted_iota(jnp.int32, (HEAD, ROWS), 1)

    def seq_head(i, hd):
        hs = slice(hd * HEAD, (hd + 1) * HEAD)
        seq_rows = slice(i * DEC_SEQ, (i + 1) * DEC_SEQ)
        state = stin_ref[0, i, hd]
        qi = qin_ref[seq_rows, hs].astype(BF16)
        oint_ref[seq_rows, hs] = jnp.dot(qi, state.astype(BF16), preferred_element_type=F32)
        yield
        in_seq = (lane >> (DEC_SEQ.bit_length() - 1)) == i
        k_seq = jnp.where(in_seq, kot_ref[hd], 0.0).astype(BF16)
        decay = tot_ref[hd, :, i * DEC_SEQ:i * DEC_SEQ + 1]
        yield
        stout_ref[0, i, hd] = decay * state + jnp.dot(
            k_seq, vi16[:, hs], preferred_element_type=F32)

    for grp in range(SEQS_PER_TILE // SEQ_GROUP):
        _lockstep([seq_head(grp * SEQ_GROUP + s, hd)
                   for s in range(SEQ_GROUP) for hd in range(N_HEADS)])

    hg = hg_ref[...]
    bst = bst_ref[...]
    v_n = _layer_norm(jax.nn.gelu(col(5)), lng_ref[...], lnb_ref[...])
    vch_ref[...] = v_n
    v_n16 = v_n.astype(BF16)
    mixed = []
    for hd in range(N_HEADS):
        hs = slice(hd * HEAD, (hd + 1) * HEAD)
        mixed.append(_hgrn_finish(o_intra[hd] + oint_ref[:, hs], col(3, hs), hg[:, hs]))
    for hd in range(N_HEADS):
        hs = slice(hd * HEAD, (hd + 1) * HEAD)
        mixed.append(_mlp_head(hd, col(4, hs), v_n16[:, hs], col(6, hs), ws_ref, bst, causal))
    out = x + jnp.dot(jnp.concatenate(mixed, axis=-1), wout_ref[...], preferred_element_type=F32)
    y_ref[...] = _rms_norm(out, fg_ref[...])


def _const_spec(shape):
    return pl.BlockSpec(shape, lambda *_: (0,) * len(shape), pipeline_mode=pl.Buffered(1))


def _weight_specs(bst_rows):
    return [
        _const_spec((1, D_MODEL)),
        _const_spec((D_MODEL, D_IN)),
        _const_spec((2, D_A)),
        _const_spec((1, D_A)),
        _const_spec((1, D_B)),
        _const_spec((1, D_B)),
        _const_spec((N_HEADS, ROWS, ROWS)),
        _const_spec((bst_rows, N_HEADS)),
        _const_spec((D_MODEL, D_MODEL)),
        _const_spec((1, D_MODEL)),
    ]


def kernel(x_prompt, x_sample, state_hgrn, norm_g, w_in, lb_logits, hgrn_norm_g, sgu_ln_g,
           sgu_ln_b, w_s, b_s, w_out, final_norm_g):
    depth = norm_g.shape[0]
    assert depth == 1 and lb_logits.shape == (2, D_A)
    batch, seq, _ = x_prompt.shape
    dec_batch, dec_seq, _ = x_sample.shape
    assert seq % PROMPT_TILE == 0 and dec_seq == DEC_SEQ and dec_batch % SEQS_PER_TILE == 0

    weights = (norm_g, w_in[0].astype(BF16), lb_logits, hgrn_norm_g, sgu_ln_g, sgu_ln_b)
    tail = (w_out[0].astype(BF16), final_norm_g.reshape(1, D_MODEL))
    params = pltpu.CompilerParams(dimension_semantics=("arbitrary",),
                                  vmem_limit_bytes=VMEM_LIMIT_BYTES)

    tiles_per_seq = seq // PROMPT_TILE
    n_tiles = batch * tiles_per_seq

    def cur_tile(g):
        t = jnp.minimum(g, n_tiles - 1)
        return t // tiles_per_seq, t % tiles_per_seq

    def prev_tile(g):
        t = jnp.maximum(g - 1, 0)
        return t // tiles_per_seq, t % tiles_per_seq

    y_p, st_p, vch_p = pl.pallas_call(
        functools.partial(_prompt_kernel, tiles_per_seq=tiles_per_seq),
        grid=(n_tiles + 1,),
        in_specs=[pl.BlockSpec((1, PROMPT_TILE, D_MODEL), lambda g: (*cur_tile(g), 0)),
                  pl.BlockSpec((1, PROMPT_TILE, D_MODEL), lambda g: (*prev_tile(g), 0))]
        + _weight_specs(ROWS),
        out_specs=[
            pl.BlockSpec((1, PROMPT_TILE, D_MODEL), lambda g: (*prev_tile(g), 0)),
            pl.BlockSpec((1, 1, N_HEADS, HEAD, HEAD), lambda g: (0, prev_tile(g)[0], 0, 0, 0)),
            pl.BlockSpec((1, 1, ROWS, D_B), lambda g: (0, prev_tile(g)[0], 0, 0)),
        ],
        out_shape=[
            jax.ShapeDtypeStruct((batch, seq, D_MODEL), F32),
            jax.ShapeDtypeStruct((1, batch, N_HEADS, HEAD, HEAD), F32),
            jax.ShapeDtypeStruct((1, batch, ROWS, D_B), F32),
        ],
        scratch_shapes=[pltpu.VMEM((PROMPT_TILE, D_IN), F32),
                        pltpu.VMEM((PROMPT_TILE, D_IN), F32),
                        pltpu.VMEM((PROMPT_TILE, D_MODEL), BF16),
                        pltpu.VMEM((PROMPT_TILE, D_MODEL), BF16)],
        compiler_params=params,
        name="prompt_layer",
    )(x_prompt, x_prompt, *weights, w_s[0], b_s[0].T, *tail)

    ws_dec = jnp.tile(w_s[0, :, :DEC_SEQ, :DEC_SEQ], (1, SEQS_PER_TILE, SEQS_PER_TILE))
    bst_dec = jnp.tile(b_s[0, :, :DEC_SEQ].T, (SEQS_PER_TILE, 1))
    n_rows = dec_batch * DEC_SEQ
    y_s, st_s, vch_s = pl.pallas_call(
        _sample_kernel,
        grid=(n_rows // ROWS,),
        in_specs=[pl.BlockSpec((ROWS, D_MODEL), lambda i: (i, 0))]
        + _weight_specs(ROWS)
        + [pl.BlockSpec((1, SEQS_PER_TILE, N_HEADS, HEAD, HEAD), lambda i: (0, i, 0, 0, 0))],
        out_specs=[
            pl.BlockSpec((ROWS, D_MODEL), lambda i: (i, 0)),
            pl.BlockSpec((1, SEQS_PER_TILE, N_HEADS, HEAD, HEAD), lambda i: (0, i, 0, 0, 0)),
            pl.BlockSpec((ROWS, D_B), lambda i: (i, 0)),
        ],
        out_shape=[
            jax.ShapeDtypeStruct((n_rows, D_MODEL), F32),
            jax.ShapeDtypeStruct((1, dec_batch, N_HEADS, HEAD, HEAD), F32),
            jax.ShapeDtypeStruct((n_rows, D_B), F32),
        ],
        scratch_shapes=[pltpu.VMEM((ROWS, D_A), F32),
                        pltpu.VMEM((N_HEADS, HEAD, ROWS), F32),
                        pltpu.VMEM((N_HEADS, HEAD, ROWS), F32),
                        pltpu.VMEM((ROWS, D_A), F32)],
        compiler_params=pltpu.CompilerParams(dimension_semantics=("arbitrary",),
                                             vmem_limit_bytes=VMEM_LIMIT_BYTES),
        name="decode_layer",
    )(x_sample.reshape(n_rows, D_MODEL), *weights, ws_dec, bst_dec, *tail, state_hgrn)

    return (y_p,
            y_s.reshape(dec_batch, DEC_SEQ, D_MODEL),
            st_p,
            st_s,
            vch_p.reshape(1, batch, ROWS, N_HEADS, HEAD),
            vch_s.reshape(1, dec_batch, DEC_SEQ, N_HEADS, HEAD))
```

```python
import functools

import jax
import jax.numpy as jnp
from jax import lax
from jax.experimental import pallas as pl
from jax.experimental.pallas import tpu as pltpu

F32 = jnp.float32
BF16 = jnp.bfloat16

D_MODEL = 1024
D_A = 512
D_B = 512
HEAD = 128
N_HEADS = 4
D_IN = 4 * D_A + 3 * D_B
ROWS = 128
SUBLANES = 8
N_TILES = ROWS // SUBLANES
SUBLANE_LEVELS = 3
EPS = 1e-6
PROMPT_TILE = 256
PROJ_BLOCK = 512
HEAD_ROUNDS = 12
DEC_SEQ = 8
SEQS_PER_TILE = ROWS // DEC_SEQ
SEQ_GROUP = 4
VMEM_LIMIT_BYTES = 56 * 1024 * 1024

_NT = (((1,), (1,)), ((), ()))


def _silu(x):
    return x * jax.nn.sigmoid(x)


def _rms_norm(x, g):
    return x * lax.rsqrt(jnp.mean(x * x, axis=-1, keepdims=True) + EPS) * g


def _layer_norm(x, g, b):
    mu = jnp.mean(x, axis=-1, keepdims=True)
    xc = x - mu
    var = jnp.mean(xc * xc, axis=-1, keepdims=True)
    return xc * lax.rsqrt(var + EPS) * g + b


def _lower_bound(lbl):
    rows = [lbl[r:r + 1, :] for r in range(lbl.shape[0])]
    mx = functools.reduce(jnp.maximum, rows)
    es = [jnp.exp(r - mx) for r in rows]
    return es[0] / functools.reduce(lambda a, c: a + c, es)


def _chunk_constants(seg_len):
    t = lax.broadcasted_iota(jnp.int32, (ROWS, ROWS), 0)
    s = lax.broadcasted_iota(jnp.int32, (ROWS, ROWS), 1)
    n_levels = seg_len.bit_length() - 1
    assert seg_len == 1 << n_levels and n_levels >= SUBLANE_LEVELS
    same_seq = (t >> n_levels) == (s >> n_levels)
    causal = (t >= s) & same_seq
    x = t ^ s
    lvl = jnp.full((ROWS, ROWS), -1, jnp.int32)
    for j in range(n_levels):
        lvl = jnp.where(((x >> j) == 1) & (((t >> j) & 1) == 1), j, lvl)
    rowi = lax.broadcasted_iota(jnp.int32, (SUBLANES, HEAD), 0)
    bits = [((rowi >> j) & 1) == 1 for j in range(SUBLANE_LEVELS)]
    return causal, lvl, bits, n_levels


def _hgrn_head(gates, lvl, bits, n_levels):
    one = jnp.ones((SUBLANES, HEAD), F32)
    z = [[] for _ in range(n_levels)]
    qp_t, ks_t, blk_t, diag_t = [], [], [], []
    for pair in range(N_TILES // 2):
        z_pair = [[] for _ in range(SUBLANE_LEVELS)]
        for i in (2 * pair, 2 * pair + 1):
            q, kk, f = gates(i)
            diag_t.append(jnp.sum(q * kk, axis=-1, keepdims=True))
            qp, ks, blk = q * f, kk, f
            for j in range(SUBLANE_LEVELS):
                m = 1 << j
                z_pair[j].append(jnp.where(bits[j], qp, ks))
                sib = pltpu.roll(blk, m, 0)
                if 2 * m != SUBLANES:
                    sib = jnp.where(bits[j], sib, pltpu.roll(blk, SUBLANES - m, 0))
                qp = qp * jnp.where(bits[j], sib, one)
                ks = ks * jnp.where(bits[j], one, sib)
                blk = blk * sib
            qp_t.append(qp)
            ks_t.append(ks)
            blk_t.append(blk)
        for j in range(SUBLANE_LEVELS):
            z[j].append(jnp.concatenate(z_pair[j], axis=0).astype(BF16))
    yield

    blk_levels, blk_b = [], blk_t
    for j in range(SUBLANE_LEVELS, n_levels):
        blk_levels.append(blk_b)
        blk_b = [blk_b[2 * n] * blk_b[2 * n + 1] for n in range(len(blk_b) // 2)]
    for pair in range(N_TILES // 2):
        z_pair = [[] for _ in range(SUBLANE_LEVELS, n_levels)]
        for i in (2 * pair, 2 * pair + 1):
            for j in range(SUBLANE_LEVELS, n_levels):
                block = i >> (j - SUBLANE_LEVELS)
                sib = blk_levels[j - SUBLANE_LEVELS][block ^ 1]
                if block & 1:
                    z_pair[j - SUBLANE_LEVELS].append(qp_t[i])
                    qp_t[i] = qp_t[i] * sib
                else:
                    z_pair[j - SUBLANE_LEVELS].append(ks_t[i])
                    ks_t[i] = ks_t[i] * sib
        for j in range(SUBLANE_LEVELS, n_levels):
            z[j].append(jnp.concatenate(z_pair[j - SUBLANE_LEVELS], axis=0).astype(BF16))
    q_in = jnp.concatenate(qp_t, axis=0)
    k_out = jnp.concatenate(ks_t, axis=0)
    per_tile = N_TILES // len(blk_b)
    total = jnp.concatenate([blk_b[i // per_tile] for i in range(N_TILES)], axis=0)
    diag = jnp.concatenate(diag_t, axis=0)
    yield

    p = []
    for j in range(n_levels):
        zj = jnp.concatenate(z[j], axis=0)
        p.append(lax.dot_general(zj, zj, _NT, preferred_element_type=F32))
        yield
    sc_t = []
    for i in range(N_TILES):
        rows = slice(i * SUBLANES, (i + 1) * SUBLANES)
        lvl_i = lvl[rows, :]
        acc = jnp.zeros((SUBLANES, ROWS), F32)
        for j in range(n_levels):
            if j < SUBLANE_LEVELS or (i >> (j - SUBLANE_LEVELS)) & 1:
                acc = jnp.where(lvl_i == j, p[j][rows, :], acc)
        sc_t.append(acc)
    sc = jnp.concatenate(sc_t, axis=0).astype(BF16)
    return sc, diag, q_in, k_out, total


def _lockstep(generators, starts=None):
    starts = starts or [0] * len(generators)
    results = [None] * len(generators)
    waiting = sorted(range(len(generators)), key=lambda i: starts[i])
    active = []
    rnd = 0
    while waiting or active:
        while waiting and starts[waiting[0]] <= rnd:
            active.append(waiting.pop(0))
        still = []
        for idx in active:
            try:
                next(generators[idx])
                still.append(idx)
            except StopIteration as done:
                results[idx] = done.value
        active = still
        rnd += 1
    return results


def _hgrn_gates(pq, pf, lb):
    q = _silu(pq)
    forget = lb + (1.0 - lb) * jax.nn.sigmoid(pf)
    return q, 1.0 - forget, forget


def _hgrn_finish(o, pz, g):
    return (_rms_norm(o, g) * _silu(pz)).astype(BF16)


def _mlp_head(hd, u, v_n16, gate, ws_ref, bst, causal):
    w = jnp.where(causal, ws_ref[hd], 0.0).astype(BF16)
    mix = jnp.dot(w, v_n16, preferred_element_type=F32) + bst[:, hd:hd + 1]
    return (jax.nn.gelu(u) * mix * _silu(gate)).astype(BF16)


def _prompt_kernel(x_ref, xprev_ref, ng_ref, win_ref, lbl_ref, hg_ref, lng_ref, lnb_ref, ws_ref,
                   bst_ref, wout_ref, fg_ref, y_ref, st_ref, vch_ref, proj_a, proj_b, h_ref,
                   mixed_ref, *, tiles_per_seq):
    g = pl.program_id(0)

    @pl.when(g == 0)
    def _():
        proj_b[...] = jnp.zeros_like(proj_b)

    @pl.when(lax.rem(jnp.maximum(g - 1, 0), tiles_per_seq) == 0)
    def _():
        st_ref[...] = jnp.zeros_like(st_ref)

    def step(proj_next, proj_prev):
        h_ref[...] = _rms_norm(x_ref[0], ng_ref[...]).astype(BF16)
        lb = _lower_bound(lbl_ref[...])
        causal, lvl, bits, n_levels = _chunk_constants(ROWS)
        hg = hg_ref[...]
        lng = lng_ref[...]
        lnb = lnb_ref[...]
        bst = bst_ref[...]
        n_chunks = PROMPT_TILE // ROWS
        n_blocks = D_IN // PROJ_BLOCK

        def col(c, k, hs=slice(0, D_A)):
            return proj_prev[c * ROWS:(c + 1) * ROWS, k * D_A + hs.start:k * D_A + hs.stop]

        def head(c, hd):
            hs = slice(hd * HEAD, (hd + 1) * HEAD)

            def gates(i):
                rows = slice(c * ROWS + i * SUBLANES, c * ROWS + (i + 1) * SUBLANES)
                return _hgrn_gates(proj_prev[rows, hs],
                                   proj_prev[rows, D_A + hd * HEAD:D_A + (hd + 1) * HEAD],
                                   lb[:, hs])

            sc, diag, q_in, k_out, total = yield from _hgrn_head(gates, lvl, bits, n_levels)
            state = st_ref[0, 0, hd]
            vh = col(c, 2, hs)
            o = jnp.dot(jnp.concatenate([sc, q_in.astype(BF16)], axis=1),
                        jnp.concatenate([vh.astype(BF16), state.astype(BF16)], axis=0),
                        preferred_element_type=F32) + diag * vh
            yield
            st_ref[0, 0, hd] = total.T * state + jnp.dot(
                k_out.T.astype(BF16), vh.astype(BF16), preferred_element_type=F32)
            yield
            mixed_ref[c * ROWS:(c + 1) * ROWS, hs] = _hgrn_finish(o, col(c, 3, hs), hg[:, hs])

        def mlp(c):
            v_n = _layer_norm(jax.nn.gelu(col(c, 5)), lng, lnb)
            if c == n_chunks - 1:
                vch_ref[0, 0] = v_n
            v_n16 = v_n.astype(BF16)
            yield
            for hd in range(N_HEADS):
                hs = slice(hd * HEAD, (hd + 1) * HEAD)
                mixed_ref[c * ROWS:(c + 1) * ROWS, D_A + hd * HEAD:D_A + (hd + 1) * HEAD] = (
                    _mlp_head(hd, col(c, 4, hs), v_n16[:, hs], col(c, 6, hs), ws_ref, bst, causal))
                yield
                yield

        def projection(rounds_per_block):
            for blk in range(n_blocks):
                cols = slice(blk * PROJ_BLOCK, (blk + 1) * PROJ_BLOCK)
                proj_next[:, cols] = jnp.dot(h_ref[...], win_ref[:, cols],
                                             preferred_element_type=F32)
                for _ in range(rounds_per_block):
                    yield

        items = [gen for c in range(n_chunks)
                 for gen in [head(c, hd) for hd in range(N_HEADS)] + [mlp(c)]]
        starts = [c * HEAD_ROUNDS for c in range(n_chunks) for _ in range(N_HEADS + 1)]
        rounds = n_chunks * HEAD_ROUNDS
        _lockstep(items + [projection(max(1, rounds // n_blocks))], starts + [0])

        out = xprev_ref[0] + jnp.dot(mixed_ref[...], wout_ref[...], preferred_element_type=F32)
        y_ref[0] = _rms_norm(out, fg_ref[...])

    parity = lax.rem(g, 2)

    @pl.when(parity == 0)
    def _():
        step(proj_a, proj_b)

    @pl.when(parity == 1)
    def _():
        step(proj_b, proj_a)


def _sample_kernel(x_ref, ng_ref, win_ref, lbl_ref, hg_ref, lng_ref, lnb_ref, ws_ref, bst_ref,
                   wout_ref, fg_ref, stin_ref, y_ref, stout_ref, vch_ref,
                   qin_ref, kot_ref, tot_ref, oint_ref):
    x = x_ref[...]
    h = _rms_norm(x, ng_ref[...]).astype(BF16)
    proj = jnp.dot(h, win_ref[...], preferred_element_type=F32)

    def col(k, hs=slice(0, D_A)):
        return proj[:, k * D_A + hs.start:k * D_A + hs.stop]

    lb = _lower_bound(lbl_ref[...])
    causal, lvl, bits, n_levels = _chunk_constants(DEC_SEQ)
    vi16 = col(2).astype(BF16)

    def head(hd):
        hs = slice(hd * HEAD, (hd + 1) * HEAD)

        def gates(i):
            rows = slice(i * SUBLANES, (i + 1) * SUBLANES)
            return _hgrn_gates(proj[rows, hs], proj[rows, D_A + hd * HEAD:D_A + (hd + 1) * HEAD],
                               lb[:, hs])

        sc, diag, q_in, k_out, total = yield from _hgrn_head(gates, lvl, bits, n_levels)
        qin_ref[:, hs] = q_in
        kot_ref[hd] = k_out.T
        tot_ref[hd] = total.T
        return jnp.dot(sc, vi16[:, hs], preferred_element_type=F32) + diag * col(2, hs)

    o_intra = _lockstep([head(hd) for hd in range(N_HEADS)])

    lane = lax.broadcasted_iota(jnp.int32, (HEAD, ROWS), 1)

    def seq_head(i, hd):
        hs = slice(hd * HEAD, (hd + 1) * HEAD)
        seq_rows = slice(i * DEC_SEQ, (i + 1) * DEC_SEQ)
        state = stin_ref[0, i, hd]
        qi = qin_ref[seq_rows, hs].astype(BF16)
        oint_ref[seq_rows, hs] = jnp.dot(qi, state.astype(BF16), preferred_element_type=F32)
        yield
        in_seq = (lane >> (DEC_SEQ.bit_length() - 1)) == i
        k_seq = jnp.where(in_seq, kot_ref[hd], 0.0).astype(BF16)
        decay = tot_ref[hd, :, i * DEC_SEQ:i * DEC_SEQ + 1]
        yield
        stout_ref[0, i, hd] = decay * state + jnp.dot(
            k_seq, vi16[:, hs], preferred_element_type=F32)

    for grp in range(SEQS_PER_TILE // SEQ_GROUP):
        _lockstep([seq_head(grp * SEQ_GROUP + s, hd)
                   for s in range(SEQ_GROUP) for hd in range(N_HEADS)])

    hg = hg_ref[...]
    bst = bst_ref[...]
    v_n = _layer_norm(jax.nn.gelu(col(5)), lng_ref[...], lnb_ref[...])
    vch_ref[...] = v_n
    v_n16 = v_n.astype(BF16)
    mixed = []
    for hd in range(N_HEADS):
        hs = slice(hd * HEAD, (hd + 1) * HEAD)
        mixed.append(_hgrn_finish(o_intra[hd] + oint_ref[:, hs], col(3, hs), hg[:, hs]))
    for hd in range(N_HEADS):
        hs = slice(hd * HEAD, (hd + 1) * HEAD)
        mixed.append(_mlp_head(hd, col(4, hs), v_n16[:, hs], col(6, hs), ws_ref, bst, causal))
    out = x + jnp.dot(jnp.concatenate(mixed, axis=-1), wout_ref[...], preferred_element_type=F32)
    y_ref[...] = _rms_norm(out, fg_ref[...])


def _const_spec(shape):
    return pl.BlockSpec(shape, lambda *_: (0,) * len(shape), pipeline_mode=pl.Buffered(1))


def _weight_specs(bst_rows):
    return [
        _const_spec((1, D_MODEL)),
        _const_spec((D_MODEL, D_IN)),
        _const_spec((2, D_A)),
        _const_spec((1, D_A)),
        _const_spec((1, D_B)),
        _const_spec((1, D_B)),
        _const_spec((N_HEADS, ROWS, ROWS)),
        _const_spec((bst_rows, N_HEADS)),
        _const_spec((D_MODEL, D_MODEL)),
        _const_spec((1, D_MODEL)),
    ]


def kernel(x_prompt, x_sample, state_hgrn, norm_g, w_in, lb_logits, hgrn_norm_g, sgu_ln_g,
           sgu_ln_b, w_s, b_s, w_out, final_norm_g):
    depth = norm_g.shape[0]
    assert depth == 1 and lb_logits.shape == (2, D_A)
    batch, seq, _ = x_prompt.shape
    dec_batch, dec_seq, _ = x_sample.shape
    assert seq % PROMPT_TILE == 0 and dec_seq == DEC_SEQ and dec_batch % SEQS_PER_TILE == 0

    weights = (norm_g, w_in[0].astype(BF16), lb_logits, hgrn_norm_g, sgu_ln_g, sgu_ln_b)
    tail = (w_out[0].astype(BF16), final_norm_g.reshape(1, D_MODEL))
    params = pltpu.CompilerParams(dimension_semantics=("arbitrary",),
                                  vmem_limit_bytes=VMEM_LIMIT_BYTES)

    tiles_per_seq = seq // PROMPT_TILE
    n_tiles = batch * tiles_per_seq

    def cur_tile(g):
        t = jnp.minimum(g, n_tiles - 1)
        return t // tiles_per_seq, t % tiles_per_seq

    def prev_tile(g):
        t = jnp.maximum(g - 1, 0)
        return t // tiles_per_seq, t % tiles_per_seq

    y_p, st_p, vch_p = pl.pallas_call(
        functools.partial(_prompt_kernel, tiles_per_seq=tiles_per_seq),
        grid=(n_tiles + 1,),
        in_specs=[pl.BlockSpec((1, PROMPT_TILE, D_MODEL), lambda g: (*cur_tile(g), 0)),
                  pl.BlockSpec((1, PROMPT_TILE, D_MODEL), lambda g: (*prev_tile(g), 0))]
        + _weight_specs(ROWS),
        out_specs=[
            pl.BlockSpec((1, PROMPT_TILE, D_MODEL), lambda g: (*prev_tile(g), 0)),
            pl.BlockSpec((1, 1, N_HEADS, HEAD, HEAD), lambda g: (0, prev_tile(g)[0], 0, 0, 0)),
            pl.BlockSpec((1, 1, ROWS, D_B), lambda g: (0, prev_tile(g)[0], 0, 0)),
        ],
        out_shape=[
            jax.ShapeDtypeStruct((batch, seq, D_MODEL), F32),
            jax.ShapeDtypeStruct((1, batch, N_HEADS, HEAD, HEAD), F32),
            jax.ShapeDtypeStruct((1, batch, ROWS, D_B), F32),
        ],
        scratch_shapes=[pltpu.VMEM((PROMPT_TILE, D_IN), F32),
                        pltpu.VMEM((PROMPT_TILE, D_IN), F32),
                        pltpu.VMEM((PROMPT_TILE, D_MODEL), BF16),
                        pltpu.VMEM((PROMPT_TILE, D_MODEL), BF16)],
        compiler_params=params,
        name="prompt_layer",
    )(x_prompt, x_prompt, *weights, w_s[0], b_s[0].T, *tail)

    ws_dec = jnp.tile(w_s[0, :, :DEC_SEQ, :DEC_SEQ], (1, SEQS_PER_TILE, SEQS_PER_TILE))
    bst_dec = jnp.tile(b_s[0, :, :DEC_SEQ].T, (SEQS_PER_TILE, 1))
    n_rows = dec_batch * DEC_SEQ
    y_s, st_s, vch_s = pl.pallas_call(
        _sample_kernel,
        grid=(n_rows // ROWS,),
        in_specs=[pl.BlockSpec((ROWS, D_MODEL), lambda i: (i, 0))]
        + _weight_specs(ROWS)
        + [pl.BlockSpec((1, SEQS_PER_TILE, N_HEADS, HEAD, HEAD), lambda i: (0, i, 0, 0, 0))],
        out_specs=[
            pl.BlockSpec((ROWS, D_MODEL), lambda i: (i, 0)),
            pl.BlockSpec((1, SEQS_PER_TILE, N_HEADS, HEAD, HEAD), lambda i: (0, i, 0, 0, 0)),
            pl.BlockSpec((ROWS, D_B), lambda i: (i, 0)),
        ],
        out_shape=[
            jax.ShapeDtypeStruct((n_rows, D_MODEL), F32),
            jax.ShapeDtypeStruct((1, dec_batch, N_HEADS, HEAD, HEAD), F32),
            jax.ShapeDtypeStruct((n_rows, D_B), F32),
        ],
        scratch_shapes=[pltpu.VMEM((ROWS, D_A), F32),
                        pltpu.VMEM((N_HEADS, HEAD, ROWS), F32),
                        pltpu.VMEM((N_HEADS, HEAD, ROWS), F32),
                        pltpu.VMEM((ROWS, D_A), F32)],
        compiler_params=pltpu.CompilerParams(dimension_semantics=("arbitrary",),
                                             vmem_limit_bytes=VMEM_LIMIT_BYTES),
        name="decode_layer",
    )(x_sample.reshape(n_rows, D_MODEL), *weights, ws_dec, bst_dec, *tail, state_hgrn)

    return (y_p,
            y_s.reshape(dec_batch, DEC_SEQ, D_MODEL),
            st_p,
            st_s,
            vch_p.reshape(1, batch, ROWS, N_HEADS, HEAD),
            vch_s.reshape(1, dec_batch, DEC_SEQ, N_HEADS, HEAD))
```

```python
import functools

import jax
import jax.numpy as jnp
from jax import lax
from jax.experimental import pallas as pl
from jax.experimental.pallas import tpu as pltpu

F32 = jnp.float32
BF16 = jnp.bfloat16

D_MODEL = 1024
D_A = 512
D_B = 512
HEAD = 128
N_HEADS = 4
D_IN = 4 * D_A + 3 * D_B
ROWS = 128
SUBLANES = 8
N_TILES = ROWS // SUBLANES
SUBLANE_LEVELS = 3
EPS = 1e-6
PROMPT_TILE = 256
PROJ_BLOCK = 512
HEAD_ROUNDS = 12
DEC_SEQ = 8
SEQS_PER_TILE = ROWS // DEC_SEQ
SEQ_GROUP = 4
VMEM_LIMIT_BYTES = 56 * 1024 * 1024

_NT = (((1,), (1,)), ((), ()))


def _silu(x):
    return x * jax.nn.sigmoid(x)


def _gelu(x):
    c = -2.0 * 0.7978845608028654 * 1.4426950408889634
    return x / (1.0 + jnp.exp2(x * (c * 0.044715 * (x * x) + c)))


def _rms_norm(x, g):
    return x * lax.rsqrt(jnp.mean(x * x, axis=-1, keepdims=True) + EPS) * g


def _layer_norm(x, g, b):
    mu = jnp.mean(x, axis=-1, keepdims=True)
    xc = x - mu
    var = jnp.mean(xc * xc, axis=-1, keepdims=True)
    return xc * lax.rsqrt(var + EPS) * g + b


def _lower_bound(lbl):
    rows = [lbl[r:r + 1, :] for r in range(lbl.shape[0])]
    mx = functools.reduce(jnp.maximum, rows)
    es = [jnp.exp(r - mx) for r in rows]
    return es[0] / functools.reduce(lambda a, c: a + c, es)


def _n_levels(seg_len):
    n_levels = seg_len.bit_length() - 1
    assert seg_len == 1 << n_levels and n_levels >= SUBLANE_LEVELS
    return n_levels


def _store_constants(seg_len, ws_ref, lvl_ref, wc_ref):
    t = lax.broadcasted_iota(jnp.int32, (ROWS, ROWS), 0)
    s = lax.broadcasted_iota(jnp.int32, (ROWS, ROWS), 1)
    n_levels = _n_levels(seg_len)
    x = t ^ s
    lvl = jnp.where(t == s, n_levels, -1)
    for j in range(n_levels):
        lvl = jnp.where(((x >> j) == 1) & (((t >> j) & 1) == 1), j, lvl)
    lvl_ref[...] = lvl
    causal = (t >= s) & ((t >> n_levels) == (s >> n_levels))
    for hd in range(N_HEADS):
        wc_ref[hd] = jnp.where(causal, ws_ref[hd], 0.0).astype(BF16)


def _row_bits():
    rowi = lax.broadcasted_iota(jnp.int32, (SUBLANES, HEAD), 0)
    return [((rowi >> j) & 1) == 1 for j in range(SUBLANE_LEVELS)]


def _hgrn_head(gates, lvl_ref, bits, n_levels):
    one = jnp.ones((SUBLANES, HEAD), F32)
    z = [[] for _ in range(n_levels)]
    qp_t, ks_t, blk_t, diag_t = [], [], [], []
    for pair in range(N_TILES // 2):
        z_pair = [[] for _ in range(SUBLANE_LEVELS)]
        for i in (2 * pair, 2 * pair + 1):
            q, kk, f = gates(i)
            diag_t.append(jnp.sum(q * kk, axis=-1, keepdims=True))
            qp, ks, blk = q * f, kk, f
            for j in range(SUBLANE_LEVELS):
                m = 1 << j
                z_pair[j].append(jnp.where(bits[j], qp, ks))
                sib = pltpu.roll(blk, m, 0)
                if 2 * m != SUBLANES:
                    sib = jnp.where(bits[j], sib, pltpu.roll(blk, SUBLANES - m, 0))
                qp = qp * jnp.where(bits[j], sib, one)
                ks = ks * jnp.where(bits[j], one, sib)
                blk = blk * sib
            qp_t.append(qp)
            ks_t.append(ks)
            blk_t.append(blk)
        for j in range(SUBLANE_LEVELS):
            z[j].append(jnp.concatenate(z_pair[j], axis=0).astype(BF16))
    yield

    blk_levels, blk_b = [], blk_t
    for j in range(SUBLANE_LEVELS, n_levels):
        blk_levels.append(blk_b)
        blk_b = [blk_b[2 * n] * blk_b[2 * n + 1] for n in range(len(blk_b) // 2)]
    for pair in range(N_TILES // 2):
        z_pair = [[] for _ in range(SUBLANE_LEVELS, n_levels)]
        for i in (2 * pair, 2 * pair + 1):
            for j in range(SUBLANE_LEVELS, n_levels):
                block = i >> (j - SUBLANE_LEVELS)
                sib = blk_levels[j - SUBLANE_LEVELS][block ^ 1]
                if block & 1:
                    z_pair[j - SUBLANE_LEVELS].append(qp_t[i])
                    qp_t[i] = qp_t[i] * sib
                else:
                    z_pair[j - SUBLANE_LEVELS].append(ks_t[i])
                    ks_t[i] = ks_t[i] * sib
        for j in range(SUBLANE_LEVELS, n_levels):
            z[j].append(jnp.concatenate(z_pair[j - SUBLANE_LEVELS], axis=0).astype(BF16))
    q_in = jnp.concatenate(qp_t, axis=0)
    k_out = jnp.concatenate(ks_t, axis=0)
    per_tile = N_TILES // len(blk_b)
    total = jnp.concatenate([blk_b[i // per_tile] for i in range(N_TILES)], axis=0)
    yield

    p = []
    for j in range(n_levels):
        zj = jnp.concatenate(z[j], axis=0)
        p.append(lax.dot_general(zj, zj, _NT, preferred_element_type=F32))
        yield
    sc_t = []
    for i in range(N_TILES):
        rows = slice(i * SUBLANES, (i + 1) * SUBLANES)
        lvl_i = lvl_ref[rows, :]
        acc = jnp.where(lvl_i == n_levels, diag_t[i], 0.0)
        for j in range(n_levels):
            if j < SUBLANE_LEVELS or (i >> (j - SUBLANE_LEVELS)) & 1:
                acc = jnp.where(lvl_i == j, p[j][rows, :], acc)
        sc_t.append(acc)
    sc = jnp.concatenate(sc_t, axis=0).astype(BF16)
    return sc, q_in, k_out, total


def _lockstep(generators, starts=None):
    starts = starts or [0] * len(generators)
    results = [None] * len(generators)
    waiting = sorted(range(len(generators)), key=lambda i: starts[i])
    active = []
    rnd = 0
    while waiting or active:
        while waiting and starts[waiting[0]] <= rnd:
            active.append(waiting.pop(0))
        still = []
        for idx in active:
            try:
                next(generators[idx])
                still.append(idx)
            except StopIteration as done:
                results[idx] = done.value
        active = still
        rnd += 1
    return results


def _hgrn_gates(pq, pf, lb):
    q = _silu(pq)
    forget = lb + (1.0 - lb) * jax.nn.sigmoid(pf)
    return q, 1.0 - forget, forget


def _hgrn_finish(o, pz, g):
    return (_rms_norm(o, g) * _silu(pz)).astype(BF16)


def _mlp_head(hd, u, v_n16, gate, wc_ref, bst):
    mix = jnp.dot(wc_ref[hd], v_n16, preferred_element_type=F32) + bst[:, hd:hd + 1]
    return (_gelu(u) * mix * _silu(gate)).astype(BF16)


def _prompt_kernel(x_ref, xprev_ref, ng_ref, win_ref, lbl_ref, hg_ref, lng_ref, lnb_ref, ws_ref,
                   bst_ref, wout_ref, fg_ref, y_ref, st_ref, vch_ref, proj_a, proj_b, h_ref,
                   mixed_ref, lvl_ref, wc_ref, *, tiles_per_seq):
    g = pl.program_id(0)

    @pl.when(g == 0)
    def _():
        proj_b[...] = jnp.zeros_like(proj_b)
        _store_constants(ROWS, ws_ref, lvl_ref, wc_ref)

    @pl.when(lax.rem(jnp.maximum(g - 1, 0), tiles_per_seq) == 0)
    def _():
        st_ref[...] = jnp.zeros_like(st_ref)

    def step(proj_next, proj_prev):
        h_ref[...] = _rms_norm(x_ref[0], ng_ref[...]).astype(BF16)
        lb = _lower_bound(lbl_ref[...])
        bits, n_levels = _row_bits(), _n_levels(ROWS)
        hg = hg_ref[...]
        lng = lng_ref[...]
        lnb = lnb_ref[...]
        bst = bst_ref[...]
        n_chunks = PROMPT_TILE // ROWS
        n_blocks = D_IN // PROJ_BLOCK

        def col(c, k, hs=slice(0, D_A)):
            return proj_prev[c * ROWS:(c + 1) * ROWS, k * D_A + hs.start:k * D_A + hs.stop]

        def head(c, hd):
            hs = slice(hd * HEAD, (hd + 1) * HEAD)

            def gates(i):
                rows = slice(c * ROWS + i * SUBLANES, c * ROWS + (i + 1) * SUBLANES)
                return _hgrn_gates(proj_prev[rows, hs],
                                   proj_prev[rows, D_A + hd * HEAD:D_A + (hd + 1) * HEAD],
                                   lb[:, hs])

            sc, q_in, k_out, total = yield from _hgrn_head(gates, lvl_ref, bits, n_levels)
            state = st_ref[0, 0, hd]
            vh = col(c, 2, hs)
            o = jnp.dot(jnp.concatenate([sc, q_in.astype(BF16)], axis=1),
                        jnp.concatenate([vh.astype(BF16), state.astype(BF16)], axis=0),
                        preferred_element_type=F32)
            yield
            st_ref[0, 0, hd] = total.T * state + jnp.dot(
                k_out.T.astype(BF16), vh.astype(BF16), preferred_element_type=F32)
            yield
            mixed_ref[c * ROWS:(c + 1) * ROWS, hs] = _hgrn_finish(o, col(c, 3, hs), hg[:, hs])

        def mlp(c):
            v_n = _layer_norm(_gelu(col(c, 5)), lng, lnb)
            if c == n_chunks - 1:
                vch_ref[0, 0] = v_n
            v_n16 = v_n.astype(BF16)
            yield
            for hd in range(N_HEADS):
                hs = slice(hd * HEAD, (hd + 1) * HEAD)
                mixed_ref[c * ROWS:(c + 1) * ROWS, D_A + hd * HEAD:D_A + (hd + 1) * HEAD] = (
                    _mlp_head(hd, col(c, 4, hs), v_n16[:, hs], col(c, 6, hs), wc_ref, bst))
                yield
                yield

        def projection(rounds_per_block):
            for blk in range(n_blocks):
                cols = slice(blk * PROJ_BLOCK, (blk + 1) * PROJ_BLOCK)
                proj_next[:, cols] = jnp.dot(h_ref[...], win_ref[:, cols],
                                             preferred_element_type=F32)
                for _ in range(rounds_per_block):
                    yield

        items = [gen for c in range(n_chunks)
                 for gen in [head(c, hd) for hd in range(N_HEADS)] + [mlp(c)]]
        starts = [c * HEAD_ROUNDS for c in range(n_chunks) for _ in range(N_HEADS + 1)]
        rounds = n_chunks * HEAD_ROUNDS
        _lockstep(items + [projection(max(1, rounds // n_blocks))], starts + [0])

        out = xprev_ref[0] + jnp.dot(mixed_ref[...], wout_ref[...], preferred_element_type=F32)
        y_ref[0] = _rms_norm(out, fg_ref[...])

    parity = lax.rem(g, 2)

    @pl.when(parity == 0)
    def _():
        step(proj_a, proj_b)

    @pl.when(parity == 1)
    def _():
        step(proj_b, proj_a)


def _sample_kernel(x_ref, ng_ref, win_ref, lbl_ref, hg_ref, lng_ref, lnb_ref, ws_ref, bst_ref,
                   wout_ref, fg_ref, stin_ref, y_ref, stout_ref, vch_ref,
                   qin_ref, kot_ref, tot_ref, oint_ref, lvl_ref, wc_ref):
    @pl.when(pl.program_id(0) == 0)
    def _():
        _store_constants(DEC_SEQ, ws_ref, lvl_ref, wc_ref)

    x = x_ref[...]
    h = _rms_norm(x, ng_ref[...]).astype(BF16)
    proj = jnp.dot(h, win_ref[...], preferred_element_type=F32)

    def col(k, hs=slice(0, D_A)):
        return proj[:, k * D_A + hs.start:k * D_A + hs.stop]

    lb = _lower_bound(lbl_ref[...])
    bits, n_levels = _row_bits(), _n_levels(DEC_SEQ)
    vi16 = col(2).astype(BF16)

    def head(hd):
        hs = slice(hd * HEAD, (hd + 1) * HEAD)

        def gates(i):
            rows = slice(i * SUBLANES, (i + 1) * SUBLANES)
            return _hgrn_gates(proj[rows, hs], proj[rows, D_A + hd * HEAD:D_A + (hd + 1) * HEAD],
                               lb[:, hs])

        sc, q_in, k_out, total = yield from _hgrn_head(gates, lvl_ref, bits, n_levels)
        qin_ref[:, hs] = q_in
        kot_ref[hd] = k_out.T
        tot_ref[hd] = total.T
        return jnp.dot(sc, vi16[:, hs], preferred_element_type=F32)

    o_intra = _lockstep([head(hd) for hd in range(N_HEADS)])

    lane = lax.broadcasted_iota(jnp.int32, (HEAD, ROWS), 1)

    def seq_head(i, hd):
        hs = slice(hd * HEAD, (hd + 1) * HEAD)
        seq_rows = slice(i * DEC_SEQ, (i + 1) * DEC_SEQ)
        state = stin_ref[0, i, hd]
        qi = qin_ref[seq_rows, hs].astype(BF16)
        oint_ref[seq_rows, hs] = jnp.dot(qi, state.astype(BF16), preferred_element_type=F32)
        yield
        in_seq = (lane >> (DEC_SEQ.bit_length() - 1)) == i
        k_seq = jnp.where(in_seq, kot_ref[hd], 0.0).astype(BF16)
        decay = tot_ref[hd, :, i * DEC_SEQ:i * DEC_SEQ + 1]
        yield
        stout_ref[0, i, hd] = decay * state + jnp.dot(
            k_seq, vi16[:, hs], preferred_element_type=F32)

    for grp in range(SEQS_PER_TILE // SEQ_GROUP):
        _lockstep([seq_head(grp * SEQ_GROUP + s, hd)
                   for s in range(SEQ_GROUP) for hd in range(N_HEADS)])

    hg = hg_ref[...]
    bst = bst_ref[...]
    v_n = _layer_norm(_gelu(col(5)), lng_ref[...], lnb_ref[...])
    vch_ref[...] = v_n
    v_n16 = v_n.astype(BF16)
    mixed = []
    for hd in range(N_HEADS):
        hs = slice(hd * HEAD, (hd + 1) * HEAD)
        mixed.append(_hgrn_finish(o_intra[hd] + oint_ref[:, hs], col(3, hs), hg[:, hs]))
    for hd in range(N_HEADS):
        hs = slice(hd * HEAD, (hd + 1) * HEAD)
        mixed.append(_mlp_head(hd, col(4, hs), v_n16[:, hs], col(6, hs), wc_ref, bst))
    out = x + jnp.dot(jnp.concatenate(mixed, axis=-1), wout_ref[...], preferred_element_type=F32)
    y_ref[...] = _rms_norm(out, fg_ref[...])


def _const_spec(shape):
    return pl.BlockSpec(shape, lambda *_: (0,) * len(shape), pipeline_mode=pl.Buffered(1))


def _weight_specs(bst_rows):
    return [
        _const_spec((1, D_MODEL)),
        _const_spec((D_MODEL, D_IN)),
        _const_spec((2, D_A)),
        _const_spec((1, D_A)),
        _const_spec((1, D_B)),
        _const_spec((1, D_B)),
        _const_spec((N_HEADS, ROWS, ROWS)),
        _const_spec((bst_rows, N_HEADS)),
        _const_spec((D_MODEL, D_MODEL)),
        _const_spec((1, D_MODEL)),
    ]


def kernel(x_prompt, x_sample, state_hgrn, norm_g, w_in, lb_logits, hgrn_norm_g, sgu_ln_g,
           sgu_ln_b, w_s, b_s, w_out, final_norm_g):
    depth = norm_g.shape[0]
    assert depth == 1 and lb_logits.shape == (2, D_A)
    batch, seq, _ = x_prompt.shape
    dec_batch, dec_seq, _ = x_sample.shape
    assert seq % PROMPT_TILE == 0 and dec_seq == DEC_SEQ and dec_batch % SEQS_PER_TILE == 0

    weights = (norm_g, w_in[0].astype(BF16), lb_logits, hgrn_norm_g, sgu_ln_g, sgu_ln_b)
    tail = (w_out[0].astype(BF16), final_norm_g.reshape(1, D_MODEL))
    params = pltpu.CompilerParams(dimension_semantics=("arbitrary",),
                                  vmem_limit_bytes=VMEM_LIMIT_BYTES)

    tiles_per_seq = seq // PROMPT_TILE
    n_tiles = batch * tiles_per_seq

    def cur_tile(g):
        t = jnp.minimum(g, n_tiles - 1)
        return t // tiles_per_seq, t % tiles_per_seq

    def prev_tile(g):
        t = jnp.maximum(g - 1, 0)
        return t // tiles_per_seq, t % tiles_per_seq

    y_p, st_p, vch_p = pl.pallas_call(
        functools.partial(_prompt_kernel, tiles_per_seq=tiles_per_seq),
        grid=(n_tiles + 1,),
        in_specs=[pl.BlockSpec((1, PROMPT_TILE, D_MODEL), lambda g: (*cur_tile(g), 0)),
                  pl.BlockSpec((1, PROMPT_TILE, D_MODEL), lambda g: (*prev_tile(g), 0))]
        + _weight_specs(ROWS),
        out_specs=[
            pl.BlockSpec((1, PROMPT_TILE, D_MODEL), lambda g: (*prev_tile(g), 0)),
            pl.BlockSpec((1, 1, N_HEADS, HEAD, HEAD), lambda g: (0, prev_tile(g)[0], 0, 0, 0)),
            pl.BlockSpec((1, 1, ROWS, D_B), lambda g: (0, prev_tile(g)[0], 0, 0)),
        ],
        out_shape=[
            jax.ShapeDtypeStruct((batch, seq, D_MODEL), F32),
            jax.ShapeDtypeStruct((1, batch, N_HEADS, HEAD, HEAD), F32),
            jax.ShapeDtypeStruct((1, batch, ROWS, D_B), F32),
        ],
        scratch_shapes=[pltpu.VMEM((PROMPT_TILE, D_IN), F32),
                        pltpu.VMEM((PROMPT_TILE, D_IN), F32),
                        pltpu.VMEM((PROMPT_TILE, D_MODEL), BF16),
                        pltpu.VMEM((PROMPT_TILE, D_MODEL), BF16),
                        pltpu.VMEM((ROWS, ROWS), jnp.int32),
                        pltpu.VMEM((N_HEADS, ROWS, ROWS), BF16)],
        compiler_params=params,
        name="prompt_layer",
    )(x_prompt, x_prompt, *weights, w_s[0], b_s[0].T, *tail)

    ws_dec = jnp.tile(w_s[0, :, :DEC_SEQ, :DEC_SEQ], (1, SEQS_PER_TILE, SEQS_PER_TILE))
    bst_dec = jnp.tile(b_s[0, :, :DEC_SEQ].T, (SEQS_PER_TILE, 1))
    n_rows = dec_batch * DEC_SEQ
    y_s, st_s, vch_s = pl.pallas_call(
        _sample_kernel,
        grid=(n_rows // ROWS,),
        in_specs=[pl.BlockSpec((ROWS, D_MODEL), lambda i: (i, 0))]
        + _weight_specs(ROWS)
        + [pl.BlockSpec((1, SEQS_PER_TILE, N_HEADS, HEAD, HEAD), lambda i: (0, i, 0, 0, 0))],
        out_specs=[
            pl.BlockSpec((ROWS, D_MODEL), lambda i: (i, 0)),
            pl.BlockSpec((1, SEQS_PER_TILE, N_HEADS, HEAD, HEAD), lambda i: (0, i, 0, 0, 0)),
            pl.BlockSpec((ROWS, D_B), lambda i: (i, 0)),
        ],
        out_shape=[
            jax.ShapeDtypeStruct((n_rows, D_MODEL), F32),
            jax.ShapeDtypeStruct((1, dec_batch, N_HEADS, HEAD, HEAD), F32),
            jax.ShapeDtypeStruct((n_rows, D_B), F32),
        ],
        scratch_shapes=[pltpu.VMEM((ROWS, D_A), F32),
                        pltpu.VMEM((N_HEADS, HEAD, ROWS), F32),
                        pltpu.VMEM((N_HEADS, HEAD, ROWS), F32),
                        pltpu.VMEM((ROWS, D_A), F32),
                        pltpu.VMEM((ROWS, ROWS), jnp.int32),
                        pltpu.VMEM((N_HEADS, ROWS, ROWS), BF16)],
        compiler_params=pltpu.CompilerParams(dimension_semantics=("arbitrary",),
                                             vmem_limit_bytes=VMEM_LIMIT_BYTES),
        name="decode_layer",
    )(x_sample.reshape(n_rows, D_MODEL), *weights, ws_dec, bst_dec, *tail, state_hgrn)

    return (y_p,
            y_s.reshape(dec_batch, DEC_SEQ, D_MODEL),
            st_p,
            st_s,
            vch_p.reshape(1, batch, ROWS, N_HEADS, HEAD),
            vch_s.reshape(1, dec_batch, DEC_SEQ, N_HEADS, HEAD))
```

```python
import functools

import jax
import jax.numpy as jnp
from jax import lax
from jax.experimental import pallas as pl
from jax.experimental.pallas import tpu as pltpu

F32 = jnp.float32
BF16 = jnp.bfloat16

D_MODEL = 1024
D_A = 512
D_B = 512
HEAD = 128
N_HEADS = 4
D_IN = 4 * D_A + 3 * D_B
ROWS = 128
SUBLANES = 8
N_TILES = ROWS // SUBLANES
SUBLANE_LEVELS = 3
EPS = 1e-6
PROMPT_TILE = 256
PROJ_BLOCK = 512
HEAD_ROUNDS = 12
OUT_BLOCK = 256
NORM_SLAB = 64
DEC_SEQ = 8
SEQS_PER_TILE = ROWS // DEC_SEQ
SEQ_GROUP = 4
VMEM_LIMIT_BYTES = 56 * 1024 * 1024

_NT = (((1,), (1,)), ((), ()))


def _silu(x):
    return x * jax.nn.sigmoid(x)


def _gelu(x):
    c = -2.0 * 0.7978845608028654 * 1.4426950408889634
    return x / (1.0 + jnp.exp2(x * (c * 0.044715 * (x * x) + c)))


def _rms_norm(x, g):
    return x * lax.rsqrt(jnp.mean(x * x, axis=-1, keepdims=True) + EPS) * g


def _layer_norm(x, g, b):
    mu = jnp.mean(x, axis=-1, keepdims=True)
    xc = x - mu
    var = jnp.mean(xc * xc, axis=-1, keepdims=True)
    return xc * lax.rsqrt(var + EPS) * g + b


def _lower_bound(lbl):
    rows = [lbl[r:r + 1, :] for r in range(lbl.shape[0])]
    mx = functools.reduce(jnp.maximum, rows)
    es = [jnp.exp(r - mx) for r in rows]
    return es[0] / functools.reduce(lambda a, c: a + c, es)


def _n_levels(seg_len):
    n_levels = seg_len.bit_length() - 1
    assert seg_len == 1 << n_levels and n_levels >= SUBLANE_LEVELS
    return n_levels


def _store_constants(seg_len, ws_ref, lvl_ref, wc_ref):
    t = lax.broadcasted_iota(jnp.int32, (ROWS, ROWS), 0)
    s = lax.broadcasted_iota(jnp.int32, (ROWS, ROWS), 1)
    n_levels = _n_levels(seg_len)
    x = t ^ s
    lvl = jnp.where(t == s, n_levels, -1)
    for j in range(n_levels):
        lvl = jnp.where(((x >> j) == 1) & (((t >> j) & 1) == 1), j, lvl)
    lvl_ref[...] = lvl
    causal = (t >= s) & ((t >> n_levels) == (s >> n_levels))
    for hd in range(N_HEADS):
        wc_ref[hd] = jnp.where(causal, ws_ref[hd], 0.0).astype(BF16)


def _row_bits():
    rowi = lax.broadcasted_iota(jnp.int32, (SUBLANES, HEAD), 0)
    return [((rowi >> j) & 1) == 1 for j in range(SUBLANE_LEVELS)]


def _hgrn_head(gates, lvl_ref, bits, n_levels):
    one = jnp.ones((SUBLANES, HEAD), F32)
    z = [[] for _ in range(n_levels)]
    qp_t, ks_t, blk_t, diag_t = [], [], [], []
    for pair in range(N_TILES // 2):
        z_pair = [[] for _ in range(SUBLANE_LEVELS)]
        for i in (2 * pair, 2 * pair + 1):
            q, kk, f = gates(i)
            diag_t.append(jnp.sum(q * kk, axis=-1, keepdims=True))
            qp, ks, blk = q * f, kk, f
            for j in range(SUBLANE_LEVELS):
                m = 1 << j
                z_pair[j].append(jnp.where(bits[j], qp, ks))
                sib = pltpu.roll(blk, m, 0)
                if 2 * m != SUBLANES:
                    sib = jnp.where(bits[j], sib, pltpu.roll(blk, SUBLANES - m, 0))
                qp = qp * jnp.where(bits[j], sib, one)
                ks = ks * jnp.where(bits[j], one, sib)
                blk = blk * sib
            qp_t.append(qp)
            ks_t.append(ks)
            blk_t.append(blk)
        for j in range(SUBLANE_LEVELS):
            z[j].append(jnp.concatenate(z_pair[j], axis=0).astype(BF16))
    yield

    blk_levels, blk_b = [], blk_t
    for j in range(SUBLANE_LEVELS, n_levels):
        blk_levels.append(blk_b)
        blk_b = [blk_b[2 * n] * blk_b[2 * n + 1] for n in range(len(blk_b) // 2)]
    for pair in range(N_TILES // 2):
        z_pair = [[] for _ in range(SUBLANE_LEVELS, n_levels)]
        for i in (2 * pair, 2 * pair + 1):
            for j in range(SUBLANE_LEVELS, n_levels):
                block = i >> (j - SUBLANE_LEVELS)
                sib = blk_levels[j - SUBLANE_LEVELS][block ^ 1]
                if block & 1:
                    z_pair[j - SUBLANE_LEVELS].append(qp_t[i])
                    qp_t[i] = qp_t[i] * sib
                else:
                    z_pair[j - SUBLANE_LEVELS].append(ks_t[i])
                    ks_t[i] = ks_t[i] * sib
        for j in range(SUBLANE_LEVELS, n_levels):
            z[j].append(jnp.concatenate(z_pair[j - SUBLANE_LEVELS], axis=0).astype(BF16))
    q_in = jnp.concatenate(qp_t, axis=0)
    k_out = jnp.concatenate(ks_t, axis=0)
    per_tile = N_TILES // len(blk_b)
    total = jnp.concatenate([blk_b[i // per_tile] for i in range(N_TILES)], axis=0)
    yield

    p = []
    for j in range(n_levels):
        zj = jnp.concatenate(z[j], axis=0)
        p.append(lax.dot_general(zj, zj, _NT, preferred_element_type=F32))
        yield
    sc_t = []
    for i in range(N_TILES):
        rows = slice(i * SUBLANES, (i + 1) * SUBLANES)
        lvl_i = lvl_ref[rows, :]
        acc = jnp.where(lvl_i == n_levels, diag_t[i], 0.0)
        for j in range(n_levels):
            if j < SUBLANE_LEVELS or (i >> (j - SUBLANE_LEVELS)) & 1:
                acc = jnp.where(lvl_i == j, p[j][rows, :], acc)
        sc_t.append(acc)
    sc = jnp.concatenate(sc_t, axis=0).astype(BF16)
    return sc, q_in, k_out, total


def _lockstep(generators, starts=None):
    starts = starts or [0] * len(generators)
    results = [None] * len(generators)
    waiting = sorted(range(len(generators)), key=lambda i: starts[i])
    active = []
    rnd = 0
    while waiting or active:
        while waiting and starts[waiting[0]] <= rnd:
            active.append(waiting.pop(0))
        still = []
        for idx in active:
            try:
                next(generators[idx])
                still.append(idx)
            except StopIteration as done:
                results[idx] = done.value
        active = still
        rnd += 1
    return results


def _hgrn_gates(pq, pf, lb):
    q = _silu(pq)
    forget = lb + (1.0 - lb) * jax.nn.sigmoid(pf)
    return q, 1.0 - forget, forget


def _hgrn_finish(o, pz, g):
    return (_rms_norm(o, g) * _silu(pz)).astype(BF16)


def _mlp_head(hd, u, v_n16, gate, wc_ref, bst):
    mix = jnp.dot(wc_ref[hd], v_n16, preferred_element_type=F32) + bst[:, hd:hd + 1]
    return (_gelu(u) * mix * _silu(gate)).astype(BF16)


def _prompt_kernel(xnext_ref, xold_ref, ng_ref, win_ref, lbl_ref, hg_ref, lng_ref, lnb_ref, ws_ref,
                   bst_ref, wout_ref, fg_ref, y_ref, st_ref, vch_ref, proj_a, proj_b, h_a, h_b,
                   mixed_a, mixed_b, lvl_ref, wc_ref, *, tiles_per_seq):
    g = pl.program_id(0)

    @pl.when(g == 0)
    def _():
        h_a[...] = _rms_norm(xold_ref[0], ng_ref[...]).astype(BF16)
        proj_b[...] = jnp.zeros_like(proj_b)
        mixed_a[...] = jnp.zeros_like(mixed_a)
        _store_constants(ROWS, ws_ref, lvl_ref, wc_ref)

    @pl.when(lax.rem(jnp.maximum(g - 1, 0), tiles_per_seq) == 0)
    def _():
        st_ref[...] = jnp.zeros_like(st_ref)

    def step(h_cur, h_nxt, proj_next, proj_prev, mixed_new, mixed_old):
        lb = _lower_bound(lbl_ref[...])
        bits, n_levels = _row_bits(), _n_levels(ROWS)
        ng = ng_ref[...]
        hg = hg_ref[...]
        lng = lng_ref[...]
        lnb = lnb_ref[...]
        bst = bst_ref[...]
        fg = fg_ref[...]
        n_chunks = PROMPT_TILE // ROWS
        rounds = n_chunks * HEAD_ROUNDS

        def col(c, k, hs=slice(0, D_A)):
            return proj_prev[c * ROWS:(c + 1) * ROWS, k * D_A + hs.start:k * D_A + hs.stop]

        def head(c, hd):
            hs = slice(hd * HEAD, (hd + 1) * HEAD)

            def gates(i):
                rows = slice(c * ROWS + i * SUBLANES, c * ROWS + (i + 1) * SUBLANES)
                return _hgrn_gates(proj_prev[rows, hs],
                                   proj_prev[rows, D_A + hd * HEAD:D_A + (hd + 1) * HEAD],
                                   lb[:, hs])

            sc, q_in, k_out, total = yield from _hgrn_head(gates, lvl_ref, bits, n_levels)
            state = st_ref[0, 0, hd]
            vh = col(c, 2, hs)
            o = jnp.dot(jnp.concatenate([sc, q_in.astype(BF16)], axis=1),
                        jnp.concatenate([vh.astype(BF16), state.astype(BF16)], axis=0),
                        preferred_element_type=F32)
            yield
            st_ref[0, 0, hd] = total.T * state + jnp.dot(
                k_out.T.astype(BF16), vh.astype(BF16), preferred_element_type=F32)
            yield
            mixed_new[c * ROWS:(c + 1) * ROWS, hs] = _hgrn_finish(o, col(c, 3, hs), hg[:, hs])

        def mlp(c):
            v_n = _layer_norm(_gelu(col(c, 5)), lng, lnb)
            if c == n_chunks - 1:
                vch_ref[0, 0] = v_n
            v_n16 = v_n.astype(BF16)
            yield
            for hd in range(N_HEADS):
                hs = slice(hd * HEAD, (hd + 1) * HEAD)
                mixed_new[c * ROWS:(c + 1) * ROWS, D_A + hd * HEAD:D_A + (hd + 1) * HEAD] = (
                    _mlp_head(hd, col(c, 4, hs), v_n16[:, hs], col(c, 6, hs), wc_ref, bst))
                yield
                yield

        def spread(n_pieces):
            return max(1, rounds // n_pieces)

        def in_proj():
            n_blocks = D_IN // PROJ_BLOCK
            for blk in range(n_blocks):
                cols = slice(blk * PROJ_BLOCK, (blk + 1) * PROJ_BLOCK)
                proj_next[:, cols] = jnp.dot(h_cur[...], win_ref[:, cols],
                                             preferred_element_type=F32)
                for _ in range(spread(n_blocks)):
                    yield

        def next_h():
            n_slabs = PROMPT_TILE // NORM_SLAB
            for slab in range(n_slabs):
                rows = slice(slab * NORM_SLAB, (slab + 1) * NORM_SLAB)
                h_nxt[rows, :] = _rms_norm(xnext_ref[0, rows, :], ng).astype(BF16)
                for _ in range(spread(n_slabs)):
                    yield

        def out_proj():
            n_blocks, n_slabs = D_MODEL // OUT_BLOCK, PROMPT_TILE // NORM_SLAB
            for blk in range(n_blocks):
                cols = slice(blk * OUT_BLOCK, (blk + 1) * OUT_BLOCK)
                y_ref[0, :, cols] = xold_ref[0, :, cols] + jnp.dot(
                    mixed_old[...], wout_ref[:, cols], preferred_element_type=F32)
                for _ in range(spread(2 * n_blocks)):
                    yield
            for slab in range(n_slabs):
                rows = slice(slab * NORM_SLAB, (slab + 1) * NORM_SLAB)
                y_ref[0, rows, :] = _rms_norm(y_ref[0, rows, :], fg)
                for _ in range(spread(2 * n_slabs)):
                    yield

        items = [gen for c in range(n_chunks)
                 for gen in [head(c, hd) for hd in range(N_HEADS)] + [mlp(c)]]
        starts = [c * HEAD_ROUNDS for c in range(n_chunks) for _ in range(N_HEADS + 1)]
        _lockstep(items + [in_proj(), out_proj(), next_h()], starts + [0, 0, 0])

    parity = lax.rem(g, 2)

    @pl.when(parity == 0)
    def _():
        step(h_a, h_b, proj_a, proj_b, mixed_b, mixed_a)

    @pl.when(parity == 1)
    def _():
        step(h_b, h_a, proj_b, proj_a, mixed_a, mixed_b)


def _sample_kernel(x_ref, ng_ref, win_ref, lbl_ref, hg_ref, lng_ref, lnb_ref, ws_ref, bst_ref,
                   wout_ref, fg_ref, stin_ref, y_ref, stout_ref, vch_ref,
                   qin_ref, kot_ref, tot_ref, oint_ref, lvl_ref, wc_ref):
    @pl.when(pl.program_id(0) == 0)
    def _():
        _store_constants(DEC_SEQ, ws_ref, lvl_ref, wc_ref)

    x = x_ref[...]
    h = _rms_norm(x, ng_ref[...]).astype(BF16)
    proj = jnp.dot(h, win_ref[...], preferred_element_type=F32)

    def col(k, hs=slice(0, D_A)):
        return proj[:, k * D_A + hs.start:k * D_A + hs.stop]

    lb = _lower_bound(lbl_ref[...])
    bits, n_levels = _row_bits(), _n_levels(DEC_SEQ)
    vi16 = col(2).astype(BF16)

    def head(hd):
        hs = slice(hd * HEAD, (hd + 1) * HEAD)

        def gates(i):
            rows = slice(i * SUBLANES, (i + 1) * SUBLANES)
            return _hgrn_gates(proj[rows, hs], proj[rows, D_A + hd * HEAD:D_A + (hd + 1) * HEAD],
                               lb[:, hs])

        sc, q_in, k_out, total = yield from _hgrn_head(gates, lvl_ref, bits, n_levels)
        qin_ref[:, hs] = q_in
        kot_ref[hd] = k_out.T
        tot_ref[hd] = total.T
        return jnp.dot(sc, vi16[:, hs], preferred_element_type=F32)

    o_intra = _lockstep([head(hd) for hd in range(N_HEADS)])

    lane = lax.broadcasted_iota(jnp.int32, (HEAD, ROWS), 1)

    def seq_head(i, hd):
        hs = slice(hd * HEAD, (hd + 1) * HEAD)
        seq_rows = slice(i * DEC_SEQ, (i + 1) * DEC_SEQ)
        state = stin_ref[0, i, hd]
        qi = qin_ref[seq_rows, hs].astype(BF16)
        oint_ref[seq_rows, hs] = jnp.dot(qi, state.astype(BF16), preferred_element_type=F32)
        yield
        in_seq = (lane >> (DEC_SEQ.bit_length() - 1)) == i
        k_seq = jnp.where(in_seq, kot_ref[hd], 0.0).astype(BF16)
        decay = tot_ref[hd, :, i * DEC_SEQ:i * DEC_SEQ + 1]
        yield
        stout_ref[0, i, hd] = decay * state + jnp.dot(
            k_seq, vi16[:, hs], preferred_element_type=F32)

    for grp in range(SEQS_PER_TILE // SEQ_GROUP):
        _lockstep([seq_head(grp * SEQ_GROUP + s, hd)
                   for s in range(SEQ_GROUP) for hd in range(N_HEADS)])

    hg = hg_ref[...]
    bst = bst_ref[...]
    v_n = _layer_norm(_gelu(col(5)), lng_ref[...], lnb_ref[...])
    vch_ref[...] = v_n
    v_n16 = v_n.astype(BF16)
    mixed = []
    for hd in range(N_HEADS):
        hs = slice(hd * HEAD, (hd + 1) * HEAD)
        mixed.append(_hgrn_finish(o_intra[hd] + oint_ref[:, hs], col(3, hs), hg[:, hs]))
    for hd in range(N_HEADS):
        hs = slice(hd * HEAD, (hd + 1) * HEAD)
        mixed.append(_mlp_head(hd, col(4, hs), v_n16[:, hs], col(6, hs), wc_ref, bst))
    out = x + jnp.dot(jnp.concatenate(mixed, axis=-1), wout_ref[...], preferred_element_type=F32)
    y_ref[...] = _rms_norm(out, fg_ref[...])


def _const_spec(shape):
    return pl.BlockSpec(shape, lambda *_: (0,) * len(shape), pipeline_mode=pl.Buffered(1))


def _weight_specs(bst_rows):
    return [
        _const_spec((1, D_MODEL)),
        _const_spec((D_MODEL, D_IN)),
        _const_spec((2, D_A)),
        _const_spec((1, D_A)),
        _const_spec((1, D_B)),
        _const_spec((1, D_B)),
        _const_spec((N_HEADS, ROWS, ROWS)),
        _const_spec((bst_rows, N_HEADS)),
        _const_spec((D_MODEL, D_MODEL)),
        _const_spec((1, D_MODEL)),
    ]


def kernel(x_prompt, x_sample, state_hgrn, norm_g, w_in, lb_logits, hgrn_norm_g, sgu_ln_g,
           sgu_ln_b, w_s, b_s, w_out, final_norm_g):
    depth = norm_g.shape[0]
    assert depth == 1 and lb_logits.shape == (2, D_A)
    batch, seq, _ = x_prompt.shape
    dec_batch, dec_seq, _ = x_sample.shape
    assert seq % PROMPT_TILE == 0 and dec_seq == DEC_SEQ and dec_batch % SEQS_PER_TILE == 0

    weights = (norm_g, w_in[0].astype(BF16), lb_logits, hgrn_norm_g, sgu_ln_g, sgu_ln_b)
    tail = (w_out[0].astype(BF16), final_norm_g.reshape(1, D_MODEL))
    params = pltpu.CompilerParams(dimension_semantics=("arbitrary",),
                                  vmem_limit_bytes=VMEM_LIMIT_BYTES)

    tiles_per_seq = seq // PROMPT_TILE
    n_tiles = batch * tiles_per_seq

    def tile(t):
        t = jnp.clip(t, 0, n_tiles - 1)
        return t // tiles_per_seq, t % tiles_per_seq

    def mixed_seq(g):
        return jnp.maximum(g - 1, 0) // tiles_per_seq

    y_p, st_p, vch_p = pl.pallas_call(
        functools.partial(_prompt_kernel, tiles_per_seq=tiles_per_seq),
        grid=(n_tiles + 2,),
        in_specs=[pl.BlockSpec((1, PROMPT_TILE, D_MODEL), lambda g: (*tile(g + 1), 0)),
                  pl.BlockSpec((1, PROMPT_TILE, D_MODEL), lambda g: (*tile(g - 2), 0))]
        + _weight_specs(ROWS),
        out_specs=[
            pl.BlockSpec((1, PROMPT_TILE, D_MODEL), lambda g: (*tile(g - 2), 0)),
            pl.BlockSpec((1, 1, N_HEADS, HEAD, HEAD), lambda g: (0, mixed_seq(g), 0, 0, 0)),
            pl.BlockSpec((1, 1, ROWS, D_B), lambda g: (0, mixed_seq(g), 0, 0)),
        ],
        out_shape=[
            jax.ShapeDtypeStruct((batch, seq, D_MODEL), F32),
            jax.ShapeDtypeStruct((1, batch + 1, N_HEADS, HEAD, HEAD), F32),
            jax.ShapeDtypeStruct((1, batch + 1, ROWS, D_B), F32),
        ],
        scratch_shapes=[pltpu.VMEM((PROMPT_TILE, D_IN), F32),
                        pltpu.VMEM((PROMPT_TILE, D_IN), F32),
                        pltpu.VMEM((PROMPT_TILE, D_MODEL), BF16),
                        pltpu.VMEM((PROMPT_TILE, D_MODEL), BF16),
                        pltpu.VMEM((PROMPT_TILE, D_MODEL), BF16),
                        pltpu.VMEM((PROMPT_TILE, D_MODEL), BF16),
                        pltpu.VMEM((ROWS, ROWS), jnp.int32),
                        pltpu.VMEM((N_HEADS, ROWS, ROWS), BF16)],
        compiler_params=params,
        name="prompt_layer",
    )(x_prompt, x_prompt, *weights, w_s[0], b_s[0].T, *tail)

    ws_dec = jnp.tile(w_s[0, :, :DEC_SEQ, :DEC_SEQ], (1, SEQS_PER_TILE, SEQS_PER_TILE))
    bst_dec = jnp.tile(b_s[0, :, :DEC_SEQ].T, (SEQS_PER_TILE, 1))
    n_rows = dec_batch * DEC_SEQ
    y_s, st_s, vch_s = pl.pallas_call(
        _sample_kernel,
        grid=(n_rows // ROWS,),
        in_specs=[pl.BlockSpec((ROWS, D_MODEL), lambda i: (i, 0))]
        + _weight_specs(ROWS)
        + [pl.BlockSpec((1, SEQS_PER_TILE, N_HEADS, HEAD, HEAD), lambda i: (0, i, 0, 0, 0))],
        out_specs=[
            pl.BlockSpec((ROWS, D_MODEL), lambda i: (i, 0)),
            pl.BlockSpec((1, SEQS_PER_TILE, N_HEADS, HEAD, HEAD), lambda i: (0, i, 0, 0, 0)),
            pl.BlockSpec((ROWS, D_B), lambda i: (i, 0)),
        ],
        out_shape=[
            jax.ShapeDtypeStruct((n_rows, D_MODEL), F32),
            jax.ShapeDtypeStruct((1, dec_batch, N_HEADS, HEAD, HEAD), F32),
            jax.ShapeDtypeStruct((n_rows, D_B), F32),
        ],
        scratch_shapes=[pltpu.VMEM((ROWS, D_A), F32),
                        pltpu.VMEM((N_HEADS, HEAD, ROWS), F32),
                        pltpu.VMEM((N_HEADS, HEAD, ROWS), F32),
                        pltpu.VMEM((ROWS, D_A), F32),
                        pltpu.VMEM((ROWS, ROWS), jnp.int32),
                        pltpu.VMEM((N_HEADS, ROWS, ROWS), BF16)],
        compiler_params=pltpu.CompilerParams(dimension_semantics=("arbitrary",),
                                             vmem_limit_bytes=VMEM_LIMIT_BYTES),
        name="decode_layer",
    )(x_sample.reshape(n_rows, D_MODEL), *weights, ws_dec, bst_dec, *tail, state_hgrn)

    return (y_p,
            y_s.reshape(dec_batch, DEC_SEQ, D_MODEL),
            st_p[:, :batch],
            st_s,
            vch_p[:, :batch].reshape(1, batch, ROWS, N_HEADS, HEAD),
            vch_s.reshape(1, dec_batch, DEC_SEQ, N_HEADS, HEAD))
```

```python
import functools

import jax
import jax.numpy as jnp
from jax import lax
from jax.experimental import pallas as pl
from jax.experimental.pallas import tpu as pltpu

F32 = jnp.float32
BF16 = jnp.bfloat16

D_MODEL = 1024
D_A = 512
D_B = 512
HEAD = 128
N_HEADS = 4
D_IN = 4 * D_A + 3 * D_B
ROWS = 128
SUBLANES = 8
PACKED = 2 * SUBLANES
N_TILES = ROWS // SUBLANES
SUBLANE_LEVELS = 3
EPS = 1e-6
PROMPT_TILE = 256
PROJ_BLOCK = 512
HEAD_ROUNDS = 12
DEC_SEQ = 8
SEQS_PER_TILE = ROWS // DEC_SEQ
SEQ_GROUP = 4
VMEM_LIMIT_BYTES = 56 * 1024 * 1024

_NT = (((1,), (1,)), ((), ()))


def _silu(x):
    return x * jax.nn.sigmoid(x)


def _gelu(x):
    c = -2.0 * 0.7978845608028654 * 1.4426950408889634
    return x / (1.0 + jnp.exp2(x * (c * 0.044715 * (x * x) + c)))


def _rms_norm(x, g):
    return x * lax.rsqrt(jnp.mean(x * x, axis=-1, keepdims=True) + EPS) * g


def _layer_norm(x, g, b):
    mu = jnp.mean(x, axis=-1, keepdims=True)
    xc = x - mu
    var = jnp.mean(xc * xc, axis=-1, keepdims=True)
    return xc * lax.rsqrt(var + EPS) * g + b


def _lower_bound(lbl):
    rows = [lbl[r:r + 1, :] for r in range(lbl.shape[0])]
    mx = functools.reduce(jnp.maximum, rows)
    es = [jnp.exp(r - mx) for r in rows]
    return es[0] / functools.reduce(lambda a, c: a + c, es)


def _n_levels(seg_len):
    n_levels = seg_len.bit_length() - 1
    assert seg_len == 1 << n_levels and n_levels >= SUBLANE_LEVELS
    return n_levels


def _split3(x):
    hi = x.astype(BF16)
    r1 = x - hi.astype(F32)
    mid = r1.astype(BF16)
    lo = (r1 - mid.astype(F32)).astype(BF16)
    return hi, mid, lo


def _store_constants(seg_len, ws_ref, bs_ref, wout_ref, lvl_ref, wc_ref, bias_ref, wout16_ref):
    t = lax.broadcasted_iota(jnp.int32, (ROWS, ROWS), 0)
    s = lax.broadcasted_iota(jnp.int32, (ROWS, ROWS), 1)
    n_levels = _n_levels(seg_len)
    x = t ^ s
    lvl = jnp.where(t == s, n_levels, -1)
    for j in range(n_levels):
        lvl = jnp.where(((x >> j) == 1) & (((t >> j) & 1) == 1), j, lvl)
    lvl_ref[...] = lvl
    causal = (t >= s) & ((t >> n_levels) == (s >> n_levels))
    pick = jnp.where(s == (t & (seg_len - 1)), 1.0, 0.0).astype(BF16)
    for hd in range(N_HEADS):
        w = ws_ref[hd].astype(BF16)
        if seg_len != ROWS:
            w = jnp.dot(pick, w, preferred_element_type=F32).astype(BF16)
            w = lax.dot_general(w, pick, _NT, preferred_element_type=F32).astype(BF16)
        wc_ref[hd] = jnp.where(causal, w, jnp.zeros_like(w))
    pad = jnp.zeros((PACKED - N_HEADS, ROWS), F32)
    bias = sum(lax.dot_general(pick, part, _NT, preferred_element_type=F32)
               for part in _split3(jnp.concatenate([bs_ref[...], pad], axis=0)))
    bias_ref[...] = bias[:, :N_HEADS]
    wout16_ref[...] = wout_ref[...].astype(BF16)


def _row_bits():
    rowi = lax.broadcasted_iota(jnp.int32, (SUBLANES, HEAD), 0)
    return [((rowi >> j) & 1) == 1 for j in range(SUBLANE_LEVELS)]


def _hgrn_head(gates, lvl_ref, bits, n_levels):
    one = jnp.ones((SUBLANES, HEAD), F32)
    z = [[] for _ in range(n_levels)]
    qp_t, ks_t, blk_t, diag_t = [], [], [], []
    for pair in range(N_TILES // 2):
        z_pair = [[] for _ in range(SUBLANE_LEVELS)]
        for i in (2 * pair, 2 * pair + 1):
            q, kk, f = gates(i)
            diag_t.append(jnp.sum(q * kk, axis=-1, keepdims=True))
            qp, ks, blk = q * f, kk, f
            for j in range(SUBLANE_LEVELS):
                m = 1 << j
                z_pair[j].append(jnp.where(bits[j], qp, ks))
                sib = pltpu.roll(blk, m, 0)
                if 2 * m != SUBLANES:
                    sib = jnp.where(bits[j], sib, pltpu.roll(blk, SUBLANES - m, 0))
                qp = qp * jnp.where(bits[j], sib, one)
                ks = ks * jnp.where(bits[j], one, sib)
                blk = blk * sib
            qp_t.append(qp)
            ks_t.append(ks)
            blk_t.append(blk)
        for j in range(SUBLANE_LEVELS):
            z[j].append(jnp.concatenate(z_pair[j], axis=0).astype(BF16))
    yield

    blk_levels, blk_b = [], blk_t
    for j in range(SUBLANE_LEVELS, n_levels):
        blk_levels.append(blk_b)
        blk_b = [blk_b[2 * n] * blk_b[2 * n + 1] for n in range(len(blk_b) // 2)]
    for pair in range(N_TILES // 2):
        z_pair = [[] for _ in range(SUBLANE_LEVELS, n_levels)]
        for i in (2 * pair, 2 * pair + 1):
            for j in range(SUBLANE_LEVELS, n_levels):
                block = i >> (j - SUBLANE_LEVELS)
                sib = blk_levels[j - SUBLANE_LEVELS][block ^ 1]
                if block & 1:
                    z_pair[j - SUBLANE_LEVELS].append(qp_t[i])
                    qp_t[i] = qp_t[i] * sib
                else:
                    z_pair[j - SUBLANE_LEVELS].append(ks_t[i])
                    ks_t[i] = ks_t[i] * sib
        for j in range(SUBLANE_LEVELS, n_levels):
            z[j].append(jnp.concatenate(z_pair[j - SUBLANE_LEVELS], axis=0).astype(BF16))
    q_in = jnp.concatenate(qp_t, axis=0)
    k_out = jnp.concatenate(ks_t, axis=0)
    per_tile = N_TILES // len(blk_b)
    total = jnp.concatenate([blk_b[i // per_tile] for i in range(N_TILES)], axis=0)
    yield

    p, p_row = [], []
    for j in range(n_levels):
        zj = jnp.concatenate(z[j], axis=0)
        span = 1 << max(j - SUBLANE_LEVELS, 0)
        if span * SUBLANES >= PACKED:
            upper = [i for i in range(N_TILES) if (i // span) & 1]
            lhs = jnp.concatenate([z[j][i // 2] for i in upper[::2]], axis=0)
        else:
            upper = list(range(N_TILES))
            lhs = zj
        p.append(lax.dot_general(lhs, zj, _NT, preferred_element_type=F32))
        p_row.append({i: n * SUBLANES for n, i in enumerate(upper)})
        yield
    sc_t = []
    for i in range(N_TILES):
        lvl_i = lvl_ref[i * SUBLANES:(i + 1) * SUBLANES, :]
        acc = jnp.where(lvl_i == n_levels, diag_t[i], 0.0)
        for j in range(n_levels):
            if j < SUBLANE_LEVELS or (i >> (j - SUBLANE_LEVELS)) & 1:
                r0 = p_row[j][i]
                acc = jnp.where(lvl_i == j, p[j][r0:r0 + SUBLANES, :], acc)
        sc_t.append(acc)
    sc = jnp.concatenate(sc_t, axis=0).astype(BF16)
    return sc, q_in, k_out, total


def _lockstep(generators, starts=None):
    starts = starts or [0] * len(generators)
    results = [None] * len(generators)
    waiting = sorted(range(len(generators)), key=lambda i: starts[i])
    active = []
    rnd = 0
    while waiting or active:
        while waiting and starts[waiting[0]] <= rnd:
            active.append(waiting.pop(0))
        still = []
        for idx in active:
            try:
                next(generators[idx])
                still.append(idx)
            except StopIteration as done:
                results[idx] = done.value
        active = still
        rnd += 1
    return results


def _hgrn_gates(pq, pf, lb):
    q = _silu(pq)
    forget = lb + (1.0 - lb) * jax.nn.sigmoid(pf)
    return q, 1.0 - forget, forget


def _hgrn_finish(o, pz, g):
    return (_rms_norm(o, g) * _silu(pz)).astype(BF16)


def _mlp_head(hd, u, v_n16, gate, wc_ref, bias):
    mix = jnp.dot(wc_ref[hd], v_n16, preferred_element_type=F32) + bias[:, hd:hd + 1]
    return (_gelu(u) * mix * _silu(gate)).astype(BF16)


def _prompt_kernel(x_ref, xprev_ref, ng_ref, win_ref, lbl_ref, hg_ref, lng_ref, lnb_ref, ws_ref,
                   bs_ref, wout_ref, fg_ref, y_ref, st_ref, vch_ref, proj_a, proj_b, h_ref,
                   mixed_ref, lvl_ref, wc_ref, bias_ref, wout16_ref, *, tiles_per_seq):
    g = pl.program_id(0)

    @pl.when(g == 0)
    def _():
        proj_b[...] = jnp.zeros_like(proj_b)
        _store_constants(ROWS, ws_ref, bs_ref, wout_ref, lvl_ref, wc_ref, bias_ref, wout16_ref)

    @pl.when(lax.rem(jnp.maximum(g - 1, 0), tiles_per_seq) == 0)
    def _():
        st_ref[...] = jnp.zeros_like(st_ref)

    def step(proj_next, proj_prev):
        h_ref[...] = _rms_norm(x_ref[0], ng_ref[...]).astype(BF16)
        lb = _lower_bound(lbl_ref[...])
        bits, n_levels = _row_bits(), _n_levels(ROWS)
        hg = hg_ref[...]
        lng = lng_ref[...]
        lnb = lnb_ref[...]
        bias = bias_ref[...]
        n_chunks = PROMPT_TILE // ROWS
        n_blocks = D_IN // PROJ_BLOCK

        def col(c, k, hs=slice(0, D_A)):
            return proj_prev[c * ROWS:(c + 1) * ROWS, k * D_A + hs.start:k * D_A + hs.stop]

        def head(c, hd):
            hs = slice(hd * HEAD, (hd + 1) * HEAD)

            def gates(i):
                rows = slice(c * ROWS + i * SUBLANES, c * ROWS + (i + 1) * SUBLANES)
                return _hgrn_gates(proj_prev[rows, hs],
                                   proj_prev[rows, D_A + hd * HEAD:D_A + (hd + 1) * HEAD],
                                   lb[:, hs])

            sc, q_in, k_out, total = yield from _hgrn_head(gates, lvl_ref, bits, n_levels)
            state = st_ref[0, 0, hd]
            vh = col(c, 2, hs)
            o = jnp.dot(jnp.concatenate([sc, q_in.astype(BF16)], axis=1),
                        jnp.concatenate([vh.astype(BF16), state.astype(BF16)], axis=0),
                        preferred_element_type=F32)
            yield
            st_ref[0, 0, hd] = total.T * state + jnp.dot(
                k_out.T.astype(BF16), vh.astype(BF16), preferred_element_type=F32)
            yield
            mixed_ref[c * ROWS:(c + 1) * ROWS, hs] = _hgrn_finish(o, col(c, 3, hs), hg[:, hs])

        def mlp(c):
            v_n = _layer_norm(_gelu(col(c, 5)), lng, lnb)
            if c == n_chunks - 1:
                vch_ref[0, 0] = v_n
            v_n16 = v_n.astype(BF16)
            yield
            for hd in range(N_HEADS):
                hs = slice(hd * HEAD, (hd + 1) * HEAD)
                mixed_ref[c * ROWS:(c + 1) * ROWS, D_A + hd * HEAD:D_A + (hd + 1) * HEAD] = (
                    _mlp_head(hd, col(c, 4, hs), v_n16[:, hs], col(c, 6, hs), wc_ref, bias))
                yield
                yield

        def projection(rounds_per_block):
            for blk in range(n_blocks):
                cols = slice(blk * PROJ_BLOCK, (blk + 1) * PROJ_BLOCK)
                proj_next[:, cols] = jnp.dot(h_ref[...], win_ref[:, cols],
                                             preferred_element_type=F32)
                for _ in range(rounds_per_block):
                    yield

        items = [gen for c in range(n_chunks)
                 for gen in [head(c, hd) for hd in range(N_HEADS)] + [mlp(c)]]
        starts = [c * HEAD_ROUNDS for c in range(n_chunks) for _ in range(N_HEADS + 1)]
        rounds = n_chunks * HEAD_ROUNDS
        _lockstep(items + [projection(max(1, rounds // n_blocks))], starts + [0])

        out = xprev_ref[0] + jnp.dot(mixed_ref[...], wout16_ref[...], preferred_element_type=F32)
        y_ref[0] = _rms_norm(out, fg_ref[...])

    parity = lax.rem(g, 2)

    @pl.when(parity == 0)
    def _():
        step(proj_a, proj_b)

    @pl.when(parity == 1)
    def _():
        step(proj_b, proj_a)


def _sample_kernel(x_ref, ng_ref, win_ref, lbl_ref, hg_ref, lng_ref, lnb_ref, ws_ref, bs_ref,
                   wout_ref, fg_ref, stin_ref, y_ref, stout_ref, vch_ref,
                   qin_ref, kot_ref, tot_ref, oint_ref, lvl_ref, wc_ref, bias_ref, wout16_ref):
    @pl.when(pl.program_id(0) == 0)
    def _():
        _store_constants(DEC_SEQ, ws_ref, bs_ref, wout_ref, lvl_ref, wc_ref, bias_ref, wout16_ref)

    x = x_ref[...]
    h = _rms_norm(x, ng_ref[...]).astype(BF16)
    proj = jnp.dot(h, win_ref[...], preferred_element_type=F32)

    def col(k, hs=slice(0, D_A)):
        return proj[:, k * D_A + hs.start:k * D_A + hs.stop]

    lb = _lower_bound(lbl_ref[...])
    bits, n_levels = _row_bits(), _n_levels(DEC_SEQ)
    vi16 = col(2).astype(BF16)

    def head(hd):
        hs = slice(hd * HEAD, (hd + 1) * HEAD)

        def gates(i):
            rows = slice(i * SUBLANES, (i + 1) * SUBLANES)
            return _hgrn_gates(proj[rows, hs], proj[rows, D_A + hd * HEAD:D_A + (hd + 1) * HEAD],
                               lb[:, hs])

        sc, q_in, k_out, total = yield from _hgrn_head(gates, lvl_ref, bits, n_levels)
        qin_ref[:, hs] = q_in
        kot_ref[hd] = k_out.T
        tot_ref[hd] = total.T
        return jnp.dot(sc, vi16[:, hs], preferred_element_type=F32)

    o_intra = _lockstep([head(hd) for hd in range(N_HEADS)])

    lane = lax.broadcasted_iota(jnp.int32, (HEAD, ROWS), 1)

    def seq_head(i, hd):
        hs = slice(hd * HEAD, (hd + 1) * HEAD)
        seq_rows = slice(i * DEC_SEQ, (i + 1) * DEC_SEQ)
        state = stin_ref[0, i, hd]
        qi = qin_ref[seq_rows, hs].astype(BF16)
        oint_ref[seq_rows, hs] = jnp.dot(qi, state.astype(BF16), preferred_element_type=F32)
        yield
        in_seq = (lane >> (DEC_SEQ.bit_length() - 1)) == i
        k_seq = jnp.where(in_seq, kot_ref[hd], 0.0).astype(BF16)
        decay = tot_ref[hd, :, i * DEC_SEQ:i * DEC_SEQ + 1]
        yield
        stout_ref[0, i, hd] = decay * state + jnp.dot(
            k_seq, vi16[:, hs], preferred_element_type=F32)

    for grp in range(SEQS_PER_TILE // SEQ_GROUP):
        _lockstep([seq_head(grp * SEQ_GROUP + s, hd)
                   for s in range(SEQ_GROUP) for hd in range(N_HEADS)])

    hg = hg_ref[...]
    bias = bias_ref[...]
    v_n = _layer_norm(_gelu(col(5)), lng_ref[...], lnb_ref[...])
    vch_ref[...] = v_n
    v_n16 = v_n.astype(BF16)
    mixed = []
    for hd in range(N_HEADS):
        hs = slice(hd * HEAD, (hd + 1) * HEAD)
        mixed.append(_hgrn_finish(o_intra[hd] + oint_ref[:, hs], col(3, hs), hg[:, hs]))
    for hd in range(N_HEADS):
        hs = slice(hd * HEAD, (hd + 1) * HEAD)
        mixed.append(_mlp_head(hd, col(4, hs), v_n16[:, hs], col(6, hs), wc_ref, bias))
    out = x + jnp.dot(jnp.concatenate(mixed, axis=-1), wout16_ref[...],
                      preferred_element_type=F32)
    y_ref[...] = _rms_norm(out, fg_ref[...])


def _const_spec(shape):
    return pl.BlockSpec(shape, lambda *_: (0,) * len(shape), pipeline_mode=pl.Buffered(1))


_WEIGHT_SPECS = [
    _const_spec((1, D_MODEL)),
    _const_spec((D_MODEL, D_IN)),
    _const_spec((2, D_A)),
    _const_spec((1, D_A)),
    _const_spec((1, D_B)),
    _const_spec((1, D_B)),
    _const_spec((N_HEADS, ROWS, ROWS)),
    _const_spec((N_HEADS, ROWS)),
    _const_spec((D_MODEL, D_MODEL)),
    _const_spec((1, D_MODEL)),
]

_CONSTANT_SCRATCH = [
    pltpu.VMEM((ROWS, ROWS), jnp.int32),
    pltpu.VMEM((N_HEADS, ROWS, ROWS), BF16),
    pltpu.VMEM((ROWS, N_HEADS), F32),
    pltpu.VMEM((D_MODEL, D_MODEL), BF16),
]


def kernel(x_prompt, x_sample, state_hgrn, norm_g, w_in, lb_logits, hgrn_norm_g, sgu_ln_g,
           sgu_ln_b, w_s, b_s, w_out, final_norm_g):
    depth = norm_g.shape[0]
    assert depth == 1 and lb_logits.shape == (2, D_A)
    batch, seq, _ = x_prompt.shape
    dec_batch, dec_seq, _ = x_sample.shape
    assert seq % PROMPT_TILE == 0 and dec_seq == DEC_SEQ and dec_batch % SEQS_PER_TILE == 0

    weights = (norm_g, w_in[0].astype(BF16), lb_logits, hgrn_norm_g, sgu_ln_g, sgu_ln_b,
               w_s[0], b_s[0], w_out[0], final_norm_g.reshape(1, D_MODEL))
    params = pltpu.CompilerParams(dimension_semantics=("arbitrary",),
                                  vmem_limit_bytes=VMEM_LIMIT_BYTES)

    tiles_per_seq = seq // PROMPT_TILE
    n_tiles = batch * tiles_per_seq

    def cur_tile(g):
        t = jnp.minimum(g, n_tiles - 1)
        return t // tiles_per_seq, t % tiles_per_seq

    def prev_tile(g):
        t = jnp.maximum(g - 1, 0)
        return t // tiles_per_seq, t % tiles_per_seq

    y_p, st_p, vch_p = pl.pallas_call(
        functools.partial(_prompt_kernel, tiles_per_seq=tiles_per_seq),
        grid=(n_tiles + 1,),
        in_specs=[pl.BlockSpec((1, PROMPT_TILE, D_MODEL), lambda g: (*cur_tile(g), 0)),
                  pl.BlockSpec((1, PROMPT_TILE, D_MODEL), lambda g: (*prev_tile(g), 0))]
        + _WEIGHT_SPECS,
        out_specs=[
            pl.BlockSpec((1, PROMPT_TILE, D_MODEL), lambda g: (*prev_tile(g), 0)),
            pl.BlockSpec((1, 1, N_HEADS, HEAD, HEAD), lambda g: (0, prev_tile(g)[0], 0, 0, 0)),
            pl.BlockSpec((1, 1, ROWS, D_B), lambda g: (0, prev_tile(g)[0], 0, 0)),
        ],
        out_shape=[
            jax.ShapeDtypeStruct((batch, seq, D_MODEL), F32),
            jax.ShapeDtypeStruct((1, batch, N_HEADS, HEAD, HEAD), F32),
            jax.ShapeDtypeStruct((1, batch, ROWS, D_B), F32),
        ],
        scratch_shapes=[pltpu.VMEM((PROMPT_TILE, D_IN), F32),
                        pltpu.VMEM((PROMPT_TILE, D_IN), F32),
                        pltpu.VMEM((PROMPT_TILE, D_MODEL), BF16),
                        pltpu.VMEM((PROMPT_TILE, D_MODEL), BF16)] + _CONSTANT_SCRATCH,
        compiler_params=params,
        name="prompt_layer",
    )(x_prompt, x_prompt, *weights)

    n_rows = dec_batch * DEC_SEQ
    y_s, st_s, vch_s = pl.pallas_call(
        _sample_kernel,
        grid=(n_rows // ROWS,),
        in_specs=[pl.BlockSpec((ROWS, D_MODEL), lambda i: (i, 0))]
        + _WEIGHT_SPECS
        + [pl.BlockSpec((1, SEQS_PER_TILE, N_HEADS, HEAD, HEAD), lambda i: (0, i, 0, 0, 0))],
        out_specs=[
            pl.BlockSpec((ROWS, D_MODEL), lambda i: (i, 0)),
            pl.BlockSpec((1, SEQS_PER_TILE, N_HEADS, HEAD, HEAD), lambda i: (0, i, 0, 0, 0)),
            pl.BlockSpec((ROWS, D_B), lambda i: (i, 0)),
        ],
        out_shape=[
            jax.ShapeDtypeStruct((n_rows, D_MODEL), F32),
            jax.ShapeDtypeStruct((1, dec_batch, N_HEADS, HEAD, HEAD), F32),
            jax.ShapeDtypeStruct((n_rows, D_B), F32),
        ],
        scratch_shapes=[pltpu.VMEM((ROWS, D_A), F32),
                        pltpu.VMEM((N_HEADS, HEAD, ROWS), F32),
                        pltpu.VMEM((N_HEADS, HEAD, ROWS), F32),
                        pltpu.VMEM((ROWS, D_A), F32)] + _CONSTANT_SCRATCH,
        compiler_params=pltpu.CompilerParams(dimension_semantics=("arbitrary",),
                                             vmem_limit_bytes=VMEM_LIMIT_BYTES),
        name="decode_layer",
    )(x_sample.reshape(n_rows, D_MODEL), *weights, state_hgrn)

    return (y_p,
            y_s.reshape(dec_batch, DEC_SEQ, D_MODEL),
            st_p,
            st_s,
            vch_p.reshape(1, batch, ROWS, N_HEADS, HEAD),
            vch_s.reshape(1, dec_batch, DEC_SEQ, N_HEADS, HEAD))
```

```python
import functools

import jax
import jax.numpy as jnp
from jax import lax
from jax.experimental import pallas as pl
from jax.experimental.pallas import tpu as pltpu

F32 = jnp.float32
BF16 = jnp.bfloat16

D_MODEL = 1024
D_A = 512
D_B = 512
HEAD = 128
N_HEADS = 4
D_IN = 4 * D_A + 3 * D_B
ROWS = 128
SUBLANES = 8
PACKED = 2 * SUBLANES
N_TILES = ROWS // SUBLANES
SUBLANE_LEVELS = 3
EPS = 1e-6
PROMPT_TILE = 256
PROJ_BLOCK = 512
HEAD_ROUNDS = 12
DEC_SEQ = 8
SEQS_PER_TILE = ROWS // DEC_SEQ
SEQ_GROUP = 4
VMEM_LIMIT_BYTES = 56 * 1024 * 1024

_NT = (((1,), (1,)), ((), ()))


def _silu(x):
    return x * jax.nn.sigmoid(x)


def _gelu(x):
    c = -2.0 * 0.7978845608028654 * 1.4426950408889634
    return x / (1.0 + jnp.exp2(x * (c * 0.044715 * (x * x) + c)))


def _rms_norm(x, g):
    return x * lax.rsqrt(jnp.mean(x * x, axis=-1, keepdims=True) + EPS) * g


def _layer_norm(x, g, b):
    mu = jnp.mean(x, axis=-1, keepdims=True)
    xc = x - mu
    var = jnp.mean(xc * xc, axis=-1, keepdims=True)
    return xc * lax.rsqrt(var + EPS) * g + b


def _lower_bound(lbl):
    rows = [lbl[r:r + 1, :] for r in range(lbl.shape[0])]
    mx = functools.reduce(jnp.maximum, rows)
    es = [jnp.exp(r - mx) for r in rows]
    return es[0] / functools.reduce(lambda a, c: a + c, es)


def _n_levels(seg_len):
    n_levels = seg_len.bit_length() - 1
    assert seg_len == 1 << n_levels and n_levels >= SUBLANE_LEVELS
    return n_levels


def _split3(x):
    hi = x.astype(BF16)
    r1 = x - hi.astype(F32)
    mid = r1.astype(BF16)
    lo = (r1 - mid.astype(F32)).astype(BF16)
    return hi, mid, lo


def _store_constants(seg_len, ws_ref, bs_ref, wout_ref, lvl_ref, wc_ref, bias_ref, wout16_ref):
    t = lax.broadcasted_iota(jnp.int32, (ROWS, ROWS), 0)
    s = lax.broadcasted_iota(jnp.int32, (ROWS, ROWS), 1)
    n_levels = _n_levels(seg_len)
    x = t ^ s
    lvl = jnp.where(t == s, n_levels, -1)
    for j in range(n_levels):
        lvl = jnp.where(((x >> j) == 1) & (((t >> j) & 1) == 1), j, lvl)
    lvl_ref[...] = lvl
    causal = (t >= s) & ((t >> n_levels) == (s >> n_levels))
    pick = jnp.where(s == (t & (seg_len - 1)), 1.0, 0.0).astype(BF16)
    for hd in range(N_HEADS):
        w = ws_ref[hd].astype(BF16)
        if seg_len != ROWS:
            w = jnp.dot(pick, w, preferred_element_type=F32).astype(BF16)
            w = lax.dot_general(w, pick, _NT, preferred_element_type=F32).astype(BF16)
        wc_ref[hd] = jnp.where(causal, w, jnp.zeros_like(w))
    pad = jnp.zeros((PACKED - N_HEADS, ROWS), F32)
    bias = sum(lax.dot_general(pick, part, _NT, preferred_element_type=F32)
               for part in _split3(jnp.concatenate([bs_ref[...], pad], axis=0)))
    bias_ref[...] = bias[:, :N_HEADS]
    wout16_ref[...] = wout_ref[...].astype(BF16)


def _row_bits():
    rowi = lax.broadcasted_iota(jnp.int32, (SUBLANES, HEAD), 0)
    return [((rowi >> j) & 1) == 1 for j in range(SUBLANE_LEVELS)]


def _hgrn_head(gates, lvl_ref, bits, n_levels):
    one = jnp.ones((SUBLANES, HEAD), F32)
    z = [[] for _ in range(n_levels)]
    qp_t, ks_t, blk_t, diag_t = [], [], [], []
    for pair in range(N_TILES // 2):
        z_pair = [[] for _ in range(SUBLANE_LEVELS)]
        for i in (2 * pair, 2 * pair + 1):
            q, kk, f = gates(i)
            diag_t.append(jnp.sum(q * kk, axis=-1, keepdims=True))
            qp, ks, blk = q * f, kk, f
            for j in range(SUBLANE_LEVELS):
                m = 1 << j
                z_pair[j].append(jnp.where(bits[j], qp, ks))
                sib = pltpu.roll(blk, m, 0)
                if 2 * m != SUBLANES:
                    sib = jnp.where(bits[j], sib, pltpu.roll(blk, SUBLANES - m, 0))
                qp = qp * jnp.where(bits[j], sib, one)
                ks = ks * jnp.where(bits[j], one, sib)
                blk = blk * sib
            qp_t.append(qp)
            ks_t.append(ks)
            blk_t.append(blk)
        for j in range(SUBLANE_LEVELS):
            z[j].append(jnp.concatenate(z_pair[j], axis=0).astype(BF16))
    yield

    blk_levels, blk_b = [], blk_t
    for j in range(SUBLANE_LEVELS, n_levels):
        blk_levels.append(blk_b)
        blk_b = [blk_b[2 * n] * blk_b[2 * n + 1] for n in range(len(blk_b) // 2)]
    for pair in range(N_TILES // 2):
        z_pair = [[] for _ in range(SUBLANE_LEVELS, n_levels)]
        for i in (2 * pair, 2 * pair + 1):
            for j in range(SUBLANE_LEVELS, n_levels):
                block = i >> (j - SUBLANE_LEVELS)
                sib = blk_levels[j - SUBLANE_LEVELS][block ^ 1]
                if block & 1:
                    z_pair[j - SUBLANE_LEVELS].append(qp_t[i])
                    qp_t[i] = qp_t[i] * sib
                else:
                    z_pair[j - SUBLANE_LEVELS].append(ks_t[i])
                    ks_t[i] = ks_t[i] * sib
        for j in range(SUBLANE_LEVELS, n_levels):
            z[j].append(jnp.concatenate(z_pair[j - SUBLANE_LEVELS], axis=0).astype(BF16))
    q_in = jnp.concatenate(qp_t, axis=0)
    k_out = jnp.concatenate(ks_t, axis=0)
    per_tile = N_TILES // len(blk_b)
    total = jnp.concatenate([blk_b[i // per_tile] for i in range(N_TILES)], axis=0)
    yield

    p, p_row = [], []
    for j in range(n_levels):
        zj = jnp.concatenate(z[j], axis=0)
        span = 1 << max(j - SUBLANE_LEVELS, 0)
        if span * SUBLANES >= PACKED:
            upper = [i for i in range(N_TILES) if (i // span) & 1]
            lhs = jnp.concatenate([z[j][i // 2] for i in upper[::2]], axis=0)
        else:
            upper = list(range(N_TILES))
            lhs = zj
        p.append(lax.dot_general(lhs, zj, _NT, preferred_element_type=F32))
        p_row.append({i: n * SUBLANES for n, i in enumerate(upper)})
        yield
    sc_t = []
    for i in range(N_TILES):
        lvl_i = lvl_ref[i * SUBLANES:(i + 1) * SUBLANES, :]
        acc = jnp.where(lvl_i == n_levels, diag_t[i], 0.0)
        for j in range(n_levels):
            if j < SUBLANE_LEVELS or (i >> (j - SUBLANE_LEVELS)) & 1:
                r0 = p_row[j][i]
                acc = jnp.where(lvl_i == j, p[j][r0:r0 + SUBLANES, :], acc)
        sc_t.append(acc)
    sc = jnp.concatenate(sc_t, axis=0).astype(BF16)
    return sc, q_in, k_out, total


def _lockstep(generators, starts=None):
    starts = starts or [0] * len(generators)
    results = [None] * len(generators)
    waiting = sorted(range(len(generators)), key=lambda i: starts[i])
    active = []
    rnd = 0
    while waiting or active:
        while waiting and starts[waiting[0]] <= rnd:
            active.append(waiting.pop(0))
        still = []
        for idx in active:
            try:
                next(generators[idx])
                still.append(idx)
            except StopIteration as done:
                results[idx] = done.value
        active = still
        rnd += 1
    return results


def _hgrn_gates(pq, pf, lb):
    q = _silu(pq)
    forget = lb + (1.0 - lb) * jax.nn.sigmoid(pf)
    return q, 1.0 - forget, forget


def _hgrn_finish(o, pz, g):
    return (_rms_norm(o, g) * _silu(pz)).astype(BF16)


def _mlp_head(hd, u, v_n16, gate, wc_ref, bias):
    mix = jnp.dot(wc_ref[hd], v_n16, preferred_element_type=F32) + bias[:, hd:hd + 1]
    return (_gelu(u) * mix * _silu(gate)).astype(BF16)


def _prompt_kernel(x_ref, xprev_ref, ng_ref, win_ref, lbl_ref, hg_ref, lng_ref, lnb_ref, ws_ref,
                   bs_ref, wout_ref, fg_ref, y_ref, st_ref, vch_ref, proj_a, proj_b, h_ref,
                   mixed_ref, lvl_ref, wc_ref, bias_ref, wout16_ref, *, tiles_per_seq):
    g = pl.program_id(0)

    @pl.when(g == 0)
    def _():
        proj_b[...] = jnp.zeros_like(proj_b)
        _store_constants(ROWS, ws_ref, bs_ref, wout_ref, lvl_ref, wc_ref, bias_ref, wout16_ref)

    @pl.when(lax.rem(jnp.maximum(g - 1, 0), tiles_per_seq) == 0)
    def _():
        st_ref[...] = jnp.zeros_like(st_ref)

    def step(proj_next, proj_prev):
        h_ref[...] = _rms_norm(x_ref[0], ng_ref[...]).astype(BF16)
        lb = _lower_bound(lbl_ref[...])
        bits, n_levels = _row_bits(), _n_levels(ROWS)
        hg = hg_ref[...]
        lng = lng_ref[...]
        lnb = lnb_ref[...]
        bias = bias_ref[...]
        n_chunks = PROMPT_TILE // ROWS
        n_blocks = D_IN // PROJ_BLOCK

        def col(c, k, hs=slice(0, D_A)):
            return proj_prev[c * ROWS:(c + 1) * ROWS, k * D_A + hs.start:k * D_A + hs.stop]

        def head(c, hd):
            hs = slice(hd * HEAD, (hd + 1) * HEAD)

            def gates(i):
                rows = slice(c * ROWS + i * SUBLANES, c * ROWS + (i + 1) * SUBLANES)
                return _hgrn_gates(proj_prev[rows, hs],
                                   proj_prev[rows, D_A + hd * HEAD:D_A + (hd + 1) * HEAD],
                                   lb[:, hs])

            sc, q_in, k_out, total = yield from _hgrn_head(gates, lvl_ref, bits, n_levels)
            state = st_ref[0, 0, hd]
            vh = col(c, 2, hs)
            o = jnp.dot(jnp.concatenate([sc, q_in.astype(BF16)], axis=1),
                        jnp.concatenate([vh.astype(BF16), state.astype(BF16)], axis=0),
                        preferred_element_type=F32)
            yield
            st_ref[0, 0, hd] = total.T * state + jnp.dot(
                k_out.T.astype(BF16), vh.astype(BF16), preferred_element_type=F32)
            yield
            mixed_ref[c * ROWS:(c + 1) * ROWS, hs] = _hgrn_finish(o, col(c, 3, hs), hg[:, hs])

        def mlp(c):
            v_n = _layer_norm(_gelu(col(c, 5)), lng, lnb)
            if c == n_chunks - 1:
                for hd in range(N_HEADS):
                    vch_ref[0, 0, :, hd, :] = v_n[:, hd * HEAD:(hd + 1) * HEAD]
            v_n16 = v_n.astype(BF16)
            yield
            for hd in range(N_HEADS):
                hs = slice(hd * HEAD, (hd + 1) * HEAD)
                mixed_ref[c * ROWS:(c + 1) * ROWS, D_A + hd * HEAD:D_A + (hd + 1) * HEAD] = (
                    _mlp_head(hd, col(c, 4, hs), v_n16[:, hs], col(c, 6, hs), wc_ref, bias))
                yield
                yield

        def projection(rounds_per_block):
            for blk in range(n_blocks):
                cols = slice(blk * PROJ_BLOCK, (blk + 1) * PROJ_BLOCK)
                proj_next[:, cols] = jnp.dot(h_ref[...], win_ref[:, cols],
                                             preferred_element_type=F32)
                for _ in range(rounds_per_block):
                    yield

        items = [gen for c in range(n_chunks)
                 for gen in [head(c, hd) for hd in range(N_HEADS)] + [mlp(c)]]
        starts = [c * HEAD_ROUNDS for c in range(n_chunks) for _ in range(N_HEADS + 1)]
        rounds = n_chunks * HEAD_ROUNDS
        _lockstep(items + [projection(max(1, rounds // n_blocks))], starts + [0])

        out = xprev_ref[0] + jnp.dot(mixed_ref[...], wout16_ref[...], preferred_element_type=F32)
        y_ref[0] = _rms_norm(out, fg_ref[...])

    parity = lax.rem(g, 2)

    @pl.when(parity == 0)
    def _():
        step(proj_a, proj_b)

    @pl.when(parity == 1)
    def _():
        step(proj_b, proj_a)


def _sample_kernel(x_ref, ng_ref, win_ref, lbl_ref, hg_ref, lng_ref, lnb_ref, ws_ref, bs_ref,
                   wout_ref, fg_ref, stin_ref, y_ref, stout_ref, vch_ref,
                   qin_ref, kot_ref, tot_ref, oint_ref, lvl_ref, wc_ref, bias_ref, wout16_ref):
    @pl.when(pl.program_id(0) == 0)
    def _():
        _store_constants(DEC_SEQ, ws_ref, bs_ref, wout_ref, lvl_ref, wc_ref, bias_ref, wout16_ref)

    x = x_ref[...]
    h = _rms_norm(x, ng_ref[...]).astype(BF16)
    proj = jnp.dot(h, win_ref[...], preferred_element_type=F32)

    def col(k, hs=slice(0, D_A)):
        return proj[:, k * D_A + hs.start:k * D_A + hs.stop]

    lb = _lower_bound(lbl_ref[...])
    bits, n_levels = _row_bits(), _n_levels(DEC_SEQ)
    vi16 = col(2).astype(BF16)

    def head(hd):
        hs = slice(hd * HEAD, (hd + 1) * HEAD)

        def gates(i):
            rows = slice(i * SUBLANES, (i + 1) * SUBLANES)
            return _hgrn_gates(proj[rows, hs], proj[rows, D_A + hd * HEAD:D_A + (hd + 1) * HEAD],
                               lb[:, hs])

        sc, q_in, k_out, total = yield from _hgrn_head(gates, lvl_ref, bits, n_levels)
        qin_ref[:, hs] = q_in
        kot_ref[hd] = k_out.T
        tot_ref[hd] = total.T
        return jnp.dot(sc, vi16[:, hs], preferred_element_type=F32)

    o_intra = _lockstep([head(hd) for hd in range(N_HEADS)])

    lane = lax.broadcasted_iota(jnp.int32, (HEAD, ROWS), 1)

    def seq_head(i, hd):
        hs = slice(hd * HEAD, (hd + 1) * HEAD)
        seq_rows = slice(i * DEC_SEQ, (i + 1) * DEC_SEQ)
        state = stin_ref[0, i, hd]
        qi = qin_ref[seq_rows, hs].astype(BF16)
        oint_ref[seq_rows, hs] = jnp.dot(qi, state.astype(BF16), preferred_element_type=F32)
        yield
        in_seq = (lane >> (DEC_SEQ.bit_length() - 1)) == i
        k_seq = jnp.where(in_seq, kot_ref[hd], 0.0).astype(BF16)
        decay = tot_ref[hd, :, i * DEC_SEQ:i * DEC_SEQ + 1]
        yield
        stout_ref[0, i, hd] = decay * state + jnp.dot(
            k_seq, vi16[:, hs], preferred_element_type=F32)

    for grp in range(SEQS_PER_TILE // SEQ_GROUP):
        _lockstep([seq_head(grp * SEQ_GROUP + s, hd)
                   for s in range(SEQ_GROUP) for hd in range(N_HEADS)])

    hg = hg_ref[...]
    bias = bias_ref[...]
    v_n = _layer_norm(_gelu(col(5)), lng_ref[...], lnb_ref[...])
    for hd in range(N_HEADS):
        vch_ref[0, :, :, hd, :] = v_n[:, hd * HEAD:(hd + 1) * HEAD].reshape(
            SEQS_PER_TILE, DEC_SEQ, HEAD)
    v_n16 = v_n.astype(BF16)
    mixed = []
    for hd in range(N_HEADS):
        hs = slice(hd * HEAD, (hd + 1) * HEAD)
        mixed.append(_hgrn_finish(o_intra[hd] + oint_ref[:, hs], col(3, hs), hg[:, hs]))
    for hd in range(N_HEADS):
        hs = slice(hd * HEAD, (hd + 1) * HEAD)
        mixed.append(_mlp_head(hd, col(4, hs), v_n16[:, hs], col(6, hs), wc_ref, bias))
    out = x + jnp.dot(jnp.concatenate(mixed, axis=-1), wout16_ref[...],
                      preferred_element_type=F32)
    y_ref[...] = _rms_norm(out, fg_ref[...])


def _const_spec(shape):
    return pl.BlockSpec(shape, lambda *_: (0,) * len(shape), pipeline_mode=pl.Buffered(1))


_WEIGHT_SPECS = [
    _const_spec((1, D_MODEL)),
    _const_spec((D_MODEL, D_IN)),
    _const_spec((2, D_A)),
    _const_spec((1, D_A)),
    _const_spec((1, D_B)),
    _const_spec((1, D_B)),
    _const_spec((N_HEADS, ROWS, ROWS)),
    _const_spec((N_HEADS, ROWS)),
    _const_spec((D_MODEL, D_MODEL)),
    _const_spec((1, D_MODEL)),
]

_CONSTANT_SCRATCH = [
    pltpu.VMEM((ROWS, ROWS), jnp.int32),
    pltpu.VMEM((N_HEADS, ROWS, ROWS), BF16),
    pltpu.VMEM((ROWS, N_HEADS), F32),
    pltpu.VMEM((D_MODEL, D_MODEL), BF16),
]


def kernel(x_prompt, x_sample, state_hgrn, norm_g, w_in, lb_logits, hgrn_norm_g, sgu_ln_g,
           sgu_ln_b, w_s, b_s, w_out, final_norm_g):
    depth = norm_g.shape[0]
    assert depth == 1 and lb_logits.shape == (2, D_A)
    batch, seq, _ = x_prompt.shape
    dec_batch, dec_seq, _ = x_sample.shape
    assert seq % PROMPT_TILE == 0 and dec_seq == DEC_SEQ and dec_batch % SEQS_PER_TILE == 0

    weights = (norm_g, w_in[0].astype(BF16), lb_logits, hgrn_norm_g, sgu_ln_g, sgu_ln_b,
               w_s[0], b_s[0], w_out[0], final_norm_g.reshape(1, D_MODEL))
    params = pltpu.CompilerParams(dimension_semantics=("arbitrary",),
                                  vmem_limit_bytes=VMEM_LIMIT_BYTES)

    tiles_per_seq = seq // PROMPT_TILE
    n_tiles = batch * tiles_per_seq

    def cur_tile(g):
        t = jnp.minimum(g, n_tiles - 1)
        return t // tiles_per_seq, t % tiles_per_seq

    def prev_tile(g):
        t = jnp.maximum(g - 1, 0)
        return t // tiles_per_seq, t % tiles_per_seq

    y_p, st_p, vch_p = pl.pallas_call(
        functools.partial(_prompt_kernel, tiles_per_seq=tiles_per_seq),
        grid=(n_tiles + 1,),
        in_specs=[pl.BlockSpec((1, PROMPT_TILE, D_MODEL), lambda g: (*cur_tile(g), 0)),
                  pl.BlockSpec((1, PROMPT_TILE, D_MODEL), lambda g: (*prev_tile(g), 0))]
        + _WEIGHT_SPECS,
        out_specs=[
            pl.BlockSpec((1, PROMPT_TILE, D_MODEL), lambda g: (*prev_tile(g), 0)),
            pl.BlockSpec((1, 1, N_HEADS, HEAD, HEAD), lambda g: (0, prev_tile(g)[0], 0, 0, 0)),
            pl.BlockSpec((1, 1, ROWS, N_HEADS, HEAD), lambda g: (0, prev_tile(g)[0], 0, 0, 0)),
        ],
        out_shape=[
            jax.ShapeDtypeStruct((batch, seq, D_MODEL), F32),
            jax.ShapeDtypeStruct((1, batch, N_HEADS, HEAD, HEAD), F32),
            jax.ShapeDtypeStruct((1, batch, ROWS, N_HEADS, HEAD), F32),
        ],
        scratch_shapes=[pltpu.VMEM((PROMPT_TILE, D_IN), F32),
                        pltpu.VMEM((PROMPT_TILE, D_IN), F32),
                        pltpu.VMEM((PROMPT_TILE, D_MODEL), BF16),
                        pltpu.VMEM((PROMPT_TILE, D_MODEL), BF16)] + _CONSTANT_SCRATCH,
        compiler_params=params,
        name="prompt_layer",
    )(x_prompt, x_prompt, *weights)

    n_rows = dec_batch * DEC_SEQ
    y_s, st_s, vch_s = pl.pallas_call(
        _sample_kernel,
        grid=(n_rows // ROWS,),
        in_specs=[pl.BlockSpec((ROWS, D_MODEL), lambda i: (i, 0))]
        + _WEIGHT_SPECS
        + [pl.BlockSpec((1, SEQS_PER_TILE, N_HEADS, HEAD, HEAD), lambda i: (0, i, 0, 0, 0))],
        out_specs=[
            pl.BlockSpec((ROWS, D_MODEL), lambda i: (i, 0)),
            pl.BlockSpec((1, SEQS_PER_TILE, N_HEADS, HEAD, HEAD), lambda i: (0, i, 0, 0, 0)),
            pl.BlockSpec((1, SEQS_PER_TILE, DEC_SEQ, N_HEADS, HEAD), lambda i: (0, i, 0, 0, 0)),
        ],
        out_shape=[
            jax.ShapeDtypeStruct((n_rows, D_MODEL), F32),
            jax.ShapeDtypeStruct((1, dec_batch, N_HEADS, HEAD, HEAD), F32),
            jax.ShapeDtypeStruct((1, dec_batch, DEC_SEQ, N_HEADS, HEAD), F32),
        ],
        scratch_shapes=[pltpu.VMEM((ROWS, D_A), F32),
                        pltpu.VMEM((N_HEADS, HEAD, ROWS), F32),
                        pltpu.VMEM((N_HEADS, HEAD, ROWS), F32),
                        pltpu.VMEM((ROWS, D_A), F32)] + _CONSTANT_SCRATCH,
        compiler_params=pltpu.CompilerParams(dimension_semantics=("arbitrary",),
                                             vmem_limit_bytes=VMEM_LIMIT_BYTES),
        name="decode_layer",
    )(x_sample.reshape(n_rows, D_MODEL), *weights, state_hgrn)

    return (y_p,
            y_s.reshape(dec_batch, DEC_SEQ, D_MODEL),
            st_p,
            st_s,
            vch_p,
            vch_s)
```

```python
import functools

import jax
import jax.numpy as jnp
from jax import lax
from jax.experimental import pallas as pl
from jax.experimental.pallas import tpu as pltpu

F32 = jnp.float32
BF16 = jnp.bfloat16

D_MODEL = 1024
D_A = 512
D_B = 512
HEAD = 128
N_HEADS = 4
D_IN = 4 * D_A + 3 * D_B
ROWS = 128
SUBLANES = 8
PACKED = 2 * SUBLANES
N_TILES = ROWS // SUBLANES
SUBLANE_LEVELS = 3
EPS = 1e-6
PROMPT_TILE = 512
PROJ_BLOCK = 512
HEAD_ROUNDS = 12
DEC_SEQ = 8
SEQS_PER_TILE = ROWS // DEC_SEQ
SEQ_GROUP = 4
VMEM_LIMIT_BYTES = 56 * 1024 * 1024

_NT = (((1,), (1,)), ((), ()))


def _silu(x):
    return x * jax.nn.sigmoid(x)


def _gelu(x):
    c = -2.0 * 0.7978845608028654 * 1.4426950408889634
    return x / (1.0 + jnp.exp2(x * (c * 0.044715 * (x * x) + c)))


def _rms_norm(x, g):
    return x * lax.rsqrt(jnp.mean(x * x, axis=-1, keepdims=True) + EPS) * g


def _layer_norm(x, g, b):
    mu = jnp.mean(x, axis=-1, keepdims=True)
    xc = x - mu
    var = jnp.mean(xc * xc, axis=-1, keepdims=True)
    return xc * lax.rsqrt(var + EPS) * g + b


def _lower_bound(lbl):
    rows = [lbl[r:r + 1, :] for r in range(lbl.shape[0])]
    mx = functools.reduce(jnp.maximum, rows)
    es = [jnp.exp(r - mx) for r in rows]
    return es[0] / functools.reduce(lambda a, c: a + c, es)


def _n_levels(seg_len):
    n_levels = seg_len.bit_length() - 1
    assert seg_len == 1 << n_levels and n_levels >= SUBLANE_LEVELS
    return n_levels


def _split3(x):
    hi = x.astype(BF16)
    r1 = x - hi.astype(F32)
    mid = r1.astype(BF16)
    lo = (r1 - mid.astype(F32)).astype(BF16)
    return hi, mid, lo


def _store_constants(seg_len, ws_ref, bs_ref, wout_ref, lvl_ref, wc_ref, bias_ref, wout16_ref):
    t = lax.broadcasted_iota(jnp.int32, (ROWS, ROWS), 0)
    s = lax.broadcasted_iota(jnp.int32, (ROWS, ROWS), 1)
    n_levels = _n_levels(seg_len)
    x = t ^ s
    lvl = jnp.where(t == s, n_levels, -1)
    for j in range(n_levels):
        lvl = jnp.where(((x >> j) == 1) & (((t >> j) & 1) == 1), j, lvl)
    lvl_ref[...] = lvl
    causal = (t >= s) & ((t >> n_levels) == (s >> n_levels))
    pick = jnp.where(s == (t & (seg_len - 1)), 1.0, 0.0).astype(BF16)
    for hd in range(N_HEADS):
        w = ws_ref[hd].astype(BF16)
        if seg_len != ROWS:
            w = jnp.dot(pick, w, preferred_element_type=F32).astype(BF16)
            w = lax.dot_general(w, pick, _NT, preferred_element_type=F32).astype(BF16)
        wc_ref[hd] = jnp.where(causal, w, jnp.zeros_like(w))
    pad = jnp.zeros((PACKED - N_HEADS, ROWS), F32)
    bias = sum(lax.dot_general(pick, part, _NT, preferred_element_type=F32)
               for part in _split3(jnp.concatenate([bs_ref[...], pad], axis=0)))
    bias_ref[...] = bias[:, :N_HEADS]
    wout16_ref[...] = wout_ref[...].astype(BF16)


def _row_bits():
    rowi = lax.broadcasted_iota(jnp.int32, (SUBLANES, HEAD), 0)
    return [((rowi >> j) & 1) == 1 for j in range(SUBLANE_LEVELS)]


def _hgrn_head(gates, lvl_ref, bits, n_levels):
    one = jnp.ones((SUBLANES, HEAD), F32)
    z = [[] for _ in range(n_levels)]
    qp_t, ks_t, blk_t, diag_t = [], [], [], []
    for pair in range(N_TILES // 2):
        z_pair = [[] for _ in range(SUBLANE_LEVELS)]
        for i in (2 * pair, 2 * pair + 1):
            q, kk, f = gates(i)
            diag_t.append(jnp.sum(q * kk, axis=-1, keepdims=True))
            qp, ks, blk = q * f, kk, f
            for j in range(SUBLANE_LEVELS):
                m = 1 << j
                z_pair[j].append(jnp.where(bits[j], qp, ks))
                sib = pltpu.roll(blk, m, 0)
                if 2 * m != SUBLANES:
                    sib = jnp.where(bits[j], sib, pltpu.roll(blk, SUBLANES - m, 0))
                qp = qp * jnp.where(bits[j], sib, one)
                ks = ks * jnp.where(bits[j], one, sib)
                blk = blk * sib
            qp_t.append(qp)
            ks_t.append(ks)
            blk_t.append(blk)
        for j in range(SUBLANE_LEVELS):
            z[j].append(jnp.concatenate(z_pair[j], axis=0).astype(BF16))
    yield

    blk_levels, blk_b = [], blk_t
    for j in range(SUBLANE_LEVELS, n_levels):
        blk_levels.append(blk_b)
        blk_b = [blk_b[2 * n] * blk_b[2 * n + 1] for n in range(len(blk_b) // 2)]
    for pair in range(N_TILES // 2):
        z_pair = [[] for _ in range(SUBLANE_LEVELS, n_levels)]
        for i in (2 * pair, 2 * pair + 1):
            for j in range(SUBLANE_LEVELS, n_levels):
                block = i >> (j - SUBLANE_LEVELS)
                sib = blk_levels[j - SUBLANE_LEVELS][block ^ 1]
                if block & 1:
                    z_pair[j - SUBLANE_LEVELS].append(qp_t[i])
                    qp_t[i] = qp_t[i] * sib
                else:
                    z_pair[j - SUBLANE_LEVELS].append(ks_t[i])
                    ks_t[i] = ks_t[i] * sib
        for j in range(SUBLANE_LEVELS, n_levels):
            z[j].append(jnp.concatenate(z_pair[j - SUBLANE_LEVELS], axis=0).astype(BF16))
    q_in = jnp.concatenate(qp_t, axis=0)
    k_out = jnp.concatenate(ks_t, axis=0)
    per_tile = N_TILES // len(blk_b)
    total = jnp.concatenate([blk_b[i // per_tile] for i in range(N_TILES)], axis=0)
    yield

    p, p_row = [], []
    for j in range(n_levels):
        zj = jnp.concatenate(z[j], axis=0)
        span = 1 << max(j - SUBLANE_LEVELS, 0)
        if span * SUBLANES >= PACKED:
            upper = [i for i in range(N_TILES) if (i // span) & 1]
            lhs = jnp.concatenate([z[j][i // 2] for i in upper[::2]], axis=0)
        else:
            upper = list(range(N_TILES))
            lhs = zj
        p.append(lax.dot_general(lhs, zj, _NT, preferred_element_type=F32))
        p_row.append({i: n * SUBLANES for n, i in enumerate(upper)})
        yield
    sc_t = []
    for i in range(N_TILES):
        lvl_i = lvl_ref[i * SUBLANES:(i + 1) * SUBLANES, :]
        acc = jnp.where(lvl_i == n_levels, diag_t[i], 0.0)
        for j in range(n_levels):
            if j < SUBLANE_LEVELS or (i >> (j - SUBLANE_LEVELS)) & 1:
                r0 = p_row[j][i]
                acc = jnp.where(lvl_i == j, p[j][r0:r0 + SUBLANES, :], acc)
        sc_t.append(acc)
    sc = jnp.concatenate(sc_t, axis=0).astype(BF16)
    return sc, q_in, k_out, total


def _lockstep(generators, starts=None):
    starts = starts or [0] * len(generators)
    results = [None] * len(generators)
    waiting = sorted(range(len(generators)), key=lambda i: starts[i])
    active = []
    rnd = 0
    while waiting or active:
        while waiting and starts[waiting[0]] <= rnd:
            active.append(waiting.pop(0))
        still = []
        for idx in active:
            try:
                next(generators[idx])
                still.append(idx)
            except StopIteration as done:
                results[idx] = done.value
        active = still
        rnd += 1
    return results


def _hgrn_gates(pq, pf, lb):
    q = _silu(pq)
    forget = lb + (1.0 - lb) * jax.nn.sigmoid(pf)
    return q, 1.0 - forget, forget


def _hgrn_finish(o, pz, g):
    return (_rms_norm(o, g) * _silu(pz)).astype(BF16)


def _mlp_head(hd, u, v_n16, gate, wc_ref, bias):
    mix = jnp.dot(wc_ref[hd], v_n16, preferred_element_type=F32) + bias[:, hd:hd + 1]
    return (_gelu(u) * mix * _silu(gate)).astype(BF16)


def _prompt_kernel(x_ref, xprev_ref, ng_ref, win_ref, lbl_ref, hg_ref, lng_ref, lnb_ref, ws_ref,
                   bs_ref, wout_ref, fg_ref, y_ref, st_ref, vch_ref, proj_a, proj_b, h_ref,
                   mixed_ref, lvl_ref, wc_ref, bias_ref, wout16_ref, *, tiles_per_seq):
    g = pl.program_id(0)

    @pl.when(g == 0)
    def _():
        proj_b[...] = jnp.zeros_like(proj_b)
        _store_constants(ROWS, ws_ref, bs_ref, wout_ref, lvl_ref, wc_ref, bias_ref, wout16_ref)

    @pl.when(lax.rem(jnp.maximum(g - 1, 0), tiles_per_seq) == 0)
    def _():
        st_ref[...] = jnp.zeros_like(st_ref)

    def step(proj_next, proj_prev):
        h_ref[...] = _rms_norm(x_ref[0], ng_ref[...]).astype(BF16)
        lb = _lower_bound(lbl_ref[...])
        bits, n_levels = _row_bits(), _n_levels(ROWS)
        hg = hg_ref[...]
        lng = lng_ref[...]
        lnb = lnb_ref[...]
        bias = bias_ref[...]
        n_chunks = PROMPT_TILE // ROWS
        n_blocks = D_IN // PROJ_BLOCK

        def col(c, k, hs=slice(0, D_A)):
            return proj_prev[c * ROWS:(c + 1) * ROWS, k * D_A + hs.start:k * D_A + hs.stop]

        def head(c, hd):
            hs = slice(hd * HEAD, (hd + 1) * HEAD)

            def gates(i):
                rows = slice(c * ROWS + i * SUBLANES, c * ROWS + (i + 1) * SUBLANES)
                return _hgrn_gates(proj_prev[rows, hs],
                                   proj_prev[rows, D_A + hd * HEAD:D_A + (hd + 1) * HEAD],
                                   lb[:, hs])

            sc, q_in, k_out, total = yield from _hgrn_head(gates, lvl_ref, bits, n_levels)
            state = st_ref[0, 0, hd]
            vh = col(c, 2, hs)
            o = jnp.dot(jnp.concatenate([sc, q_in.astype(BF16)], axis=1),
                        jnp.concatenate([vh.astype(BF16), state.astype(BF16)], axis=0),
                        preferred_element_type=F32)
            yield
            st_ref[0, 0, hd] = total.T * state + jnp.dot(
                k_out.T.astype(BF16), vh.astype(BF16), preferred_element_type=F32)
            yield
            mixed_ref[c * ROWS:(c + 1) * ROWS, hs] = _hgrn_finish(o, col(c, 3, hs), hg[:, hs])

        def mlp(c):
            v_n = _layer_norm(_gelu(col(c, 5)), lng, lnb)
            if c == n_chunks - 1:
                for hd in range(N_HEADS):
                    vch_ref[0, 0, :, hd, :] = v_n[:, hd * HEAD:(hd + 1) * HEAD]
            v_n16 = v_n.astype(BF16)
            yield
            for hd in range(N_HEADS):
                hs = slice(hd * HEAD, (hd + 1) * HEAD)
                mixed_ref[c * ROWS:(c + 1) * ROWS, D_A + hd * HEAD:D_A + (hd + 1) * HEAD] = (
                    _mlp_head(hd, col(c, 4, hs), v_n16[:, hs], col(c, 6, hs), wc_ref, bias))
                yield
                yield

        def projection(rounds_per_block):
            for blk in range(n_blocks):
                cols = slice(blk * PROJ_BLOCK, (blk + 1) * PROJ_BLOCK)
                proj_next[:, cols] = jnp.dot(h_ref[...], win_ref[:, cols],
                                             preferred_element_type=F32)
                for _ in range(rounds_per_block):
                    yield

        items = [gen for c in range(n_chunks)
                 for gen in [head(c, hd) for hd in range(N_HEADS)] + [mlp(c)]]
        starts = [c * HEAD_ROUNDS for c in range(n_chunks) for _ in range(N_HEADS + 1)]
        rounds = n_chunks * HEAD_ROUNDS
        _lockstep(items + [projection(max(1, rounds // n_blocks))], starts + [0])

        out = xprev_ref[0] + jnp.dot(mixed_ref[...], wout16_ref[...], preferred_element_type=F32)
        y_ref[0] = _rms_norm(out, fg_ref[...])

    parity = lax.rem(g, 2)

    @pl.when(parity == 0)
    def _():
        step(proj_a, proj_b)

    @pl.when(parity == 1)
    def _():
        step(proj_b, proj_a)


def _sample_kernel(x_ref, ng_ref, win_ref, lbl_ref, hg_ref, lng_ref, lnb_ref, ws_ref, bs_ref,
                   wout_ref, fg_ref, stin_ref, y_ref, stout_ref, vch_ref,
                   qin_ref, kot_ref, tot_ref, oint_ref, lvl_ref, wc_ref, bias_ref, wout16_ref):
    @pl.when(pl.program_id(0) == 0)
    def _():
        _store_constants(DEC_SEQ, ws_ref, bs_ref, wout_ref, lvl_ref, wc_ref, bias_ref, wout16_ref)

    x = x_ref[...]
    h = _rms_norm(x, ng_ref[...]).astype(BF16)
    proj = jnp.dot(h, win_ref[...], preferred_element_type=F32)

    def col(k, hs=slice(0, D_A)):
        return proj[:, k * D_A + hs.start:k * D_A + hs.stop]

    lb = _lower_bound(lbl_ref[...])
    bits, n_levels = _row_bits(), _n_levels(DEC_SEQ)
    vi16 = col(2).astype(BF16)

    def head(hd):
        hs = slice(hd * HEAD, (hd + 1) * HEAD)

        def gates(i):
            rows = slice(i * SUBLANES, (i + 1) * SUBLANES)
            return _hgrn_gates(proj[rows, hs], proj[rows, D_A + hd * HEAD:D_A + (hd + 1) * HEAD],
                               lb[:, hs])

        sc, q_in, k_out, total = yield from _hgrn_head(gates, lvl_ref, bits, n_levels)
        qin_ref[:, hs] = q_in
        kot_ref[hd] = k_out.T
        tot_ref[hd] = total.T
        return jnp.dot(sc, vi16[:, hs], preferred_element_type=F32)

    o_intra = _lockstep([head(hd) for hd in range(N_HEADS)])

    lane = lax.broadcasted_iota(jnp.int32, (HEAD, ROWS), 1)

    def seq_head(i, hd):
        hs = slice(hd * HEAD, (hd + 1) * HEAD)
        seq_rows = slice(i * DEC_SEQ, (i + 1) * DEC_SEQ)
        state = stin_ref[0, i, hd]
        qi = qin_ref[seq_rows, hs].astype(BF16)
        oint_ref[seq_rows, hs] = jnp.dot(qi, state.astype(BF16), preferred_element_type=F32)
        yield
        in_seq = (lane >> (DEC_SEQ.bit_length() - 1)) == i
        k_seq = jnp.where(in_seq, kot_ref[hd], 0.0).astype(BF16)
        decay = tot_ref[hd, :, i * DEC_SEQ:i * DEC_SEQ + 1]
        yield
        stout_ref[0, i, hd] = decay * state + jnp.dot(
            k_seq, vi16[:, hs], preferred_element_type=F32)

    for grp in range(SEQS_PER_TILE // SEQ_GROUP):
        _lockstep([seq_head(grp * SEQ_GROUP + s, hd)
                   for s in range(SEQ_GROUP) for hd in range(N_HEADS)])

    hg = hg_ref[...]
    bias = bias_ref[...]
    v_n = _layer_norm(_gelu(col(5)), lng_ref[...], lnb_ref[...])
    for hd in range(N_HEADS):
        vch_ref[0, :, :, hd, :] = v_n[:, hd * HEAD:(hd + 1) * HEAD].reshape(
            SEQS_PER_TILE, DEC_SEQ, HEAD)
    v_n16 = v_n.astype(BF16)
    mixed = []
    for hd in range(N_HEADS):
        hs = slice(hd * HEAD, (hd + 1) * HEAD)
        mixed.append(_hgrn_finish(o_intra[hd] + oint_ref[:, hs], col(3, hs), hg[:, hs]))
    for hd in range(N_HEADS):
        hs = slice(hd * HEAD, (hd + 1) * HEAD)
        mixed.append(_mlp_head(hd, col(4, hs), v_n16[:, hs], col(6, hs), wc_ref, bias))
    out = x + jnp.dot(jnp.concatenate(mixed, axis=-1), wout16_ref[...],
                      preferred_element_type=F32)
    y_ref[...] = _rms_norm(out, fg_ref[...])


def _const_spec(shape):
    return pl.BlockSpec(shape, lambda *_: (0,) * len(shape), pipeline_mode=pl.Buffered(1))


_WEIGHT_SPECS = [
    _const_spec((1, D_MODEL)),
    _const_spec((D_MODEL, D_IN)),
    _const_spec((2, D_A)),
    _const_spec((1, D_A)),
    _const_spec((1, D_B)),
    _const_spec((1, D_B)),
    _const_spec((N_HEADS, ROWS, ROWS)),
    _const_spec((N_HEADS, ROWS)),
    _const_spec((D_MODEL, D_MODEL)),
    _const_spec((1, D_MODEL)),
]

_CONSTANT_SCRATCH = [
    pltpu.VMEM((ROWS, ROWS), jnp.int32),
    pltpu.VMEM((N_HEADS, ROWS, ROWS), BF16),
    pltpu.VMEM((ROWS, N_HEADS), F32),
    pltpu.VMEM((D_MODEL, D_MODEL), BF16),
]


def kernel(x_prompt, x_sample, state_hgrn, norm_g, w_in, lb_logits, hgrn_norm_g, sgu_ln_g,
           sgu_ln_b, w_s, b_s, w_out, final_norm_g):
    depth = norm_g.shape[0]
    assert depth == 1 and lb_logits.shape == (2, D_A)
    batch, seq, _ = x_prompt.shape
    dec_batch, dec_seq, _ = x_sample.shape
    assert seq % PROMPT_TILE == 0 and dec_seq == DEC_SEQ and dec_batch % SEQS_PER_TILE == 0

    weights = (norm_g, w_in[0].astype(BF16), lb_logits, hgrn_norm_g, sgu_ln_g, sgu_ln_b,
               w_s[0], b_s[0], w_out[0], final_norm_g.reshape(1, D_MODEL))
    params = pltpu.CompilerParams(dimension_semantics=("arbitrary",),
                                  vmem_limit_bytes=VMEM_LIMIT_BYTES)

    tiles_per_seq = seq // PROMPT_TILE
    n_tiles = batch * tiles_per_seq

    def cur_tile(g):
        t = jnp.minimum(g, n_tiles - 1)
        return t // tiles_per_seq, t % tiles_per_seq

    def prev_tile(g):
        t = jnp.maximum(g - 1, 0)
        return t // tiles_per_seq, t % tiles_per_seq

    y_p, st_p, vch_p = pl.pallas_call(
        functools.partial(_prompt_kernel, tiles_per_seq=tiles_per_seq),
        grid=(n_tiles + 1,),
        in_specs=[pl.BlockSpec((1, PROMPT_TILE, D_MODEL), lambda g: (*cur_tile(g), 0)),
                  pl.BlockSpec((1, PROMPT_TILE, D_MODEL), lambda g: (*prev_tile(g), 0))]
        + _WEIGHT_SPECS,
        out_specs=[
            pl.BlockSpec((1, PROMPT_TILE, D_MODEL), lambda g: (*prev_tile(g), 0)),
            pl.BlockSpec((1, 1, N_HEADS, HEAD, HEAD), lambda g: (0, prev_tile(g)[0], 0, 0, 0)),
            pl.BlockSpec((1, 1, ROWS, N_HEADS, HEAD), lambda g: (0, prev_tile(g)[0], 0, 0, 0)),
        ],
        out_shape=[
            jax.ShapeDtypeStruct((batch, seq, D_MODEL), F32),
            jax.ShapeDtypeStruct((1, batch, N_HEADS, HEAD, HEAD), F32),
            jax.ShapeDtypeStruct((1, batch, ROWS, N_HEADS, HEAD), F32),
        ],
        scratch_shapes=[pltpu.VMEM((PROMPT_TILE, D_IN), F32),
                        pltpu.VMEM((PROMPT_TILE, D_IN), F32),
                        pltpu.VMEM((PROMPT_TILE, D_MODEL), BF16),
                        pltpu.VMEM((PROMPT_TILE, D_MODEL), BF16)] + _CONSTANT_SCRATCH,
        compiler_params=params,
        name="prompt_layer",
    )(x_prompt, x_prompt, *weights)

    n_rows = dec_batch * DEC_SEQ
    y_s, st_s, vch_s = pl.pallas_call(
        _sample_kernel,
        grid=(n_rows // ROWS,),
        in_specs=[pl.BlockSpec((ROWS, D_MODEL), lambda i: (i, 0))]
        + _WEIGHT_SPECS
        + [pl.BlockSpec((1, SEQS_PER_TILE, N_HEADS, HEAD, HEAD), lambda i: (0, i, 0, 0, 0))],
        out_specs=[
            pl.BlockSpec((ROWS, D_MODEL), lambda i: (i, 0)),
            pl.BlockSpec((1, SEQS_PER_TILE, N_HEADS, HEAD, HEAD), lambda i: (0, i, 0, 0, 0)),
            pl.BlockSpec((1, SEQS_PER_TILE, DEC_SEQ, N_HEADS, HEAD), lambda i: (0, i, 0, 0, 0)),
        ],
        out_shape=[
            jax.ShapeDtypeStruct((n_rows, D_MODEL), F32),
            jax.ShapeDtypeStruct((1, dec_batch, N_HEADS, HEAD, HEAD), F32),
            jax.ShapeDtypeStruct((1, dec_batch, DEC_SEQ, N_HEADS, HEAD), F32),
        ],
        scratch_shapes=[pltpu.VMEM((ROWS, D_A), F32),
                        pltpu.VMEM((N_HEADS, HEAD, ROWS), F32),
                        pltpu.VMEM((N_HEADS, HEAD, ROWS), F32),
                        pltpu.VMEM((ROWS, D_A), F32)] + _CONSTANT_SCRATCH,
        compiler_params=pltpu.CompilerParams(dimension_semantics=("arbitrary",),
                                             vmem_limit_bytes=VMEM_LIMIT_BYTES),
        name="decode_layer",
    )(x_sample.reshape(n_rows, D_MODEL), *weights, state_hgrn)

    return (y_p,
            y_s.reshape(dec_batch, DEC_SEQ, D_MODEL),
            st_p,
            st_s,
            vch_p,
            vch_s)
```

```python
import functools

import jax
import jax.numpy as jnp
from jax import lax
from jax.experimental import pallas as pl
from jax.experimental.pallas import tpu as pltpu

F32 = jnp.float32
BF16 = jnp.bfloat16

D_MODEL = 1024
D_A = 512
D_B = 512
HEAD = 128
N_HEADS = 4
D_IN = 4 * D_A + 3 * D_B
ROWS = 128
SUBLANES = 8
PACKED = 2 * SUBLANES
N_TILES = ROWS // SUBLANES
SUBLANE_LEVELS = 3
EPS = 1e-6
PROMPT_TILE = 512
PROJ_BLOCK = 512
HEAD_ROUNDS = 12
CHUNK_STAGGER = 3
DEC_SEQ = 8
SEQS_PER_TILE = ROWS // DEC_SEQ
SEQ_GROUP = 4
VMEM_LIMIT_BYTES = 56 * 1024 * 1024

_NT = (((1,), (1,)), ((), ()))


def _silu(x):
    return x * jax.nn.sigmoid(x)


def _gelu(x):
    c = -2.0 * 0.7978845608028654 * 1.4426950408889634
    return x / (1.0 + jnp.exp2(x * (c * 0.044715 * (x * x) + c)))


def _rms_norm(x, g):
    return x * lax.rsqrt(jnp.mean(x * x, axis=-1, keepdims=True) + EPS) * g


def _layer_norm(x, g, b):
    mu = jnp.mean(x, axis=-1, keepdims=True)
    xc = x - mu
    var = jnp.mean(xc * xc, axis=-1, keepdims=True)
    return xc * lax.rsqrt(var + EPS) * g + b


def _lower_bound(lbl):
    rows = [lbl[r:r + 1, :] for r in range(lbl.shape[0])]
    mx = functools.reduce(jnp.maximum, rows)
    es = [jnp.exp(r - mx) for r in rows]
    return es[0] / functools.reduce(lambda a, c: a + c, es)


def _n_levels(seg_len):
    n_levels = seg_len.bit_length() - 1
    assert seg_len == 1 << n_levels and n_levels >= SUBLANE_LEVELS
    return n_levels


def _split3(x):
    hi = x.astype(BF16)
    r1 = x - hi.astype(F32)
    mid = r1.astype(BF16)
    lo = (r1 - mid.astype(F32)).astype(BF16)
    return hi, mid, lo


def _store_constants(seg_len, ws_ref, bs_ref, wout_ref, lvl_ref, wc_ref, bias_ref, wout16_ref):
    t = lax.broadcasted_iota(jnp.int32, (ROWS, ROWS), 0)
    s = lax.broadcasted_iota(jnp.int32, (ROWS, ROWS), 1)
    n_levels = _n_levels(seg_len)
    x = t ^ s
    lvl = jnp.where(t == s, n_levels, -1)
    for j in range(n_levels):
        lvl = jnp.where(((x >> j) == 1) & (((t >> j) & 1) == 1), j, lvl)
    lvl_ref[...] = lvl
    causal = (t >= s) & ((t >> n_levels) == (s >> n_levels))
    pick = jnp.where(s == (t & (seg_len - 1)), 1.0, 0.0).astype(BF16)
    for hd in range(N_HEADS):
        w = ws_ref[hd].astype(BF16)
        if seg_len != ROWS:
            w = jnp.dot(pick, w, preferred_element_type=F32).astype(BF16)
            w = lax.dot_general(w, pick, _NT, preferred_element_type=F32).astype(BF16)
        wc_ref[hd] = jnp.where(causal, w, jnp.zeros_like(w))
    pad = jnp.zeros((PACKED - N_HEADS, ROWS), F32)
    bias = sum(lax.dot_general(pick, part, _NT, preferred_element_type=F32)
               for part in _split3(jnp.concatenate([bs_ref[...], pad], axis=0)))
    bias_ref[...] = bias[:, :N_HEADS]
    wout16_ref[...] = wout_ref[...].astype(BF16)


def _row_bits():
    rowi = lax.broadcasted_iota(jnp.int32, (SUBLANES, HEAD), 0)
    return [((rowi >> j) & 1) == 1 for j in range(SUBLANE_LEVELS)]


def _hgrn_head(gates, lvl_ref, bits, n_levels):
    one = jnp.ones((SUBLANES, HEAD), F32)
    z = [[] for _ in range(n_levels)]
    qp_t, ks_t, blk_t, diag_t = [], [], [], []
    for pair in range(N_TILES // 2):
        z_pair = [[] for _ in range(SUBLANE_LEVELS)]
        for i in (2 * pair, 2 * pair + 1):
            q, kk, f = gates(i)
            diag_t.append(jnp.sum(q * kk, axis=-1, keepdims=True))
            qp, ks, blk = q * f, kk, f
            for j in range(SUBLANE_LEVELS):
                m = 1 << j
                z_pair[j].append(jnp.where(bits[j], qp, ks))
                sib = pltpu.roll(blk, m, 0)
                if 2 * m != SUBLANES:
                    sib = jnp.where(bits[j], sib, pltpu.roll(blk, SUBLANES - m, 0))
                qp = qp * jnp.where(bits[j], sib, one)
                ks = ks * jnp.where(bits[j], one, sib)
                blk = blk * sib
            qp_t.append(qp)
            ks_t.append(ks)
            blk_t.append(blk)
        for j in range(SUBLANE_LEVELS):
            z[j].append(jnp.concatenate(z_pair[j], axis=0).astype(BF16))
    yield

    blk_levels, blk_b = [], blk_t
    for j in range(SUBLANE_LEVELS, n_levels):
        blk_levels.append(blk_b)
        blk_b = [blk_b[2 * n] * blk_b[2 * n + 1] for n in range(len(blk_b) // 2)]
    for pair in range(N_TILES // 2):
        z_pair = [[] for _ in range(SUBLANE_LEVELS, n_levels)]
        for i in (2 * pair, 2 * pair + 1):
            for j in range(SUBLANE_LEVELS, n_levels):
                block = i >> (j - SUBLANE_LEVELS)
                sib = blk_levels[j - SUBLANE_LEVELS][block ^ 1]
                if block & 1:
                    z_pair[j - SUBLANE_LEVELS].append(qp_t[i])
                    qp_t[i] = qp_t[i] * sib
                else:
                    z_pair[j - SUBLANE_LEVELS].append(ks_t[i])
                    ks_t[i] = ks_t[i] * sib
        for j in range(SUBLANE_LEVELS, n_levels):
            z[j].append(jnp.concatenate(z_pair[j - SUBLANE_LEVELS], axis=0).astype(BF16))
    q_in = jnp.concatenate(qp_t, axis=0)
    k_out = jnp.concatenate(ks_t, axis=0)
    per_tile = N_TILES // len(blk_b)
    total = jnp.concatenate([blk_b[i // per_tile] for i in range(N_TILES)], axis=0)
    yield

    p, p_row = [], []
    for j in range(n_levels):
        zj = jnp.concatenate(z[j], axis=0)
        span = 1 << max(j - SUBLANE_LEVELS, 0)
        if span * SUBLANES >= PACKED:
            upper = [i for i in range(N_TILES) if (i // span) & 1]
            lhs = jnp.concatenate([z[j][i // 2] for i in upper[::2]], axis=0)
        else:
            upper = list(range(N_TILES))
            lhs = zj
        p.append(lax.dot_general(lhs, zj, _NT, preferred_element_type=F32))
        p_row.append({i: n * SUBLANES for n, i in enumerate(upper)})
        yield
    sc_t = []
    for i in range(N_TILES):
        lvl_i = lvl_ref[i * SUBLANES:(i + 1) * SUBLANES, :]
        acc = jnp.where(lvl_i == n_levels, diag_t[i], 0.0)
        for j in range(n_levels):
            if j < SUBLANE_LEVELS or (i >> (j - SUBLANE_LEVELS)) & 1:
                r0 = p_row[j][i]
                acc = jnp.where(lvl_i == j, p[j][r0:r0 + SUBLANES, :], acc)
        sc_t.append(acc)
    sc = jnp.concatenate(sc_t, axis=0).astype(BF16)
    return sc, q_in, k_out, total


def _lockstep(generators, starts=None):
    starts = starts or [0] * len(generators)
    results = [None] * len(generators)
    waiting = sorted(range(len(generators)), key=lambda i: starts[i])
    active = []
    rnd = 0
    while waiting or active:
        while waiting and starts[waiting[0]] <= rnd:
            active.append(waiting.pop(0))
        still = []
        for idx in active:
            try:
                next(generators[idx])
                still.append(idx)
            except StopIteration as done:
                results[idx] = done.value
        active = still
        rnd += 1
    return results


def _hgrn_gates(pq, pf, lb):
    q = _silu(pq)
    forget = lb + (1.0 - lb) * jax.nn.sigmoid(pf)
    return q, 1.0 - forget, forget


def _hgrn_finish(o, pz, g):
    return (_rms_norm(o, g) * _silu(pz)).astype(BF16)


def _mlp_head(hd, u, v_n16, gate, wc_ref, bias):
    mix = jnp.dot(wc_ref[hd], v_n16, preferred_element_type=F32) + bias[:, hd:hd + 1]
    return (_gelu(u) * mix * _silu(gate)).astype(BF16)


def _prompt_kernel(x_ref, xprev_ref, ng_ref, win_ref, lbl_ref, hg_ref, lng_ref, lnb_ref, ws_ref,
                   bs_ref, wout_ref, fg_ref, y_ref, st_ref, vch_ref, proj_a, proj_b, h_ref,
                   mixed_ref, lvl_ref, wc_ref, bias_ref, wout16_ref, *, tiles_per_seq):
    g = pl.program_id(0)

    @pl.when(g == 0)
    def _():
        proj_b[...] = jnp.zeros_like(proj_b)
        _store_constants(ROWS, ws_ref, bs_ref, wout_ref, lvl_ref, wc_ref, bias_ref, wout16_ref)

    @pl.when(lax.rem(jnp.maximum(g - 1, 0), tiles_per_seq) == 0)
    def _():
        st_ref[...] = jnp.zeros_like(st_ref)

    def step(proj_next, proj_prev):
        h_ref[...] = _rms_norm(x_ref[0], ng_ref[...]).astype(BF16)
        lb = _lower_bound(lbl_ref[...])
        bits, n_levels = _row_bits(), _n_levels(ROWS)
        hg = hg_ref[...]
        lng = lng_ref[...]
        lnb = lnb_ref[...]
        bias = bias_ref[...]
        n_chunks = PROMPT_TILE // ROWS
        n_blocks = D_IN // PROJ_BLOCK

        def col(c, k, hs=slice(0, D_A)):
            return proj_prev[c * ROWS:(c + 1) * ROWS, k * D_A + hs.start:k * D_A + hs.stop]

        def head(c, hd):
            hs = slice(hd * HEAD, (hd + 1) * HEAD)

            def gates(i):
                rows = slice(c * ROWS + i * SUBLANES, c * ROWS + (i + 1) * SUBLANES)
                return _hgrn_gates(proj_prev[rows, hs],
                                   proj_prev[rows, D_A + hd * HEAD:D_A + (hd + 1) * HEAD],
                                   lb[:, hs])

            sc, q_in, k_out, total = yield from _hgrn_head(gates, lvl_ref, bits, n_levels)
            state = st_ref[0, 0, hd]
            vh = col(c, 2, hs)
            o = jnp.dot(jnp.concatenate([sc, q_in.astype(BF16)], axis=1),
                        jnp.concatenate([vh.astype(BF16), state.astype(BF16)], axis=0),
                        preferred_element_type=F32)
            yield
            st_ref[0, 0, hd] = total.T * state + jnp.dot(
                k_out.T.astype(BF16), vh.astype(BF16), preferred_element_type=F32)
            yield
            mixed_ref[c * ROWS:(c + 1) * ROWS, hs] = _hgrn_finish(o, col(c, 3, hs), hg[:, hs])

        def mlp(c):
            v_n = _layer_norm(_gelu(col(c, 5)), lng, lnb)
            if c == n_chunks - 1:
                for hd in range(N_HEADS):
                    vch_ref[0, 0, :, hd, :] = v_n[:, hd * HEAD:(hd + 1) * HEAD]
            v_n16 = v_n.astype(BF16)
            yield
            for hd in range(N_HEADS):
                hs = slice(hd * HEAD, (hd + 1) * HEAD)
                mixed_ref[c * ROWS:(c + 1) * ROWS, D_A + hd * HEAD:D_A + (hd + 1) * HEAD] = (
                    _mlp_head(hd, col(c, 4, hs), v_n16[:, hs], col(c, 6, hs), wc_ref, bias))
                yield
                yield

        def projection(rounds_per_block):
            for blk in range(n_blocks):
                cols = slice(blk * PROJ_BLOCK, (blk + 1) * PROJ_BLOCK)
                proj_next[:, cols] = jnp.dot(h_ref[...], win_ref[:, cols],
                                             preferred_element_type=F32)
                for _ in range(rounds_per_block):
                    yield

        items = [gen for c in range(n_chunks)
                 for gen in [head(c, hd) for hd in range(N_HEADS)] + [mlp(c)]]
        starts = [c * CHUNK_STAGGER for c in range(n_chunks) for _ in range(N_HEADS + 1)]
        rounds = (n_chunks - 1) * CHUNK_STAGGER + HEAD_ROUNDS
        _lockstep(items + [projection(max(1, rounds // n_blocks))], starts + [0])

        out = xprev_ref[0] + jnp.dot(mixed_ref[...], wout16_ref[...], preferred_element_type=F32)
        y_ref[0] = _rms_norm(out, fg_ref[...])

    parity = lax.rem(g, 2)

    @pl.when(parity == 0)
    def _():
        step(proj_a, proj_b)

    @pl.when(parity == 1)
    def _():
        step(proj_b, proj_a)


def _sample_kernel(x_ref, ng_ref, win_ref, lbl_ref, hg_ref, lng_ref, lnb_ref, ws_ref, bs_ref,
                   wout_ref, fg_ref, stin_ref, y_ref, stout_ref, vch_ref,
                   qin_ref, kot_ref, tot_ref, oint_ref, lvl_ref, wc_ref, bias_ref, wout16_ref):
    @pl.when(pl.program_id(0) == 0)
    def _():
        _store_constants(DEC_SEQ, ws_ref, bs_ref, wout_ref, lvl_ref, wc_ref, bias_ref, wout16_ref)

    x = x_ref[...]
    h = _rms_norm(x, ng_ref[...]).astype(BF16)
    proj = jnp.dot(h, win_ref[...], preferred_element_type=F32)

    def col(k, hs=slice(0, D_A)):
        return proj[:, k * D_A + hs.start:k * D_A + hs.stop]

    lb = _lower_bound(lbl_ref[...])
    bits, n_levels = _row_bits(), _n_levels(DEC_SEQ)
    vi16 = col(2).astype(BF16)

    def head(hd):
        hs = slice(hd * HEAD, (hd + 1) * HEAD)

        def gates(i):
            rows = slice(i * SUBLANES, (i + 1) * SUBLANES)
            return _hgrn_gates(proj[rows, hs], proj[rows, D_A + hd * HEAD:D_A + (hd + 1) * HEAD],
                               lb[:, hs])

        sc, q_in, k_out, total = yield from _hgrn_head(gates, lvl_ref, bits, n_levels)
        qin_ref[:, hs] = q_in
        kot_ref[hd] = k_out.T
        tot_ref[hd] = total.T
        return jnp.dot(sc, vi16[:, hs], preferred_element_type=F32)

    o_intra = _lockstep([head(hd) for hd in range(N_HEADS)])

    lane = lax.broadcasted_iota(jnp.int32, (HEAD, ROWS), 1)

    def seq_head(i, hd):
        hs = slice(hd * HEAD, (hd + 1) * HEAD)
        seq_rows = slice(i * DEC_SEQ, (i + 1) * DEC_SEQ)
        state = stin_ref[0, i, hd]
        qi = qin_ref[seq_rows, hs].astype(BF16)
        oint_ref[seq_rows, hs] = jnp.dot(qi, state.astype(BF16), preferred_element_type=F32)
        yield
        in_seq = (lane >> (DEC_SEQ.bit_length() - 1)) == i
        k_seq = jnp.where(in_seq, kot_ref[hd], 0.0).astype(BF16)
        decay = tot_ref[hd, :, i * DEC_SEQ:i * DEC_SEQ + 1]
        yield
        stout_ref[0, i, hd] = decay * state + jnp.dot(
            k_seq, vi16[:, hs], preferred_element_type=F32)

    for grp in range(SEQS_PER_TILE // SEQ_GROUP):
        _lockstep([seq_head(grp * SEQ_GROUP + s, hd)
                   for s in range(SEQ_GROUP) for hd in range(N_HEADS)])

    hg = hg_ref[...]
    bias = bias_ref[...]
    v_n = _layer_norm(_gelu(col(5)), lng_ref[...], lnb_ref[...])
    for hd in range(N_HEADS):
        vch_ref[0, :, :, hd, :] = v_n[:, hd * HEAD:(hd + 1) * HEAD].reshape(
            SEQS_PER_TILE, DEC_SEQ, HEAD)
    v_n16 = v_n.astype(BF16)
    mixed = []
    for hd in range(N_HEADS):
        hs = slice(hd * HEAD, (hd + 1) * HEAD)
        mixed.append(_hgrn_finish(o_intra[hd] + oint_ref[:, hs], col(3, hs), hg[:, hs]))
    for hd in range(N_HEADS):
        hs = slice(hd * HEAD, (hd + 1) * HEAD)
        mixed.append(_mlp_head(hd, col(4, hs), v_n16[:, hs], col(6, hs), wc_ref, bias))
    out = x + jnp.dot(jnp.concatenate(mixed, axis=-1), wout16_ref[...],
                      preferred_element_type=F32)
    y_ref[...] = _rms_norm(out, fg_ref[...])


def _const_spec(shape):
    return pl.BlockSpec(shape, lambda *_: (0,) * len(shape), pipeline_mode=pl.Buffered(1))


_WEIGHT_SPECS = [
    _const_spec((1, D_MODEL)),
    _const_spec((D_MODEL, D_IN)),
    _const_spec((2, D_A)),
    _const_spec((1, D_A)),
    _const_spec((1, D_B)),
    _const_spec((1, D_B)),
    _const_spec((N_HEADS, ROWS, ROWS)),
    _const_spec((N_HEADS, ROWS)),
    _const_spec((D_MODEL, D_MODEL)),
    _const_spec((1, D_MODEL)),
]

_CONSTANT_SCRATCH = [
    pltpu.VMEM((ROWS, ROWS), jnp.int32),
    pltpu.VMEM((N_HEADS, ROWS, ROWS), BF16),
    pltpu.VMEM((ROWS, N_HEADS), F32),
    pltpu.VMEM((D_MODEL, D_MODEL), BF16),
]


def kernel(x_prompt, x_sample, state_hgrn, norm_g, w_in, lb_logits, hgrn_norm_g, sgu_ln_g,
           sgu_ln_b, w_s, b_s, w_out, final_norm_g):
    depth = norm_g.shape[0]
    assert depth == 1 and lb_logits.shape == (2, D_A)
    batch, seq, _ = x_prompt.shape
    dec_batch, dec_seq, _ = x_sample.shape
    assert seq % PROMPT_TILE == 0 and dec_seq == DEC_SEQ and dec_batch % SEQS_PER_TILE == 0

    weights = (norm_g, w_in[0].astype(BF16), lb_logits, hgrn_norm_g, sgu_ln_g, sgu_ln_b,
               w_s[0], b_s[0], w_out[0], final_norm_g.reshape(1, D_MODEL))
    params = pltpu.CompilerParams(dimension_semantics=("arbitrary",),
                                  vmem_limit_bytes=VMEM_LIMIT_BYTES)

    tiles_per_seq = seq // PROMPT_TILE
    n_tiles = batch * tiles_per_seq

    def cur_tile(g):
        t = jnp.minimum(g, n_tiles - 1)
        return t // tiles_per_seq, t % tiles_per_seq

    def prev_tile(g):
        t = jnp.maximum(g - 1, 0)
        return t // tiles_per_seq, t % tiles_per_seq

    y_p, st_p, vch_p = pl.pallas_call(
        functools.partial(_prompt_kernel, tiles_per_seq=tiles_per_seq),
        grid=(n_tiles + 1,),
        in_specs=[pl.BlockSpec((1, PROMPT_TILE, D_MODEL), lambda g: (*cur_tile(g), 0)),
                  pl.BlockSpec((1, PROMPT_TILE, D_MODEL), lambda g: (*prev_tile(g), 0))]
        + _WEIGHT_SPECS,
        out_specs=[
            pl.BlockSpec((1, PROMPT_TILE, D_MODEL), lambda g: (*prev_tile(g), 0)),
            pl.BlockSpec((1, 1, N_HEADS, HEAD, HEAD), lambda g: (0, prev_tile(g)[0], 0, 0, 0)),
            pl.BlockSpec((1, 1, ROWS, N_HEADS, HEAD), lambda g: (0, prev_tile(g)[0], 0, 0, 0)),
        ],
        out_shape=[
            jax.ShapeDtypeStruct((batch, seq, D_MODEL), F32),
            jax.ShapeDtypeStruct((1, batch, N_HEADS, HEAD, HEAD), F32),
            jax.ShapeDtypeStruct((1, batch, ROWS, N_HEADS, HEAD), F32),
        ],
        scratch_shapes=[pltpu.VMEM((PROMPT_TILE, D_IN), F32),
                        pltpu.VMEM((PROMPT_TILE, D_IN), F32),
                        pltpu.VMEM((PROMPT_TILE, D_MODEL), BF16),
                        pltpu.VMEM((PROMPT_TILE, D_MODEL), BF16)] + _CONSTANT_SCRATCH,
        compiler_params=params,
        name="prompt_layer",
    )(x_prompt, x_prompt, *weights)

    n_rows = dec_batch * DEC_SEQ
    y_s, st_s, vch_s = pl.pallas_call(
        _sample_kernel,
        grid=(n_rows // ROWS,),
        in_specs=[pl.BlockSpec((ROWS, D_MODEL), lambda i: (i, 0))]
        + _WEIGHT_SPECS
        + [pl.BlockSpec((1, SEQS_PER_TILE, N_HEADS, HEAD, HEAD), lambda i: (0, i, 0, 0, 0))],
        out_specs=[
            pl.BlockSpec((ROWS, D_MODEL), lambda i: (i, 0)),
            pl.BlockSpec((1, SEQS_PER_TILE, N_HEADS, HEAD, HEAD), lambda i: (0, i, 0, 0, 0)),
            pl.BlockSpec((1, SEQS_PER_TILE, DEC_SEQ, N_HEADS, HEAD), lambda i: (0, i, 0, 0, 0)),
        ],
        out_shape=[
            jax.ShapeDtypeStruct((n_rows, D_MODEL), F32),
            jax.ShapeDtypeStruct((1, dec_batch, N_HEADS, HEAD, HEAD), F32),
            jax.ShapeDtypeStruct((1, dec_batch, DEC_SEQ, N_HEADS, HEAD), F32),
        ],
        scratch_shapes=[pltpu.VMEM((ROWS, D_A), F32),
                        pltpu.VMEM((N_HEADS, HEAD, ROWS), F32),
                        pltpu.VMEM((N_HEADS, HEAD, ROWS), F32),
                        pltpu.VMEM((ROWS, D_A), F32)] + _CONSTANT_SCRATCH,
        compiler_params=pltpu.CompilerParams(dimension_semantics=("arbitrary",),
                                             vmem_limit_bytes=VMEM_LIMIT_BYTES),
        name="decode_layer",
    )(x_sample.reshape(n_rows, D_MODEL), *weights, state_hgrn)

    return (y_p,
            y_s.reshape(dec_batch, DEC_SEQ, D_MODEL),
            st_p,
            st_s,
            vch_p,
            vch_s)
```

```python
import functools

import jax
import jax.numpy as jnp
from jax import lax
from jax.experimental import pallas as pl
from jax.experimental.pallas import tpu as pltpu

F32 = jnp.float32
BF16 = jnp.bfloat16

D_MODEL = 1024
D_A = 512
D_B = 512
HEAD = 128
N_HEADS = 4
D_IN = 4 * D_A + 3 * D_B
ROWS = 128
SUBLANES = 8
PACKED = 2 * SUBLANES
N_TILES = ROWS // SUBLANES
SUBLANE_LEVELS = 3
EPS = 1e-6
PROMPT_TILE = 512
PROJ_BLOCK = 512
HEAD_ROUNDS = 12
CHUNK_STAGGER = 2
DEC_SEQ = 8
SEQS_PER_TILE = ROWS // DEC_SEQ
SEQ_GROUP = 4
VMEM_LIMIT_BYTES = 56 * 1024 * 1024

_NT = (((1,), (1,)), ((), ()))


def _silu(x):
    return x * jax.nn.sigmoid(x)


def _gelu(x):
    c = -2.0 * 0.7978845608028654 * 1.4426950408889634
    return x / (1.0 + jnp.exp2(x * (c * 0.044715 * (x * x) + c)))


def _rms_norm(x, g):
    return x * lax.rsqrt(jnp.mean(x * x, axis=-1, keepdims=True) + EPS) * g


def _layer_norm(x, g, b):
    mu = jnp.mean(x, axis=-1, keepdims=True)
    xc = x - mu
    var = jnp.mean(xc * xc, axis=-1, keepdims=True)
    return xc * lax.rsqrt(var + EPS) * g + b


def _lower_bound(lbl):
    rows = [lbl[r:r + 1, :] for r in range(lbl.shape[0])]
    mx = functools.reduce(jnp.maximum, rows)
    es = [jnp.exp(r - mx) for r in rows]
    return es[0] / functools.reduce(lambda a, c: a + c, es)


def _n_levels(seg_len):
    n_levels = seg_len.bit_length() - 1
    assert seg_len == 1 << n_levels and n_levels >= SUBLANE_LEVELS
    return n_levels


def _split3(x):
    hi = x.astype(BF16)
    r1 = x - hi.astype(F32)
    mid = r1.astype(BF16)
    lo = (r1 - mid.astype(F32)).astype(BF16)
    return hi, mid, lo


def _store_constants(seg_len, ws_ref, bs_ref, wout_ref, lvl_ref, wc_ref, bias_ref, wout16_ref):
    t = lax.broadcasted_iota(jnp.int32, (ROWS, ROWS), 0)
    s = lax.broadcasted_iota(jnp.int32, (ROWS, ROWS), 1)
    n_levels = _n_levels(seg_len)
    x = t ^ s
    lvl = jnp.where(t == s, n_levels, -1)
    for j in range(n_levels):
        lvl = jnp.where(((x >> j) == 1) & (((t >> j) & 1) == 1), j, lvl)
    lvl_ref[...] = lvl
    causal = (t >= s) & ((t >> n_levels) == (s >> n_levels))
    pick = jnp.where(s == (t & (seg_len - 1)), 1.0, 0.0).astype(BF16)
    for hd in range(N_HEADS):
        w = ws_ref[hd].astype(BF16)
        if seg_len != ROWS:
            w = jnp.dot(pick, w, preferred_element_type=F32).astype(BF16)
            w = lax.dot_general(w, pick, _NT, preferred_element_type=F32).astype(BF16)
        wc_ref[hd] = jnp.where(causal, w, jnp.zeros_like(w))
    pad = jnp.zeros((PACKED - N_HEADS, ROWS), F32)
    bias = sum(lax.dot_general(pick, part, _NT, preferred_element_type=F32)
               for part in _split3(jnp.concatenate([bs_ref[...], pad], axis=0)))
    bias_ref[...] = bias[:, :N_HEADS]
    wout16_ref[...] = wout_ref[...].astype(BF16)


def _row_bits():
    rowi = lax.broadcasted_iota(jnp.int32, (SUBLANES, HEAD), 0)
    return [((rowi >> j) & 1) == 1 for j in range(SUBLANE_LEVELS)]


def _hgrn_head(gates, lvl_ref, bits, n_levels):
    one = jnp.ones((SUBLANES, HEAD), F32)
    z = [[] for _ in range(n_levels)]
    qp_t, ks_t, blk_t, diag_t = [], [], [], []
    for pair in range(N_TILES // 2):
        z_pair = [[] for _ in range(SUBLANE_LEVELS)]
        for i in (2 * pair, 2 * pair + 1):
            q, kk, f = gates(i)
            diag_t.append(jnp.sum(q * kk, axis=-1, keepdims=True))
            qp, ks, blk = q * f, kk, f
            for j in range(SUBLANE_LEVELS):
                m = 1 << j
                z_pair[j].append(jnp.where(bits[j], qp, ks))
                sib = pltpu.roll(blk, m, 0)
                if 2 * m != SUBLANES:
                    sib = jnp.where(bits[j], sib, pltpu.roll(blk, SUBLANES - m, 0))
                qp = qp * jnp.where(bits[j], sib, one)
                ks = ks * jnp.where(bits[j], one, sib)
                blk = blk * sib
            qp_t.append(qp)
            ks_t.append(ks)
            blk_t.append(blk)
        for j in range(SUBLANE_LEVELS):
            z[j].append(jnp.concatenate(z_pair[j], axis=0).astype(BF16))
    yield

    blk_levels, blk_b = [], blk_t
    for j in range(SUBLANE_LEVELS, n_levels):
        blk_levels.append(blk_b)
        blk_b = [blk_b[2 * n] * blk_b[2 * n + 1] for n in range(len(blk_b) // 2)]
    for pair in range(N_TILES // 2):
        z_pair = [[] for _ in range(SUBLANE_LEVELS, n_levels)]
        for i in (2 * pair, 2 * pair + 1):
            for j in range(SUBLANE_LEVELS, n_levels):
                block = i >> (j - SUBLANE_LEVELS)
                sib = blk_levels[j - SUBLANE_LEVELS][block ^ 1]
                if block & 1:
                    z_pair[j - SUBLANE_LEVELS].append(qp_t[i])
                    qp_t[i] = qp_t[i] * sib
                else:
                    z_pair[j - SUBLANE_LEVELS].append(ks_t[i])
                    ks_t[i] = ks_t[i] * sib
        for j in range(SUBLANE_LEVELS, n_levels):
            z[j].append(jnp.concatenate(z_pair[j - SUBLANE_LEVELS], axis=0).astype(BF16))
    q_in = jnp.concatenate(qp_t, axis=0)
    k_out = jnp.concatenate(ks_t, axis=0)
    per_tile = N_TILES // len(blk_b)
    total = jnp.concatenate([blk_b[i // per_tile] for i in range(N_TILES)], axis=0)
    yield

    p, p_row = [], []
    for j in range(n_levels):
        zj = jnp.concatenate(z[j], axis=0)
        span = 1 << max(j - SUBLANE_LEVELS, 0)
        if span * SUBLANES >= PACKED:
            upper = [i for i in range(N_TILES) if (i // span) & 1]
            lhs = jnp.concatenate([z[j][i // 2] for i in upper[::2]], axis=0)
        else:
            upper = list(range(N_TILES))
            lhs = zj
        p.append(lax.dot_general(lhs, zj, _NT, preferred_element_type=F32))
        p_row.append({i: n * SUBLANES for n, i in enumerate(upper)})
        yield
    sc_t = []
    for i in range(N_TILES):
        lvl_i = lvl_ref[i * SUBLANES:(i + 1) * SUBLANES, :]
        acc = jnp.where(lvl_i == n_levels, diag_t[i], 0.0)
        for j in range(n_levels):
            if j < SUBLANE_LEVELS or (i >> (j - SUBLANE_LEVELS)) & 1:
                r0 = p_row[j][i]
                acc = jnp.where(lvl_i == j, p[j][r0:r0 + SUBLANES, :], acc)
        sc_t.append(acc)
    sc = jnp.concatenate(sc_t, axis=0).astype(BF16)
    return sc, q_in, k_out, total


def _lockstep(generators, starts=None):
    starts = starts or [0] * len(generators)
    results = [None] * len(generators)
    waiting = sorted(range(len(generators)), key=lambda i: starts[i])
    active = []
    rnd = 0
    while waiting or active:
        while waiting and starts[waiting[0]] <= rnd:
            active.append(waiting.pop(0))
        still = []
        for idx in active:
            try:
                next(generators[idx])
                still.append(idx)
            except StopIteration as done:
                results[idx] = done.value
        active = still
        rnd += 1
    return results


def _hgrn_gates(pq, pf, lb):
    q = _silu(pq)
    forget = lb + (1.0 - lb) * jax.nn.sigmoid(pf)
    return q, 1.0 - forget, forget


def _hgrn_finish(o, pz, g):
    return (_rms_norm(o, g) * _silu(pz)).astype(BF16)


def _mlp_head(hd, u, v_n16, gate, wc_ref, bias):
    mix = jnp.dot(wc_ref[hd], v_n16, preferred_element_type=F32) + bias[:, hd:hd + 1]
    return (_gelu(u) * mix * _silu(gate)).astype(BF16)


def _prompt_kernel(x_ref, xprev_ref, ng_ref, win_ref, lbl_ref, hg_ref, lng_ref, lnb_ref, ws_ref,
                   bs_ref, wout_ref, fg_ref, y_ref, st_ref, vch_ref, proj_a, proj_b, h_ref,
                   mixed_ref, lvl_ref, wc_ref, bias_ref, wout16_ref, *, tiles_per_seq):
    g = pl.program_id(0)

    @pl.when(g == 0)
    def _():
        proj_b[...] = jnp.zeros_like(proj_b)
        _store_constants(ROWS, ws_ref, bs_ref, wout_ref, lvl_ref, wc_ref, bias_ref, wout16_ref)

    @pl.when(lax.rem(jnp.maximum(g - 1, 0), tiles_per_seq) == 0)
    def _():
        st_ref[...] = jnp.zeros_like(st_ref)

    def step(proj_next, proj_prev):
        h_ref[...] = _rms_norm(x_ref[0], ng_ref[...]).astype(BF16)
        lb = _lower_bound(lbl_ref[...])
        bits, n_levels = _row_bits(), _n_levels(ROWS)
        hg = hg_ref[...]
        lng = lng_ref[...]
        lnb = lnb_ref[...]
        bias = bias_ref[...]
        n_chunks = PROMPT_TILE // ROWS
        n_blocks = D_IN // PROJ_BLOCK

        def col(c, k, hs=slice(0, D_A)):
            return proj_prev[c * ROWS:(c + 1) * ROWS, k * D_A + hs.start:k * D_A + hs.stop]

        def head(c, hd):
            hs = slice(hd * HEAD, (hd + 1) * HEAD)

            def gates(i):
                rows = slice(c * ROWS + i * SUBLANES, c * ROWS + (i + 1) * SUBLANES)
                return _hgrn_gates(proj_prev[rows, hs],
                                   proj_prev[rows, D_A + hd * HEAD:D_A + (hd + 1) * HEAD],
                                   lb[:, hs])

            sc, q_in, k_out, total = yield from _hgrn_head(gates, lvl_ref, bits, n_levels)
            state = st_ref[0, 0, hd]
            vh = col(c, 2, hs)
            o = jnp.dot(jnp.concatenate([sc, q_in.astype(BF16)], axis=1),
                        jnp.concatenate([vh.astype(BF16), state.astype(BF16)], axis=0),
                        preferred_element_type=F32)
            yield
            st_ref[0, 0, hd] = total.T * state + jnp.dot(
                k_out.T.astype(BF16), vh.astype(BF16), preferred_element_type=F32)
            yield
            mixed_ref[c * ROWS:(c + 1) * ROWS, hs] = _hgrn_finish(o, col(c, 3, hs), hg[:, hs])

        def mlp(c):
            v_n = _layer_norm(_gelu(col(c, 5)), lng, lnb)
            if c == n_chunks - 1:
                for hd in range(N_HEADS):
                    vch_ref[0, 0, :, hd, :] = v_n[:, hd * HEAD:(hd + 1) * HEAD]
            v_n16 = v_n.astype(BF16)
            yield
            for hd in range(N_HEADS):
                hs = slice(hd * HEAD, (hd + 1) * HEAD)
                mixed_ref[c * ROWS:(c + 1) * ROWS, D_A + hd * HEAD:D_A + (hd + 1) * HEAD] = (
                    _mlp_head(hd, col(c, 4, hs), v_n16[:, hs], col(c, 6, hs), wc_ref, bias))
                yield
                yield

        def projection(rounds_per_block):
            for blk in range(n_blocks):
                cols = slice(blk * PROJ_BLOCK, (blk + 1) * PROJ_BLOCK)
                proj_next[:, cols] = jnp.dot(h_ref[...], win_ref[:, cols],
                                             preferred_element_type=F32)
                for _ in range(rounds_per_block):
                    yield

        items = [gen for c in range(n_chunks)
                 for gen in [head(c, hd) for hd in range(N_HEADS)] + [mlp(c)]]
        starts = [c * CHUNK_STAGGER for c in range(n_chunks) for _ in range(N_HEADS + 1)]
        rounds = (n_chunks - 1) * CHUNK_STAGGER + HEAD_ROUNDS
        _lockstep(items + [projection(max(1, rounds // n_blocks))], starts + [0])

        out = xprev_ref[0] + jnp.dot(mixed_ref[...], wout16_ref[...], preferred_element_type=F32)
        y_ref[0] = _rms_norm(out, fg_ref[...])

    parity = lax.rem(g, 2)

    @pl.when(parity == 0)
    def _():
        step(proj_a, proj_b)

    @pl.when(parity == 1)
    def _():
        step(proj_b, proj_a)


def _sample_kernel(x_ref, ng_ref, win_ref, lbl_ref, hg_ref, lng_ref, lnb_ref, ws_ref, bs_ref,
                   wout_ref, fg_ref, stin_ref, y_ref, stout_ref, vch_ref,
                   qin_ref, kot_ref, tot_ref, oint_ref, lvl_ref, wc_ref, bias_ref, wout16_ref):
    @pl.when(pl.program_id(0) == 0)
    def _():
        _store_constants(DEC_SEQ, ws_ref, bs_ref, wout_ref, lvl_ref, wc_ref, bias_ref, wout16_ref)

    x = x_ref[...]
    h = _rms_norm(x, ng_ref[...]).astype(BF16)
    proj = jnp.dot(h, win_ref[...], preferred_element_type=F32)

    def col(k, hs=slice(0, D_A)):
        return proj[:, k * D_A + hs.start:k * D_A + hs.stop]

    lb = _lower_bound(lbl_ref[...])
    bits, n_levels = _row_bits(), _n_levels(DEC_SEQ)
    vi16 = col(2).astype(BF16)

    def head(hd):
        hs = slice(hd * HEAD, (hd + 1) * HEAD)

        def gates(i):
            rows = slice(i * SUBLANES, (i + 1) * SUBLANES)
            return _hgrn_gates(proj[rows, hs], proj[rows, D_A + hd * HEAD:D_A + (hd + 1) * HEAD],
                               lb[:, hs])

        sc, q_in, k_out, total = yield from _hgrn_head(gates, lvl_ref, bits, n_levels)
        qin_ref[:, hs] = q_in
        kot_ref[hd] = k_out.T
        tot_ref[hd] = total.T
        return jnp.dot(sc, vi16[:, hs], preferred_element_type=F32)

    o_intra = _lockstep([head(hd) for hd in range(N_HEADS)])

    lane = lax.broadcasted_iota(jnp.int32, (HEAD, ROWS), 1)

    def seq_head(i, hd):
        hs = slice(hd * HEAD, (hd + 1) * HEAD)
        seq_rows = slice(i * DEC_SEQ, (i + 1) * DEC_SEQ)
        state = stin_ref[0, i, hd]
        qi = qin_ref[seq_rows, hs].astype(BF16)
        oint_ref[seq_rows, hs] = jnp.dot(qi, state.astype(BF16), preferred_element_type=F32)
        yield
        in_seq = (lane >> (DEC_SEQ.bit_length() - 1)) == i
        k_seq = jnp.where(in_seq, kot_ref[hd], 0.0).astype(BF16)
        decay = tot_ref[hd, :, i * DEC_SEQ:i * DEC_SEQ + 1]
        yield
        stout_ref[0, i, hd] = decay * state + jnp.dot(
            k_seq, vi16[:, hs], preferred_element_type=F32)

    for grp in range(SEQS_PER_TILE // SEQ_GROUP):
        _lockstep([seq_head(grp * SEQ_GROUP + s, hd)
                   for s in range(SEQ_GROUP) for hd in range(N_HEADS)])

    hg = hg_ref[...]
    bias = bias_ref[...]
    v_n = _layer_norm(_gelu(col(5)), lng_ref[...], lnb_ref[...])
    for hd in range(N_HEADS):
        vch_ref[0, :, :, hd, :] = v_n[:, hd * HEAD:(hd + 1) * HEAD].reshape(
            SEQS_PER_TILE, DEC_SEQ, HEAD)
    v_n16 = v_n.astype(BF16)
    mixed = []
    for hd in range(N_HEADS):
        hs = slice(hd * HEAD, (hd + 1) * HEAD)
        mixed.append(_hgrn_finish(o_intra[hd] + oint_ref[:, hs], col(3, hs), hg[:, hs]))
    for hd in range(N_HEADS):
        hs = slice(hd * HEAD, (hd + 1) * HEAD)
        mixed.append(_mlp_head(hd, col(4, hs), v_n16[:, hs], col(6, hs), wc_ref, bias))
    out = x + jnp.dot(jnp.concatenate(mixed, axis=-1), wout16_ref[...],
                      preferred_element_type=F32)
    y_ref[...] = _rms_norm(out, fg_ref[...])


def _const_spec(shape):
    return pl.BlockSpec(shape, lambda *_: (0,) * len(shape), pipeline_mode=pl.Buffered(1))


_WEIGHT_SPECS = [
    _const_spec((1, D_MODEL)),
    _const_spec((D_MODEL, D_IN)),
    _const_spec((2, D_A)),
    _const_spec((1, D_A)),
    _const_spec((1, D_B)),
    _const_spec((1, D_B)),
    _const_spec((N_HEADS, ROWS, ROWS)),
    _const_spec((N_HEADS, ROWS)),
    _const_spec((D_MODEL, D_MODEL)),
    _const_spec((1, D_MODEL)),
]

_CONSTANT_SCRATCH = [
    pltpu.VMEM((ROWS, ROWS), jnp.int32),
    pltpu.VMEM((N_HEADS, ROWS, ROWS), BF16),
    pltpu.VMEM((ROWS, N_HEADS), F32),
    pltpu.VMEM((D_MODEL, D_MODEL), BF16),
]


def kernel(x_prompt, x_sample, state_hgrn, norm_g, w_in, lb_logits, hgrn_norm_g, sgu_ln_g,
           sgu_ln_b, w_s, b_s, w_out, final_norm_g):
    depth = norm_g.shape[0]
    assert depth == 1 and lb_logits.shape == (2, D_A)
    batch, seq, _ = x_prompt.shape
    dec_batch, dec_seq, _ = x_sample.shape
    assert seq % PROMPT_TILE == 0 and dec_seq == DEC_SEQ and dec_batch % SEQS_PER_TILE == 0

    weights = (norm_g, w_in[0].astype(BF16), lb_logits, hgrn_norm_g, sgu_ln_g, sgu_ln_b,
               w_s[0], b_s[0], w_out[0], final_norm_g.reshape(1, D_MODEL))
    params = pltpu.CompilerParams(dimension_semantics=("arbitrary",),
                                  vmem_limit_bytes=VMEM_LIMIT_BYTES)

    tiles_per_seq = seq // PROMPT_TILE
    n_tiles = batch * tiles_per_seq

    def cur_tile(g):
        t = jnp.minimum(g, n_tiles - 1)
        return t // tiles_per_seq, t % tiles_per_seq

    def prev_tile(g):
        t = jnp.maximum(g - 1, 0)
        return t // tiles_per_seq, t % tiles_per_seq

    y_p, st_p, vch_p = pl.pallas_call(
        functools.partial(_prompt_kernel, tiles_per_seq=tiles_per_seq),
        grid=(n_tiles + 1,),
        in_specs=[pl.BlockSpec((1, PROMPT_TILE, D_MODEL), lambda g: (*cur_tile(g), 0)),
                  pl.BlockSpec((1, PROMPT_TILE, D_MODEL), lambda g: (*prev_tile(g), 0))]
        + _WEIGHT_SPECS,
        out_specs=[
            pl.BlockSpec((1, PROMPT_TILE, D_MODEL), lambda g: (*prev_tile(g), 0)),
            pl.BlockSpec((1, 1, N_HEADS, HEAD, HEAD), lambda g: (0, prev_tile(g)[0], 0, 0, 0)),
            pl.BlockSpec((1, 1, ROWS, N_HEADS, HEAD), lambda g: (0, prev_tile(g)[0], 0, 0, 0)),
        ],
        out_shape=[
            jax.ShapeDtypeStruct((batch, seq, D_MODEL), F32),
            jax.ShapeDtypeStruct((1, batch, N_HEADS, HEAD, HEAD), F32),
            jax.ShapeDtypeStruct((1, batch, ROWS, N_HEADS, HEAD), F32),
        ],
        scratch_shapes=[pltpu.VMEM((PROMPT_TILE, D_IN), F32),
                        pltpu.VMEM((PROMPT_TILE, D_IN), F32),
                        pltpu.VMEM((PROMPT_TILE, D_MODEL), BF16),
                        pltpu.VMEM((PROMPT_TILE, D_MODEL), BF16)] + _CONSTANT_SCRATCH,
        compiler_params=params,
        name="prompt_layer",
    )(x_prompt, x_prompt, *weights)

    n_rows = dec_batch * DEC_SEQ
    y_s, st_s, vch_s = pl.pallas_call(
        _sample_kernel,
        grid=(n_rows // ROWS,),
        in_specs=[pl.BlockSpec((ROWS, D_MODEL), lambda i: (i, 0))]
        + _WEIGHT_SPECS
        + [pl.BlockSpec((1, SEQS_PER_TILE, N_HEADS, HEAD, HEAD), lambda i: (0, i, 0, 0, 0))],
        out_specs=[
            pl.BlockSpec((ROWS, D_MODEL), lambda i: (i, 0)),
            pl.BlockSpec((1, SEQS_PER_TILE, N_HEADS, HEAD, HEAD), lambda i: (0, i, 0, 0, 0)),
            pl.BlockSpec((1, SEQS_PER_TILE, DEC_SEQ, N_HEADS, HEAD), lambda i: (0, i, 0, 0, 0)),
        ],
        out_shape=[
            jax.ShapeDtypeStruct((n_rows, D_MODEL), F32),
            jax.ShapeDtypeStruct((1, dec_batch, N_HEADS, HEAD, HEAD), F32),
            jax.ShapeDtypeStruct((1, dec_batch, DEC_SEQ, N_HEADS, HEAD), F32),
        ],
        scratch_shapes=[pltpu.VMEM((ROWS, D_A), F32),
                        pltpu.VMEM((N_HEADS, HEAD, ROWS), F32),
                        pltpu.VMEM((N_HEADS, HEAD, ROWS), F32),
                        pltpu.VMEM((ROWS, D_A), F32)] + _CONSTANT_SCRATCH,
        compiler_params=pltpu.CompilerParams(dimension_semantics=("arbitrary",),
                                             vmem_limit_bytes=VMEM_LIMIT_BYTES),
        name="decode_layer",
    )(x_sample.reshape(n_rows, D_MODEL), *weights, state_hgrn)

    return (y_p,
            y_s.reshape(dec_batch, DEC_SEQ, D_MODEL),
            st_p,
            st_s,
            vch_p,
            vch_s)
```

```python
import functools

import jax
import jax.numpy as jnp
from jax import lax
from jax.experimental import pallas as pl
from jax.experimental.pallas import tpu as pltpu

F32 = jnp.float32
BF16 = jnp.bfloat16

D_MODEL = 1024
D_A = 512
D_B = 512
HEAD = 128
N_HEADS = 4
D_IN = 4 * D_A + 3 * D_B
ROWS = 128
SUBLANES = 8
PACKED = 2 * SUBLANES
N_TILES = ROWS // SUBLANES
SUBLANE_LEVELS = 3
EPS = 1e-6
PROMPT_TILE = 512
PROJ_BLOCK = 512
HEAD_ROUNDS = 12
CHUNK_STAGGER = 4
DEC_SEQ = 8
SEQS_PER_TILE = ROWS // DEC_SEQ
SEQ_GROUP = 4
VMEM_LIMIT_BYTES = 56 * 1024 * 1024

_NT = (((1,), (1,)), ((), ()))


def _silu(x):
    return x * jax.nn.sigmoid(x)


def _gelu(x):
    c = -2.0 * 0.7978845608028654 * 1.4426950408889634
    return x / (1.0 + jnp.exp2(x * (c * 0.044715 * (x * x) + c)))


def _rms_norm(x, g):
    return x * lax.rsqrt(jnp.mean(x * x, axis=-1, keepdims=True) + EPS) * g


def _layer_norm(x, g, b):
    mu = jnp.mean(x, axis=-1, keepdims=True)
    xc = x - mu
    var = jnp.mean(xc * xc, axis=-1, keepdims=True)
    return xc * lax.rsqrt(var + EPS) * g + b


def _lower_bound(lbl):
    rows = [lbl[r:r + 1, :] for r in range(lbl.shape[0])]
    mx = functools.reduce(jnp.maximum, rows)
    es = [jnp.exp(r - mx) for r in rows]
    return es[0] / functools.reduce(lambda a, c: a + c, es)


def _n_levels(seg_len):
    n_levels = seg_len.bit_length() - 1
    assert seg_len == 1 << n_levels and n_levels >= SUBLANE_LEVELS
    return n_levels


def _split3(x):
    hi = x.astype(BF16)
    r1 = x - hi.astype(F32)
    mid = r1.astype(BF16)
    lo = (r1 - mid.astype(F32)).astype(BF16)
    return hi, mid, lo


def _store_constants(seg_len, ws_ref, bs_ref, wout_ref, lvl_ref, wc_ref, bias_ref, wout16_ref):
    t = lax.broadcasted_iota(jnp.int32, (ROWS, ROWS), 0)
    s = lax.broadcasted_iota(jnp.int32, (ROWS, ROWS), 1)
    n_levels = _n_levels(seg_len)
    x = t ^ s
    lvl = jnp.where(t == s, n_levels, -1)
    for j in range(n_levels):
        lvl = jnp.where(((x >> j) == 1) & (((t >> j) & 1) == 1), j, lvl)
    lvl_ref[...] = lvl
    causal = (t >= s) & ((t >> n_levels) == (s >> n_levels))
    pick = jnp.where(s == (t & (seg_len - 1)), 1.0, 0.0).astype(BF16)
    for hd in range(N_HEADS):
        w = ws_ref[hd].astype(BF16)
        if seg_len != ROWS:
            w = jnp.dot(pick, w, preferred_element_type=F32).astype(BF16)
            w = lax.dot_general(w, pick, _NT, preferred_element_type=F32).astype(BF16)
        wc_ref[hd] = jnp.where(causal, w, jnp.zeros_like(w))
    pad = jnp.zeros((PACKED - N_HEADS, ROWS), F32)
    bias = sum(lax.dot_general(pick, part, _NT, preferred_element_type=F32)
               for part in _split3(jnp.concatenate([bs_ref[...], pad], axis=0)))
    bias_ref[...] = bias[:, :N_HEADS]
    wout16_ref[...] = wout_ref[...].astype(BF16)


def _row_bits():
    rowi = lax.broadcasted_iota(jnp.int32, (SUBLANES, HEAD), 0)
    return [((rowi >> j) & 1) == 1 for j in range(SUBLANE_LEVELS)]


def _hgrn_head(gates, lvl_ref, bits, n_levels):
    one = jnp.ones((SUBLANES, HEAD), F32)
    z = [[] for _ in range(n_levels)]
    qp_t, ks_t, blk_t, diag_t = [], [], [], []
    for pair in range(N_TILES // 2):
        z_pair = [[] for _ in range(SUBLANE_LEVELS)]
        for i in (2 * pair, 2 * pair + 1):
            q, kk, f = gates(i)
            diag_t.append(jnp.sum(q * kk, axis=-1, keepdims=True))
            qp, ks, blk = q * f, kk, f
            for j in range(SUBLANE_LEVELS):
                m = 1 << j
                z_pair[j].append(jnp.where(bits[j], qp, ks))
                sib = pltpu.roll(blk, m, 0)
                if 2 * m != SUBLANES:
                    sib = jnp.where(bits[j], sib, pltpu.roll(blk, SUBLANES - m, 0))
                qp = qp * jnp.where(bits[j], sib, one)
                ks = ks * jnp.where(bits[j], one, sib)
                blk = blk * sib
            qp_t.append(qp)
            ks_t.append(ks)
            blk_t.append(blk)
        for j in range(SUBLANE_LEVELS):
            z[j].append(jnp.concatenate(z_pair[j], axis=0).astype(BF16))
    yield

    blk_levels, blk_b = [], blk_t
    for j in range(SUBLANE_LEVELS, n_levels):
        blk_levels.append(blk_b)
        blk_b = [blk_b[2 * n] * blk_b[2 * n + 1] for n in range(len(blk_b) // 2)]
    for pair in range(N_TILES // 2):
        z_pair = [[] for _ in range(SUBLANE_LEVELS, n_levels)]
        for i in (2 * pair, 2 * pair + 1):
            for j in range(SUBLANE_LEVELS, n_levels):
                block = i >> (j - SUBLANE_LEVELS)
                sib = blk_levels[j - SUBLANE_LEVELS][block ^ 1]
                if block & 1:
                    z_pair[j - SUBLANE_LEVELS].append(qp_t[i])
                    qp_t[i] = qp_t[i] * sib
                else:
                    z_pair[j - SUBLANE_LEVELS].append(ks_t[i])
                    ks_t[i] = ks_t[i] * sib
        for j in range(SUBLANE_LEVELS, n_levels):
            z[j].append(jnp.concatenate(z_pair[j - SUBLANE_LEVELS], axis=0).astype(BF16))
    q_in = jnp.concatenate(qp_t, axis=0)
    k_out = jnp.concatenate(ks_t, axis=0)
    per_tile = N_TILES // len(blk_b)
    total = jnp.concatenate([blk_b[i // per_tile] for i in range(N_TILES)], axis=0)
    yield

    p, p_row = [], []
    for j in range(n_levels):
        zj = jnp.concatenate(z[j], axis=0)
        span = 1 << max(j - SUBLANE_LEVELS, 0)
        if span * SUBLANES >= PACKED:
            upper = [i for i in range(N_TILES) if (i // span) & 1]
            lhs = jnp.concatenate([z[j][i // 2] for i in upper[::2]], axis=0)
        else:
            upper = list(range(N_TILES))
            lhs = zj
        p.append(lax.dot_general(lhs, zj, _NT, preferred_element_type=F32))
        p_row.append({i: n * SUBLANES for n, i in enumerate(upper)})
        yield
    sc_t = []
    for i in range(N_TILES):
        lvl_i = lvl_ref[i * SUBLANES:(i + 1) * SUBLANES, :]
        acc = jnp.where(lvl_i == n_levels, diag_t[i], 0.0)
        for j in range(n_levels):
            if j < SUBLANE_LEVELS or (i >> (j - SUBLANE_LEVELS)) & 1:
                r0 = p_row[j][i]
                acc = jnp.where(lvl_i == j, p[j][r0:r0 + SUBLANES, :], acc)
        sc_t.append(acc)
    sc = jnp.concatenate(sc_t, axis=0).astype(BF16)
    return sc, q_in, k_out, total


def _lockstep(generators, starts=None):
    starts = starts or [0] * len(generators)
    results = [None] * len(generators)
    waiting = sorted(range(len(generators)), key=lambda i: starts[i])
    active = []
    rnd = 0
    while waiting or active:
        while waiting and starts[waiting[0]] <= rnd:
            active.append(waiting.pop(0))
        still = []
        for idx in active:
            try:
                next(generators[idx])
                still.append(idx)
            except StopIteration as done:
                results[idx] = done.value
        active = still
        rnd += 1
    return results


def _hgrn_gates(pq, pf, lb):
    q = _silu(pq)
    forget = lb + (1.0 - lb) * jax.nn.sigmoid(pf)
    return q, 1.0 - forget, forget


def _hgrn_finish(o, pz, g):
    return (_rms_norm(o, g) * _silu(pz)).astype(BF16)


def _mlp_head(hd, u, v_n16, gate, wc_ref, bias):
    mix = jnp.dot(wc_ref[hd], v_n16, preferred_element_type=F32) + bias[:, hd:hd + 1]
    return (_gelu(u) * mix * _silu(gate)).astype(BF16)


def _prompt_kernel(x_ref, xprev_ref, ng_ref, win_ref, lbl_ref, hg_ref, lng_ref, lnb_ref, ws_ref,
                   bs_ref, wout_ref, fg_ref, y_ref, st_ref, vch_ref, proj_a, proj_b, h_ref,
                   mixed_ref, lvl_ref, wc_ref, bias_ref, wout16_ref, *, tiles_per_seq):
    g = pl.program_id(0)

    @pl.when(g == 0)
    def _():
        proj_b[...] = jnp.zeros_like(proj_b)
        _store_constants(ROWS, ws_ref, bs_ref, wout_ref, lvl_ref, wc_ref, bias_ref, wout16_ref)

    @pl.when(lax.rem(jnp.maximum(g - 1, 0), tiles_per_seq) == 0)
    def _():
        st_ref[...] = jnp.zeros_like(st_ref)

    def step(proj_next, proj_prev):
        h_ref[...] = _rms_norm(x_ref[0], ng_ref[...]).astype(BF16)
        lb = _lower_bound(lbl_ref[...])
        bits, n_levels = _row_bits(), _n_levels(ROWS)
        hg = hg_ref[...]
        lng = lng_ref[...]
        lnb = lnb_ref[...]
        bias = bias_ref[...]
        n_chunks = PROMPT_TILE // ROWS
        n_blocks = D_IN // PROJ_BLOCK

        def col(c, k, hs=slice(0, D_A)):
            return proj_prev[c * ROWS:(c + 1) * ROWS, k * D_A + hs.start:k * D_A + hs.stop]

        def head(c, hd):
            hs = slice(hd * HEAD, (hd + 1) * HEAD)

            def gates(i):
                rows = slice(c * ROWS + i * SUBLANES, c * ROWS + (i + 1) * SUBLANES)
                return _hgrn_gates(proj_prev[rows, hs],
                                   proj_prev[rows, D_A + hd * HEAD:D_A + (hd + 1) * HEAD],
                                   lb[:, hs])

            sc, q_in, k_out, total = yield from _hgrn_head(gates, lvl_ref, bits, n_levels)
            state = st_ref[0, 0, hd]
            vh = col(c, 2, hs)
            o = jnp.dot(jnp.concatenate([sc, q_in.astype(BF16)], axis=1),
                        jnp.concatenate([vh.astype(BF16), state.astype(BF16)], axis=0),
                        preferred_element_type=F32)
            yield
            st_ref[0, 0, hd] = total.T * state + jnp.dot(
                k_out.T.astype(BF16), vh.astype(BF16), preferred_element_type=F32)
            yield
            mixed_ref[c * ROWS:(c + 1) * ROWS, hs] = _hgrn_finish(o, col(c, 3, hs), hg[:, hs])

        def mlp(c):
            v_n = _layer_norm(_gelu(col(c, 5)), lng, lnb)
            if c == n_chunks - 1:
                for hd in range(N_HEADS):
                    vch_ref[0, 0, :, hd, :] = v_n[:, hd * HEAD:(hd + 1) * HEAD]
            v_n16 = v_n.astype(BF16)
            yield
            for hd in range(N_HEADS):
                hs = slice(hd * HEAD, (hd + 1) * HEAD)
                mixed_ref[c * ROWS:(c + 1) * ROWS, D_A + hd * HEAD:D_A + (hd + 1) * HEAD] = (
                    _mlp_head(hd, col(c, 4, hs), v_n16[:, hs], col(c, 6, hs), wc_ref, bias))
                yield
                yield

        def projection(rounds_per_block):
            for blk in range(n_blocks):
                cols = slice(blk * PROJ_BLOCK, (blk + 1) * PROJ_BLOCK)
                proj_next[:, cols] = jnp.dot(h_ref[...], win_ref[:, cols],
                                             preferred_element_type=F32)
                for _ in range(rounds_per_block):
                    yield

        items = [gen for c in range(n_chunks)
                 for gen in [head(c, hd) for hd in range(N_HEADS)] + [mlp(c)]]
        starts = [c * CHUNK_STAGGER for c in range(n_chunks) for _ in range(N_HEADS + 1)]
        rounds = (n_chunks - 1) * CHUNK_STAGGER + HEAD_ROUNDS
        _lockstep(items + [projection(max(1, rounds // n_blocks))], starts + [0])

        out = xprev_ref[0] + jnp.dot(mixed_ref[...], wout16_ref[...], preferred_element_type=F32)
        y_ref[0] = _rms_norm(out, fg_ref[...])

    parity = lax.rem(g, 2)

    @pl.when(parity == 0)
    def _():
        step(proj_a, proj_b)

    @pl.when(parity == 1)
    def _():
        step(proj_b, proj_a)


def _sample_kernel(x_ref, ng_ref, win_ref, lbl_ref, hg_ref, lng_ref, lnb_ref, ws_ref, bs_ref,
                   wout_ref, fg_ref, stin_ref, y_ref, stout_ref, vch_ref,
                   qin_ref, kot_ref, tot_ref, oint_ref, lvl_ref, wc_ref, bias_ref, wout16_ref):
    @pl.when(pl.program_id(0) == 0)
    def _():
        _store_constants(DEC_SEQ, ws_ref, bs_ref, wout_ref, lvl_ref, wc_ref, bias_ref, wout16_ref)

    x = x_ref[...]
    h = _rms_norm(x, ng_ref[...]).astype(BF16)
    proj = jnp.dot(h, win_ref[...], preferred_element_type=F32)

    def col(k, hs=slice(0, D_A)):
        return proj[:, k * D_A + hs.start:k * D_A + hs.stop]

    lb = _lower_bound(lbl_ref[...])
    bits, n_levels = _row_bits(), _n_levels(DEC_SEQ)
    vi16 = col(2).astype(BF16)

    def head(hd):
        hs = slice(hd * HEAD, (hd + 1) * HEAD)

        def gates(i):
            rows = slice(i * SUBLANES, (i + 1) * SUBLANES)
            return _hgrn_gates(proj[rows, hs], proj[rows, D_A + hd * HEAD:D_A + (hd + 1) * HEAD],
                               lb[:, hs])

        sc, q_in, k_out, total = yield from _hgrn_head(gates, lvl_ref, bits, n_levels)
        qin_ref[:, hs] = q_in
        kot_ref[hd] = k_out.T
        tot_ref[hd] = total.T
        return jnp.dot(sc, vi16[:, hs], preferred_element_type=F32)

    o_intra = _lockstep([head(hd) for hd in range(N_HEADS)])

    lane = lax.broadcasted_iota(jnp.int32, (HEAD, ROWS), 1)

    def seq_head(i, hd):
        hs = slice(hd * HEAD, (hd + 1) * HEAD)
        seq_rows = slice(i * DEC_SEQ, (i + 1) * DEC_SEQ)
        state = stin_ref[0, i, hd]
        qi = qin_ref[seq_rows, hs].astype(BF16)
        oint_ref[seq_rows, hs] = jnp.dot(qi, state.astype(BF16), preferred_element_type=F32)
        yield
        in_seq = (lane >> (DEC_SEQ.bit_length() - 1)) == i
        k_seq = jnp.where(in_seq, kot_ref[hd], 0.0).astype(BF16)
        decay = tot_ref[hd, :, i * DEC_SEQ:i * DEC_SEQ + 1]
        yield
        stout_ref[0, i, hd] = decay * state + jnp.dot(
            k_seq, vi16[:, hs], preferred_element_type=F32)

    for grp in range(SEQS_PER_TILE // SEQ_GROUP):
        _lockstep([seq_head(grp * SEQ_GROUP + s, hd)
                   for s in range(SEQ_GROUP) for hd in range(N_HEADS)])

    hg = hg_ref[...]
    bias = bias_ref[...]
    v_n = _layer_norm(_gelu(col(5)), lng_ref[...], lnb_ref[...])
    for hd in range(N_HEADS):
        vch_ref[0, :, :, hd, :] = v_n[:, hd * HEAD:(hd + 1) * HEAD].reshape(
            SEQS_PER_TILE, DEC_SEQ, HEAD)
    v_n16 = v_n.astype(BF16)
    mixed = []
    for hd in range(N_HEADS):
        hs = slice(hd * HEAD, (hd + 1) * HEAD)
        mixed.append(_hgrn_finish(o_intra[hd] + oint_ref[:, hs], col(3, hs), hg[:, hs]))
    for hd in range(N_HEADS):
        hs = slice(hd * HEAD, (hd + 1) * HEAD)
        mixed.append(_mlp_head(hd, col(4, hs), v_n16[:, hs], col(6, hs), wc_ref, bias))
    out = x + jnp.dot(jnp.concatenate(mixed, axis=-1), wout16_ref[...],
                      preferred_element_type=F32)
    y_ref[...] = _rms_norm(out, fg_ref[...])


def _const_spec(shape):
    return pl.BlockSpec(shape, lambda *_: (0,) * len(shape), pipeline_mode=pl.Buffered(1))


_WEIGHT_SPECS = [
    _const_spec((1, D_MODEL)),
    _const_spec((D_MODEL, D_IN)),
    _const_spec((2, D_A)),
    _const_spec((1, D_A)),
    _const_spec((1, D_B)),
    _const_spec((1, D_B)),
    _const_spec((N_HEADS, ROWS, ROWS)),
    _const_spec((N_HEADS, ROWS)),
    _const_spec((D_MODEL, D_MODEL)),
    _const_spec((1, D_MODEL)),
]

_CONSTANT_SCRATCH = [
    pltpu.VMEM((ROWS, ROWS), jnp.int32),
    pltpu.VMEM((N_HEADS, ROWS, ROWS), BF16),
    pltpu.VMEM((ROWS, N_HEADS), F32),
    pltpu.VMEM((D_MODEL, D_MODEL), BF16),
]


def kernel(x_prompt, x_sample, state_hgrn, norm_g, w_in, lb_logits, hgrn_norm_g, sgu_ln_g,
           sgu_ln_b, w_s, b_s, w_out, final_norm_g):
    depth = norm_g.shape[0]
    assert depth == 1 and lb_logits.shape == (2, D_A)
    batch, seq, _ = x_prompt.shape
    dec_batch, dec_seq, _ = x_sample.shape
    assert seq % PROMPT_TILE == 0 and dec_seq == DEC_SEQ and dec_batch % SEQS_PER_TILE == 0

    weights = (norm_g, w_in[0].astype(BF16), lb_logits, hgrn_norm_g, sgu_ln_g, sgu_ln_b,
               w_s[0], b_s[0], w_out[0], final_norm_g.reshape(1, D_MODEL))
    params = pltpu.CompilerParams(dimension_semantics=("arbitrary",),
                                  vmem_limit_bytes=VMEM_LIMIT_BYTES)

    tiles_per_seq = seq // PROMPT_TILE
    n_tiles = batch * tiles_per_seq

    def cur_tile(g):
        t = jnp.minimum(g, n_tiles - 1)
        return t // tiles_per_seq, t % tiles_per_seq

    def prev_tile(g):
        t = jnp.maximum(g - 1, 0)
        return t // tiles_per_seq, t % tiles_per_seq

    y_p, st_p, vch_p = pl.pallas_call(
        functools.partial(_prompt_kernel, tiles_per_seq=tiles_per_seq),
        grid=(n_tiles + 1,),
        in_specs=[pl.BlockSpec((1, PROMPT_TILE, D_MODEL), lambda g: (*cur_tile(g), 0)),
                  pl.BlockSpec((1, PROMPT_TILE, D_MODEL), lambda g: (*prev_tile(g), 0))]
        + _WEIGHT_SPECS,
        out_specs=[
            pl.BlockSpec((1, PROMPT_TILE, D_MODEL), lambda g: (*prev_tile(g), 0)),
            pl.BlockSpec((1, 1, N_HEADS, HEAD, HEAD), lambda g: (0, prev_tile(g)[0], 0, 0, 0)),
            pl.BlockSpec((1, 1, ROWS, N_HEADS, HEAD), lambda g: (0, prev_tile(g)[0], 0, 0, 0)),
        ],
        out_shape=[
            jax.ShapeDtypeStruct((batch, seq, D_MODEL), F32),
            jax.ShapeDtypeStruct((1, batch, N_HEADS, HEAD, HEAD), F32),
            jax.ShapeDtypeStruct((1, batch, ROWS, N_HEADS, HEAD), F32),
        ],
        scratch_shapes=[pltpu.VMEM((PROMPT_TILE, D_IN), F32),
                        pltpu.VMEM((PROMPT_TILE, D_IN), F32),
                        pltpu.VMEM((PROMPT_TILE, D_MODEL), BF16),
                        pltpu.VMEM((PROMPT_TILE, D_MODEL), BF16)] + _CONSTANT_SCRATCH,
        compiler_params=params,
        name="prompt_layer",
    )(x_prompt, x_prompt, *weights)

    n_rows = dec_batch * DEC_SEQ
    y_s, st_s, vch_s = pl.pallas_call(
        _sample_kernel,
        grid=(n_rows // ROWS,),
        in_specs=[pl.BlockSpec((ROWS, D_MODEL), lambda i: (i, 0))]
        + _WEIGHT_SPECS
        + [pl.BlockSpec((1, SEQS_PER_TILE, N_HEADS, HEAD, HEAD), lambda i: (0, i, 0, 0, 0))],
        out_specs=[
            pl.BlockSpec((ROWS, D_MODEL), lambda i: (i, 0)),
            pl.BlockSpec((1, SEQS_PER_TILE, N_HEADS, HEAD, HEAD), lambda i: (0, i, 0, 0, 0)),
            pl.BlockSpec((1, SEQS_PER_TILE, DEC_SEQ, N_HEADS, HEAD), lambda i: (0, i, 0, 0, 0)),
        ],
        out_shape=[
            jax.ShapeDtypeStruct((n_rows, D_MODEL), F32),
            jax.ShapeDtypeStruct((1, dec_batch, N_HEADS, HEAD, HEAD), F32),
            jax.ShapeDtypeStruct((1, dec_batch, DEC_SEQ, N_HEADS, HEAD), F32),
        ],
        scratch_shapes=[pltpu.VMEM((ROWS, D_A), F32),
                        pltpu.VMEM((N_HEADS, HEAD, ROWS), F32),
                        pltpu.VMEM((N_HEADS, HEAD, ROWS), F32),
                        pltpu.VMEM((ROWS, D_A), F32)] + _CONSTANT_SCRATCH,
        compiler_params=pltpu.CompilerParams(dimension_semantics=("arbitrary",),
                                             vmem_limit_bytes=VMEM_LIMIT_BYTES),
        name="decode_layer",
    )(x_sample.reshape(n_rows, D_MODEL), *weights, state_hgrn)

    return (y_p,
            y_s.reshape(dec_batch, DEC_SEQ, D_MODEL),
            st_p,
            st_s,
            vch_p,
            vch_s)
```

```python
import functools

import jax
import jax.numpy as jnp
from jax import lax
from jax.experimental import pallas as pl
from jax.experimental.pallas import tpu as pltpu

F32 = jnp.float32
BF16 = jnp.bfloat16

D_MODEL = 1024
D_A = 512
D_B = 512
HEAD = 128
N_HEADS = 4
D_IN = 4 * D_A + 3 * D_B
ROWS = 128
SUBLANES = 8
PACKED = 2 * SUBLANES
N_TILES = ROWS // SUBLANES
SUBLANE_LEVELS = 3
EPS = 1e-6
PROMPT_TILE = 512
PROJ_BLOCK = 512
HEAD_ROUNDS = 12
CHUNK_STAGGER = 4
HEAD_STAGGER = 1
DEC_SEQ = 8
SEQS_PER_TILE = ROWS // DEC_SEQ
SEQ_GROUP = 4
VMEM_LIMIT_BYTES = 56 * 1024 * 1024

_NT = (((1,), (1,)), ((), ()))


def _silu(x):
    return x * jax.nn.sigmoid(x)


def _gelu(x):
    c = -2.0 * 0.7978845608028654 * 1.4426950408889634
    return x / (1.0 + jnp.exp2(x * (c * 0.044715 * (x * x) + c)))


def _rms_norm(x, g):
    return x * lax.rsqrt(jnp.mean(x * x, axis=-1, keepdims=True) + EPS) * g


def _layer_norm(x, g, b):
    mu = jnp.mean(x, axis=-1, keepdims=True)
    xc = x - mu
    var = jnp.mean(xc * xc, axis=-1, keepdims=True)
    return xc * lax.rsqrt(var + EPS) * g + b


def _lower_bound(lbl):
    rows = [lbl[r:r + 1, :] for r in range(lbl.shape[0])]
    mx = functools.reduce(jnp.maximum, rows)
    es = [jnp.exp(r - mx) for r in rows]
    return es[0] / functools.reduce(lambda a, c: a + c, es)


def _n_levels(seg_len):
    n_levels = seg_len.bit_length() - 1
    assert seg_len == 1 << n_levels and n_levels >= SUBLANE_LEVELS
    return n_levels


def _split3(x):
    hi = x.astype(BF16)
    r1 = x - hi.astype(F32)
    mid = r1.astype(BF16)
    lo = (r1 - mid.astype(F32)).astype(BF16)
    return hi, mid, lo


def _store_constants(seg_len, ws_ref, bs_ref, wout_ref, lvl_ref, wc_ref, bias_ref, wout16_ref):
    t = lax.broadcasted_iota(jnp.int32, (ROWS, ROWS), 0)
    s = lax.broadcasted_iota(jnp.int32, (ROWS, ROWS), 1)
    n_levels = _n_levels(seg_len)
    x = t ^ s
    lvl = jnp.where(t == s, n_levels, -1)
    for j in range(n_levels):
        lvl = jnp.where(((x >> j) == 1) & (((t >> j) & 1) == 1), j, lvl)
    lvl_ref[...] = lvl
    causal = (t >= s) & ((t >> n_levels) == (s >> n_levels))
    pick = jnp.where(s == (t & (seg_len - 1)), 1.0, 0.0).astype(BF16)
    for hd in range(N_HEADS):
        w = ws_ref[hd].astype(BF16)
        if seg_len != ROWS:
            w = jnp.dot(pick, w, preferred_element_type=F32).astype(BF16)
            w = lax.dot_general(w, pick, _NT, preferred_element_type=F32).astype(BF16)
        wc_ref[hd] = jnp.where(causal, w, jnp.zeros_like(w))
    pad = jnp.zeros((PACKED - N_HEADS, ROWS), F32)
    bias = sum(lax.dot_general(pick, part, _NT, preferred_element_type=F32)
               for part in _split3(jnp.concatenate([bs_ref[...], pad], axis=0)))
    bias_ref[...] = bias[:, :N_HEADS]
    wout16_ref[...] = wout_ref[...].astype(BF16)


def _row_bits():
    rowi = lax.broadcasted_iota(jnp.int32, (SUBLANES, HEAD), 0)
    return [((rowi >> j) & 1) == 1 for j in range(SUBLANE_LEVELS)]


def _hgrn_head(gates, lvl_ref, bits, n_levels):
    one = jnp.ones((SUBLANES, HEAD), F32)
    z = [[] for _ in range(n_levels)]
    qp_t, ks_t, blk_t, diag_t = [], [], [], []
    for pair in range(N_TILES // 2):
        z_pair = [[] for _ in range(SUBLANE_LEVELS)]
        for i in (2 * pair, 2 * pair + 1):
            q, kk, f = gates(i)
            diag_t.append(jnp.sum(q * kk, axis=-1, keepdims=True))
            qp, ks, blk = q * f, kk, f
            for j in range(SUBLANE_LEVELS):
                m = 1 << j
                z_pair[j].append(jnp.where(bits[j], qp, ks))
                sib = pltpu.roll(blk, m, 0)
                if 2 * m != SUBLANES:
                    sib = jnp.where(bits[j], sib, pltpu.roll(blk, SUBLANES - m, 0))
                qp = qp * jnp.where(bits[j], sib, one)
                ks = ks * jnp.where(bits[j], one, sib)
                blk = blk * sib
            qp_t.append(qp)
            ks_t.append(ks)
            blk_t.append(blk)
        for j in range(SUBLANE_LEVELS):
            z[j].append(jnp.concatenate(z_pair[j], axis=0).astype(BF16))
    yield

    blk_levels, blk_b = [], blk_t
    for j in range(SUBLANE_LEVELS, n_levels):
        blk_levels.append(blk_b)
        blk_b = [blk_b[2 * n] * blk_b[2 * n + 1] for n in range(len(blk_b) // 2)]
    for pair in range(N_TILES // 2):
        z_pair = [[] for _ in range(SUBLANE_LEVELS, n_levels)]
        for i in (2 * pair, 2 * pair + 1):
            for j in range(SUBLANE_LEVELS, n_levels):
                block = i >> (j - SUBLANE_LEVELS)
                sib = blk_levels[j - SUBLANE_LEVELS][block ^ 1]
                if block & 1:
                    z_pair[j - SUBLANE_LEVELS].append(qp_t[i])
                    qp_t[i] = qp_t[i] * sib
                else:
                    z_pair[j - SUBLANE_LEVELS].append(ks_t[i])
                    ks_t[i] = ks_t[i] * sib
        for j in range(SUBLANE_LEVELS, n_levels):
            z[j].append(jnp.concatenate(z_pair[j - SUBLANE_LEVELS], axis=0).astype(BF16))
    q_in = jnp.concatenate(qp_t, axis=0)
    k_out = jnp.concatenate(ks_t, axis=0)
    per_tile = N_TILES // len(blk_b)
    total = jnp.concatenate([blk_b[i // per_tile] for i in range(N_TILES)], axis=0)
    yield

    p, p_row = [], []
    for j in range(n_levels):
        zj = jnp.concatenate(z[j], axis=0)
        span = 1 << max(j - SUBLANE_LEVELS, 0)
        if span * SUBLANES >= PACKED:
            upper = [i for i in range(N_TILES) if (i // span) & 1]
            lhs = jnp.concatenate([z[j][i // 2] for i in upper[::2]], axis=0)
        else:
            upper = list(range(N_TILES))
            lhs = zj
        p.append(lax.dot_general(lhs, zj, _NT, preferred_element_type=F32))
        p_row.append({i: n * SUBLANES for n, i in enumerate(upper)})
        yield
    sc_t = []
    for i in range(N_TILES):
        lvl_i = lvl_ref[i * SUBLANES:(i + 1) * SUBLANES, :]
        acc = jnp.where(lvl_i == n_levels, diag_t[i], 0.0)
        for j in range(n_levels):
            if j < SUBLANE_LEVELS or (i >> (j - SUBLANE_LEVELS)) & 1:
                r0 = p_row[j][i]
                acc = jnp.where(lvl_i == j, p[j][r0:r0 + SUBLANES, :], acc)
        sc_t.append(acc)
    sc = jnp.concatenate(sc_t, axis=0).astype(BF16)
    return sc, q_in, k_out, total


def _lockstep(generators, starts=None):
    starts = starts or [0] * len(generators)
    results = [None] * len(generators)
    waiting = sorted(range(len(generators)), key=lambda i: starts[i])
    active = []
    rnd = 0
    while waiting or active:
        while waiting and starts[waiting[0]] <= rnd:
            active.append(waiting.pop(0))
        still = []
        for idx in active:
            try:
                next(generators[idx])
                still.append(idx)
            except StopIteration as done:
                results[idx] = done.value
        active = still
        rnd += 1
    return results


def _hgrn_gates(pq, pf, lb):
    q = _silu(pq)
    forget = lb + (1.0 - lb) * jax.nn.sigmoid(pf)
    return q, 1.0 - forget, forget


def _hgrn_finish(o, pz, g):
    return (_rms_norm(o, g) * _silu(pz)).astype(BF16)


def _mlp_head(hd, u, v_n16, gate, wc_ref, bias):
    mix = jnp.dot(wc_ref[hd], v_n16, preferred_element_type=F32) + bias[:, hd:hd + 1]
    return (_gelu(u) * mix * _silu(gate)).astype(BF16)


def _prompt_kernel(x_ref, xprev_ref, ng_ref, win_ref, lbl_ref, hg_ref, lng_ref, lnb_ref, ws_ref,
                   bs_ref, wout_ref, fg_ref, y_ref, st_ref, vch_ref, proj_a, proj_b, h_ref,
                   mixed_ref, lvl_ref, wc_ref, bias_ref, wout16_ref, *, tiles_per_seq):
    g = pl.program_id(0)

    @pl.when(g == 0)
    def _():
        proj_b[...] = jnp.zeros_like(proj_b)
        _store_constants(ROWS, ws_ref, bs_ref, wout_ref, lvl_ref, wc_ref, bias_ref, wout16_ref)

    @pl.when(lax.rem(jnp.maximum(g - 1, 0), tiles_per_seq) == 0)
    def _():
        st_ref[...] = jnp.zeros_like(st_ref)

    def step(proj_next, proj_prev):
        h_ref[...] = _rms_norm(x_ref[0], ng_ref[...]).astype(BF16)
        lb = _lower_bound(lbl_ref[...])
        bits, n_levels = _row_bits(), _n_levels(ROWS)
        hg = hg_ref[...]
        lng = lng_ref[...]
        lnb = lnb_ref[...]
        bias = bias_ref[...]
        n_chunks = PROMPT_TILE // ROWS
        n_blocks = D_IN // PROJ_BLOCK

        def col(c, k, hs=slice(0, D_A)):
            return proj_prev[c * ROWS:(c + 1) * ROWS, k * D_A + hs.start:k * D_A + hs.stop]

        def head(c, hd):
            hs = slice(hd * HEAD, (hd + 1) * HEAD)

            def gates(i):
                rows = slice(c * ROWS + i * SUBLANES, c * ROWS + (i + 1) * SUBLANES)
                return _hgrn_gates(proj_prev[rows, hs],
                                   proj_prev[rows, D_A + hd * HEAD:D_A + (hd + 1) * HEAD],
                                   lb[:, hs])

            sc, q_in, k_out, total = yield from _hgrn_head(gates, lvl_ref, bits, n_levels)
            state = st_ref[0, 0, hd]
            vh = col(c, 2, hs)
            o = jnp.dot(jnp.concatenate([sc, q_in.astype(BF16)], axis=1),
                        jnp.concatenate([vh.astype(BF16), state.astype(BF16)], axis=0),
                        preferred_element_type=F32)
            yield
            st_ref[0, 0, hd] = total.T * state + jnp.dot(
                k_out.T.astype(BF16), vh.astype(BF16), preferred_element_type=F32)
            yield
            mixed_ref[c * ROWS:(c + 1) * ROWS, hs] = _hgrn_finish(o, col(c, 3, hs), hg[:, hs])

        def mlp(c):
            v_n = _layer_norm(_gelu(col(c, 5)), lng, lnb)
            if c == n_chunks - 1:
                for hd in range(N_HEADS):
                    vch_ref[0, 0, :, hd, :] = v_n[:, hd * HEAD:(hd + 1) * HEAD]
            v_n16 = v_n.astype(BF16)
            yield
            for hd in range(N_HEADS):
                hs = slice(hd * HEAD, (hd + 1) * HEAD)
                mixed_ref[c * ROWS:(c + 1) * ROWS, D_A + hd * HEAD:D_A + (hd + 1) * HEAD] = (
                    _mlp_head(hd, col(c, 4, hs), v_n16[:, hs], col(c, 6, hs), wc_ref, bias))
                yield
                yield

        def projection(rounds_per_block):
            for blk in range(n_blocks):
                cols = slice(blk * PROJ_BLOCK, (blk + 1) * PROJ_BLOCK)
                proj_next[:, cols] = jnp.dot(h_ref[...], win_ref[:, cols],
                                             preferred_element_type=F32)
                for _ in range(rounds_per_block):
                    yield

        items = [gen for c in range(n_chunks)
                 for gen in [head(c, hd) for hd in range(N_HEADS)] + [mlp(c)]]
        starts = [c * CHUNK_STAGGER + k * HEAD_STAGGER
                  for c in range(n_chunks) for k in list(range(N_HEADS)) + [0]]
        rounds = (n_chunks - 1) * CHUNK_STAGGER + (N_HEADS - 1) * HEAD_STAGGER + HEAD_ROUNDS
        _lockstep(items + [projection(max(1, rounds // n_blocks))], starts + [0])

        out = xprev_ref[0] + jnp.dot(mixed_ref[...], wout16_ref[...], preferred_element_type=F32)
        y_ref[0] = _rms_norm(out, fg_ref[...])

    parity = lax.rem(g, 2)

    @pl.when(parity == 0)
    def _():
        step(proj_a, proj_b)

    @pl.when(parity == 1)
    def _():
        step(proj_b, proj_a)


def _sample_kernel(x_ref, ng_ref, win_ref, lbl_ref, hg_ref, lng_ref, lnb_ref, ws_ref, bs_ref,
                   wout_ref, fg_ref, stin_ref, y_ref, stout_ref, vch_ref,
                   qin_ref, kot_ref, tot_ref, oint_ref, lvl_ref, wc_ref, bias_ref, wout16_ref):
    @pl.when(pl.program_id(0) == 0)
    def _():
        _store_constants(DEC_SEQ, ws_ref, bs_ref, wout_ref, lvl_ref, wc_ref, bias_ref, wout16_ref)

    x = x_ref[...]
    h = _rms_norm(x, ng_ref[...]).astype(BF16)
    proj = jnp.dot(h, win_ref[...], preferred_element_type=F32)

    def col(k, hs=slice(0, D_A)):
        return proj[:, k * D_A + hs.start:k * D_A + hs.stop]

    lb = _lower_bound(lbl_ref[...])
    bits, n_levels = _row_bits(), _n_levels(DEC_SEQ)
    vi16 = col(2).astype(BF16)

    def head(hd):
        hs = slice(hd * HEAD, (hd + 1) * HEAD)

        def gates(i):
            rows = slice(i * SUBLANES, (i + 1) * SUBLANES)
            return _hgrn_gates(proj[rows, hs], proj[rows, D_A + hd * HEAD:D_A + (hd + 1) * HEAD],
                               lb[:, hs])

        sc, q_in, k_out, total = yield from _hgrn_head(gates, lvl_ref, bits, n_levels)
        qin_ref[:, hs] = q_in
        kot_ref[hd] = k_out.T
        tot_ref[hd] = total.T
        return jnp.dot(sc, vi16[:, hs], preferred_element_type=F32)

    o_intra = _lockstep([head(hd) for hd in range(N_HEADS)])

    lane = lax.broadcasted_iota(jnp.int32, (HEAD, ROWS), 1)

    def seq_head(i, hd):
        hs = slice(hd * HEAD, (hd + 1) * HEAD)
        seq_rows = slice(i * DEC_SEQ, (i + 1) * DEC_SEQ)
        state = stin_ref[0, i, hd]
        qi = qin_ref[seq_rows, hs].astype(BF16)
        oint_ref[seq_rows, hs] = jnp.dot(qi, state.astype(BF16), preferred_element_type=F32)
        yield
        in_seq = (lane >> (DEC_SEQ.bit_length() - 1)) == i
        k_seq = jnp.where(in_seq, kot_ref[hd], 0.0).astype(BF16)
        decay = tot_ref[hd, :, i * DEC_SEQ:i * DEC_SEQ + 1]
        yield
        stout_ref[0, i, hd] = decay * state + jnp.dot(
            k_seq, vi16[:, hs], preferred_element_type=F32)

    for grp in range(SEQS_PER_TILE // SEQ_GROUP):
        _lockstep([seq_head(grp * SEQ_GROUP + s, hd)
                   for s in range(SEQ_GROUP) for hd in range(N_HEADS)])

    hg = hg_ref[...]
    bias = bias_ref[...]
    v_n = _layer_norm(_gelu(col(5)), lng_ref[...], lnb_ref[...])
    for hd in range(N_HEADS):
        vch_ref[0, :, :, hd, :] = v_n[:, hd * HEAD:(hd + 1) * HEAD].reshape(
            SEQS_PER_TILE, DEC_SEQ, HEAD)
    v_n16 = v_n.astype(BF16)
    mixed = []
    for hd in range(N_HEADS):
        hs = slice(hd * HEAD, (hd + 1) * HEAD)
        mixed.append(_hgrn_finish(o_intra[hd] + oint_ref[:, hs], col(3, hs), hg[:, hs]))
    for hd in range(N_HEADS):
        hs = slice(hd * HEAD, (hd + 1) * HEAD)
        mixed.append(_mlp_head(hd, col(4, hs), v_n16[:, hs], col(6, hs), wc_ref, bias))
    out = x + jnp.dot(jnp.concatenate(mixed, axis=-1), wout16_ref[...],
                      preferred_element_type=F32)
    y_ref[...] = _rms_norm(out, fg_ref[...])


def _const_spec(shape):
    return pl.BlockSpec(shape, lambda *_: (0,) * len(shape), pipeline_mode=pl.Buffered(1))


_WEIGHT_SPECS = [
    _const_spec((1, D_MODEL)),
    _const_spec((D_MODEL, D_IN)),
    _const_spec((2, D_A)),
    _const_spec((1, D_A)),
    _const_spec((1, D_B)),
    _const_spec((1, D_B)),
    _const_spec((N_HEADS, ROWS, ROWS)),
    _const_spec((N_HEADS, ROWS)),
    _const_spec((D_MODEL, D_MODEL)),
    _const_spec((1, D_MODEL)),
]

_CONSTANT_SCRATCH = [
    pltpu.VMEM((ROWS, ROWS), jnp.int32),
    pltpu.VMEM((N_HEADS, ROWS, ROWS), BF16),
    pltpu.VMEM((ROWS, N_HEADS), F32),
    pltpu.VMEM((D_MODEL, D_MODEL), BF16),
]


def kernel(x_prompt, x_sample, state_hgrn, norm_g, w_in, lb_logits, hgrn_norm_g, sgu_ln_g,
           sgu_ln_b, w_s, b_s, w_out, final_norm_g):
    depth = norm_g.shape[0]
    assert depth == 1 and lb_logits.shape == (2, D_A)
    batch, seq, _ = x_prompt.shape
    dec_batch, dec_seq, _ = x_sample.shape
    assert seq % PROMPT_TILE == 0 and dec_seq == DEC_SEQ and dec_batch % SEQS_PER_TILE == 0

    weights = (norm_g, w_in[0].astype(BF16), lb_logits, hgrn_norm_g, sgu_ln_g, sgu_ln_b,
               w_s[0], b_s[0], w_out[0], final_norm_g.reshape(1, D_MODEL))
    params = pltpu.CompilerParams(dimension_semantics=("arbitrary",),
                                  vmem_limit_bytes=VMEM_LIMIT_BYTES)

    tiles_per_seq = seq // PROMPT_TILE
    n_tiles = batch * tiles_per_seq

    def cur_tile(g):
        t = jnp.minimum(g, n_tiles - 1)
        return t // tiles_per_seq, t % tiles_per_seq

    def prev_tile(g):
        t = jnp.maximum(g - 1, 0)
        return t // tiles_per_seq, t % tiles_per_seq

    y_p, st_p, vch_p = pl.pallas_call(
        functools.partial(_prompt_kernel, tiles_per_seq=tiles_per_seq),
        grid=(n_tiles + 1,),
        in_specs=[pl.BlockSpec((1, PROMPT_TILE, D_MODEL), lambda g: (*cur_tile(g), 0)),
                  pl.BlockSpec((1, PROMPT_TILE, D_MODEL), lambda g: (*prev_tile(g), 0))]
        + _WEIGHT_SPECS,
        out_specs=[
            pl.BlockSpec((1, PROMPT_TILE, D_MODEL), lambda g: (*prev_tile(g), 0)),
            pl.BlockSpec((1, 1, N_HEADS, HEAD, HEAD), lambda g: (0, prev_tile(g)[0], 0, 0, 0)),
            pl.BlockSpec((1, 1, ROWS, N_HEADS, HEAD), lambda g: (0, prev_tile(g)[0], 0, 0, 0)),
        ],
        out_shape=[
            jax.ShapeDtypeStruct((batch, seq, D_MODEL), F32),
            jax.ShapeDtypeStruct((1, batch, N_HEADS, HEAD, HEAD), F32),
            jax.ShapeDtypeStruct((1, batch, ROWS, N_HEADS, HEAD), F32),
        ],
        scratch_shapes=[pltpu.VMEM((PROMPT_TILE, D_IN), F32),
                        pltpu.VMEM((PROMPT_TILE, D_IN), F32),
                        pltpu.VMEM((PROMPT_TILE, D_MODEL), BF16),
                        pltpu.VMEM((PROMPT_TILE, D_MODEL), BF16)] + _CONSTANT_SCRATCH,
        compiler_params=params,
        name="prompt_layer",
    )(x_prompt, x_prompt, *weights)

    n_rows = dec_batch * DEC_SEQ
    y_s, st_s, vch_s = pl.pallas_call(
        _sample_kernel,
        grid=(n_rows // ROWS,),
        in_specs=[pl.BlockSpec((ROWS, D_MODEL), lambda i: (i, 0))]
        + _WEIGHT_SPECS
        + [pl.BlockSpec((1, SEQS_PER_TILE, N_HEADS, HEAD, HEAD), lambda i: (0, i, 0, 0, 0))],
        out_specs=[
            pl.BlockSpec((ROWS, D_MODEL), lambda i: (i, 0)),
            pl.BlockSpec((1, SEQS_PER_TILE, N_HEADS, HEAD, HEAD), lambda i: (0, i, 0, 0, 0)),
            pl.BlockSpec((1, SEQS_PER_TILE, DEC_SEQ, N_HEADS, HEAD), lambda i: (0, i, 0, 0, 0)),
        ],
        out_shape=[
            jax.ShapeDtypeStruct((n_rows, D_MODEL), F32),
            jax.ShapeDtypeStruct((1, dec_batch, N_HEADS, HEAD, HEAD), F32),
            jax.ShapeDtypeStruct((1, dec_batch, DEC_SEQ, N_HEADS, HEAD), F32),
        ],
        scratch_shapes=[pltpu.VMEM((ROWS, D_A), F32),
                        pltpu.VMEM((N_HEADS, HEAD, ROWS), F32),
                        pltpu.VMEM((N_HEADS, HEAD, ROWS), F32),
                        pltpu.VMEM((ROWS, D_A), F32)] + _CONSTANT_SCRATCH,
        compiler_params=pltpu.CompilerParams(dimension_semantics=("arbitrary",),
                                             vmem_limit_bytes=VMEM_LIMIT_BYTES),
        name="decode_layer",
    )(x_sample.reshape(n_rows, D_MODEL), *weights, state_hgrn)

    return (y_p,
            y_s.reshape(dec_batch, DEC_SEQ, D_MODEL),
            st_p,
            st_s,
            vch_p,
            vch_s)
```

```python
import functools

import jax
import jax.numpy as jnp
from jax import lax
from jax.experimental import pallas as pl
from jax.experimental.pallas import tpu as pltpu

F32 = jnp.float32
BF16 = jnp.bfloat16

D_MODEL = 1024
D_A = 512
D_B = 512
HEAD = 128
N_HEADS = 4
D_IN = 4 * D_A + 3 * D_B
ROWS = 128
SUBLANES = 8
PACKED = 2 * SUBLANES
N_TILES = ROWS // SUBLANES
SUBLANE_LEVELS = 3
EPS = 1e-6
PROMPT_TILE = 512
PROJ_BLOCK = 512
HEAD_ROUNDS = 12
CHUNK_STAGGER = 8
HEAD_STAGGER = 2
DEC_SEQ = 8
SEQS_PER_TILE = ROWS // DEC_SEQ
SEQ_GROUP = 4
VMEM_LIMIT_BYTES = 56 * 1024 * 1024

_NT = (((1,), (1,)), ((), ()))


def _silu(x):
    return x * jax.nn.sigmoid(x)


def _gelu(x):
    c = -2.0 * 0.7978845608028654 * 1.4426950408889634
    return x / (1.0 + jnp.exp2(x * (c * 0.044715 * (x * x) + c)))


def _rms_norm(x, g):
    return x * lax.rsqrt(jnp.mean(x * x, axis=-1, keepdims=True) + EPS) * g


def _layer_norm(x, g, b):
    mu = jnp.mean(x, axis=-1, keepdims=True)
    xc = x - mu
    var = jnp.mean(xc * xc, axis=-1, keepdims=True)
    return xc * lax.rsqrt(var + EPS) * g + b


def _lower_bound(lbl):
    rows = [lbl[r:r + 1, :] for r in range(lbl.shape[0])]
    mx = functools.reduce(jnp.maximum, rows)
    es = [jnp.exp(r - mx) for r in rows]
    return es[0] / functools.reduce(lambda a, c: a + c, es)


def _n_levels(seg_len):
    n_levels = seg_len.bit_length() - 1
    assert seg_len == 1 << n_levels and n_levels >= SUBLANE_LEVELS
    return n_levels


def _split3(x):
    hi = x.astype(BF16)
    r1 = x - hi.astype(F32)
    mid = r1.astype(BF16)
    lo = (r1 - mid.astype(F32)).astype(BF16)
    return hi, mid, lo


def _store_constants(seg_len, ws_ref, bs_ref, wout_ref, lvl_ref, wc_ref, bias_ref, wout16_ref):
    t = lax.broadcasted_iota(jnp.int32, (ROWS, ROWS), 0)
    s = lax.broadcasted_iota(jnp.int32, (ROWS, ROWS), 1)
    n_levels = _n_levels(seg_len)
    x = t ^ s
    lvl = jnp.where(t == s, n_levels, -1)
    for j in range(n_levels):
        lvl = jnp.where(((x >> j) == 1) & (((t >> j) & 1) == 1), j, lvl)
    lvl_ref[...] = lvl
    causal = (t >= s) & ((t >> n_levels) == (s >> n_levels))
    pick = jnp.where(s == (t & (seg_len - 1)), 1.0, 0.0).astype(BF16)
    for hd in range(N_HEADS):
        w = ws_ref[hd].astype(BF16)
        if seg_len != ROWS:
            w = jnp.dot(pick, w, preferred_element_type=F32).astype(BF16)
            w = lax.dot_general(w, pick, _NT, preferred_element_type=F32).astype(BF16)
        wc_ref[hd] = jnp.where(causal, w, jnp.zeros_like(w))
    pad = jnp.zeros((PACKED - N_HEADS, ROWS), F32)
    bias = sum(lax.dot_general(pick, part, _NT, preferred_element_type=F32)
               for part in _split3(jnp.concatenate([bs_ref[...], pad], axis=0)))
    bias_ref[...] = bias[:, :N_HEADS]
    wout16_ref[...] = wout_ref[...].astype(BF16)


def _row_bits():
    rowi = lax.broadcasted_iota(jnp.int32, (SUBLANES, HEAD), 0)
    return [((rowi >> j) & 1) == 1 for j in range(SUBLANE_LEVELS)]


def _hgrn_head(gates, lvl_ref, bits, n_levels):
    one = jnp.ones((SUBLANES, HEAD), F32)
    z = [[] for _ in range(n_levels)]
    qp_t, ks_t, blk_t, diag_t = [], [], [], []
    for pair in range(N_TILES // 2):
        z_pair = [[] for _ in range(SUBLANE_LEVELS)]
        for i in (2 * pair, 2 * pair + 1):
            q, kk, f = gates(i)
            diag_t.append(jnp.sum(q * kk, axis=-1, keepdims=True))
            qp, ks, blk = q * f, kk, f
            for j in range(SUBLANE_LEVELS):
                m = 1 << j
                z_pair[j].append(jnp.where(bits[j], qp, ks))
                sib = pltpu.roll(blk, m, 0)
                if 2 * m != SUBLANES:
                    sib = jnp.where(bits[j], sib, pltpu.roll(blk, SUBLANES - m, 0))
                qp = qp * jnp.where(bits[j], sib, one)
                ks = ks * jnp.where(bits[j], one, sib)
                blk = blk * sib
            qp_t.append(qp)
            ks_t.append(ks)
            blk_t.append(blk)
        for j in range(SUBLANE_LEVELS):
            z[j].append(jnp.concatenate(z_pair[j], axis=0).astype(BF16))
    yield

    blk_levels, blk_b = [], blk_t
    for j in range(SUBLANE_LEVELS, n_levels):
        blk_levels.append(blk_b)
        blk_b = [blk_b[2 * n] * blk_b[2 * n + 1] for n in range(len(blk_b) // 2)]
    for pair in range(N_TILES // 2):
        z_pair = [[] for _ in range(SUBLANE_LEVELS, n_levels)]
        for i in (2 * pair, 2 * pair + 1):
            for j in range(SUBLANE_LEVELS, n_levels):
                block = i >> (j - SUBLANE_LEVELS)
                sib = blk_levels[j - SUBLANE_LEVELS][block ^ 1]
                if block & 1:
                    z_pair[j - SUBLANE_LEVELS].append(qp_t[i])
                    qp_t[i] = qp_t[i] * sib
                else:
                    z_pair[j - SUBLANE_LEVELS].append(ks_t[i])
                    ks_t[i] = ks_t[i] * sib
        for j in range(SUBLANE_LEVELS, n_levels):
            z[j].append(jnp.concatenate(z_pair[j - SUBLANE_LEVELS], axis=0).astype(BF16))
    q_in = jnp.concatenate(qp_t, axis=0)
    k_out = jnp.concatenate(ks_t, axis=0)
    per_tile = N_TILES // len(blk_b)
    total = jnp.concatenate([blk_b[i // per_tile] for i in range(N_TILES)], axis=0)
    yield

    p, p_row = [], []
    for j in range(n_levels):
        zj = jnp.concatenate(z[j], axis=0)
        span = 1 << max(j - SUBLANE_LEVELS, 0)
        if span * SUBLANES >= PACKED:
            upper = [i for i in range(N_TILES) if (i // span) & 1]
            lhs = jnp.concatenate([z[j][i // 2] for i in upper[::2]], axis=0)
        else:
            upper = list(range(N_TILES))
            lhs = zj
        p.append(lax.dot_general(lhs, zj, _NT, preferred_element_type=F32))
        p_row.append({i: n * SUBLANES for n, i in enumerate(upper)})
        yield
    sc_t = []
    for i in range(N_TILES):
        lvl_i = lvl_ref[i * SUBLANES:(i + 1) * SUBLANES, :]
        acc = jnp.where(lvl_i == n_levels, diag_t[i], 0.0)
        for j in range(n_levels):
            if j < SUBLANE_LEVELS or (i >> (j - SUBLANE_LEVELS)) & 1:
                r0 = p_row[j][i]
                acc = jnp.where(lvl_i == j, p[j][r0:r0 + SUBLANES, :], acc)
        sc_t.append(acc)
    sc = jnp.concatenate(sc_t, axis=0).astype(BF16)
    return sc, q_in, k_out, total


def _lockstep(generators, starts=None):
    starts = starts or [0] * len(generators)
    results = [None] * len(generators)
    waiting = sorted(range(len(generators)), key=lambda i: starts[i])
    active = []
    rnd = 0
    while waiting or active:
        while waiting and starts[waiting[0]] <= rnd:
            active.append(waiting.pop(0))
        still = []
        for idx in active:
            try:
                next(generators[idx])
                still.append(idx)
            except StopIteration as done:
                results[idx] = done.value
        active = still
        rnd += 1
    return results


def _hgrn_gates(pq, pf, lb):
    q = _silu(pq)
    forget = lb + (1.0 - lb) * jax.nn.sigmoid(pf)
    return q, 1.0 - forget, forget


def _hgrn_finish(o, pz, g):
    return (_rms_norm(o, g) * _silu(pz)).astype(BF16)


def _mlp_head(hd, u, v_n16, gate, wc_ref, bias):
    mix = jnp.dot(wc_ref[hd], v_n16, preferred_element_type=F32) + bias[:, hd:hd + 1]
    return (_gelu(u) * mix * _silu(gate)).astype(BF16)


def _prompt_kernel(x_ref, xprev_ref, ng_ref, win_ref, lbl_ref, hg_ref, lng_ref, lnb_ref, ws_ref,
                   bs_ref, wout_ref, fg_ref, y_ref, st_ref, vch_ref, proj_a, proj_b, h_ref,
                   mixed_ref, lvl_ref, wc_ref, bias_ref, wout16_ref, *, tiles_per_seq):
    g = pl.program_id(0)

    @pl.when(g == 0)
    def _():
        proj_b[...] = jnp.zeros_like(proj_b)
        _store_constants(ROWS, ws_ref, bs_ref, wout_ref, lvl_ref, wc_ref, bias_ref, wout16_ref)

    @pl.when(lax.rem(jnp.maximum(g - 1, 0), tiles_per_seq) == 0)
    def _():
        st_ref[...] = jnp.zeros_like(st_ref)

    def step(proj_next, proj_prev):
        h_ref[...] = _rms_norm(x_ref[0], ng_ref[...]).astype(BF16)
        lb = _lower_bound(lbl_ref[...])
        bits, n_levels = _row_bits(), _n_levels(ROWS)
        hg = hg_ref[...]
        lng = lng_ref[...]
        lnb = lnb_ref[...]
        bias = bias_ref[...]
        n_chunks = PROMPT_TILE // ROWS
        n_blocks = D_IN // PROJ_BLOCK

        def col(c, k, hs=slice(0, D_A)):
            return proj_prev[c * ROWS:(c + 1) * ROWS, k * D_A + hs.start:k * D_A + hs.stop]

        def head(c, hd):
            hs = slice(hd * HEAD, (hd + 1) * HEAD)

            def gates(i):
                rows = slice(c * ROWS + i * SUBLANES, c * ROWS + (i + 1) * SUBLANES)
                return _hgrn_gates(proj_prev[rows, hs],
                                   proj_prev[rows, D_A + hd * HEAD:D_A + (hd + 1) * HEAD],
                                   lb[:, hs])

            sc, q_in, k_out, total = yield from _hgrn_head(gates, lvl_ref, bits, n_levels)
            state = st_ref[0, 0, hd]
            vh = col(c, 2, hs)
            o = jnp.dot(jnp.concatenate([sc, q_in.astype(BF16)], axis=1),
                        jnp.concatenate([vh.astype(BF16), state.astype(BF16)], axis=0),
                        preferred_element_type=F32)
            yield
            st_ref[0, 0, hd] = total.T * state + jnp.dot(
                k_out.T.astype(BF16), vh.astype(BF16), preferred_element_type=F32)
            yield
            mixed_ref[c * ROWS:(c + 1) * ROWS, hs] = _hgrn_finish(o, col(c, 3, hs), hg[:, hs])

        def mlp(c):
            v_n = _layer_norm(_gelu(col(c, 5)), lng, lnb)
            if c == n_chunks - 1:
                for hd in range(N_HEADS):
                    vch_ref[0, 0, :, hd, :] = v_n[:, hd * HEAD:(hd + 1) * HEAD]
            v_n16 = v_n.astype(BF16)
            yield
            for hd in range(N_HEADS):
                hs = slice(hd * HEAD, (hd + 1) * HEAD)
                mixed_ref[c * ROWS:(c + 1) * ROWS, D_A + hd * HEAD:D_A + (hd + 1) * HEAD] = (
                    _mlp_head(hd, col(c, 4, hs), v_n16[:, hs], col(c, 6, hs), wc_ref, bias))
                yield
                yield

        def projection(rounds_per_block):
            for blk in range(n_blocks):
                cols = slice(blk * PROJ_BLOCK, (blk + 1) * PROJ_BLOCK)
                proj_next[:, cols] = jnp.dot(h_ref[...], win_ref[:, cols],
                                             preferred_element_type=F32)
                for _ in range(rounds_per_block):
                    yield

        items = [gen for c in range(n_chunks)
                 for gen in [head(c, hd) for hd in range(N_HEADS)] + [mlp(c)]]
        starts = [c * CHUNK_STAGGER + k * HEAD_STAGGER
                  for c in range(n_chunks) for k in list(range(N_HEADS)) + [0]]
        rounds = (n_chunks - 1) * CHUNK_STAGGER + (N_HEADS - 1) * HEAD_STAGGER + HEAD_ROUNDS
        _lockstep(items + [projection(max(1, rounds // n_blocks))], starts + [0])

        out = xprev_ref[0] + jnp.dot(mixed_ref[...], wout16_ref[...], preferred_element_type=F32)
        y_ref[0] = _rms_norm(out, fg_ref[...])

    parity = lax.rem(g, 2)

    @pl.when(parity == 0)
    def _():
        step(proj_a, proj_b)

    @pl.when(parity == 1)
    def _():
        step(proj_b, proj_a)


def _sample_kernel(x_ref, ng_ref, win_ref, lbl_ref, hg_ref, lng_ref, lnb_ref, ws_ref, bs_ref,
                   wout_ref, fg_ref, stin_ref, y_ref, stout_ref, vch_ref,
                   qin_ref, kot_ref, tot_ref, oint_ref, lvl_ref, wc_ref, bias_ref, wout16_ref):
    @pl.when(pl.program_id(0) == 0)
    def _():
        _store_constants(DEC_SEQ, ws_ref, bs_ref, wout_ref, lvl_ref, wc_ref, bias_ref, wout16_ref)

    x = x_ref[...]
    h = _rms_norm(x, ng_ref[...]).astype(BF16)
    proj = jnp.dot(h, win_ref[...], preferred_element_type=F32)

    def col(k, hs=slice(0, D_A)):
        return proj[:, k * D_A + hs.start:k * D_A + hs.stop]

    lb = _lower_bound(lbl_ref[...])
    bits, n_levels = _row_bits(), _n_levels(DEC_SEQ)
    vi16 = col(2).astype(BF16)

    def head(hd):
        hs = slice(hd * HEAD, (hd + 1) * HEAD)

        def gates(i):
            rows = slice(i * SUBLANES, (i + 1) * SUBLANES)
            return _hgrn_gates(proj[rows, hs], proj[rows, D_A + hd * HEAD:D_A + (hd + 1) * HEAD],
                               lb[:, hs])

        sc, q_in, k_out, total = yield from _hgrn_head(gates, lvl_ref, bits, n_levels)
        qin_ref[:, hs] = q_in
        kot_ref[hd] = k_out.T
        tot_ref[hd] = total.T
        return jnp.dot(sc, vi16[:, hs], preferred_element_type=F32)

    o_intra = _lockstep([head(hd) for hd in range(N_HEADS)])

    lane = lax.broadcasted_iota(jnp.int32, (HEAD, ROWS), 1)

    def seq_head(i, hd):
        hs = slice(hd * HEAD, (hd + 1) * HEAD)
        seq_rows = slice(i * DEC_SEQ, (i + 1) * DEC_SEQ)
        state = stin_ref[0, i, hd]
        qi = qin_ref[seq_rows, hs].astype(BF16)
        oint_ref[seq_rows, hs] = jnp.dot(qi, state.astype(BF16), preferred_element_type=F32)
        yield
        in_seq = (lane >> (DEC_SEQ.bit_length() - 1)) == i
        k_seq = jnp.where(in_seq, kot_ref[hd], 0.0).astype(BF16)
        decay = tot_ref[hd, :, i * DEC_SEQ:i * DEC_SEQ + 1]
        yield
        stout_ref[0, i, hd] = decay * state + jnp.dot(
            k_seq, vi16[:, hs], preferred_element_type=F32)

    for grp in range(SEQS_PER_TILE // SEQ_GROUP):
        _lockstep([seq_head(grp * SEQ_GROUP + s, hd)
                   for s in range(SEQ_GROUP) for hd in range(N_HEADS)])

    hg = hg_ref[...]
    bias = bias_ref[...]
    v_n = _layer_norm(_gelu(col(5)), lng_ref[...], lnb_ref[...])
    for hd in range(N_HEADS):
        vch_ref[0, :, :, hd, :] = v_n[:, hd * HEAD:(hd + 1) * HEAD].reshape(
            SEQS_PER_TILE, DEC_SEQ, HEAD)
    v_n16 = v_n.astype(BF16)
    mixed = []
    for hd in range(N_HEADS):
        hs = slice(hd * HEAD, (hd + 1) * HEAD)
        mixed.append(_hgrn_finish(o_intra[hd] + oint_ref[:, hs], col(3, hs), hg[:, hs]))
    for hd in range(N_HEADS):
        hs = slice(hd * HEAD, (hd + 1) * HEAD)
        mixed.append(_mlp_head(hd, col(4, hs), v_n16[:, hs], col(6, hs), wc_ref, bias))
    out = x + jnp.dot(jnp.concatenate(mixed, axis=-1), wout16_ref[...],
                      preferred_element_type=F32)
    y_ref[...] = _rms_norm(out, fg_ref[...])


def _const_spec(shape):
    return pl.BlockSpec(shape, lambda *_: (0,) * len(shape), pipeline_mode=pl.Buffered(1))


_WEIGHT_SPECS = [
    _const_spec((1, D_MODEL)),
    _const_spec((D_MODEL, D_IN)),
    _const_spec((2, D_A)),
    _const_spec((1, D_A)),
    _const_spec((1, D_B)),
    _const_spec((1, D_B)),
    _const_spec((N_HEADS, ROWS, ROWS)),
    _const_spec((N_HEADS, ROWS)),
    _const_spec((D_MODEL, D_MODEL)),
    _const_spec((1, D_MODEL)),
]

_CONSTANT_SCRATCH = [
    pltpu.VMEM((ROWS, ROWS), jnp.int32),
    pltpu.VMEM((N_HEADS, ROWS, ROWS), BF16),
    pltpu.VMEM((ROWS, N_HEADS), F32),
    pltpu.VMEM((D_MODEL, D_MODEL), BF16),
]


def kernel(x_prompt, x_sample, state_hgrn, norm_g, w_in, lb_logits, hgrn_norm_g, sgu_ln_g,
           sgu_ln_b, w_s, b_s, w_out, final_norm_g):
    depth = norm_g.shape[0]
    assert depth == 1 and lb_logits.shape == (2, D_A)
    batch, seq, _ = x_prompt.shape
    dec_batch, dec_seq, _ = x_sample.shape
    assert seq % PROMPT_TILE == 0 and dec_seq == DEC_SEQ and dec_batch % SEQS_PER_TILE == 0

    weights = (norm_g, w_in[0].astype(BF16), lb_logits, hgrn_norm_g, sgu_ln_g, sgu_ln_b,
               w_s[0], b_s[0], w_out[0], final_norm_g.reshape(1, D_MODEL))
    params = pltpu.CompilerParams(dimension_semantics=("arbitrary",),
                                  vmem_limit_bytes=VMEM_LIMIT_BYTES)

    tiles_per_seq = seq // PROMPT_TILE
    n_tiles = batch * tiles_per_seq

    def cur_tile(g):
        t = jnp.minimum(g, n_tiles - 1)
        return t // tiles_per_seq, t % tiles_per_seq

    def prev_tile(g):
        t = jnp.maximum(g - 1, 0)
        return t // tiles_per_seq, t % tiles_per_seq

    y_p, st_p, vch_p = pl.pallas_call(
        functools.partial(_prompt_kernel, tiles_per_seq=tiles_per_seq),
        grid=(n_tiles + 1,),
        in_specs=[pl.BlockSpec((1, PROMPT_TILE, D_MODEL), lambda g: (*cur_tile(g), 0)),
                  pl.BlockSpec((1, PROMPT_TILE, D_MODEL), lambda g: (*prev_tile(g), 0))]
        + _WEIGHT_SPECS,
        out_specs=[
            pl.BlockSpec((1, PROMPT_TILE, D_MODEL), lambda g: (*prev_tile(g), 0)),
            pl.BlockSpec((1, 1, N_HEADS, HEAD, HEAD), lambda g: (0, prev_tile(g)[0], 0, 0, 0)),
            pl.BlockSpec((1, 1, ROWS, N_HEADS, HEAD), lambda g: (0, prev_tile(g)[0], 0, 0, 0)),
        ],
        out_shape=[
            jax.ShapeDtypeStruct((batch, seq, D_MODEL), F32),
            jax.ShapeDtypeStruct((1, batch, N_HEADS, HEAD, HEAD), F32),
            jax.ShapeDtypeStruct((1, batch, ROWS, N_HEADS, HEAD), F32),
        ],
        scratch_shapes=[pltpu.VMEM((PROMPT_TILE, D_IN), F32),
                        pltpu.VMEM((PROMPT_TILE, D_IN), F32),
                        pltpu.VMEM((PROMPT_TILE, D_MODEL), BF16),
                        pltpu.VMEM((PROMPT_TILE, D_MODEL), BF16)] + _CONSTANT_SCRATCH,
        compiler_params=params,
        name="prompt_layer",
    )(x_prompt, x_prompt, *weights)

    n_rows = dec_batch * DEC_SEQ
    y_s, st_s, vch_s = pl.pallas_call(
        _sample_kernel,
        grid=(n_rows // ROWS,),
        in_specs=[pl.BlockSpec((ROWS, D_MODEL), lambda i: (i, 0))]
        + _WEIGHT_SPECS
        + [pl.BlockSpec((1, SEQS_PER_TILE, N_HEADS, HEAD, HEAD), lambda i: (0, i, 0, 0, 0))],
        out_specs=[
            pl.BlockSpec((ROWS, D_MODEL), lambda i: (i, 0)),
            pl.BlockSpec((1, SEQS_PER_TILE, N_HEADS, HEAD, HEAD), lambda i: (0, i, 0, 0, 0)),
            pl.BlockSpec((1, SEQS_PER_TILE, DEC_SEQ, N_HEADS, HEAD), lambda i: (0, i, 0, 0, 0)),
        ],
        out_shape=[
            jax.ShapeDtypeStruct((n_rows, D_MODEL), F32),
            jax.ShapeDtypeStruct((1, dec_batch, N_HEADS, HEAD, HEAD), F32),
            jax.ShapeDtypeStruct((1, dec_batch, DEC_SEQ, N_HEADS, HEAD), F32),
        ],
        scratch_shapes=[pltpu.VMEM((ROWS, D_A), F32),
                        pltpu.VMEM((N_HEADS, HEAD, ROWS), F32),
                        pltpu.VMEM((N_HEADS, HEAD, ROWS), F32),
                        pltpu.VMEM((ROWS, D_A), F32)] + _CONSTANT_SCRATCH,
        compiler_params=pltpu.CompilerParams(dimension_semantics=("arbitrary",),
                                             vmem_limit_bytes=VMEM_LIMIT_BYTES),
        name="decode_layer",
    )(x_sample.reshape(n_rows, D_MODEL), *weights, state_hgrn)

    return (y_p,
            y_s.reshape(dec_batch, DEC_SEQ, D_MODEL),
            st_p,
            st_s,
            vch_p,
            vch_s)
```

```python
import functools

import jax
import jax.numpy as jnp
from jax import lax
from jax.experimental import pallas as pl
from jax.experimental.pallas import tpu as pltpu

F32 = jnp.float32
BF16 = jnp.bfloat16

D_MODEL = 1024
D_A = 512
D_B = 512
HEAD = 128
N_HEADS = 4
D_IN = 4 * D_A + 3 * D_B
ROWS = 128
SUBLANES = 8
PACKED = 2 * SUBLANES
N_TILES = ROWS // SUBLANES
SUBLANE_LEVELS = 3
EPS = 1e-6
PROMPT_TILE = 512
PROJ_BLOCK = 512
HEAD_ROUNDS = 12
CHUNK_STAGGER = 3
HEAD_STAGGER = 1
DEC_SEQ = 8
SEQS_PER_TILE = ROWS // DEC_SEQ
SEQ_GROUP = 4
VMEM_LIMIT_BYTES = 56 * 1024 * 1024

_NT = (((1,), (1,)), ((), ()))


def _silu(x):
    return x * jax.nn.sigmoid(x)


def _gelu(x):
    c = -2.0 * 0.7978845608028654 * 1.4426950408889634
    return x / (1.0 + jnp.exp2(x * (c * 0.044715 * (x * x) + c)))


def _rms_norm(x, g):
    return x * lax.rsqrt(jnp.mean(x * x, axis=-1, keepdims=True) + EPS) * g


def _layer_norm(x, g, b):
    mu = jnp.mean(x, axis=-1, keepdims=True)
    xc = x - mu
    var = jnp.mean(xc * xc, axis=-1, keepdims=True)
    return xc * lax.rsqrt(var + EPS) * g + b


def _lower_bound(lbl):
    rows = [lbl[r:r + 1, :] for r in range(lbl.shape[0])]
    mx = functools.reduce(jnp.maximum, rows)
    es = [jnp.exp(r - mx) for r in rows]
    return es[0] / functools.reduce(lambda a, c: a + c, es)


def _n_levels(seg_len):
    n_levels = seg_len.bit_length() - 1
    assert seg_len == 1 << n_levels and n_levels >= SUBLANE_LEVELS
    return n_levels


def _split3(x):
    hi = x.astype(BF16)
    r1 = x - hi.astype(F32)
    mid = r1.astype(BF16)
    lo = (r1 - mid.astype(F32)).astype(BF16)
    return hi, mid, lo


def _store_constants(seg_len, ws_ref, bs_ref, wout_ref, lvl_ref, wc_ref, bias_ref, wout16_ref):
    t = lax.broadcasted_iota(jnp.int32, (ROWS, ROWS), 0)
    s = lax.broadcasted_iota(jnp.int32, (ROWS, ROWS), 1)
    n_levels = _n_levels(seg_len)
    x = t ^ s
    lvl = jnp.where(t == s, n_levels, -1)
    for j in range(n_levels):
        lvl = jnp.where(((x >> j) == 1) & (((t >> j) & 1) == 1), j, lvl)
    lvl_ref[...] = lvl
    causal = (t >= s) & ((t >> n_levels) == (s >> n_levels))
    pick = jnp.where(s == (t & (seg_len - 1)), 1.0, 0.0).astype(BF16)
    for hd in range(N_HEADS):
        w = ws_ref[hd].astype(BF16)
        if seg_len != ROWS:
            w = jnp.dot(pick, w, preferred_element_type=F32).astype(BF16)
            w = lax.dot_general(w, pick, _NT, preferred_element_type=F32).astype(BF16)
        wc_ref[hd] = jnp.where(causal, w, jnp.zeros_like(w))
    pad = jnp.zeros((PACKED - N_HEADS, ROWS), F32)
    bias = sum(lax.dot_general(pick, part, _NT, preferred_element_type=F32)
               for part in _split3(jnp.concatenate([bs_ref[...], pad], axis=0)))
    bias_ref[...] = bias[:, :N_HEADS]
    wout16_ref[...] = wout_ref[...].astype(BF16)


def _row_bits():
    rowi = lax.broadcasted_iota(jnp.int32, (SUBLANES, HEAD), 0)
    return [((rowi >> j) & 1) == 1 for j in range(SUBLANE_LEVELS)]


def _hgrn_head(gates, lvl_ref, bits, n_levels):
    one = jnp.ones((SUBLANES, HEAD), F32)
    z = [[] for _ in range(n_levels)]
    qp_t, ks_t, blk_t, diag_t = [], [], [], []
    for pair in range(N_TILES // 2):
        z_pair = [[] for _ in range(SUBLANE_LEVELS)]
        for i in (2 * pair, 2 * pair + 1):
            q, kk, f = gates(i)
            diag_t.append(jnp.sum(q * kk, axis=-1, keepdims=True))
            qp, ks, blk = q * f, kk, f
            for j in range(SUBLANE_LEVELS):
                m = 1 << j
                z_pair[j].append(jnp.where(bits[j], qp, ks))
                sib = pltpu.roll(blk, m, 0)
                if 2 * m != SUBLANES:
                    sib = jnp.where(bits[j], sib, pltpu.roll(blk, SUBLANES - m, 0))
                qp = qp * jnp.where(bits[j], sib, one)
                ks = ks * jnp.where(bits[j], one, sib)
                blk = blk * sib
            qp_t.append(qp)
            ks_t.append(ks)
            blk_t.append(blk)
        for j in range(SUBLANE_LEVELS):
            z[j].append(jnp.concatenate(z_pair[j], axis=0).astype(BF16))
    yield

    blk_levels, blk_b = [], blk_t
    for j in range(SUBLANE_LEVELS, n_levels):
        blk_levels.append(blk_b)
        blk_b = [blk_b[2 * n] * blk_b[2 * n + 1] for n in range(len(blk_b) // 2)]
    for pair in range(N_TILES // 2):
        z_pair = [[] for _ in range(SUBLANE_LEVELS, n_levels)]
        for i in (2 * pair, 2 * pair + 1):
            for j in range(SUBLANE_LEVELS, n_levels):
                block = i >> (j - SUBLANE_LEVELS)
                sib = blk_levels[j - SUBLANE_LEVELS][block ^ 1]
                if block & 1:
                    z_pair[j - SUBLANE_LEVELS].append(qp_t[i])
                    qp_t[i] = qp_t[i] * sib
                else:
                    z_pair[j - SUBLANE_LEVELS].append(ks_t[i])
                    ks_t[i] = ks_t[i] * sib
        for j in range(SUBLANE_LEVELS, n_levels):
            z[j].append(jnp.concatenate(z_pair[j - SUBLANE_LEVELS], axis=0).astype(BF16))
    q_in = jnp.concatenate(qp_t, axis=0)
    k_out = jnp.concatenate(ks_t, axis=0)
    per_tile = N_TILES // len(blk_b)
    total = jnp.concatenate([blk_b[i // per_tile] for i in range(N_TILES)], axis=0)
    yield

    p, p_row = [], []
    for j in range(n_levels):
        zj = jnp.concatenate(z[j], axis=0)
        span = 1 << max(j - SUBLANE_LEVELS, 0)
        if span * SUBLANES >= PACKED:
            upper = [i for i in range(N_TILES) if (i // span) & 1]
            lhs = jnp.concatenate([z[j][i // 2] for i in upper[::2]], axis=0)
        else:
            upper = list(range(N_TILES))
            lhs = zj
        p.append(lax.dot_general(lhs, zj, _NT, preferred_element_type=F32))
        p_row.append({i: n * SUBLANES for n, i in enumerate(upper)})
        yield
    sc_t = []
    for i in range(N_TILES):
        lvl_i = lvl_ref[i * SUBLANES:(i + 1) * SUBLANES, :]
        acc = jnp.where(lvl_i == n_levels, diag_t[i], 0.0)
        for j in range(n_levels):
            if j < SUBLANE_LEVELS or (i >> (j - SUBLANE_LEVELS)) & 1:
                r0 = p_row[j][i]
                acc = jnp.where(lvl_i == j, p[j][r0:r0 + SUBLANES, :], acc)
        sc_t.append(acc)
    sc = jnp.concatenate(sc_t, axis=0).astype(BF16)
    return sc, q_in, k_out, total


def _lockstep(generators, starts=None):
    starts = starts or [0] * len(generators)
    results = [None] * len(generators)
    waiting = sorted(range(len(generators)), key=lambda i: starts[i])
    active = []
    rnd = 0
    while waiting or active:
        while waiting and starts[waiting[0]] <= rnd:
            active.append(waiting.pop(0))
        still = []
        for idx in active:
            try:
                next(generators[idx])
                still.append(idx)
            except StopIteration as done:
                results[idx] = done.value
        active = still
        rnd += 1
    return results


def _hgrn_gates(pq, pf, lb):
    q = _silu(pq)
    forget = lb + (1.0 - lb) * jax.nn.sigmoid(pf)
    return q, 1.0 - forget, forget


def _hgrn_finish(o, pz, g):
    return (_rms_norm(o, g) * _silu(pz)).astype(BF16)


def _mlp_head(hd, u, v_n16, gate, wc_ref, bias):
    mix = jnp.dot(wc_ref[hd], v_n16, preferred_element_type=F32) + bias[:, hd:hd + 1]
    return (_gelu(u) * mix * _silu(gate)).astype(BF16)


def _prompt_kernel(x_ref, xprev_ref, ng_ref, win_ref, lbl_ref, hg_ref, lng_ref, lnb_ref, ws_ref,
                   bs_ref, wout_ref, fg_ref, y_ref, st_ref, vch_ref, proj_a, proj_b, h_ref,
                   mixed_ref, lvl_ref, wc_ref, bias_ref, wout16_ref, *, tiles_per_seq):
    g = pl.program_id(0)

    @pl.when(g == 0)
    def _():
        proj_b[...] = jnp.zeros_like(proj_b)
        _store_constants(ROWS, ws_ref, bs_ref, wout_ref, lvl_ref, wc_ref, bias_ref, wout16_ref)

    @pl.when(lax.rem(jnp.maximum(g - 1, 0), tiles_per_seq) == 0)
    def _():
        st_ref[...] = jnp.zeros_like(st_ref)

    def step(proj_next, proj_prev):
        h_ref[...] = _rms_norm(x_ref[0], ng_ref[...]).astype(BF16)
        lb = _lower_bound(lbl_ref[...])
        bits, n_levels = _row_bits(), _n_levels(ROWS)
        hg = hg_ref[...]
        lng = lng_ref[...]
        lnb = lnb_ref[...]
        bias = bias_ref[...]
        n_chunks = PROMPT_TILE // ROWS
        n_blocks = D_IN // PROJ_BLOCK

        def col(c, k, hs=slice(0, D_A)):
            return proj_prev[c * ROWS:(c + 1) * ROWS, k * D_A + hs.start:k * D_A + hs.stop]

        def head(c, hd):
            hs = slice(hd * HEAD, (hd + 1) * HEAD)

            def gates(i):
                rows = slice(c * ROWS + i * SUBLANES, c * ROWS + (i + 1) * SUBLANES)
                return _hgrn_gates(proj_prev[rows, hs],
                                   proj_prev[rows, D_A + hd * HEAD:D_A + (hd + 1) * HEAD],
                                   lb[:, hs])

            sc, q_in, k_out, total = yield from _hgrn_head(gates, lvl_ref, bits, n_levels)
            state = st_ref[0, 0, hd]
            vh = col(c, 2, hs)
            o = jnp.dot(jnp.concatenate([sc, q_in.astype(BF16)], axis=1),
                        jnp.concatenate([vh.astype(BF16), state.astype(BF16)], axis=0),
                        preferred_element_type=F32)
            yield
            st_ref[0, 0, hd] = total.T * state + jnp.dot(
                k_out.T.astype(BF16), vh.astype(BF16), preferred_element_type=F32)
            yield
            mixed_ref[c * ROWS:(c + 1) * ROWS, hs] = _hgrn_finish(o, col(c, 3, hs), hg[:, hs])

        def mlp(c):
            v_n = _layer_norm(_gelu(col(c, 5)), lng, lnb)
            if c == n_chunks - 1:
                for hd in range(N_HEADS):
                    vch_ref[0, 0, :, hd, :] = v_n[:, hd * HEAD:(hd + 1) * HEAD]
            v_n16 = v_n.astype(BF16)
            yield
            for hd in range(N_HEADS):
                hs = slice(hd * HEAD, (hd + 1) * HEAD)
                mixed_ref[c * ROWS:(c + 1) * ROWS, D_A + hd * HEAD:D_A + (hd + 1) * HEAD] = (
                    _mlp_head(hd, col(c, 4, hs), v_n16[:, hs], col(c, 6, hs), wc_ref, bias))
                yield
                yield

        def projection(rounds_per_block):
            for blk in range(n_blocks):
                cols = slice(blk * PROJ_BLOCK, (blk + 1) * PROJ_BLOCK)
                proj_next[:, cols] = jnp.dot(h_ref[...], win_ref[:, cols],
                                             preferred_element_type=F32)
                for _ in range(rounds_per_block):
                    yield

        items = [gen for c in range(n_chunks)
                 for gen in [head(c, hd) for hd in range(N_HEADS)] + [mlp(c)]]
        starts = [c * CHUNK_STAGGER + k * HEAD_STAGGER
                  for c in range(n_chunks) for k in list(range(N_HEADS)) + [0]]
        rounds = (n_chunks - 1) * CHUNK_STAGGER + (N_HEADS - 1) * HEAD_STAGGER + HEAD_ROUNDS
        _lockstep(items + [projection(max(1, rounds // n_blocks))], starts + [0])

        out = xprev_ref[0] + jnp.dot(mixed_ref[...], wout16_ref[...], preferred_element_type=F32)
        y_ref[0] = _rms_norm(out, fg_ref[...])

    parity = lax.rem(g, 2)

    @pl.when(parity == 0)
    def _():
        step(proj_a, proj_b)

    @pl.when(parity == 1)
    def _():
        step(proj_b, proj_a)


def _sample_kernel(x_ref, ng_ref, win_ref, lbl_ref, hg_ref, lng_ref, lnb_ref, ws_ref, bs_ref,
                   wout_ref, fg_ref, stin_ref, y_ref, stout_ref, vch_ref,
                   qin_ref, kot_ref, tot_ref, oint_ref, lvl_ref, wc_ref, bias_ref, wout16_ref):
    @pl.when(pl.program_id(0) == 0)
    def _():
        _store_constants(DEC_SEQ, ws_ref, bs_ref, wout_ref, lvl_ref, wc_ref, bias_ref, wout16_ref)

    x = x_ref[...]
    h = _rms_norm(x, ng_ref[...]).astype(BF16)
    proj = jnp.dot(h, win_ref[...], preferred_element_type=F32)

    def col(k, hs=slice(0, D_A)):
        return proj[:, k * D_A + hs.start:k * D_A + hs.stop]

    lb = _lower_bound(lbl_ref[...])
    bits, n_levels = _row_bits(), _n_levels(DEC_SEQ)
    vi16 = col(2).astype(BF16)

    def head(hd):
        hs = slice(hd * HEAD, (hd + 1) * HEAD)

        def gates(i):
            rows = slice(i * SUBLANES, (i + 1) * SUBLANES)
            return _hgrn_gates(proj[rows, hs], proj[rows, D_A + hd * HEAD:D_A + (hd + 1) * HEAD],
                               lb[:, hs])

        sc, q_in, k_out, total = yield from _hgrn_head(gates, lvl_ref, bits, n_levels)
        qin_ref[:, hs] = q_in
        kot_ref[hd] = k_out.T
        tot_ref[hd] = total.T
        return jnp.dot(sc, vi16[:, hs], preferred_element_type=F32)

    o_intra = _lockstep([head(hd) for hd in range(N_HEADS)])

    lane = lax.broadcasted_iota(jnp.int32, (HEAD, ROWS), 1)

    def seq_head(i, hd):
        hs = slice(hd * HEAD, (hd + 1) * HEAD)
        seq_rows = slice(i * DEC_SEQ, (i + 1) * DEC_SEQ)
        state = stin_ref[0, i, hd]
        qi = qin_ref[seq_rows, hs].astype(BF16)
        oint_ref[seq_rows, hs] = jnp.dot(qi, state.astype(BF16), preferred_element_type=F32)
        yield
        in_seq = (lane >> (DEC_SEQ.bit_length() - 1)) == i
        k_seq = jnp.where(in_seq, kot_ref[hd], 0.0).astype(BF16)
        decay = tot_ref[hd, :, i * DEC_SEQ:i * DEC_SEQ + 1]
        yield
        stout_ref[0, i, hd] = decay * state + jnp.dot(
            k_seq, vi16[:, hs], preferred_element_type=F32)

    for grp in range(SEQS_PER_TILE // SEQ_GROUP):
        _lockstep([seq_head(grp * SEQ_GROUP + s, hd)
                   for s in range(SEQ_GROUP) for hd in range(N_HEADS)])

    hg = hg_ref[...]
    bias = bias_ref[...]
    v_n = _layer_norm(_gelu(col(5)), lng_ref[...], lnb_ref[...])
    for hd in range(N_HEADS):
        vch_ref[0, :, :, hd, :] = v_n[:, hd * HEAD:(hd + 1) * HEAD].reshape(
            SEQS_PER_TILE, DEC_SEQ, HEAD)
    v_n16 = v_n.astype(BF16)
    mixed = []
    for hd in range(N_HEADS):
        hs = slice(hd * HEAD, (hd + 1) * HEAD)
        mixed.append(_hgrn_finish(o_intra[hd] + oint_ref[:, hs], col(3, hs), hg[:, hs]))
    for hd in range(N_HEADS):
        hs = slice(hd * HEAD, (hd + 1) * HEAD)
        mixed.append(_mlp_head(hd, col(4, hs), v_n16[:, hs], col(6, hs), wc_ref, bias))
    out = x + jnp.dot(jnp.concatenate(mixed, axis=-1), wout16_ref[...],
                      preferred_element_type=F32)
    y_ref[...] = _rms_norm(out, fg_ref[...])


def _const_spec(shape):
    return pl.BlockSpec(shape, lambda *_: (0,) * len(shape), pipeline_mode=pl.Buffered(1))


_WEIGHT_SPECS = [
    _const_spec((1, D_MODEL)),
    _const_spec((D_MODEL, D_IN)),
    _const_spec((2, D_A)),
    _const_spec((1, D_A)),
    _const_spec((1, D_B)),
    _const_spec((1, D_B)),
    _const_spec((N_HEADS, ROWS, ROWS)),
    _const_spec((N_HEADS, ROWS)),
    _const_spec((D_MODEL, D_MODEL)),
    _const_spec((1, D_MODEL)),
]

_CONSTANT_SCRATCH = [
    pltpu.VMEM((ROWS, ROWS), jnp.int32),
    pltpu.VMEM((N_HEADS, ROWS, ROWS), BF16),
    pltpu.VMEM((ROWS, N_HEADS), F32),
    pltpu.VMEM((D_MODEL, D_MODEL), BF16),
]


def kernel(x_prompt, x_sample, state_hgrn, norm_g, w_in, lb_logits, hgrn_norm_g, sgu_ln_g,
           sgu_ln_b, w_s, b_s, w_out, final_norm_g):
    depth = norm_g.shape[0]
    assert depth == 1 and lb_logits.shape == (2, D_A)
    batch, seq, _ = x_prompt.shape
    dec_batch, dec_seq, _ = x_sample.shape
    assert seq % PROMPT_TILE == 0 and dec_seq == DEC_SEQ and dec_batch % SEQS_PER_TILE == 0

    weights = (norm_g, w_in[0].astype(BF16), lb_logits, hgrn_norm_g, sgu_ln_g, sgu_ln_b,
               w_s[0], b_s[0], w_out[0], final_norm_g.reshape(1, D_MODEL))
    params = pltpu.CompilerParams(dimension_semantics=("arbitrary",),
                                  vmem_limit_bytes=VMEM_LIMIT_BYTES)

    tiles_per_seq = seq // PROMPT_TILE
    n_tiles = batch * tiles_per_seq

    def cur_tile(g):
        t = jnp.minimum(g, n_tiles - 1)
        return t // tiles_per_seq, t % tiles_per_seq

    def prev_tile(g):
        t = jnp.maximum(g - 1, 0)
        return t // tiles_per_seq, t % tiles_per_seq

    y_p, st_p, vch_p = pl.pallas_call(
        functools.partial(_prompt_kernel, tiles_per_seq=tiles_per_seq),
        grid=(n_tiles + 1,),
        in_specs=[pl.BlockSpec((1, PROMPT_TILE, D_MODEL), lambda g: (*cur_tile(g), 0)),
                  pl.BlockSpec((1, PROMPT_TILE, D_MODEL), lambda g: (*prev_tile(g), 0))]
        + _WEIGHT_SPECS,
        out_specs=[
            pl.BlockSpec((1, PROMPT_TILE, D_MODEL), lambda g: (*prev_tile(g), 0)),
            pl.BlockSpec((1, 1, N_HEADS, HEAD, HEAD), lambda g: (0, prev_tile(g)[0], 0, 0, 0)),
            pl.BlockSpec((1, 1, ROWS, N_HEADS, HEAD), lambda g: (0, prev_tile(g)[0], 0, 0, 0)),
        ],
        out_shape=[
            jax.ShapeDtypeStruct((batch, seq, D_MODEL), F32),
            jax.ShapeDtypeStruct((1, batch, N_HEADS, HEAD, HEAD), F32),
            jax.ShapeDtypeStruct((1, batch, ROWS, N_HEADS, HEAD), F32),
        ],
        scratch_shapes=[pltpu.VMEM((PROMPT_TILE, D_IN), F32),
                        pltpu.VMEM((PROMPT_TILE, D_IN), F32),
                        pltpu.VMEM((PROMPT_TILE, D_MODEL), BF16),
                        pltpu.VMEM((PROMPT_TILE, D_MODEL), BF16)] + _CONSTANT_SCRATCH,
        compiler_params=params,
        name="prompt_layer",
    )(x_prompt, x_prompt, *weights)

    n_rows = dec_batch * DEC_SEQ
    y_s, st_s, vch_s = pl.pallas_call(
        _sample_kernel,
        grid=(n_rows // ROWS,),
        in_specs=[pl.BlockSpec((ROWS, D_MODEL), lambda i: (i, 0))]
        + _WEIGHT_SPECS
        + [pl.BlockSpec((1, SEQS_PER_TILE, N_HEADS, HEAD, HEAD), lambda i: (0, i, 0, 0, 0))],
        out_specs=[
            pl.BlockSpec((ROWS, D_MODEL), lambda i: (i, 0)),
            pl.BlockSpec((1, SEQS_PER_TILE, N_HEADS, HEAD, HEAD), lambda i: (0, i, 0, 0, 0)),
            pl.BlockSpec((1, SEQS_PER_TILE, DEC_SEQ, N_HEADS, HEAD), lambda i: (0, i, 0, 0, 0)),
        ],
        out_shape=[
            jax.ShapeDtypeStruct((n_rows, D_MODEL), F32),
            jax.ShapeDtypeStruct((1, dec_batch, N_HEADS, HEAD, HEAD), F32),
            jax.ShapeDtypeStruct((1, dec_batch, DEC_SEQ, N_HEADS, HEAD), F32),
        ],
        scratch_shapes=[pltpu.VMEM((ROWS, D_A), F32),
                        pltpu.VMEM((N_HEADS, HEAD, ROWS), F32),
                        pltpu.VMEM((N_HEADS, HEAD, ROWS), F32),
                        pltpu.VMEM((ROWS, D_A), F32)] + _CONSTANT_SCRATCH,
        compiler_params=pltpu.CompilerParams(dimension_semantics=("arbitrary",),
                                             vmem_limit_bytes=VMEM_LIMIT_BYTES),
        name="decode_layer",
    )(x_sample.reshape(n_rows, D_MODEL), *weights, state_hgrn)

    return (y_p,
            y_s.reshape(dec_batch, DEC_SEQ, D_MODEL),
            st_p,
            st_s,
            vch_p,
            vch_s)
```

```python
import functools

import jax
import jax.numpy as jnp
from jax import lax
from jax.experimental import pallas as pl
from jax.experimental.pallas import tpu as pltpu

F32 = jnp.float32
BF16 = jnp.bfloat16

D_MODEL = 1024
D_A = 512
D_B = 512
HEAD = 128
N_HEADS = 4
D_IN = 4 * D_A + 3 * D_B
ROWS = 128
SUBLANES = 8
PACKED = 2 * SUBLANES
N_TILES = ROWS // SUBLANES
SUBLANE_LEVELS = 3
EPS = 1e-6
PROMPT_TILE = 512
PROJ_BLOCK = 512
HEAD_ROUNDS = 12
CHUNK_STAGGER = 5
HEAD_STAGGER = 1
DEC_SEQ = 8
SEQS_PER_TILE = ROWS // DEC_SEQ
SEQ_GROUP = 4
VMEM_LIMIT_BYTES = 56 * 1024 * 1024

_NT = (((1,), (1,)), ((), ()))


def _silu(x):
    return x * jax.nn.sigmoid(x)


def _gelu(x):
    c = -2.0 * 0.7978845608028654 * 1.4426950408889634
    return x / (1.0 + jnp.exp2(x * (c * 0.044715 * (x * x) + c)))


def _rms_norm(x, g):
    return x * lax.rsqrt(jnp.mean(x * x, axis=-1, keepdims=True) + EPS) * g


def _layer_norm(x, g, b):
    mu = jnp.mean(x, axis=-1, keepdims=True)
    xc = x - mu
    var = jnp.mean(xc * xc, axis=-1, keepdims=True)
    return xc * lax.rsqrt(var + EPS) * g + b


def _lower_bound(lbl):
    rows = [lbl[r:r + 1, :] for r in range(lbl.shape[0])]
    mx = functools.reduce(jnp.maximum, rows)
    es = [jnp.exp(r - mx) for r in rows]
    return es[0] / functools.reduce(lambda a, c: a + c, es)


def _n_levels(seg_len):
    n_levels = seg_len.bit_length() - 1
    assert seg_len == 1 << n_levels and n_levels >= SUBLANE_LEVELS
    return n_levels


def _split3(x):
    hi = x.astype(BF16)
    r1 = x - hi.astype(F32)
    mid = r1.astype(BF16)
    lo = (r1 - mid.astype(F32)).astype(BF16)
    return hi, mid, lo


def _store_constants(seg_len, ws_ref, bs_ref, wout_ref, lvl_ref, wc_ref, bias_ref, wout16_ref):
    t = lax.broadcasted_iota(jnp.int32, (ROWS, ROWS), 0)
    s = lax.broadcasted_iota(jnp.int32, (ROWS, ROWS), 1)
    n_levels = _n_levels(seg_len)
    x = t ^ s
    lvl = jnp.where(t == s, n_levels, -1)
    for j in range(n_levels):
        lvl = jnp.where(((x >> j) == 1) & (((t >> j) & 1) == 1), j, lvl)
    lvl_ref[...] = lvl
    causal = (t >= s) & ((t >> n_levels) == (s >> n_levels))
    pick = jnp.where(s == (t & (seg_len - 1)), 1.0, 0.0).astype(BF16)
    for hd in range(N_HEADS):
        w = ws_ref[hd].astype(BF16)
        if seg_len != ROWS:
            w = jnp.dot(pick, w, preferred_element_type=F32).astype(BF16)
            w = lax.dot_general(w, pick, _NT, preferred_element_type=F32).astype(BF16)
        wc_ref[hd] = jnp.where(causal, w, jnp.zeros_like(w))
    pad = jnp.zeros((PACKED - N_HEADS, ROWS), F32)
    bias = sum(lax.dot_general(pick, part, _NT, preferred_element_type=F32)
               for part in _split3(jnp.concatenate([bs_ref[...], pad], axis=0)))
    bias_ref[...] = bias[:, :N_HEADS]
    wout16_ref[...] = wout_ref[...].astype(BF16)


def _row_bits():
    rowi = lax.broadcasted_iota(jnp.int32, (SUBLANES, HEAD), 0)
    return [((rowi >> j) & 1) == 1 for j in range(SUBLANE_LEVELS)]


def _hgrn_head(gates, lvl_ref, bits, n_levels):
    one = jnp.ones((SUBLANES, HEAD), F32)
    z = [[] for _ in range(n_levels)]
    qp_t, ks_t, blk_t, diag_t = [], [], [], []
    for pair in range(N_TILES // 2):
        z_pair = [[] for _ in range(SUBLANE_LEVELS)]
        for i in (2 * pair, 2 * pair + 1):
            q, kk, f = gates(i)
            diag_t.append(jnp.sum(q * kk, axis=-1, keepdims=True))
            qp, ks, blk = q * f, kk, f
            for j in range(SUBLANE_LEVELS):
                m = 1 << j
                z_pair[j].append(jnp.where(bits[j], qp, ks))
                sib = pltpu.roll(blk, m, 0)
                if 2 * m != SUBLANES:
                    sib = jnp.where(bits[j], sib, pltpu.roll(blk, SUBLANES - m, 0))
                qp = qp * jnp.where(bits[j], sib, one)
                ks = ks * jnp.where(bits[j], one, sib)
                blk = blk * sib
            qp_t.append(qp)
            ks_t.append(ks)
            blk_t.append(blk)
        for j in range(SUBLANE_LEVELS):
            z[j].append(jnp.concatenate(z_pair[j], axis=0).astype(BF16))
    yield

    blk_levels, blk_b = [], blk_t
    for j in range(SUBLANE_LEVELS, n_levels):
        blk_levels.append(blk_b)
        blk_b = [blk_b[2 * n] * blk_b[2 * n + 1] for n in range(len(blk_b) // 2)]
    for pair in range(N_TILES // 2):
        z_pair = [[] for _ in range(SUBLANE_LEVELS, n_levels)]
        for i in (2 * pair, 2 * pair + 1):
            for j in range(SUBLANE_LEVELS, n_levels):
                block = i >> (j - SUBLANE_LEVELS)
                sib = blk_levels[j - SUBLANE_LEVELS][block ^ 1]
                if block & 1:
                    z_pair[j - SUBLANE_LEVELS].append(qp_t[i])
                    qp_t[i] = qp_t[i] * sib
                else:
                    z_pair[j - SUBLANE_LEVELS].append(ks_t[i])
                    ks_t[i] = ks_t[i] * sib
        for j in range(SUBLANE_LEVELS, n_levels):
            z[j].append(jnp.concatenate(z_pair[j - SUBLANE_LEVELS], axis=0).astype(BF16))
    q_in = jnp.concatenate(qp_t, axis=0)
    k_out = jnp.concatenate(ks_t, axis=0)
    per_tile = N_TILES // len(blk_b)
    total = jnp.concatenate([blk_b[i // per_tile] for i in range(N_TILES)], axis=0)
    yield

    p, p_row = [], []
    for j in range(n_levels):
        zj = jnp.concatenate(z[j], axis=0)
        span = 1 << max(j - SUBLANE_LEVELS, 0)
        if span * SUBLANES >= PACKED:
            upper = [i for i in range(N_TILES) if (i // span) & 1]
            lhs = jnp.concatenate([z[j][i // 2] for i in upper[::2]], axis=0)
        else:
            upper = list(range(N_TILES))
            lhs = zj
        p.append(lax.dot_general(lhs, zj, _NT, preferred_element_type=F32))
        p_row.append({i: n * SUBLANES for n, i in enumerate(upper)})
        yield
    sc_t = []
    for i in range(N_TILES):
        lvl_i = lvl_ref[i * SUBLANES:(i + 1) * SUBLANES, :]
        acc = jnp.where(lvl_i == n_levels, diag_t[i], 0.0)
        for j in range(n_levels):
            if j < SUBLANE_LEVELS or (i >> (j - SUBLANE_LEVELS)) & 1:
                r0 = p_row[j][i]
                acc = jnp.where(lvl_i == j, p[j][r0:r0 + SUBLANES, :], acc)
        sc_t.append(acc)
    sc = jnp.concatenate(sc_t, axis=0).astype(BF16)
    return sc, q_in, k_out, total


def _lockstep(generators, starts=None):
    starts = starts or [0] * len(generators)
    results = [None] * len(generators)
    waiting = sorted(range(len(generators)), key=lambda i: starts[i])
    active = []
    rnd = 0
    while waiting or active:
        while waiting and starts[waiting[0]] <= rnd:
            active.append(waiting.pop(0))
        still = []
        for idx in active:
            try:
                next(generators[idx])
                still.append(idx)
            except StopIteration as done:
                results[idx] = done.value
        active = still
        rnd += 1
    return results


def _hgrn_gates(pq, pf, lb):
    q = _silu(pq)
    forget = lb + (1.0 - lb) * jax.nn.sigmoid(pf)
    return q, 1.0 - forget, forget


def _hgrn_finish(o, pz, g):
    return (_rms_norm(o, g) * _silu(pz)).astype(BF16)


def _mlp_head(hd, u, v_n16, gate, wc_ref, bias):
    mix = jnp.dot(wc_ref[hd], v_n16, preferred_element_type=F32) + bias[:, hd:hd + 1]
    return (_gelu(u) * mix * _silu(gate)).astype(BF16)


def _prompt_kernel(x_ref, xprev_ref, ng_ref, win_ref, lbl_ref, hg_ref, lng_ref, lnb_ref, ws_ref,
                   bs_ref, wout_ref, fg_ref, y_ref, st_ref, vch_ref, proj_a, proj_b, h_ref,
                   mixed_ref, lvl_ref, wc_ref, bias_ref, wout16_ref, *, tiles_per_seq):
    g = pl.program_id(0)

    @pl.when(g == 0)
    def _():
        proj_b[...] = jnp.zeros_like(proj_b)
        _store_constants(ROWS, ws_ref, bs_ref, wout_ref, lvl_ref, wc_ref, bias_ref, wout16_ref)

    @pl.when(lax.rem(jnp.maximum(g - 1, 0), tiles_per_seq) == 0)
    def _():
        st_ref[...] = jnp.zeros_like(st_ref)

    def step(proj_next, proj_prev):
        h_ref[...] = _rms_norm(x_ref[0], ng_ref[...]).astype(BF16)
        lb = _lower_bound(lbl_ref[...])
        bits, n_levels = _row_bits(), _n_levels(ROWS)
        hg = hg_ref[...]
        lng = lng_ref[...]
        lnb = lnb_ref[...]
        bias = bias_ref[...]
        n_chunks = PROMPT_TILE // ROWS
        n_blocks = D_IN // PROJ_BLOCK

        def col(c, k, hs=slice(0, D_A)):
            return proj_prev[c * ROWS:(c + 1) * ROWS, k * D_A + hs.start:k * D_A + hs.stop]

        def head(c, hd):
            hs = slice(hd * HEAD, (hd + 1) * HEAD)

            def gates(i):
                rows = slice(c * ROWS + i * SUBLANES, c * ROWS + (i + 1) * SUBLANES)
                return _hgrn_gates(proj_prev[rows, hs],
                                   proj_prev[rows, D_A + hd * HEAD:D_A + (hd + 1) * HEAD],
                                   lb[:, hs])

            sc, q_in, k_out, total = yield from _hgrn_head(gates, lvl_ref, bits, n_levels)
            state = st_ref[0, 0, hd]
            vh = col(c, 2, hs)
            o = jnp.dot(jnp.concatenate([sc, q_in.astype(BF16)], axis=1),
                        jnp.concatenate([vh.astype(BF16), state.astype(BF16)], axis=0),
                        preferred_element_type=F32)
            yield
            st_ref[0, 0, hd] = total.T * state + jnp.dot(
                k_out.T.astype(BF16), vh.astype(BF16), preferred_element_type=F32)
            yield
            mixed_ref[c * ROWS:(c + 1) * ROWS, hs] = _hgrn_finish(o, col(c, 3, hs), hg[:, hs])

        def mlp(c):
            v_n = _layer_norm(_gelu(col(c, 5)), lng, lnb)
            if c == n_chunks - 1:
                for hd in range(N_HEADS):
                    vch_ref[0, 0, :, hd, :] = v_n[:, hd * HEAD:(hd + 1) * HEAD]
            v_n16 = v_n.astype(BF16)
            yield
            for hd in range(N_HEADS):
                hs = slice(hd * HEAD, (hd + 1) * HEAD)
                mixed_ref[c * ROWS:(c + 1) * ROWS, D_A + hd * HEAD:D_A + (hd + 1) * HEAD] = (
                    _mlp_head(hd, col(c, 4, hs), v_n16[:, hs], col(c, 6, hs), wc_ref, bias))
                yield
                yield

        def projection(rounds_per_block):
            for blk in range(n_blocks):
                cols = slice(blk * PROJ_BLOCK, (blk + 1) * PROJ_BLOCK)
                proj_next[:, cols] = jnp.dot(h_ref[...], win_ref[:, cols],
                                             preferred_element_type=F32)
                for _ in range(rounds_per_block):
                    yield

        items = [gen for c in range(n_chunks)
                 for gen in [head(c, hd) for hd in range(N_HEADS)] + [mlp(c)]]
        starts = [c * CHUNK_STAGGER + k * HEAD_STAGGER
                  for c in range(n_chunks) for k in list(range(N_HEADS)) + [0]]
        rounds = (n_chunks - 1) * CHUNK_STAGGER + (N_HEADS - 1) * HEAD_STAGGER + HEAD_ROUNDS
        _lockstep(items + [projection(max(1, rounds // n_blocks))], starts + [0])

        out = xprev_ref[0] + jnp.dot(mixed_ref[...], wout16_ref[...], preferred_element_type=F32)
        y_ref[0] = _rms_norm(out, fg_ref[...])

    parity = lax.rem(g, 2)

    @pl.when(parity == 0)
    def _():
        step(proj_a, proj_b)

    @pl.when(parity == 1)
    def _():
        step(proj_b, proj_a)


def _sample_kernel(x_ref, ng_ref, win_ref, lbl_ref, hg_ref, lng_ref, lnb_ref, ws_ref, bs_ref,
                   wout_ref, fg_ref, stin_ref, y_ref, stout_ref, vch_ref,
                   qin_ref, kot_ref, tot_ref, oint_ref, lvl_ref, wc_ref, bias_ref, wout16_ref):
    @pl.when(pl.program_id(0) == 0)
    def _():
        _store_constants(DEC_SEQ, ws_ref, bs_ref, wout_ref, lvl_ref, wc_ref, bias_ref, wout16_ref)

    x = x_ref[...]
    h = _rms_norm(x, ng_ref[...]).astype(BF16)
    proj = jnp.dot(h, win_ref[...], preferred_element_type=F32)

    def col(k, hs=slice(0, D_A)):
        return proj[:, k * D_A + hs.start:k * D_A + hs.stop]

    lb = _lower_bound(lbl_ref[...])
    bits, n_levels = _row_bits(), _n_levels(DEC_SEQ)
    vi16 = col(2).astype(BF16)

    def head(hd):
        hs = slice(hd * HEAD, (hd + 1) * HEAD)

        def gates(i):
            rows = slice(i * SUBLANES, (i + 1) * SUBLANES)
            return _hgrn_gates(proj[rows, hs], proj[rows, D_A + hd * HEAD:D_A + (hd + 1) * HEAD],
                               lb[:, hs])

        sc, q_in, k_out, total = yield from _hgrn_head(gates, lvl_ref, bits, n_levels)
        qin_ref[:, hs] = q_in
        kot_ref[hd] = k_out.T
        tot_ref[hd] = total.T
        return jnp.dot(sc, vi16[:, hs], preferred_element_type=F32)

    o_intra = _lockstep([head(hd) for hd in range(N_HEADS)])

    lane = lax.broadcasted_iota(jnp.int32, (HEAD, ROWS), 1)

    def seq_head(i, hd):
        hs = slice(hd * HEAD, (hd + 1) * HEAD)
        seq_rows = slice(i * DEC_SEQ, (i + 1) * DEC_SEQ)
        state = stin_ref[0, i, hd]
        qi = qin_ref[seq_rows, hs].astype(BF16)
        oint_ref[seq_rows, hs] = jnp.dot(qi, state.astype(BF16), preferred_element_type=F32)
        yield
        in_seq = (lane >> (DEC_SEQ.bit_length() - 1)) == i
        k_seq = jnp.where(in_seq, kot_ref[hd], 0.0).astype(BF16)
        decay = tot_ref[hd, :, i * DEC_SEQ:i * DEC_SEQ + 1]
        yield
        stout_ref[0, i, hd] = decay * state + jnp.dot(
            k_seq, vi16[:, hs], preferred_element_type=F32)

    for grp in range(SEQS_PER_TILE // SEQ_GROUP):
        _lockstep([seq_head(grp * SEQ_GROUP + s, hd)
                   for s in range(SEQ_GROUP) for hd in range(N_HEADS)])

    hg = hg_ref[...]
    bias = bias_ref[...]
    v_n = _layer_norm(_gelu(col(5)), lng_ref[...], lnb_ref[...])
    for hd in range(N_HEADS):
        vch_ref[0, :, :, hd, :] = v_n[:, hd * HEAD:(hd + 1) * HEAD].reshape(
            SEQS_PER_TILE, DEC_SEQ, HEAD)
    v_n16 = v_n.astype(BF16)
    mixed = []
    for hd in range(N_HEADS):
        hs = slice(hd * HEAD, (hd + 1) * HEAD)
        mixed.append(_hgrn_finish(o_intra[hd] + oint_ref[:, hs], col(3, hs), hg[:, hs]))
    for hd in range(N_HEADS):
        hs = slice(hd * HEAD, (hd + 1) * HEAD)
        mixed.append(_mlp_head(hd, col(4, hs), v_n16[:, hs], col(6, hs), wc_ref, bias))
    out = x + jnp.dot(jnp.concatenate(mixed, axis=-1), wout16_ref[...],
                      preferred_element_type=F32)
    y_ref[...] = _rms_norm(out, fg_ref[...])


def _const_spec(shape):
    return pl.BlockSpec(shape, lambda *_: (0,) * len(shape), pipeline_mode=pl.Buffered(1))


_WEIGHT_SPECS = [
    _const_spec((1, D_MODEL)),
    _const_spec((D_MODEL, D_IN)),
    _const_spec((2, D_A)),
    _const_spec((1, D_A)),
    _const_spec((1, D_B)),
    _const_spec((1, D_B)),
    _const_spec((N_HEADS, ROWS, ROWS)),
    _const_spec((N_HEADS, ROWS)),
    _const_spec((D_MODEL, D_MODEL)),
    _const_spec((1, D_MODEL)),
]

_CONSTANT_SCRATCH = [
    pltpu.VMEM((ROWS, ROWS), jnp.int32),
    pltpu.VMEM((N_HEADS, ROWS, ROWS), BF16),
    pltpu.VMEM((ROWS, N_HEADS), F32),
    pltpu.VMEM((D_MODEL, D_MODEL), BF16),
]


def kernel(x_prompt, x_sample, state_hgrn, norm_g, w_in, lb_logits, hgrn_norm_g, sgu_ln_g,
           sgu_ln_b, w_s, b_s, w_out, final_norm_g):
    depth = norm_g.shape[0]
    assert depth == 1 and lb_logits.shape == (2, D_A)
    batch, seq, _ = x_prompt.shape
    dec_batch, dec_seq, _ = x_sample.shape
    assert seq % PROMPT_TILE == 0 and dec_seq == DEC_SEQ and dec_batch % SEQS_PER_TILE == 0

    weights = (norm_g, w_in[0].astype(BF16), lb_logits, hgrn_norm_g, sgu_ln_g, sgu_ln_b,
               w_s[0], b_s[0], w_out[0], final_norm_g.reshape(1, D_MODEL))
    params = pltpu.CompilerParams(dimension_semantics=("arbitrary",),
                                  vmem_limit_bytes=VMEM_LIMIT_BYTES)

    tiles_per_seq = seq // PROMPT_TILE
    n_tiles = batch * tiles_per_seq

    def cur_tile(g):
        t = jnp.minimum(g, n_tiles - 1)
        return t // tiles_per_seq, t % tiles_per_seq

    def prev_tile(g):
        t = jnp.maximum(g - 1, 0)
        return t // tiles_per_seq, t % tiles_per_seq

    y_p, st_p, vch_p = pl.pallas_call(
        functools.partial(_prompt_kernel, tiles_per_seq=tiles_per_seq),
        grid=(n_tiles + 1,),
        in_specs=[pl.BlockSpec((1, PROMPT_TILE, D_MODEL), lambda g: (*cur_tile(g), 0)),
                  pl.BlockSpec((1, PROMPT_TILE, D_MODEL), lambda g: (*prev_tile(g), 0))]
        + _WEIGHT_SPECS,
        out_specs=[
            pl.BlockSpec((1, PROMPT_TILE, D_MODEL), lambda g: (*prev_tile(g), 0)),
            pl.BlockSpec((1, 1, N_HEADS, HEAD, HEAD), lambda g: (0, prev_tile(g)[0], 0, 0, 0)),
            pl.BlockSpec((1, 1, ROWS, N_HEADS, HEAD), lambda g: (0, prev_tile(g)[0], 0, 0, 0)),
        ],
        out_shape=[
            jax.ShapeDtypeStruct((batch, seq, D_MODEL), F32),
            jax.ShapeDtypeStruct((1, batch, N_HEADS, HEAD, HEAD), F32),
            jax.ShapeDtypeStruct((1, batch, ROWS, N_HEADS, HEAD), F32),
        ],
        scratch_shapes=[pltpu.VMEM((PROMPT_TILE, D_IN), F32),
                        pltpu.VMEM((PROMPT_TILE, D_IN), F32),
                        pltpu.VMEM((PROMPT_TILE, D_MODEL), BF16),
                        pltpu.VMEM((PROMPT_TILE, D_MODEL), BF16)] + _CONSTANT_SCRATCH,
        compiler_params=params,
        name="prompt_layer",
    )(x_prompt, x_prompt, *weights)

    n_rows = dec_batch * DEC_SEQ
    y_s, st_s, vch_s = pl.pallas_call(
        _sample_kernel,
        grid=(n_rows // ROWS,),
        in_specs=[pl.BlockSpec((ROWS, D_MODEL), lambda i: (i, 0))]
        + _WEIGHT_SPECS
        + [pl.BlockSpec((1, SEQS_PER_TILE, N_HEADS, HEAD, HEAD), lambda i: (0, i, 0, 0, 0))],
        out_specs=[
            pl.BlockSpec((ROWS, D_MODEL), lambda i: (i, 0)),
            pl.BlockSpec((1, SEQS_PER_TILE, N_HEADS, HEAD, HEAD), lambda i: (0, i, 0, 0, 0)),
            pl.BlockSpec((1, SEQS_PER_TILE, DEC_SEQ, N_HEADS, HEAD), lambda i: (0, i, 0, 0, 0)),
        ],
        out_shape=[
            jax.ShapeDtypeStruct((n_rows, D_MODEL), F32),
            jax.ShapeDtypeStruct((1, dec_batch, N_HEADS, HEAD, HEAD), F32),
            jax.ShapeDtypeStruct((1, dec_batch, DEC_SEQ, N_HEADS, HEAD), F32),
        ],
        scratch_shapes=[pltpu.VMEM((ROWS, D_A), F32),
                        pltpu.VMEM((N_HEADS, HEAD, ROWS), F32),
                        pltpu.VMEM((N_HEADS, HEAD, ROWS), F32),
                        pltpu.VMEM((ROWS, D_A), F32)] + _CONSTANT_SCRATCH,
        compiler_params=pltpu.CompilerParams(dimension_semantics=("arbitrary",),
                                             vmem_limit_bytes=VMEM_LIMIT_BYTES),
        name="decode_layer",
    )(x_sample.reshape(n_rows, D_MODEL), *weights, state_hgrn)

    return (y_p,
            y_s.reshape(dec_batch, DEC_SEQ, D_MODEL),
            st_p,
            st_s,
            vch_p,
            vch_s)
```

```python
import functools

import jax
import jax.numpy as jnp
from jax import lax
from jax.experimental import pallas as pl
from jax.experimental.pallas import tpu as pltpu

F32 = jnp.float32
BF16 = jnp.bfloat16

D_MODEL = 1024
D_A = 512
D_B = 512
HEAD = 128
N_HEADS = 4
D_IN = 4 * D_A + 3 * D_B
ROWS = 128
SUBLANES = 8
PACKED = 2 * SUBLANES
N_TILES = ROWS // SUBLANES
SUBLANE_LEVELS = 3
EPS = 1e-6
PROMPT_TILE = 512
PROJ_BLOCK = 256
HEAD_ROUNDS = 12
CHUNK_STAGGER = 4
HEAD_STAGGER = 1
DEC_SEQ = 8
SEQS_PER_TILE = ROWS // DEC_SEQ
SEQ_GROUP = 4
VMEM_LIMIT_BYTES = 56 * 1024 * 1024

_NT = (((1,), (1,)), ((), ()))


def _silu(x):
    return x * jax.nn.sigmoid(x)


def _gelu(x):
    c = -2.0 * 0.7978845608028654 * 1.4426950408889634
    return x / (1.0 + jnp.exp2(x * (c * 0.044715 * (x * x) + c)))


def _rms_norm(x, g):
    return x * lax.rsqrt(jnp.mean(x * x, axis=-1, keepdims=True) + EPS) * g


def _layer_norm(x, g, b):
    mu = jnp.mean(x, axis=-1, keepdims=True)
    xc = x - mu
    var = jnp.mean(xc * xc, axis=-1, keepdims=True)
    return xc * lax.rsqrt(var + EPS) * g + b


def _lower_bound(lbl):
    rows = [lbl[r:r + 1, :] for r in range(lbl.shape[0])]
    mx = functools.reduce(jnp.maximum, rows)
    es = [jnp.exp(r - mx) for r in rows]
    return es[0] / functools.reduce(lambda a, c: a + c, es)


def _n_levels(seg_len):
    n_levels = seg_len.bit_length() - 1
    assert seg_len == 1 << n_levels and n_levels >= SUBLANE_LEVELS
    return n_levels


def _split3(x):
    hi = x.astype(BF16)
    r1 = x - hi.astype(F32)
    mid = r1.astype(BF16)
    lo = (r1 - mid.astype(F32)).astype(BF16)
    return hi, mid, lo


def _store_constants(seg_len, ws_ref, bs_ref, wout_ref, lvl_ref, wc_ref, bias_ref, wout16_ref):
    t = lax.broadcasted_iota(jnp.int32, (ROWS, ROWS), 0)
    s = lax.broadcasted_iota(jnp.int32, (ROWS, ROWS), 1)
    n_levels = _n_levels(seg_len)
    x = t ^ s
    lvl = jnp.where(t == s, n_levels, -1)
    for j in range(n_levels):
        lvl = jnp.where(((x >> j) == 1) & (((t >> j) & 1) == 1), j, lvl)
    lvl_ref[...] = lvl
    causal = (t >= s) & ((t >> n_levels) == (s >> n_levels))
    pick = jnp.where(s == (t & (seg_len - 1)), 1.0, 0.0).astype(BF16)
    for hd in range(N_HEADS):
        w = ws_ref[hd].astype(BF16)
        if seg_len != ROWS:
            w = jnp.dot(pick, w, preferred_element_type=F32).astype(BF16)
            w = lax.dot_general(w, pick, _NT, preferred_element_type=F32).astype(BF16)
        wc_ref[hd] = jnp.where(causal, w, jnp.zeros_like(w))
    pad = jnp.zeros((PACKED - N_HEADS, ROWS), F32)
    bias = sum(lax.dot_general(pick, part, _NT, preferred_element_type=F32)
               for part in _split3(jnp.concatenate([bs_ref[...], pad], axis=0)))
    bias_ref[...] = bias[:, :N_HEADS]
    wout16_ref[...] = wout_ref[...].astype(BF16)


def _row_bits():
    rowi = lax.broadcasted_iota(jnp.int32, (SUBLANES, HEAD), 0)
    return [((rowi >> j) & 1) == 1 for j in range(SUBLANE_LEVELS)]


def _hgrn_head(gates, lvl_ref, bits, n_levels):
    one = jnp.ones((SUBLANES, HEAD), F32)
    z = [[] for _ in range(n_levels)]
    qp_t, ks_t, blk_t, diag_t = [], [], [], []
    for pair in range(N_TILES // 2):
        z_pair = [[] for _ in range(SUBLANE_LEVELS)]
        for i in (2 * pair, 2 * pair + 1):
            q, kk, f = gates(i)
            diag_t.append(jnp.sum(q * kk, axis=-1, keepdims=True))
            qp, ks, blk = q * f, kk, f
            for j in range(SUBLANE_LEVELS):
                m = 1 << j
                z_pair[j].append(jnp.where(bits[j], qp, ks))
                sib = pltpu.roll(blk, m, 0)
                if 2 * m != SUBLANES:
                    sib = jnp.where(bits[j], sib, pltpu.roll(blk, SUBLANES - m, 0))
                qp = qp * jnp.where(bits[j], sib, one)
                ks = ks * jnp.where(bits[j], one, sib)
                blk = blk * sib
            qp_t.append(qp)
            ks_t.append(ks)
            blk_t.append(blk)
        for j in range(SUBLANE_LEVELS):
            z[j].append(jnp.concatenate(z_pair[j], axis=0).astype(BF16))
    yield

    blk_levels, blk_b = [], blk_t
    for j in range(SUBLANE_LEVELS, n_levels):
        blk_levels.append(blk_b)
        blk_b = [blk_b[2 * n] * blk_b[2 * n + 1] for n in range(len(blk_b) // 2)]
    for pair in range(N_TILES // 2):
        z_pair = [[] for _ in range(SUBLANE_LEVELS, n_levels)]
        for i in (2 * pair, 2 * pair + 1):
            for j in range(SUBLANE_LEVELS, n_levels):
                block = i >> (j - SUBLANE_LEVELS)
                sib = blk_levels[j - SUBLANE_LEVELS][block ^ 1]
                if block & 1:
                    z_pair[j - SUBLANE_LEVELS].append(qp_t[i])
                    qp_t[i] = qp_t[i] * sib
                else:
                    z_pair[j - SUBLANE_LEVELS].append(ks_t[i])
                    ks_t[i] = ks_t[i] * sib
        for j in range(SUBLANE_LEVELS, n_levels):
            z[j].append(jnp.concatenate(z_pair[j - SUBLANE_LEVELS], axis=0).astype(BF16))
    q_in = jnp.concatenate(qp_t, axis=0)
    k_out = jnp.concatenate(ks_t, axis=0)
    per_tile = N_TILES // len(blk_b)
    total = jnp.concatenate([blk_b[i // per_tile] for i in range(N_TILES)], axis=0)
    yield

    p, p_row = [], []
    for j in range(n_levels):
        zj = jnp.concatenate(z[j], axis=0)
        span = 1 << max(j - SUBLANE_LEVELS, 0)
        if span * SUBLANES >= PACKED:
            upper = [i for i in range(N_TILES) if (i // span) & 1]
            lhs = jnp.concatenate([z[j][i // 2] for i in upper[::2]], axis=0)
        else:
            upper = list(range(N_TILES))
            lhs = zj
        p.append(lax.dot_general(lhs, zj, _NT, preferred_element_type=F32))
        p_row.append({i: n * SUBLANES for n, i in enumerate(upper)})
        yield
    sc_t = []
    for i in range(N_TILES):
        lvl_i = lvl_ref[i * SUBLANES:(i + 1) * SUBLANES, :]
        acc = jnp.where(lvl_i == n_levels, diag_t[i], 0.0)
        for j in range(n_levels):
            if j < SUBLANE_LEVELS or (i >> (j - SUBLANE_LEVELS)) & 1:
                r0 = p_row[j][i]
                acc = jnp.where(lvl_i == j, p[j][r0:r0 + SUBLANES, :], acc)
        sc_t.append(acc)
    sc = jnp.concatenate(sc_t, axis=0).astype(BF16)
    return sc, q_in, k_out, total


def _lockstep(generators, starts=None):
    starts = starts or [0] * len(generators)
    results = [None] * len(generators)
    waiting = sorted(range(len(generators)), key=lambda i: starts[i])
    active = []
    rnd = 0
    while waiting or active:
        while waiting and starts[waiting[0]] <= rnd:
            active.append(waiting.pop(0))
        still = []
        for idx in active:
            try:
                next(generators[idx])
                still.append(idx)
            except StopIteration as done:
                results[idx] = done.value
        active = still
        rnd += 1
    return results


def _hgrn_gates(pq, pf, lb):
    q = _silu(pq)
    forget = lb + (1.0 - lb) * jax.nn.sigmoid(pf)
    return q, 1.0 - forget, forget


def _hgrn_finish(o, pz, g):
    return (_rms_norm(o, g) * _silu(pz)).astype(BF16)


def _mlp_head(hd, u, v_n16, gate, wc_ref, bias):
    mix = jnp.dot(wc_ref[hd], v_n16, preferred_element_type=F32) + bias[:, hd:hd + 1]
    return (_gelu(u) * mix * _silu(gate)).astype(BF16)


def _prompt_kernel(x_ref, xprev_ref, ng_ref, win_ref, lbl_ref, hg_ref, lng_ref, lnb_ref, ws_ref,
                   bs_ref, wout_ref, fg_ref, y_ref, st_ref, vch_ref, proj_a, proj_b, h_ref,
                   mixed_ref, lvl_ref, wc_ref, bias_ref, wout16_ref, *, tiles_per_seq):
    g = pl.program_id(0)

    @pl.when(g == 0)
    def _():
        proj_b[...] = jnp.zeros_like(proj_b)
        _store_constants(ROWS, ws_ref, bs_ref, wout_ref, lvl_ref, wc_ref, bias_ref, wout16_ref)

    @pl.when(lax.rem(jnp.maximum(g - 1, 0), tiles_per_seq) == 0)
    def _():
        st_ref[...] = jnp.zeros_like(st_ref)

    def step(proj_next, proj_prev):
        h_ref[...] = _rms_norm(x_ref[0], ng_ref[...]).astype(BF16)
        lb = _lower_bound(lbl_ref[...])
        bits, n_levels = _row_bits(), _n_levels(ROWS)
        hg = hg_ref[...]
        lng = lng_ref[...]
        lnb = lnb_ref[...]
        bias = bias_ref[...]
        n_chunks = PROMPT_TILE // ROWS
        n_blocks = D_IN // PROJ_BLOCK

        def col(c, k, hs=slice(0, D_A)):
            return proj_prev[c * ROWS:(c + 1) * ROWS, k * D_A + hs.start:k * D_A + hs.stop]

        def head(c, hd):
            hs = slice(hd * HEAD, (hd + 1) * HEAD)

            def gates(i):
                rows = slice(c * ROWS + i * SUBLANES, c * ROWS + (i + 1) * SUBLANES)
                return _hgrn_gates(proj_prev[rows, hs],
                                   proj_prev[rows, D_A + hd * HEAD:D_A + (hd + 1) * HEAD],
                                   lb[:, hs])

            sc, q_in, k_out, total = yield from _hgrn_head(gates, lvl_ref, bits, n_levels)
            state = st_ref[0, 0, hd]
            vh = col(c, 2, hs)
            o = jnp.dot(jnp.concatenate([sc, q_in.astype(BF16)], axis=1),
                        jnp.concatenate([vh.astype(BF16), state.astype(BF16)], axis=0),
                        preferred_element_type=F32)
            yield
            st_ref[0, 0, hd] = total.T * state + jnp.dot(
                k_out.T.astype(BF16), vh.astype(BF16), preferred_element_type=F32)
            yield
            mixed_ref[c * ROWS:(c + 1) * ROWS, hs] = _hgrn_finish(o, col(c, 3, hs), hg[:, hs])

        def mlp(c):
            v_n = _layer_norm(_gelu(col(c, 5)), lng, lnb)
            if c == n_chunks - 1:
                for hd in range(N_HEADS):
                    vch_ref[0, 0, :, hd, :] = v_n[:, hd * HEAD:(hd + 1) * HEAD]
            v_n16 = v_n.astype(BF16)
            yield
            for hd in range(N_HEADS):
                hs = slice(hd * HEAD, (hd + 1) * HEAD)
                mixed_ref[c * ROWS:(c + 1) * ROWS, D_A + hd * HEAD:D_A + (hd + 1) * HEAD] = (
                    _mlp_head(hd, col(c, 4, hs), v_n16[:, hs], col(c, 6, hs), wc_ref, bias))
                yield
                yield

        def projection(rounds_per_block):
            for blk in range(n_blocks):
                cols = slice(blk * PROJ_BLOCK, (blk + 1) * PROJ_BLOCK)
                proj_next[:, cols] = jnp.dot(h_ref[...], win_ref[:, cols],
                                             preferred_element_type=F32)
                for _ in range(rounds_per_block):
                    yield

        items = [gen for c in range(n_chunks)
                 for gen in [head(c, hd) for hd in range(N_HEADS)] + [mlp(c)]]
        starts = [c * CHUNK_STAGGER + k * HEAD_STAGGER
                  for c in range(n_chunks) for k in list(range(N_HEADS)) + [0]]
        rounds = (n_chunks - 1) * CHUNK_STAGGER + (N_HEADS - 1) * HEAD_STAGGER + HEAD_ROUNDS
        _lockstep(items + [projection(max(1, rounds // n_blocks))], starts + [0])

        out = xprev_ref[0] + jnp.dot(mixed_ref[...], wout16_ref[...], preferred_element_type=F32)
        y_ref[0] = _rms_norm(out, fg_ref[...])

    parity = lax.rem(g, 2)

    @pl.when(parity == 0)
    def _():
        step(proj_a, proj_b)

    @pl.when(parity == 1)
    def _():
        step(proj_b, proj_a)


def _sample_kernel(x_ref, ng_ref, win_ref, lbl_ref, hg_ref, lng_ref, lnb_ref, ws_ref, bs_ref,
                   wout_ref, fg_ref, stin_ref, y_ref, stout_ref, vch_ref,
                   qin_ref, kot_ref, tot_ref, oint_ref, lvl_ref, wc_ref, bias_ref, wout16_ref):
    @pl.when(pl.program_id(0) == 0)
    def _():
        _store_constants(DEC_SEQ, ws_ref, bs_ref, wout_ref, lvl_ref, wc_ref, bias_ref, wout16_ref)

    x = x_ref[...]
    h = _rms_norm(x, ng_ref[...]).astype(BF16)
    proj = jnp.dot(h, win_ref[...], preferred_element_type=F32)

    def col(k, hs=slice(0, D_A)):
        return proj[:, k * D_A + hs.start:k * D_A + hs.stop]

    lb = _lower_bound(lbl_ref[...])
    bits, n_levels = _row_bits(), _n_levels(DEC_SEQ)
    vi16 = col(2).astype(BF16)

    def head(hd):
        hs = slice(hd * HEAD, (hd + 1) * HEAD)

        def gates(i):
            rows = slice(i * SUBLANES, (i + 1) * SUBLANES)
            return _hgrn_gates(proj[rows, hs], proj[rows, D_A + hd * HEAD:D_A + (hd + 1) * HEAD],
                               lb[:, hs])

        sc, q_in, k_out, total = yield from _hgrn_head(gates, lvl_ref, bits, n_levels)
        qin_ref[:, hs] = q_in
        kot_ref[hd] = k_out.T
        tot_ref[hd] = total.T
        return jnp.dot(sc, vi16[:, hs], preferred_element_type=F32)

    o_intra = _lockstep([head(hd) for hd in range(N_HEADS)])

    lane = lax.broadcasted_iota(jnp.int32, (HEAD, ROWS), 1)

    def seq_head(i, hd):
        hs = slice(hd * HEAD, (hd + 1) * HEAD)
        seq_rows = slice(i * DEC_SEQ, (i + 1) * DEC_SEQ)
        state = stin_ref[0, i, hd]
        qi = qin_ref[seq_rows, hs].astype(BF16)
        oint_ref[seq_rows, hs] = jnp.dot(qi, state.astype(BF16), preferred_element_type=F32)
        yield
        in_seq = (lane >> (DEC_SEQ.bit_length() - 1)) == i
        k_seq = jnp.where(in_seq, kot_ref[hd], 0.0).astype(BF16)
        decay = tot_ref[hd, :, i * DEC_SEQ:i * DEC_SEQ + 1]
        yield
        stout_ref[0, i, hd] = decay * state + jnp.dot(
            k_seq, vi16[:, hs], preferred_element_type=F32)

    for grp in range(SEQS_PER_TILE // SEQ_GROUP):
        _lockstep([seq_head(grp * SEQ_GROUP + s, hd)
                   for s in range(SEQ_GROUP) for hd in range(N_HEADS)])

    hg = hg_ref[...]
    bias = bias_ref[...]
    v_n = _layer_norm(_gelu(col(5)), lng_ref[...], lnb_ref[...])
    for hd in range(N_HEADS):
        vch_ref[0, :, :, hd, :] = v_n[:, hd * HEAD:(hd + 1) * HEAD].reshape(
            SEQS_PER_TILE, DEC_SEQ, HEAD)
    v_n16 = v_n.astype(BF16)
    mixed = []
    for hd in range(N_HEADS):
        hs = slice(hd * HEAD, (hd + 1) * HEAD)
        mixed.append(_hgrn_finish(o_intra[hd] + oint_ref[:, hs], col(3, hs), hg[:, hs]))
    for hd in range(N_HEADS):
        hs = slice(hd * HEAD, (hd + 1) * HEAD)
        mixed.append(_mlp_head(hd, col(4, hs), v_n16[:, hs], col(6, hs), wc_ref, bias))
    out = x + jnp.dot(jnp.concatenate(mixed, axis=-1), wout16_ref[...],
                      preferred_element_type=F32)
    y_ref[...] = _rms_norm(out, fg_ref[...])


def _const_spec(shape):
    return pl.BlockSpec(shape, lambda *_: (0,) * len(shape), pipeline_mode=pl.Buffered(1))


_WEIGHT_SPECS = [
    _const_spec((1, D_MODEL)),
    _const_spec((D_MODEL, D_IN)),
    _const_spec((2, D_A)),
    _const_spec((1, D_A)),
    _const_spec((1, D_B)),
    _const_spec((1, D_B)),
    _const_spec((N_HEADS, ROWS, ROWS)),
    _const_spec((N_HEADS, ROWS)),
    _const_spec((D_MODEL, D_MODEL)),
    _const_spec((1, D_MODEL)),
]

_CONSTANT_SCRATCH = [
    pltpu.VMEM((ROWS, ROWS), jnp.int32),
    pltpu.VMEM((N_HEADS, ROWS, ROWS), BF16),
    pltpu.VMEM((ROWS, N_HEADS), F32),
    pltpu.VMEM((D_MODEL, D_MODEL), BF16),
]


def kernel(x_prompt, x_sample, state_hgrn, norm_g, w_in, lb_logits, hgrn_norm_g, sgu_ln_g,
           sgu_ln_b, w_s, b_s, w_out, final_norm_g):
    depth = norm_g.shape[0]
    assert depth == 1 and lb_logits.shape == (2, D_A)
    batch, seq, _ = x_prompt.shape
    dec_batch, dec_seq, _ = x_sample.shape
    assert seq % PROMPT_TILE == 0 and dec_seq == DEC_SEQ and dec_batch % SEQS_PER_TILE == 0

    weights = (norm_g, w_in[0].astype(BF16), lb_logits, hgrn_norm_g, sgu_ln_g, sgu_ln_b,
               w_s[0], b_s[0], w_out[0], final_norm_g.reshape(1, D_MODEL))
    params = pltpu.CompilerParams(dimension_semantics=("arbitrary",),
                                  vmem_limit_bytes=VMEM_LIMIT_BYTES)

    tiles_per_seq = seq // PROMPT_TILE
    n_tiles = batch * tiles_per_seq

    def cur_tile(g):
        t = jnp.minimum(g, n_tiles - 1)
        return t // tiles_per_seq, t % tiles_per_seq

    def prev_tile(g):
        t = jnp.maximum(g - 1, 0)
        return t // tiles_per_seq, t % tiles_per_seq

    y_p, st_p, vch_p = pl.pallas_call(
        functools.partial(_prompt_kernel, tiles_per_seq=tiles_per_seq),
        grid=(n_tiles + 1,),
        in_specs=[pl.BlockSpec((1, PROMPT_TILE, D_MODEL), lambda g: (*cur_tile(g), 0)),
                  pl.BlockSpec((1, PROMPT_TILE, D_MODEL), lambda g: (*prev_tile(g), 0))]
        + _WEIGHT_SPECS,
        out_specs=[
            pl.BlockSpec((1, PROMPT_TILE, D_MODEL), lambda g: (*prev_tile(g), 0)),
            pl.BlockSpec((1, 1, N_HEADS, HEAD, HEAD), lambda g: (0, prev_tile(g)[0], 0, 0, 0)),
            pl.BlockSpec((1, 1, ROWS, N_HEADS, HEAD), lambda g: (0, prev_tile(g)[0], 0, 0, 0)),
        ],
        out_shape=[
            jax.ShapeDtypeStruct((batch, seq, D_MODEL), F32),
            jax.ShapeDtypeStruct((1, batch, N_HEADS, HEAD, HEAD), F32),
            jax.ShapeDtypeStruct((1, batch, ROWS, N_HEADS, HEAD), F32),
        ],
        scratch_shapes=[pltpu.VMEM((PROMPT_TILE, D_IN), F32),
                        pltpu.VMEM((PROMPT_TILE, D_IN), F32),
                        pltpu.VMEM((PROMPT_TILE, D_MODEL), BF16),
                        pltpu.VMEM((PROMPT_TILE, D_MODEL), BF16)] + _CONSTANT_SCRATCH,
        compiler_params=params,
        name="prompt_layer",
    )(x_prompt, x_prompt, *weights)

    n_rows = dec_batch * DEC_SEQ
    y_s, st_s, vch_s = pl.pallas_call(
        _sample_kernel,
        grid=(n_rows // ROWS,),
        in_specs=[pl.BlockSpec((ROWS, D_MODEL), lambda i: (i, 0))]
        + _WEIGHT_SPECS
        + [pl.BlockSpec((1, SEQS_PER_TILE, N_HEADS, HEAD, HEAD), lambda i: (0, i, 0, 0, 0))],
        out_specs=[
            pl.BlockSpec((ROWS, D_MODEL), lambda i: (i, 0)),
            pl.BlockSpec((1, SEQS_PER_TILE, N_HEADS, HEAD, HEAD), lambda i: (0, i, 0, 0, 0)),
            pl.BlockSpec((1, SEQS_PER_TILE, DEC_SEQ, N_HEADS, HEAD), lambda i: (0, i, 0, 0, 0)),
        ],
        out_shape=[
            jax.ShapeDtypeStruct((n_rows, D_MODEL), F32),
            jax.ShapeDtypeStruct((1, dec_batch, N_HEADS, HEAD, HEAD), F32),
            jax.ShapeDtypeStruct((1, dec_batch, DEC_SEQ, N_HEADS, HEAD), F32),
        ],
        scratch_shapes=[pltpu.VMEM((ROWS, D_A), F32),
                        pltpu.VMEM((N_HEADS, HEAD, ROWS), F32),
                        pltpu.VMEM((N_HEADS, HEAD, ROWS), F32),
                        pltpu.VMEM((ROWS, D_A), F32)] + _CONSTANT_SCRATCH,
        compiler_params=pltpu.CompilerParams(dimension_semantics=("arbitrary",),
                                             vmem_limit_bytes=VMEM_LIMIT_BYTES),
        name="decode_layer",
    )(x_sample.reshape(n_rows, D_MODEL), *weights, state_hgrn)

    return (y_p,
            y_s.reshape(dec_batch, DEC_SEQ, D_MODEL),
            st_p,
            st_s,
            vch_p,
            vch_s)
```

```python
import functools

import jax
import jax.numpy as jnp
from jax import lax
from jax.experimental import pallas as pl
from jax.experimental.pallas import tpu as pltpu

F32 = jnp.float32
BF16 = jnp.bfloat16

D_MODEL = 1024
D_A = 512
D_B = 512
HEAD = 128
N_HEADS = 4
D_IN = 4 * D_A + 3 * D_B
ROWS = 128
SUBLANES = 8
PACKED = 2 * SUBLANES
N_TILES = ROWS // SUBLANES
SUBLANE_LEVELS = 3
EPS = 1e-6
PROMPT_TILE = 512
PROJ_BLOCK = 512
OUT_BLOCK = 256
NORM_SLAB = 64
HEAD_ROUNDS = 12
CHUNK_STAGGER = 4
HEAD_STAGGER = 1
DEC_SEQ = 8
SEQS_PER_TILE = ROWS // DEC_SEQ
SEQ_GROUP = 4
VMEM_LIMIT_BYTES = 58 * 1024 * 1024

_NT = (((1,), (1,)), ((), ()))


def _silu(x):
    return x * jax.nn.sigmoid(x)


def _gelu(x):
    c = -2.0 * 0.7978845608028654 * 1.4426950408889634
    return x / (1.0 + jnp.exp2(x * (c * 0.044715 * (x * x) + c)))


def _rms_norm(x, g):
    return x * lax.rsqrt(jnp.mean(x * x, axis=-1, keepdims=True) + EPS) * g


def _layer_norm(x, g, b):
    mu = jnp.mean(x, axis=-1, keepdims=True)
    xc = x - mu
    var = jnp.mean(xc * xc, axis=-1, keepdims=True)
    return xc * lax.rsqrt(var + EPS) * g + b


def _lower_bound(lbl):
    rows = [lbl[r:r + 1, :] for r in range(lbl.shape[0])]
    mx = functools.reduce(jnp.maximum, rows)
    es = [jnp.exp(r - mx) for r in rows]
    return es[0] / functools.reduce(lambda a, c: a + c, es)


def _n_levels(seg_len):
    n_levels = seg_len.bit_length() - 1
    assert seg_len == 1 << n_levels and n_levels >= SUBLANE_LEVELS
    return n_levels


def _split3(x):
    hi = x.astype(BF16)
    r1 = x - hi.astype(F32)
    mid = r1.astype(BF16)
    lo = (r1 - mid.astype(F32)).astype(BF16)
    return hi, mid, lo


def _store_constants(seg_len, ws_ref, bs_ref, wout_ref, lvl_ref, wc_ref, bias_ref, wout16_ref):
    t = lax.broadcasted_iota(jnp.int32, (ROWS, ROWS), 0)
    s = lax.broadcasted_iota(jnp.int32, (ROWS, ROWS), 1)
    n_levels = _n_levels(seg_len)
    x = t ^ s
    lvl = jnp.where(t == s, n_levels, -1)
    for j in range(n_levels):
        lvl = jnp.where(((x >> j) == 1) & (((t >> j) & 1) == 1), j, lvl)
    lvl_ref[...] = lvl
    causal = (t >= s) & ((t >> n_levels) == (s >> n_levels))
    pick = jnp.where(s == (t & (seg_len - 1)), 1.0, 0.0).astype(BF16)
    for hd in range(N_HEADS):
        w = ws_ref[hd].astype(BF16)
        if seg_len != ROWS:
            w = jnp.dot(pick, w, preferred_element_type=F32).astype(BF16)
            w = lax.dot_general(w, pick, _NT, preferred_element_type=F32).astype(BF16)
        wc_ref[hd] = jnp.where(causal, w, jnp.zeros_like(w))
    pad = jnp.zeros((PACKED - N_HEADS, ROWS), F32)
    bias = sum(lax.dot_general(pick, part, _NT, preferred_element_type=F32)
               for part in _split3(jnp.concatenate([bs_ref[...], pad], axis=0)))
    bias_ref[...] = bias[:, :N_HEADS]
    wout16_ref[...] = wout_ref[...].astype(BF16)


def _row_bits():
    rowi = lax.broadcasted_iota(jnp.int32, (SUBLANES, HEAD), 0)
    return [((rowi >> j) & 1) == 1 for j in range(SUBLANE_LEVELS)]


def _hgrn_head(gates, lvl_ref, bits, n_levels):
    one = jnp.ones((SUBLANES, HEAD), F32)
    z = [[] for _ in range(n_levels)]
    qp_t, ks_t, blk_t, diag_t = [], [], [], []
    for pair in range(N_TILES // 2):
        z_pair = [[] for _ in range(SUBLANE_LEVELS)]
        for i in (2 * pair, 2 * pair + 1):
            q, kk, f = gates(i)
            diag_t.append(jnp.sum(q * kk, axis=-1, keepdims=True))
            qp, ks, blk = q * f, kk, f
            for j in range(SUBLANE_LEVELS):
                m = 1 << j
                z_pair[j].append(jnp.where(bits[j], qp, ks))
                sib = pltpu.roll(blk, m, 0)
                if 2 * m != SUBLANES:
                    sib = jnp.where(bits[j], sib, pltpu.roll(blk, SUBLANES - m, 0))
                qp = qp * jnp.where(bits[j], sib, one)
                ks = ks * jnp.where(bits[j], one, sib)
                blk = blk * sib
            qp_t.append(qp)
            ks_t.append(ks)
            blk_t.append(blk)
        for j in range(SUBLANE_LEVELS):
            z[j].append(jnp.concatenate(z_pair[j], axis=0).astype(BF16))
    yield

    blk_levels, blk_b = [], blk_t
    for j in range(SUBLANE_LEVELS, n_levels):
        blk_levels.append(blk_b)
        blk_b = [blk_b[2 * n] * blk_b[2 * n + 1] for n in range(len(blk_b) // 2)]
    for pair in range(N_TILES // 2):
        z_pair = [[] for _ in range(SUBLANE_LEVELS, n_levels)]
        for i in (2 * pair, 2 * pair + 1):
            for j in range(SUBLANE_LEVELS, n_levels):
                block = i >> (j - SUBLANE_LEVELS)
                sib = blk_levels[j - SUBLANE_LEVELS][block ^ 1]
                if block & 1:
                    z_pair[j - SUBLANE_LEVELS].append(qp_t[i])
                    qp_t[i] = qp_t[i] * sib
                else:
                    z_pair[j - SUBLANE_LEVELS].append(ks_t[i])
                    ks_t[i] = ks_t[i] * sib
        for j in range(SUBLANE_LEVELS, n_levels):
            z[j].append(jnp.concatenate(z_pair[j - SUBLANE_LEVELS], axis=0).astype(BF16))
    q_in = jnp.concatenate(qp_t, axis=0)
    k_out = jnp.concatenate(ks_t, axis=0)
    per_tile = N_TILES // len(blk_b)
    total = jnp.concatenate([blk_b[i // per_tile] for i in range(N_TILES)], axis=0)
    yield

    p, p_row = [], []
    for j in range(n_levels):
        zj = jnp.concatenate(z[j], axis=0)
        span = 1 << max(j - SUBLANE_LEVELS, 0)
        if span * SUBLANES >= PACKED:
            upper = [i for i in range(N_TILES) if (i // span) & 1]
            lhs = jnp.concatenate([z[j][i // 2] for i in upper[::2]], axis=0)
        else:
            upper = list(range(N_TILES))
            lhs = zj
        p.append(lax.dot_general(lhs, zj, _NT, preferred_element_type=F32))
        p_row.append({i: n * SUBLANES for n, i in enumerate(upper)})
        yield
    sc_t = []
    for i in range(N_TILES):
        lvl_i = lvl_ref[i * SUBLANES:(i + 1) * SUBLANES, :]
        acc = jnp.where(lvl_i == n_levels, diag_t[i], 0.0)
        for j in range(n_levels):
            if j < SUBLANE_LEVELS or (i >> (j - SUBLANE_LEVELS)) & 1:
                r0 = p_row[j][i]
                acc = jnp.where(lvl_i == j, p[j][r0:r0 + SUBLANES, :], acc)
        sc_t.append(acc)
    sc = jnp.concatenate(sc_t, axis=0).astype(BF16)
    return sc, q_in, k_out, total


def _lockstep(generators, starts=None):
    starts = starts or [0] * len(generators)
    results = [None] * len(generators)
    waiting = sorted(range(len(generators)), key=lambda i: starts[i])
    active = []
    rnd = 0
    while waiting or active:
        while waiting and starts[waiting[0]] <= rnd:
            active.append(waiting.pop(0))
        still = []
        for idx in active:
            try:
                next(generators[idx])
                still.append(idx)
            except StopIteration as done:
                results[idx] = done.value
        active = still
        rnd += 1
    return results


def _hgrn_gates(pq, pf, lb):
    q = _silu(pq)
    forget = lb + (1.0 - lb) * jax.nn.sigmoid(pf)
    return q, 1.0 - forget, forget


def _hgrn_finish(o, pz, g):
    return (_rms_norm(o, g) * _silu(pz)).astype(BF16)


def _mlp_head(hd, u, v_n16, gate, wc_ref, bias):
    mix = jnp.dot(wc_ref[hd], v_n16, preferred_element_type=F32) + bias[:, hd:hd + 1]
    return (_gelu(u) * mix * _silu(gate)).astype(BF16)


def _prompt_kernel(xnext_ref, xold_ref, ng_ref, win_ref, lbl_ref, hg_ref, lng_ref, lnb_ref, ws_ref,
                   bs_ref, wout_ref, fg_ref, y_ref, st_ref, vch_ref, proj_a, proj_b, h_a, h_b,
                   mixed_a, mixed_b, lvl_ref, wc_ref, bias_ref, wout16_ref, *, tiles_per_seq):
    g = pl.program_id(0)

    @pl.when(g == 0)
    def _():
        h_a[...] = _rms_norm(xold_ref[0], ng_ref[...]).astype(BF16)
        proj_b[...] = jnp.zeros_like(proj_b)
        mixed_a[...] = jnp.zeros_like(mixed_a)
        _store_constants(ROWS, ws_ref, bs_ref, wout_ref, lvl_ref, wc_ref, bias_ref, wout16_ref)

    @pl.when(lax.rem(jnp.maximum(g - 1, 0), tiles_per_seq) == 0)
    def _():
        st_ref[...] = jnp.zeros_like(st_ref)

    def step(h_cur, h_nxt, proj_next, proj_prev, mixed_new, mixed_old):
        lb = _lower_bound(lbl_ref[...])
        bits, n_levels = _row_bits(), _n_levels(ROWS)
        ng = ng_ref[...]
        hg = hg_ref[...]
        lng = lng_ref[...]
        lnb = lnb_ref[...]
        bias = bias_ref[...]
        fg = fg_ref[...]
        n_chunks = PROMPT_TILE // ROWS

        def col(c, k, hs=slice(0, D_A)):
            return proj_prev[c * ROWS:(c + 1) * ROWS, k * D_A + hs.start:k * D_A + hs.stop]

        def head(c, hd):
            hs = slice(hd * HEAD, (hd + 1) * HEAD)

            def gates(i):
                rows = slice(c * ROWS + i * SUBLANES, c * ROWS + (i + 1) * SUBLANES)
                return _hgrn_gates(proj_prev[rows, hs],
                                   proj_prev[rows, D_A + hd * HEAD:D_A + (hd + 1) * HEAD],
                                   lb[:, hs])

            sc, q_in, k_out, total = yield from _hgrn_head(gates, lvl_ref, bits, n_levels)
            state = st_ref[0, 0, hd]
            vh = col(c, 2, hs)
            o = jnp.dot(jnp.concatenate([sc, q_in.astype(BF16)], axis=1),
                        jnp.concatenate([vh.astype(BF16), state.astype(BF16)], axis=0),
                        preferred_element_type=F32)
            yield
            st_ref[0, 0, hd] = total.T * state + jnp.dot(
                k_out.T.astype(BF16), vh.astype(BF16), preferred_element_type=F32)
            yield
            mixed_new[c * ROWS:(c + 1) * ROWS, hs] = _hgrn_finish(o, col(c, 3, hs), hg[:, hs])

        def mlp(c):
            v_n = _layer_norm(_gelu(col(c, 5)), lng, lnb)
            if c == n_chunks - 1:
                for hd in range(N_HEADS):
                    vch_ref[0, 0, :, hd, :] = v_n[:, hd * HEAD:(hd + 1) * HEAD]
            v_n16 = v_n.astype(BF16)
            yield
            for hd in range(N_HEADS):
                hs = slice(hd * HEAD, (hd + 1) * HEAD)
                mixed_new[c * ROWS:(c + 1) * ROWS, D_A + hd * HEAD:D_A + (hd + 1) * HEAD] = (
                    _mlp_head(hd, col(c, 4, hs), v_n16[:, hs], col(c, 6, hs), wc_ref, bias))
                yield
                yield

        starts = [c * CHUNK_STAGGER + k * HEAD_STAGGER
                  for c in range(n_chunks) for k in list(range(N_HEADS)) + [0]]
        rounds = (n_chunks - 1) * CHUNK_STAGGER + (N_HEADS - 1) * HEAD_STAGGER + HEAD_ROUNDS

        def spread(pieces):
            gap = max(1, rounds // len(pieces))
            for piece in pieces:
                piece()
                for _ in range(gap):
                    yield

        def in_block(blk):
            cols = slice(blk * PROJ_BLOCK, (blk + 1) * PROJ_BLOCK)
            proj_next[:, cols] = jnp.dot(h_cur[...], win_ref[:, cols], preferred_element_type=F32)

        def out_block(blk):
            cols = slice(blk * OUT_BLOCK, (blk + 1) * OUT_BLOCK)
            y_ref[0, :, cols] = xold_ref[0, :, cols] + jnp.dot(
                mixed_old[...], wout16_ref[:, cols], preferred_element_type=F32)

        def out_norm(slab):
            rows = slice(slab * NORM_SLAB, (slab + 1) * NORM_SLAB)
            y_ref[0, rows, :] = _rms_norm(y_ref[0, rows, :], fg)

        def next_norm(slab):
            rows = slice(slab * NORM_SLAB, (slab + 1) * NORM_SLAB)
            h_nxt[rows, :] = _rms_norm(xnext_ref[0, rows, :], ng).astype(BF16)

        n_slabs = PROMPT_TILE // NORM_SLAB
        in_proj = [functools.partial(in_block, b) for b in range(D_IN // PROJ_BLOCK)]
        out_proj = ([functools.partial(out_block, b) for b in range(D_MODEL // OUT_BLOCK)]
                    + [functools.partial(out_norm, s) for s in range(n_slabs)])
        next_h = [functools.partial(next_norm, s) for s in range(n_slabs)]
        items = [gen for c in range(n_chunks)
                 for gen in [head(c, hd) for hd in range(N_HEADS)] + [mlp(c)]]
        _lockstep(items + [spread(in_proj), spread(out_proj), spread(next_h)], starts + [0, 0, 0])

    parity = lax.rem(g, 2)

    @pl.when(parity == 0)
    def _():
        step(h_a, h_b, proj_a, proj_b, mixed_b, mixed_a)

    @pl.when(parity == 1)
    def _():
        step(h_b, h_a, proj_b, proj_a, mixed_a, mixed_b)


def _sample_kernel(x_ref, ng_ref, win_ref, lbl_ref, hg_ref, lng_ref, lnb_ref, ws_ref, bs_ref,
                   wout_ref, fg_ref, stin_ref, y_ref, stout_ref, vch_ref,
                   qin_ref, kot_ref, tot_ref, oint_ref, lvl_ref, wc_ref, bias_ref, wout16_ref):
    @pl.when(pl.program_id(0) == 0)
    def _():
        _store_constants(DEC_SEQ, ws_ref, bs_ref, wout_ref, lvl_ref, wc_ref, bias_ref, wout16_ref)

    x = x_ref[...]
    h = _rms_norm(x, ng_ref[...]).astype(BF16)
    proj = jnp.dot(h, win_ref[...], preferred_element_type=F32)

    def col(k, hs=slice(0, D_A)):
        return proj[:, k * D_A + hs.start:k * D_A + hs.stop]

    lb = _lower_bound(lbl_ref[...])
    bits, n_levels = _row_bits(), _n_levels(DEC_SEQ)
    vi16 = col(2).astype(BF16)

    def head(hd):
        hs = slice(hd * HEAD, (hd + 1) * HEAD)

        def gates(i):
            rows = slice(i * SUBLANES, (i + 1) * SUBLANES)
            return _hgrn_gates(proj[rows, hs], proj[rows, D_A + hd * HEAD:D_A + (hd + 1) * HEAD],
                               lb[:, hs])

        sc, q_in, k_out, total = yield from _hgrn_head(gates, lvl_ref, bits, n_levels)
        qin_ref[:, hs] = q_in
        kot_ref[hd] = k_out.T
        tot_ref[hd] = total.T
        return jnp.dot(sc, vi16[:, hs], preferred_element_type=F32)

    o_intra = _lockstep([head(hd) for hd in range(N_HEADS)])

    lane = lax.broadcasted_iota(jnp.int32, (HEAD, ROWS), 1)

    def seq_head(i, hd):
        hs = slice(hd * HEAD, (hd + 1) * HEAD)
        seq_rows = slice(i * DEC_SEQ, (i + 1) * DEC_SEQ)
        state = stin_ref[0, i, hd]
        qi = qin_ref[seq_rows, hs].astype(BF16)
        oint_ref[seq_rows, hs] = jnp.dot(qi, state.astype(BF16), preferred_element_type=F32)
        yield
        in_seq = (lane >> (DEC_SEQ.bit_length() - 1)) == i
        k_seq = jnp.where(in_seq, kot_ref[hd], 0.0).astype(BF16)
        decay = tot_ref[hd, :, i * DEC_SEQ:i * DEC_SEQ + 1]
        yield
        stout_ref[0, i, hd] = decay * state + jnp.dot(
            k_seq, vi16[:, hs], preferred_element_type=F32)

    for grp in range(SEQS_PER_TILE // SEQ_GROUP):
        _lockstep([seq_head(grp * SEQ_GROUP + s, hd)
                   for s in range(SEQ_GROUP) for hd in range(N_HEADS)])

    hg = hg_ref[...]
    bias = bias_ref[...]
    v_n = _layer_norm(_gelu(col(5)), lng_ref[...], lnb_ref[...])
    for hd in range(N_HEADS):
        vch_ref[0, :, :, hd, :] = v_n[:, hd * HEAD:(hd + 1) * HEAD].reshape(
            SEQS_PER_TILE, DEC_SEQ, HEAD)
    v_n16 = v_n.astype(BF16)
    mixed = []
    for hd in range(N_HEADS):
        hs = slice(hd * HEAD, (hd + 1) * HEAD)
        mixed.append(_hgrn_finish(o_intra[hd] + oint_ref[:, hs], col(3, hs), hg[:, hs]))
    for hd in range(N_HEADS):
        hs = slice(hd * HEAD, (hd + 1) * HEAD)
        mixed.append(_mlp_head(hd, col(4, hs), v_n16[:, hs], col(6, hs), wc_ref, bias))
    out = x + jnp.dot(jnp.concatenate(mixed, axis=-1), wout16_ref[...],
                      preferred_element_type=F32)
    y_ref[...] = _rms_norm(out, fg_ref[...])


def _const_spec(shape):
    return pl.BlockSpec(shape, lambda *_: (0,) * len(shape), pipeline_mode=pl.Buffered(1))


_WEIGHT_SPECS = [
    _const_spec((1, D_MODEL)),
    _const_spec((D_MODEL, D_IN)),
    _const_spec((2, D_A)),
    _const_spec((1, D_A)),
    _const_spec((1, D_B)),
    _const_spec((1, D_B)),
    _const_spec((N_HEADS, ROWS, ROWS)),
    _const_spec((N_HEADS, ROWS)),
    _const_spec((D_MODEL, D_MODEL)),
    _const_spec((1, D_MODEL)),
]

_CONSTANT_SCRATCH = [
    pltpu.VMEM((ROWS, ROWS), jnp.int32),
    pltpu.VMEM((N_HEADS, ROWS, ROWS), BF16),
    pltpu.VMEM((ROWS, N_HEADS), F32),
    pltpu.VMEM((D_MODEL, D_MODEL), BF16),
]


def kernel(x_prompt, x_sample, state_hgrn, norm_g, w_in, lb_logits, hgrn_norm_g, sgu_ln_g,
           sgu_ln_b, w_s, b_s, w_out, final_norm_g):
    depth = norm_g.shape[0]
    assert depth == 1 and lb_logits.shape == (2, D_A)
    batch, seq, _ = x_prompt.shape
    dec_batch, dec_seq, _ = x_sample.shape
    assert seq % PROMPT_TILE == 0 and dec_seq == DEC_SEQ and dec_batch % SEQS_PER_TILE == 0

    weights = (norm_g, w_in[0].astype(BF16), lb_logits, hgrn_norm_g, sgu_ln_g, sgu_ln_b,
               w_s[0], b_s[0], w_out[0], final_norm_g.reshape(1, D_MODEL))
    params = pltpu.CompilerParams(dimension_semantics=("arbitrary",),
                                  vmem_limit_bytes=VMEM_LIMIT_BYTES)

    tiles_per_seq = seq // PROMPT_TILE
    n_tiles = batch * tiles_per_seq

    def tile(t):
        t = jnp.clip(t, 0, n_tiles - 1)
        return t // tiles_per_seq, t % tiles_per_seq

    def mixed_seq(g):
        return jnp.maximum(g - 1, 0) // tiles_per_seq

    y_p, st_p, vch_p = pl.pallas_call(
        functools.partial(_prompt_kernel, tiles_per_seq=tiles_per_seq),
        grid=(n_tiles + 2,),
        in_specs=[pl.BlockSpec((1, PROMPT_TILE, D_MODEL), lambda g: (*tile(g + 1), 0)),
                  pl.BlockSpec((1, PROMPT_TILE, D_MODEL), lambda g: (*tile(g - 2), 0))]
        + _WEIGHT_SPECS,
        out_specs=[
            pl.BlockSpec((1, PROMPT_TILE, D_MODEL), lambda g: (*tile(g - 2), 0)),
            pl.BlockSpec((1, 1, N_HEADS, HEAD, HEAD), lambda g: (0, mixed_seq(g), 0, 0, 0)),
            pl.BlockSpec((1, 1, ROWS, N_HEADS, HEAD), lambda g: (0, mixed_seq(g), 0, 0, 0)),
        ],
        out_shape=[
            jax.ShapeDtypeStruct((batch, seq, D_MODEL), F32),
            jax.ShapeDtypeStruct((1, batch + 1, N_HEADS, HEAD, HEAD), F32),
            jax.ShapeDtypeStruct((1, batch + 1, ROWS, N_HEADS, HEAD), F32),
        ],
        scratch_shapes=[pltpu.VMEM((PROMPT_TILE, D_IN), F32),
                        pltpu.VMEM((PROMPT_TILE, D_IN), F32)]
        + [pltpu.VMEM((PROMPT_TILE, D_MODEL), BF16)] * 4 + _CONSTANT_SCRATCH,
        compiler_params=params,
        name="prompt_layer",
    )(x_prompt, x_prompt, *weights)

    n_rows = dec_batch * DEC_SEQ
    y_s, st_s, vch_s = pl.pallas_call(
        _sample_kernel,
        grid=(n_rows // ROWS,),
        in_specs=[pl.BlockSpec((ROWS, D_MODEL), lambda i: (i, 0))]
        + _WEIGHT_SPECS
        + [pl.BlockSpec((1, SEQS_PER_TILE, N_HEADS, HEAD, HEAD), lambda i: (0, i, 0, 0, 0))],
        out_specs=[
            pl.BlockSpec((ROWS, D_MODEL), lambda i: (i, 0)),
            pl.BlockSpec((1, SEQS_PER_TILE, N_HEADS, HEAD, HEAD), lambda i: (0, i, 0, 0, 0)),
            pl.BlockSpec((1, SEQS_PER_TILE, DEC_SEQ, N_HEADS, HEAD), lambda i: (0, i, 0, 0, 0)),
        ],
        out_shape=[
            jax.ShapeDtypeStruct((n_rows, D_MODEL), F32),
            jax.ShapeDtypeStruct((1, dec_batch, N_HEADS, HEAD, HEAD), F32),
            jax.ShapeDtypeStruct((1, dec_batch, DEC_SEQ, N_HEADS, HEAD), F32),
        ],
        scratch_shapes=[pltpu.VMEM((ROWS, D_A), F32),
                        pltpu.VMEM((N_HEADS, HEAD, ROWS), F32),
                        pltpu.VMEM((N_HEADS, HEAD, ROWS), F32),
                        pltpu.VMEM((ROWS, D_A), F32)] + _CONSTANT_SCRATCH,
        compiler_params=pltpu.CompilerParams(dimension_semantics=("arbitrary",),
                                             vmem_limit_bytes=VMEM_LIMIT_BYTES),
        name="decode_layer",
    )(x_sample.reshape(n_rows, D_MODEL), *weights, state_hgrn)

    return (y_p,
            y_s.reshape(dec_batch, DEC_SEQ, D_MODEL),
            st_p[:, :batch],
            st_s,
            vch_p[:, :batch],
            vch_s)
```

```python
import functools

import jax
import jax.numpy as jnp
from jax import lax
from jax.experimental import pallas as pl
from jax.experimental.pallas import tpu as pltpu

F32 = jnp.float32
BF16 = jnp.bfloat16

D_MODEL = 1024
D_A = 512
D_B = 512
HEAD = 128
N_HEADS = 4
D_IN = 4 * D_A + 3 * D_B
ROWS = 128
SUBLANES = 8
PACKED = 2 * SUBLANES
N_TILES = ROWS // SUBLANES
SUBLANE_LEVELS = 3
EPS = 1e-6
PROMPT_TILE = 512
PROJ_BLOCK = 512
HEAD_ROUNDS = 12
CHUNK_STAGGER = 4
HEAD_STAGGER = 1
DEC_SEQ = 8
SEQS_PER_TILE = ROWS // DEC_SEQ
SEQ_GROUP = 4
VMEM_LIMIT_BYTES = 56 * 1024 * 1024

_NT = (((1,), (1,)), ((), ()))


def _silu(x):
    return x * jax.nn.sigmoid(x)


def _gelu(x):
    c = -2.0 * 0.7978845608028654 * 1.4426950408889634
    return x / (1.0 + jnp.exp2(x * (c * 0.044715 * (x * x) + c)))


def _rms_norm(x, g):
    return x * lax.rsqrt(jnp.mean(x * x, axis=-1, keepdims=True) + EPS) * g


def _layer_norm(x, g, b):
    mu = jnp.mean(x, axis=-1, keepdims=True)
    xc = x - mu
    var = jnp.mean(xc * xc, axis=-1, keepdims=True)
    return xc * lax.rsqrt(var + EPS) * g + b


def _lower_bound(lbl):
    rows = [lbl[r:r + 1, :] for r in range(lbl.shape[0])]
    mx = functools.reduce(jnp.maximum, rows)
    es = [jnp.exp(r - mx) for r in rows]
    return es[0] / functools.reduce(lambda a, c: a + c, es)


def _n_levels(seg_len):
    n_levels = seg_len.bit_length() - 1
    assert seg_len == 1 << n_levels and n_levels >= SUBLANE_LEVELS
    return n_levels


def _split3(x):
    hi = x.astype(BF16)
    r1 = x - hi.astype(F32)
    mid = r1.astype(BF16)
    lo = (r1 - mid.astype(F32)).astype(BF16)
    return hi, mid, lo


def _store_constants(seg_len, ws_ref, bs_ref, wout_ref, lvl_ref, wc_ref, bias_ref, wout16_ref):
    t = lax.broadcasted_iota(jnp.int32, (ROWS, ROWS), 0)
    s = lax.broadcasted_iota(jnp.int32, (ROWS, ROWS), 1)
    n_levels = _n_levels(seg_len)
    x = t ^ s
    lvl = jnp.where(t == s, n_levels, -1)
    for j in range(n_levels):
        lvl = jnp.where(((x >> j) == 1) & (((t >> j) & 1) == 1), j, lvl)
    lvl_ref[...] = lvl
    causal = (t >= s) & ((t >> n_levels) == (s >> n_levels))
    pick = jnp.where(s == (t & (seg_len - 1)), 1.0, 0.0).astype(BF16)
    for hd in range(N_HEADS):
        w = ws_ref[hd].astype(BF16)
        if seg_len != ROWS:
            w = jnp.dot(pick, w, preferred_element_type=F32).astype(BF16)
            w = lax.dot_general(w, pick, _NT, preferred_element_type=F32).astype(BF16)
        wc_ref[hd] = jnp.where(causal, w, jnp.zeros_like(w))
    pad = jnp.zeros((PACKED - N_HEADS, ROWS), F32)
    bias = sum(lax.dot_general(pick, part, _NT, preferred_element_type=F32)
               for part in _split3(jnp.concatenate([bs_ref[...], pad], axis=0)))
    bias_ref[...] = bias[:, :N_HEADS]
    wout16_ref[...] = wout_ref[...].astype(BF16)


def _row_bits():
    rowi = lax.broadcasted_iota(jnp.int32, (SUBLANES, HEAD), 0)
    return [((rowi >> j) & 1) == 1 for j in range(SUBLANE_LEVELS)]


def _hgrn_head(gates, lvl_ref, bits, n_levels):
    one = jnp.ones((SUBLANES, HEAD), F32)
    z = [[] for _ in range(n_levels)]
    qp_t, ks_t, blk_t, diag_t = [], [], [], []
    for pair in range(N_TILES // 2):
        z_pair = [[] for _ in range(SUBLANE_LEVELS)]
        for i in (2 * pair, 2 * pair + 1):
            q, kk, f = gates(i)
            diag_t.append(jnp.sum(q * kk, axis=-1, keepdims=True))
            qp, ks, blk = q * f, kk, f
            for j in range(SUBLANE_LEVELS):
                m = 1 << j
                z_pair[j].append(jnp.where(bits[j], qp, ks))
                sib = pltpu.roll(blk, m, 0)
                if 2 * m != SUBLANES:
                    sib = jnp.where(bits[j], sib, pltpu.roll(blk, SUBLANES - m, 0))
                qp = qp * jnp.where(bits[j], sib, one)
                ks = ks * jnp.where(bits[j], one, sib)
                blk = blk * sib
            qp_t.append(qp)
            ks_t.append(ks)
            blk_t.append(blk)
        for j in range(SUBLANE_LEVELS):
            z[j].append(jnp.concatenate(z_pair[j], axis=0).astype(BF16))
    yield

    blk_levels, blk_b = [], blk_t
    for j in range(SUBLANE_LEVELS, n_levels):
        blk_levels.append(blk_b)
        blk_b = [blk_b[2 * n] * blk_b[2 * n + 1] for n in range(len(blk_b) // 2)]
    for pair in range(N_TILES // 2):
        z_pair = [[] for _ in range(SUBLANE_LEVELS, n_levels)]
        for i in (2 * pair, 2 * pair + 1):
            for j in range(SUBLANE_LEVELS, n_levels):
                block = i >> (j - SUBLANE_LEVELS)
                sib = blk_levels[j - SUBLANE_LEVELS][block ^ 1]
                if block & 1:
                    z_pair[j - SUBLANE_LEVELS].append(qp_t[i])
                    qp_t[i] = qp_t[i] * sib
                else:
                    z_pair[j - SUBLANE_LEVELS].append(ks_t[i])
                    ks_t[i] = ks_t[i] * sib
        for j in range(SUBLANE_LEVELS, n_levels):
            z[j].append(jnp.concatenate(z_pair[j - SUBLANE_LEVELS], axis=0).astype(BF16))
    q_in = jnp.concatenate(qp_t, axis=0)
    k_out = jnp.concatenate(ks_t, axis=0)
    per_tile = N_TILES // len(blk_b)
    total = jnp.concatenate([blk_b[i // per_tile] for i in range(N_TILES)], axis=0)
    yield

    p, p_row = [], []
    for j in range(n_levels):
        zj = jnp.concatenate(z[j], axis=0)
        span = 1 << max(j - SUBLANE_LEVELS, 0)
        if span * SUBLANES >= PACKED:
            upper = [i for i in range(N_TILES) if (i // span) & 1]
            lhs = jnp.concatenate([z[j][i // 2] for i in upper[::2]], axis=0)
        else:
            upper = list(range(N_TILES))
            lhs = zj
        p.append(lax.dot_general(lhs, zj, _NT, preferred_element_type=F32))
        p_row.append({i: n * SUBLANES for n, i in enumerate(upper)})
        yield
    sc_t = []
    for i in range(N_TILES):
        lvl_i = lvl_ref[i * SUBLANES:(i + 1) * SUBLANES, :]
        acc = jnp.where(lvl_i == n_levels, diag_t[i], 0.0)
        for j in range(n_levels):
            if j < SUBLANE_LEVELS or (i >> (j - SUBLANE_LEVELS)) & 1:
                r0 = p_row[j][i]
                acc = jnp.where(lvl_i == j, p[j][r0:r0 + SUBLANES, :], acc)
        sc_t.append(acc)
    sc = jnp.concatenate(sc_t, axis=0).astype(BF16)
    return sc, q_in, k_out, total


def _lockstep(generators, starts=None):
    starts = starts or [0] * len(generators)
    results = [None] * len(generators)
    waiting = sorted(range(len(generators)), key=lambda i: starts[i])
    active = []
    rnd = 0
    while waiting or active:
        while waiting and starts[waiting[0]] <= rnd:
            active.append(waiting.pop(0))
        still = []
        for idx in active:
            try:
                next(generators[idx])
                still.append(idx)
            except StopIteration as done:
                results[idx] = done.value
        active = still
        rnd += 1
    return results


def _hgrn_gates(pq, pf, lb):
    q = _silu(pq)
    forget = lb + (1.0 - lb) * jax.nn.sigmoid(pf)
    return q, 1.0 - forget, forget


def _hgrn_finish(o, pz, g):
    return (_rms_norm(o, g) * _silu(pz)).astype(BF16)


def _mlp_head(hd, u, v_n16, gate, wc_ref, bias):
    mix = jnp.dot(wc_ref[hd], v_n16, preferred_element_type=F32) + bias[:, hd:hd + 1]
    return (_gelu(u) * mix * _silu(gate)).astype(BF16)


def _prompt_kernel(x_ref, xprev_ref, ng_ref, win_ref, lbl_ref, hg_ref, lng_ref, lnb_ref, ws_ref,
                   bs_ref, wout_ref, fg_ref, y_ref, st_ref, vch_ref, proj_a, proj_b, h_ref,
                   mixed_ref, lvl_ref, wc_ref, bias_ref, wout16_ref, *, tiles_per_seq):
    g = pl.program_id(0)

    @pl.when(g == 0)
    def _():
        proj_b[...] = jnp.zeros_like(proj_b)
        _store_constants(ROWS, ws_ref, bs_ref, wout_ref, lvl_ref, wc_ref, bias_ref, wout16_ref)

    @pl.when(lax.rem(jnp.maximum(g - 1, 0), tiles_per_seq) == 0)
    def _():
        st_ref[...] = jnp.zeros_like(st_ref)

    def step(proj_next, proj_prev):
        h_ref[...] = _rms_norm(x_ref[0], ng_ref[...]).astype(BF16)
        lb = _lower_bound(lbl_ref[...])
        bits, n_levels = _row_bits(), _n_levels(ROWS)
        hg = hg_ref[...]
        lng = lng_ref[...]
        lnb = lnb_ref[...]
        bias = bias_ref[...]
        n_chunks = PROMPT_TILE // ROWS
        n_blocks = D_IN // PROJ_BLOCK

        def col(c, k, hs=slice(0, D_A)):
            return proj_prev[c * ROWS:(c + 1) * ROWS, k * D_A + hs.start:k * D_A + hs.stop]

        def head(c, hd):
            hs = slice(hd * HEAD, (hd + 1) * HEAD)

            def gates(i):
                rows = slice(c * ROWS + i * SUBLANES, c * ROWS + (i + 1) * SUBLANES)
                return _hgrn_gates(proj_prev[rows, hs],
                                   proj_prev[rows, D_A + hd * HEAD:D_A + (hd + 1) * HEAD],
                                   lb[:, hs])

            sc, q_in, k_out, total = yield from _hgrn_head(gates, lvl_ref, bits, n_levels)
            state = st_ref[0, 0, hd]
            vh = col(c, 2, hs)
            o = jnp.dot(jnp.concatenate([sc, q_in.astype(BF16)], axis=1),
                        jnp.concatenate([vh.astype(BF16), state.astype(BF16)], axis=0),
                        preferred_element_type=F32)
            yield
            st_ref[0, 0, hd] = total.T * state + jnp.dot(
                k_out.T.astype(BF16), vh.astype(BF16), preferred_element_type=F32)
            yield
            mixed_ref[c * ROWS:(c + 1) * ROWS, hs] = _hgrn_finish(o, col(c, 3, hs), hg[:, hs])

        def mlp(c):
            v_n = _layer_norm(_gelu(col(c, 5)), lng, lnb)
            if c == n_chunks - 1:
                for hd in range(N_HEADS):
                    vch_ref[0, 0, :, hd, :] = v_n[:, hd * HEAD:(hd + 1) * HEAD]
            v_n16 = v_n.astype(BF16)
            yield
            for hd in range(N_HEADS):
                hs = slice(hd * HEAD, (hd + 1) * HEAD)
                mixed_ref[c * ROWS:(c + 1) * ROWS, D_A + hd * HEAD:D_A + (hd + 1) * HEAD] = (
                    _mlp_head(hd, col(c, 4, hs), v_n16[:, hs], col(c, 6, hs), wc_ref, bias))
                yield
                yield

        def projection(rounds_per_block):
            for blk in range(n_blocks):
                cols = slice(blk * PROJ_BLOCK, (blk + 1) * PROJ_BLOCK)
                proj_next[:, cols] = jnp.dot(h_ref[...], win_ref[:, cols],
                                             preferred_element_type=F32)
                for _ in range(rounds_per_block):
                    yield

        items = [gen for c in range(n_chunks)
                 for gen in [head(c, hd) for hd in range(N_HEADS)] + [mlp(c)]]
        starts = [c * CHUNK_STAGGER + k * HEAD_STAGGER
                  for c in range(n_chunks) for k in list(range(N_HEADS)) + [0]]
        rounds = (n_chunks - 1) * CHUNK_STAGGER + (N_HEADS - 1) * HEAD_STAGGER + HEAD_ROUNDS
        _lockstep(items + [projection(max(1, rounds // n_blocks))], starts + [0])

        out = xprev_ref[0] + jnp.dot(mixed_ref[...], wout16_ref[...], preferred_element_type=F32)
        y_ref[0] = _rms_norm(out, fg_ref[...])

    parity = lax.rem(g, 2)

    @pl.when(parity == 0)
    def _():
        step(proj_a, proj_b)

    @pl.when(parity == 1)
    def _():
        step(proj_b, proj_a)


def _sample_kernel(x_ref, ng_ref, win_ref, lbl_ref, hg_ref, lng_ref, lnb_ref, ws_ref, bs_ref,
                   wout_ref, fg_ref, stin_ref, y_ref, stout_ref, vch_ref,
                   qin_ref, kot_ref, tot_ref, oint_ref, lvl_ref, wc_ref, bias_ref, wout16_ref):
    @pl.when(pl.program_id(0) == 0)
    def _():
        _store_constants(DEC_SEQ, ws_ref, bs_ref, wout_ref, lvl_ref, wc_ref, bias_ref, wout16_ref)

    x = x_ref[...]
    h = _rms_norm(x, ng_ref[...]).astype(BF16)
    proj = jnp.dot(h, win_ref[...], preferred_element_type=F32)

    def col(k, hs=slice(0, D_A)):
        return proj[:, k * D_A + hs.start:k * D_A + hs.stop]

    lb = _lower_bound(lbl_ref[...])
    bits, n_levels = _row_bits(), _n_levels(DEC_SEQ)
    vi16 = col(2).astype(BF16)

    def head(hd):
        hs = slice(hd * HEAD, (hd + 1) * HEAD)

        def gates(i):
            rows = slice(i * SUBLANES, (i + 1) * SUBLANES)
            return _hgrn_gates(proj[rows, hs], proj[rows, D_A + hd * HEAD:D_A + (hd + 1) * HEAD],
                               lb[:, hs])

        sc, q_in, k_out, total = yield from _hgrn_head(gates, lvl_ref, bits, n_levels)
        qin_ref[:, hs] = q_in
        kot_ref[hd] = k_out.T
        tot_ref[hd] = total.T
        return jnp.dot(sc, vi16[:, hs], preferred_element_type=F32)

    o_intra = _lockstep([head(hd) for hd in range(N_HEADS)], list(range(N_HEADS)))

    lane = lax.broadcasted_iota(jnp.int32, (HEAD, ROWS), 1)

    def seq_head(i, hd):
        hs = slice(hd * HEAD, (hd + 1) * HEAD)
        seq_rows = slice(i * DEC_SEQ, (i + 1) * DEC_SEQ)
        state = stin_ref[0, i, hd]
        qi = qin_ref[seq_rows, hs].astype(BF16)
        oint_ref[seq_rows, hs] = jnp.dot(qi, state.astype(BF16), preferred_element_type=F32)
        yield
        in_seq = (lane >> (DEC_SEQ.bit_length() - 1)) == i
        k_seq = jnp.where(in_seq, kot_ref[hd], 0.0).astype(BF16)
        decay = tot_ref[hd, :, i * DEC_SEQ:i * DEC_SEQ + 1]
        yield
        stout_ref[0, i, hd] = decay * state + jnp.dot(
            k_seq, vi16[:, hs], preferred_element_type=F32)

    _lockstep([seq_head(i, hd) for i in range(SEQS_PER_TILE) for hd in range(N_HEADS)],
              [i for i in range(SEQS_PER_TILE) for _ in range(N_HEADS)])

    hg = hg_ref[...]
    bias = bias_ref[...]
    v_n = _layer_norm(_gelu(col(5)), lng_ref[...], lnb_ref[...])
    for hd in range(N_HEADS):
        vch_ref[0, :, :, hd, :] = v_n[:, hd * HEAD:(hd + 1) * HEAD].reshape(
            SEQS_PER_TILE, DEC_SEQ, HEAD)
    v_n16 = v_n.astype(BF16)
    mixed = []
    for hd in range(N_HEADS):
        hs = slice(hd * HEAD, (hd + 1) * HEAD)
        mixed.append(_hgrn_finish(o_intra[hd] + oint_ref[:, hs], col(3, hs), hg[:, hs]))
    for hd in range(N_HEADS):
        hs = slice(hd * HEAD, (hd + 1) * HEAD)
        mixed.append(_mlp_head(hd, col(4, hs), v_n16[:, hs], col(6, hs), wc_ref, bias))
    out = x + jnp.dot(jnp.concatenate(mixed, axis=-1), wout16_ref[...],
                      preferred_element_type=F32)
    y_ref[...] = _rms_norm(out, fg_ref[...])


def _const_spec(shape):
    return pl.BlockSpec(shape, lambda *_: (0,) * len(shape), pipeline_mode=pl.Buffered(1))


_WEIGHT_SPECS = [
    _const_spec((1, D_MODEL)),
    _const_spec((D_MODEL, D_IN)),
    _const_spec((2, D_A)),
    _const_spec((1, D_A)),
    _const_spec((1, D_B)),
    _const_spec((1, D_B)),
    _const_spec((N_HEADS, ROWS, ROWS)),
    _const_spec((N_HEADS, ROWS)),
    _const_spec((D_MODEL, D_MODEL)),
    _const_spec((1, D_MODEL)),
]

_CONSTANT_SCRATCH = [
    pltpu.VMEM((ROWS, ROWS), jnp.int32),
    pltpu.VMEM((N_HEADS, ROWS, ROWS), BF16),
    pltpu.VMEM((ROWS, N_HEADS), F32),
    pltpu.VMEM((D_MODEL, D_MODEL), BF16),
]


def kernel(x_prompt, x_sample, state_hgrn, norm_g, w_in, lb_logits, hgrn_norm_g, sgu_ln_g,
           sgu_ln_b, w_s, b_s, w_out, final_norm_g):
    depth = norm_g.shape[0]
    assert depth == 1 and lb_logits.shape == (2, D_A)
    batch, seq, _ = x_prompt.shape
    dec_batch, dec_seq, _ = x_sample.shape
    assert seq % PROMPT_TILE == 0 and dec_seq == DEC_SEQ and dec_batch % SEQS_PER_TILE == 0

    weights = (norm_g, w_in[0].astype(BF16), lb_logits, hgrn_norm_g, sgu_ln_g, sgu_ln_b,
               w_s[0], b_s[0], w_out[0], final_norm_g.reshape(1, D_MODEL))
    params = pltpu.CompilerParams(dimension_semantics=("arbitrary",),
                                  vmem_limit_bytes=VMEM_LIMIT_BYTES)

    tiles_per_seq = seq // PROMPT_TILE
    n_tiles = batch * tiles_per_seq

    def cur_tile(g):
        t = jnp.minimum(g, n_tiles - 1)
        return t // tiles_per_seq, t % tiles_per_seq

    def prev_tile(g):
        t = jnp.maximum(g - 1, 0)
        return t // tiles_per_seq, t % tiles_per_seq

    y_p, st_p, vch_p = pl.pallas_call(
        functools.partial(_prompt_kernel, tiles_per_seq=tiles_per_seq),
        grid=(n_tiles + 1,),
        in_specs=[pl.BlockSpec((1, PROMPT_TILE, D_MODEL), lambda g: (*cur_tile(g), 0)),
                  pl.BlockSpec((1, PROMPT_TILE, D_MODEL), lambda g: (*prev_tile(g), 0))]
        + _WEIGHT_SPECS,
        out_specs=[
            pl.BlockSpec((1, PROMPT_TILE, D_MODEL), lambda g: (*prev_tile(g), 0)),
            pl.BlockSpec((1, 1, N_HEADS, HEAD, HEAD), lambda g: (0, prev_tile(g)[0], 0, 0, 0)),
            pl.BlockSpec((1, 1, ROWS, N_HEADS, HEAD), lambda g: (0, prev_tile(g)[0], 0, 0, 0)),
        ],
        out_shape=[
            jax.ShapeDtypeStruct((batch, seq, D_MODEL), F32),
            jax.ShapeDtypeStruct((1, batch, N_HEADS, HEAD, HEAD), F32),
            jax.ShapeDtypeStruct((1, batch, ROWS, N_HEADS, HEAD), F32),
        ],
        scratch_shapes=[pltpu.VMEM((PROMPT_TILE, D_IN), F32),
                        pltpu.VMEM((PROMPT_TILE, D_IN), F32),
                        pltpu.VMEM((PROMPT_TILE, D_MODEL), BF16),
                        pltpu.VMEM((PROMPT_TILE, D_MODEL), BF16)] + _CONSTANT_SCRATCH,
        compiler_params=params,
        name="prompt_layer",
    )(x_prompt, x_prompt, *weights)

    n_rows = dec_batch * DEC_SEQ
    y_s, st_s, vch_s = pl.pallas_call(
        _sample_kernel,
        grid=(n_rows // ROWS,),
        in_specs=[pl.BlockSpec((ROWS, D_MODEL), lambda i: (i, 0))]
        + _WEIGHT_SPECS
        + [pl.BlockSpec((1, SEQS_PER_TILE, N_HEADS, HEAD, HEAD), lambda i: (0, i, 0, 0, 0))],
        out_specs=[
            pl.BlockSpec((ROWS, D_MODEL), lambda i: (i, 0)),
            pl.BlockSpec((1, SEQS_PER_TILE, N_HEADS, HEAD, HEAD), lambda i: (0, i, 0, 0, 0)),
            pl.BlockSpec((1, SEQS_PER_TILE, DEC_SEQ, N_HEADS, HEAD), lambda i: (0, i, 0, 0, 0)),
        ],
        out_shape=[
            jax.ShapeDtypeStruct((n_rows, D_MODEL), F32),
            jax.ShapeDtypeStruct((1, dec_batch, N_HEADS, HEAD, HEAD), F32),
            jax.ShapeDtypeStruct((1, dec_batch, DEC_SEQ, N_HEADS, HEAD), F32),
        ],
        scratch_shapes=[pltpu.VMEM((ROWS, D_A), F32),
                        pltpu.VMEM((N_HEADS, HEAD, ROWS), F32),
                        pltpu.VMEM((N_HEADS, HEAD, ROWS), F32),
                        pltpu.VMEM((ROWS, D_A), F32)] + _CONSTANT_SCRATCH,
        compiler_params=pltpu.CompilerParams(dimension_semantics=("arbitrary",),
                                             vmem_limit_bytes=VMEM_LIMIT_BYTES),
        name="decode_layer",
    )(x_sample.reshape(n_rows, D_MODEL), *weights, state_hgrn)

    return (y_p,
            y_s.reshape(dec_batch, DEC_SEQ, D_MODEL),
            st_p,
            st_s,
            vch_p,
            vch_s)
```

```python
import functools

import jax
import jax.numpy as jnp
from jax import lax
from jax.experimental import pallas as pl
from jax.experimental.pallas import tpu as pltpu

F32 = jnp.float32
BF16 = jnp.bfloat16

D_MODEL = 1024
D_A = 512
D_B = 512
HEAD = 128
N_HEADS = 4
D_IN = 4 * D_A + 3 * D_B
ROWS = 128
SUBLANES = 8
PACKED = 2 * SUBLANES
N_TILES = ROWS // SUBLANES
SUBLANE_LEVELS = 3
EPS = 1e-6
PROMPT_TILE = 512
PROJ_BLOCK = 512
HEAD_ROUNDS = 12
CHUNK_STAGGER = 4
HEAD_STAGGER = 1
MLP_START = 6
DEC_SEQ = 8
SEQS_PER_TILE = ROWS // DEC_SEQ
SEQ_GROUP = 4
VMEM_LIMIT_BYTES = 56 * 1024 * 1024

_NT = (((1,), (1,)), ((), ()))


def _silu(x):
    return x * jax.nn.sigmoid(x)


def _gelu(x):
    c = -2.0 * 0.7978845608028654 * 1.4426950408889634
    return x / (1.0 + jnp.exp2(x * (c * 0.044715 * (x * x) + c)))


def _rms_norm(x, g):
    return x * lax.rsqrt(jnp.mean(x * x, axis=-1, keepdims=True) + EPS) * g


def _layer_norm(x, g, b):
    mu = jnp.mean(x, axis=-1, keepdims=True)
    xc = x - mu
    var = jnp.mean(xc * xc, axis=-1, keepdims=True)
    return xc * lax.rsqrt(var + EPS) * g + b


def _lower_bound(lbl):
    rows = [lbl[r:r + 1, :] for r in range(lbl.shape[0])]
    mx = functools.reduce(jnp.maximum, rows)
    es = [jnp.exp(r - mx) for r in rows]
    return es[0] / functools.reduce(lambda a, c: a + c, es)


def _n_levels(seg_len):
    n_levels = seg_len.bit_length() - 1
    assert seg_len == 1 << n_levels and n_levels >= SUBLANE_LEVELS
    return n_levels


def _split3(x):
    hi = x.astype(BF16)
    r1 = x - hi.astype(F32)
    mid = r1.astype(BF16)
    lo = (r1 - mid.astype(F32)).astype(BF16)
    return hi, mid, lo


def _store_constants(seg_len, ws_ref, bs_ref, wout_ref, lvl_ref, wc_ref, bias_ref, wout16_ref):
    t = lax.broadcasted_iota(jnp.int32, (ROWS, ROWS), 0)
    s = lax.broadcasted_iota(jnp.int32, (ROWS, ROWS), 1)
    n_levels = _n_levels(seg_len)
    x = t ^ s
    lvl = jnp.where(t == s, n_levels, -1)
    for j in range(n_levels):
        lvl = jnp.where(((x >> j) == 1) & (((t >> j) & 1) == 1), j, lvl)
    lvl_ref[...] = lvl
    causal = (t >= s) & ((t >> n_levels) == (s >> n_levels))
    pick = jnp.where(s == (t & (seg_len - 1)), 1.0, 0.0).astype(BF16)
    for hd in range(N_HEADS):
        w = ws_ref[hd].astype(BF16)
        if seg_len != ROWS:
            w = jnp.dot(pick, w, preferred_element_type=F32).astype(BF16)
            w = lax.dot_general(w, pick, _NT, preferred_element_type=F32).astype(BF16)
        wc_ref[hd] = jnp.where(causal, w, jnp.zeros_like(w))
    pad = jnp.zeros((PACKED - N_HEADS, ROWS), F32)
    bias = sum(lax.dot_general(pick, part, _NT, preferred_element_type=F32)
               for part in _split3(jnp.concatenate([bs_ref[...], pad], axis=0)))
    bias_ref[...] = bias[:, :N_HEADS]
    wout16_ref[...] = wout_ref[...].astype(BF16)


def _row_bits():
    rowi = lax.broadcasted_iota(jnp.int32, (SUBLANES, HEAD), 0)
    return [((rowi >> j) & 1) == 1 for j in range(SUBLANE_LEVELS)]


def _hgrn_head(gates, lvl_ref, bits, n_levels):
    one = jnp.ones((SUBLANES, HEAD), F32)
    z = [[] for _ in range(n_levels)]
    qp_t, ks_t, blk_t, diag_t = [], [], [], []
    for pair in range(N_TILES // 2):
        z_pair = [[] for _ in range(SUBLANE_LEVELS)]
        for i in (2 * pair, 2 * pair + 1):
            q, kk, f = gates(i)
            diag_t.append(jnp.sum(q * kk, axis=-1, keepdims=True))
            qp, ks, blk = q * f, kk, f
            for j in range(SUBLANE_LEVELS):
                m = 1 << j
                z_pair[j].append(jnp.where(bits[j], qp, ks))
                sib = pltpu.roll(blk, m, 0)
                if 2 * m != SUBLANES:
                    sib = jnp.where(bits[j], sib, pltpu.roll(blk, SUBLANES - m, 0))
                qp = qp * jnp.where(bits[j], sib, one)
                ks = ks * jnp.where(bits[j], one, sib)
                blk = blk * sib
            qp_t.append(qp)
            ks_t.append(ks)
            blk_t.append(blk)
        for j in range(SUBLANE_LEVELS):
            z[j].append(jnp.concatenate(z_pair[j], axis=0).astype(BF16))
    yield

    blk_levels, blk_b = [], blk_t
    for j in range(SUBLANE_LEVELS, n_levels):
        blk_levels.append(blk_b)
        blk_b = [blk_b[2 * n] * blk_b[2 * n + 1] for n in range(len(blk_b) // 2)]
    for pair in range(N_TILES // 2):
        z_pair = [[] for _ in range(SUBLANE_LEVELS, n_levels)]
        for i in (2 * pair, 2 * pair + 1):
            for j in range(SUBLANE_LEVELS, n_levels):
                block = i >> (j - SUBLANE_LEVELS)
                sib = blk_levels[j - SUBLANE_LEVELS][block ^ 1]
                if block & 1:
                    z_pair[j - SUBLANE_LEVELS].append(qp_t[i])
                    qp_t[i] = qp_t[i] * sib
                else:
                    z_pair[j - SUBLANE_LEVELS].append(ks_t[i])
                    ks_t[i] = ks_t[i] * sib
        for j in range(SUBLANE_LEVELS, n_levels):
            z[j].append(jnp.concatenate(z_pair[j - SUBLANE_LEVELS], axis=0).astype(BF16))
    q_in = jnp.concatenate(qp_t, axis=0)
    k_out = jnp.concatenate(ks_t, axis=0)
    per_tile = N_TILES // len(blk_b)
    total = jnp.concatenate([blk_b[i // per_tile] for i in range(N_TILES)], axis=0)
    yield

    p, p_row = [], []
    for j in range(n_levels):
        zj = jnp.concatenate(z[j], axis=0)
        span = 1 << max(j - SUBLANE_LEVELS, 0)
        if span * SUBLANES >= PACKED:
            upper = [i for i in range(N_TILES) if (i // span) & 1]
            lhs = jnp.concatenate([z[j][i // 2] for i in upper[::2]], axis=0)
        else:
            upper = list(range(N_TILES))
            lhs = zj
        p.append(lax.dot_general(lhs, zj, _NT, preferred_element_type=F32))
        p_row.append({i: n * SUBLANES for n, i in enumerate(upper)})
        yield
    sc_t = []
    for i in range(N_TILES):
        lvl_i = lvl_ref[i * SUBLANES:(i + 1) * SUBLANES, :]
        acc = jnp.where(lvl_i == n_levels, diag_t[i], 0.0)
        for j in range(n_levels):
            if j < SUBLANE_LEVELS or (i >> (j - SUBLANE_LEVELS)) & 1:
                r0 = p_row[j][i]
                acc = jnp.where(lvl_i == j, p[j][r0:r0 + SUBLANES, :], acc)
        sc_t.append(acc)
    sc = jnp.concatenate(sc_t, axis=0).astype(BF16)
    return sc, q_in, k_out, total


def _lockstep(generators, starts=None):
    starts = starts or [0] * len(generators)
    results = [None] * len(generators)
    waiting = sorted(range(len(generators)), key=lambda i: starts[i])
    active = []
    rnd = 0
    while waiting or active:
        while waiting and starts[waiting[0]] <= rnd:
            active.append(waiting.pop(0))
        still = []
        for idx in active:
            try:
                next(generators[idx])
                still.append(idx)
            except StopIteration as done:
                results[idx] = done.value
        active = still
        rnd += 1
    return results


def _hgrn_gates(pq, pf, lb):
    q = _silu(pq)
    forget = lb + (1.0 - lb) * jax.nn.sigmoid(pf)
    return q, 1.0 - forget, forget


def _hgrn_finish(o, pz, g):
    return (_rms_norm(o, g) * _silu(pz)).astype(BF16)


def _mlp_head(hd, u, v_n16, gate, wc_ref, bias):
    mix = jnp.dot(wc_ref[hd], v_n16, preferred_element_type=F32) + bias[:, hd:hd + 1]
    return (_gelu(u) * mix * _silu(gate)).astype(BF16)


def _prompt_kernel(x_ref, xprev_ref, ng_ref, win_ref, lbl_ref, hg_ref, lng_ref, lnb_ref, ws_ref,
                   bs_ref, wout_ref, fg_ref, y_ref, st_ref, vch_ref, proj_a, proj_b, h_ref,
                   mixed_ref, lvl_ref, wc_ref, bias_ref, wout16_ref, *, tiles_per_seq):
    g = pl.program_id(0)

    @pl.when(g == 0)
    def _():
        proj_b[...] = jnp.zeros_like(proj_b)
        _store_constants(ROWS, ws_ref, bs_ref, wout_ref, lvl_ref, wc_ref, bias_ref, wout16_ref)

    @pl.when(lax.rem(jnp.maximum(g - 1, 0), tiles_per_seq) == 0)
    def _():
        st_ref[...] = jnp.zeros_like(st_ref)

    def step(proj_next, proj_prev):
        h_ref[...] = _rms_norm(x_ref[0], ng_ref[...]).astype(BF16)
        lb = _lower_bound(lbl_ref[...])
        bits, n_levels = _row_bits(), _n_levels(ROWS)
        hg = hg_ref[...]
        lng = lng_ref[...]
        lnb = lnb_ref[...]
        bias = bias_ref[...]
        n_chunks = PROMPT_TILE // ROWS
        n_blocks = D_IN // PROJ_BLOCK

        def col(c, k, hs=slice(0, D_A)):
            return proj_prev[c * ROWS:(c + 1) * ROWS, k * D_A + hs.start:k * D_A + hs.stop]

        def head(c, hd):
            hs = slice(hd * HEAD, (hd + 1) * HEAD)

            def gates(i):
                rows = slice(c * ROWS + i * SUBLANES, c * ROWS + (i + 1) * SUBLANES)
                return _hgrn_gates(proj_prev[rows, hs],
                                   proj_prev[rows, D_A + hd * HEAD:D_A + (hd + 1) * HEAD],
                                   lb[:, hs])

            sc, q_in, k_out, total = yield from _hgrn_head(gates, lvl_ref, bits, n_levels)
            state = st_ref[0, 0, hd]
            vh = col(c, 2, hs)
            o = jnp.dot(jnp.concatenate([sc, q_in.astype(BF16)], axis=1),
                        jnp.concatenate([vh.astype(BF16), state.astype(BF16)], axis=0),
                        preferred_element_type=F32)
            yield
            st_ref[0, 0, hd] = total.T * state + jnp.dot(
                k_out.T.astype(BF16), vh.astype(BF16), preferred_element_type=F32)
            yield
            mixed_ref[c * ROWS:(c + 1) * ROWS, hs] = _hgrn_finish(o, col(c, 3, hs), hg[:, hs])

        def mlp(c):
            v_n = _layer_norm(_gelu(col(c, 5)), lng, lnb)
            if c == n_chunks - 1:
                for hd in range(N_HEADS):
                    vch_ref[0, 0, :, hd, :] = v_n[:, hd * HEAD:(hd + 1) * HEAD]
            v_n16 = v_n.astype(BF16)
            yield
            for hd in range(N_HEADS):
                hs = slice(hd * HEAD, (hd + 1) * HEAD)
                mixed_ref[c * ROWS:(c + 1) * ROWS, D_A + hd * HEAD:D_A + (hd + 1) * HEAD] = (
                    _mlp_head(hd, col(c, 4, hs), v_n16[:, hs], col(c, 6, hs), wc_ref, bias))
                yield
                yield

        def projection(rounds_per_block):
            for blk in range(n_blocks):
                cols = slice(blk * PROJ_BLOCK, (blk + 1) * PROJ_BLOCK)
                proj_next[:, cols] = jnp.dot(h_ref[...], win_ref[:, cols],
                                             preferred_element_type=F32)
                for _ in range(rounds_per_block):
                    yield

        items = [gen for c in range(n_chunks)
                 for gen in [head(c, hd) for hd in range(N_HEADS)] + [mlp(c)]]
        starts = [c * CHUNK_STAGGER + k * HEAD_STAGGER
                  for c in range(n_chunks) for k in list(range(N_HEADS)) + [MLP_START]]
        rounds = (n_chunks - 1) * CHUNK_STAGGER + (N_HEADS - 1) * HEAD_STAGGER + HEAD_ROUNDS
        _lockstep(items + [projection(max(1, rounds // n_blocks))], starts + [0])

        out = xprev_ref[0] + jnp.dot(mixed_ref[...], wout16_ref[...], preferred_element_type=F32)
        y_ref[0] = _rms_norm(out, fg_ref[...])

    parity = lax.rem(g, 2)

    @pl.when(parity == 0)
    def _():
        step(proj_a, proj_b)

    @pl.when(parity == 1)
    def _():
        step(proj_b, proj_a)


def _sample_kernel(x_ref, ng_ref, win_ref, lbl_ref, hg_ref, lng_ref, lnb_ref, ws_ref, bs_ref,
                   wout_ref, fg_ref, stin_ref, y_ref, stout_ref, vch_ref,
                   qin_ref, kot_ref, tot_ref, oint_ref, lvl_ref, wc_ref, bias_ref, wout16_ref):
    @pl.when(pl.program_id(0) == 0)
    def _():
        _store_constants(DEC_SEQ, ws_ref, bs_ref, wout_ref, lvl_ref, wc_ref, bias_ref, wout16_ref)

    x = x_ref[...]
    h = _rms_norm(x, ng_ref[...]).astype(BF16)
    proj = jnp.dot(h, win_ref[...], preferred_element_type=F32)

    def col(k, hs=slice(0, D_A)):
        return proj[:, k * D_A + hs.start:k * D_A + hs.stop]

    lb = _lower_bound(lbl_ref[...])
    bits, n_levels = _row_bits(), _n_levels(DEC_SEQ)
    vi16 = col(2).astype(BF16)

    def head(hd):
        hs = slice(hd * HEAD, (hd + 1) * HEAD)

        def gates(i):
            rows = slice(i * SUBLANES, (i + 1) * SUBLANES)
            return _hgrn_gates(proj[rows, hs], proj[rows, D_A + hd * HEAD:D_A + (hd + 1) * HEAD],
                               lb[:, hs])

        sc, q_in, k_out, total = yield from _hgrn_head(gates, lvl_ref, bits, n_levels)
        qin_ref[:, hs] = q_in
        kot_ref[hd] = k_out.T
        tot_ref[hd] = total.T
        return jnp.dot(sc, vi16[:, hs], preferred_element_type=F32)

    o_intra = _lockstep([head(hd) for hd in range(N_HEADS)])

    lane = lax.broadcasted_iota(jnp.int32, (HEAD, ROWS), 1)

    def seq_head(i, hd):
        hs = slice(hd * HEAD, (hd + 1) * HEAD)
        seq_rows = slice(i * DEC_SEQ, (i + 1) * DEC_SEQ)
        state = stin_ref[0, i, hd]
        qi = qin_ref[seq_rows, hs].astype(BF16)
        oint_ref[seq_rows, hs] = jnp.dot(qi, state.astype(BF16), preferred_element_type=F32)
        yield
        in_seq = (lane >> (DEC_SEQ.bit_length() - 1)) == i
        k_seq = jnp.where(in_seq, kot_ref[hd], 0.0).astype(BF16)
        decay = tot_ref[hd, :, i * DEC_SEQ:i * DEC_SEQ + 1]
        yield
        stout_ref[0, i, hd] = decay * state + jnp.dot(
            k_seq, vi16[:, hs], preferred_element_type=F32)

    for grp in range(SEQS_PER_TILE // SEQ_GROUP):
        _lockstep([seq_head(grp * SEQ_GROUP + s, hd)
                   for s in range(SEQ_GROUP) for hd in range(N_HEADS)])

    hg = hg_ref[...]
    bias = bias_ref[...]
    v_n = _layer_norm(_gelu(col(5)), lng_ref[...], lnb_ref[...])
    for hd in range(N_HEADS):
        vch_ref[0, :, :, hd, :] = v_n[:, hd * HEAD:(hd + 1) * HEAD].reshape(
            SEQS_PER_TILE, DEC_SEQ, HEAD)
    v_n16 = v_n.astype(BF16)
    mixed = []
    for hd in range(N_HEADS):
        hs = slice(hd * HEAD, (hd + 1) * HEAD)
        mixed.append(_hgrn_finish(o_intra[hd] + oint_ref[:, hs], col(3, hs), hg[:, hs]))
    for hd in range(N_HEADS):
        hs = slice(hd * HEAD, (hd + 1) * HEAD)
        mixed.append(_mlp_head(hd, col(4, hs), v_n16[:, hs], col(6, hs), wc_ref, bias))
    out = x + jnp.dot(jnp.concatenate(mixed, axis=-1), wout16_ref[...],
                      preferred_element_type=F32)
    y_ref[...] = _rms_norm(out, fg_ref[...])


def _const_spec(shape):
    return pl.BlockSpec(shape, lambda *_: (0,) * len(shape), pipeline_mode=pl.Buffered(1))


_WEIGHT_SPECS = [
    _const_spec((1, D_MODEL)),
    _const_spec((D_MODEL, D_IN)),
    _const_spec((2, D_A)),
    _const_spec((1, D_A)),
    _const_spec((1, D_B)),
    _const_spec((1, D_B)),
    _const_spec((N_HEADS, ROWS, ROWS)),
    _const_spec((N_HEADS, ROWS)),
    _const_spec((D_MODEL, D_MODEL)),
    _const_spec((1, D_MODEL)),
]

_CONSTANT_SCRATCH = [
    pltpu.VMEM((ROWS, ROWS), jnp.int32),
    pltpu.VMEM((N_HEADS, ROWS, ROWS), BF16),
    pltpu.VMEM((ROWS, N_HEADS), F32),
    pltpu.VMEM((D_MODEL, D_MODEL), BF16),
]


def kernel(x_prompt, x_sample, state_hgrn, norm_g, w_in, lb_logits, hgrn_norm_g, sgu_ln_g,
           sgu_ln_b, w_s, b_s, w_out, final_norm_g):
    depth = norm_g.shape[0]
    assert depth == 1 and lb_logits.shape == (2, D_A)
    batch, seq, _ = x_prompt.shape
    dec_batch, dec_seq, _ = x_sample.shape
    assert seq % PROMPT_TILE == 0 and dec_seq == DEC_SEQ and dec_batch % SEQS_PER_TILE == 0

    weights = (norm_g, w_in[0].astype(BF16), lb_logits, hgrn_norm_g, sgu_ln_g, sgu_ln_b,
               w_s[0], b_s[0], w_out[0], final_norm_g.reshape(1, D_MODEL))
    params = pltpu.CompilerParams(dimension_semantics=("arbitrary",),
                                  vmem_limit_bytes=VMEM_LIMIT_BYTES)

    tiles_per_seq = seq // PROMPT_TILE
    n_tiles = batch * tiles_per_seq

    def cur_tile(g):
        t = jnp.minimum(g, n_tiles - 1)
        return t // tiles_per_seq, t % tiles_per_seq

    def prev_tile(g):
        t = jnp.maximum(g - 1, 0)
        return t // tiles_per_seq, t % tiles_per_seq

    y_p, st_p, vch_p = pl.pallas_call(
        functools.partial(_prompt_kernel, tiles_per_seq=tiles_per_seq),
        grid=(n_tiles + 1,),
        in_specs=[pl.BlockSpec((1, PROMPT_TILE, D_MODEL), lambda g: (*cur_tile(g), 0)),
                  pl.BlockSpec((1, PROMPT_TILE, D_MODEL), lambda g: (*prev_tile(g), 0))]
        + _WEIGHT_SPECS,
        out_specs=[
            pl.BlockSpec((1, PROMPT_TILE, D_MODEL), lambda g: (*prev_tile(g), 0)),
            pl.BlockSpec((1, 1, N_HEADS, HEAD, HEAD), lambda g: (0, prev_tile(g)[0], 0, 0, 0)),
            pl.BlockSpec((1, 1, ROWS, N_HEADS, HEAD), lambda g: (0, prev_tile(g)[0], 0, 0, 0)),
        ],
        out_shape=[
            jax.ShapeDtypeStruct((batch, seq, D_MODEL), F32),
            jax.ShapeDtypeStruct((1, batch, N_HEADS, HEAD, HEAD), F32),
            jax.ShapeDtypeStruct((1, batch, ROWS, N_HEADS, HEAD), F32),
        ],
        scratch_shapes=[pltpu.VMEM((PROMPT_TILE, D_IN), F32),
                        pltpu.VMEM((PROMPT_TILE, D_IN), F32),
                        pltpu.VMEM((PROMPT_TILE, D_MODEL), BF16),
                        pltpu.VMEM((PROMPT_TILE, D_MODEL), BF16)] + _CONSTANT_SCRATCH,
        compiler_params=params,
        name="prompt_layer",
    )(x_prompt, x_prompt, *weights)

    n_rows = dec_batch * DEC_SEQ
    y_s, st_s, vch_s = pl.pallas_call(
        _sample_kernel,
        grid=(n_rows // ROWS,),
        in_specs=[pl.BlockSpec((ROWS, D_MODEL), lambda i: (i, 0))]
        + _WEIGHT_SPECS
        + [pl.BlockSpec((1, SEQS_PER_TILE, N_HEADS, HEAD, HEAD), lambda i: (0, i, 0, 0, 0))],
        out_specs=[
            pl.BlockSpec((ROWS, D_MODEL), lambda i: (i, 0)),
            pl.BlockSpec((1, SEQS_PER_TILE, N_HEADS, HEAD, HEAD), lambda i: (0, i, 0, 0, 0)),
            pl.BlockSpec((1, SEQS_PER_TILE, DEC_SEQ, N_HEADS, HEAD), lambda i: (0, i, 0, 0, 0)),
        ],
        out_shape=[
            jax.ShapeDtypeStruct((n_rows, D_MODEL), F32),
            jax.ShapeDtypeStruct((1, dec_batch, N_HEADS, HEAD, HEAD), F32),
            jax.ShapeDtypeStruct((1, dec_batch, DEC_SEQ, N_HEADS, HEAD), F32),
        ],
        scratch_shapes=[pltpu.VMEM((ROWS, D_A), F32),
                        pltpu.VMEM((N_HEADS, HEAD, ROWS), F32),
                        pltpu.VMEM((N_HEADS, HEAD, ROWS), F32),
                        pltpu.VMEM((ROWS, D_A), F32)] + _CONSTANT_SCRATCH,
        compiler_params=pltpu.CompilerParams(dimension_semantics=("arbitrary",),
                                             vmem_limit_bytes=VMEM_LIMIT_BYTES),
        name="decode_layer",
    )(x_sample.reshape(n_rows, D_MODEL), *weights, state_hgrn)

    return (y_p,
            y_s.reshape(dec_batch, DEC_SEQ, D_MODEL),
            st_p,
            st_s,
            vch_p,
            vch_s)
```

```python
import functools

import jax
import jax.numpy as jnp
from jax import lax
from jax.experimental import pallas as pl
from jax.experimental.pallas import tpu as pltpu

F32 = jnp.float32
BF16 = jnp.bfloat16

D_MODEL = 1024
D_A = 512
D_B = 512
HEAD = 128
N_HEADS = 4
D_IN = 4 * D_A + 3 * D_B
ROWS = 128
SUBLANES = 8
PACKED = 2 * SUBLANES
N_TILES = ROWS // SUBLANES
SUBLANE_LEVELS = 3
EPS = 1e-6
PROMPT_TILE = 512
PROJ_BLOCK = 512
HEAD_ROUNDS = 12
CHUNK_STAGGER = 4
HEAD_STAGGER = 1
DEC_SEQ = 8
SEQS_PER_TILE = ROWS // DEC_SEQ
SEQ_GROUP = 4
VMEM_LIMIT_BYTES = 56 * 1024 * 1024

_NT = (((1,), (1,)), ((), ()))


def _silu(x):
    return x * jax.nn.sigmoid(x)


def _gelu(x):
    c = -2.0 * 0.7978845608028654 * 1.4426950408889634
    return x / (1.0 + jnp.exp2(x * (c * 0.044715 * (x * x) + c)))


def _rms_norm(x, g):
    return x * lax.rsqrt(jnp.mean(x * x, axis=-1, keepdims=True) + EPS) * g


def _layer_norm(x, g, b):
    mu = jnp.mean(x, axis=-1, keepdims=True)
    xc = x - mu
    var = jnp.mean(xc * xc, axis=-1, keepdims=True)
    return xc * lax.rsqrt(var + EPS) * g + b


def _lower_bound(lbl):
    rows = [lbl[r:r + 1, :] for r in range(lbl.shape[0])]
    mx = functools.reduce(jnp.maximum, rows)
    es = [jnp.exp(r - mx) for r in rows]
    return es[0] / functools.reduce(lambda a, c: a + c, es)


def _n_levels(seg_len):
    n_levels = seg_len.bit_length() - 1
    assert seg_len == 1 << n_levels and n_levels >= SUBLANE_LEVELS
    return n_levels


def _split3(x):
    hi = x.astype(BF16)
    r1 = x - hi.astype(F32)
    mid = r1.astype(BF16)
    lo = (r1 - mid.astype(F32)).astype(BF16)
    return hi, mid, lo


def _store_constants(seg_len, ws_ref, bs_ref, wout_ref, lvl_ref, wc_ref, bias_ref, wout16_ref):
    t = lax.broadcasted_iota(jnp.int32, (ROWS, ROWS), 0)
    s = lax.broadcasted_iota(jnp.int32, (ROWS, ROWS), 1)
    n_levels = _n_levels(seg_len)
    x = t ^ s
    lvl = jnp.where(t == s, n_levels, -1)
    for j in range(n_levels):
        lvl = jnp.where(((x >> j) == 1) & (((t >> j) & 1) == 1), j, lvl)
    lvl_ref[...] = lvl
    causal = (t >= s) & ((t >> n_levels) == (s >> n_levels))
    pick = jnp.where(s == (t & (seg_len - 1)), 1.0, 0.0).astype(BF16)
    for hd in range(N_HEADS):
        w = ws_ref[hd].astype(BF16)
        if seg_len != ROWS:
            w = jnp.dot(pick, w, preferred_element_type=F32).astype(BF16)
            w = lax.dot_general(w, pick, _NT, preferred_element_type=F32).astype(BF16)
        wc_ref[hd] = jnp.where(causal, w, jnp.zeros_like(w))
    pad = jnp.zeros((PACKED - N_HEADS, ROWS), F32)
    bias = sum(lax.dot_general(pick, part, _NT, preferred_element_type=F32)
               for part in _split3(jnp.concatenate([bs_ref[...], pad], axis=0)))
    bias_ref[...] = bias[:, :N_HEADS]
    wout16_ref[...] = wout_ref[...].astype(BF16)


def _row_bits():
    rowi = lax.broadcasted_iota(jnp.int32, (SUBLANES, HEAD), 0)
    return [((rowi >> j) & 1) == 1 for j in range(SUBLANE_LEVELS)]


def _hgrn_head(gates, lvl_ref, bits, n_levels):
    one = jnp.ones((SUBLANES, HEAD), F32)
    z = [[] for _ in range(n_levels)]
    qp_t, ks_t, blk_t, diag_t = [], [], [], []
    for pair in range(N_TILES // 2):
        z_pair = [[] for _ in range(SUBLANE_LEVELS)]
        for i in (2 * pair, 2 * pair + 1):
            q, kk, f = gates(i)
            diag_t.append(jnp.sum(q * kk, axis=-1, keepdims=True))
            qp, ks, blk = q * f, kk, f
            for j in range(SUBLANE_LEVELS):
                m = 1 << j
                z_pair[j].append(jnp.where(bits[j], qp, ks))
                sib = pltpu.roll(blk, m, 0)
                if 2 * m != SUBLANES:
                    sib = jnp.where(bits[j], sib, pltpu.roll(blk, SUBLANES - m, 0))
                qp = qp * jnp.where(bits[j], sib, one)
                ks = ks * jnp.where(bits[j], one, sib)
                blk = blk * sib
            qp_t.append(qp)
            ks_t.append(ks)
            blk_t.append(blk)
        for j in range(SUBLANE_LEVELS):
            z[j].append(jnp.concatenate(z_pair[j], axis=0).astype(BF16))
    yield

    blk_levels, blk_b = [], blk_t
    for j in range(SUBLANE_LEVELS, n_levels):
        blk_levels.append(blk_b)
        blk_b = [blk_b[2 * n] * blk_b[2 * n + 1] for n in range(len(blk_b) // 2)]
    for pair in range(N_TILES // 2):
        z_pair = [[] for _ in range(SUBLANE_LEVELS, n_levels)]
        for i in (2 * pair, 2 * pair + 1):
            for j in range(SUBLANE_LEVELS, n_levels):
                block = i >> (j - SUBLANE_LEVELS)
                sib = blk_levels[j - SUBLANE_LEVELS][block ^ 1]
                if block & 1:
                    z_pair[j - SUBLANE_LEVELS].append(qp_t[i])
                    qp_t[i] = qp_t[i] * sib
                else:
                    z_pair[j - SUBLANE_LEVELS].append(ks_t[i])
                    ks_t[i] = ks_t[i] * sib
        for j in range(SUBLANE_LEVELS, n_levels):
            z[j].append(jnp.concatenate(z_pair[j - SUBLANE_LEVELS], axis=0).astype(BF16))
    q_in = jnp.concatenate(qp_t, axis=0)
    k_out = jnp.concatenate(ks_t, axis=0)
    per_tile = N_TILES // len(blk_b)
    total = jnp.concatenate([blk_b[i // per_tile] for i in range(N_TILES)], axis=0)
    yield

    p, p_row = [], []
    for j in range(n_levels):
        zj = jnp.concatenate(z[j], axis=0)
        span = 1 << max(j - SUBLANE_LEVELS, 0)
        if span * SUBLANES >= PACKED:
            upper = [i for i in range(N_TILES) if (i // span) & 1]
            lhs = jnp.concatenate([z[j][i // 2] for i in upper[::2]], axis=0)
        else:
            upper = list(range(N_TILES))
            lhs = zj
        p.append(lax.dot_general(lhs, zj, _NT, preferred_element_type=F32))
        p_row.append({i: n * SUBLANES for n, i in enumerate(upper)})
        yield
    sc_t = []
    for i in range(N_TILES):
        lvl_i = lvl_ref[i * SUBLANES:(i + 1) * SUBLANES, :]
        acc = jnp.where(lvl_i == n_levels, diag_t[i], 0.0)
        for j in range(n_levels):
            if j < SUBLANE_LEVELS or (i >> (j - SUBLANE_LEVELS)) & 1:
                r0 = p_row[j][i]
                acc = jnp.where(lvl_i == j, p[j][r0:r0 + SUBLANES, :], acc)
        sc_t.append(acc)
    sc = jnp.concatenate(sc_t, axis=0).astype(BF16)
    return sc, q_in, k_out, total


def _lockstep(generators, starts=None):
    starts = starts or [0] * len(generators)
    results = [None] * len(generators)
    waiting = sorted(range(len(generators)), key=lambda i: starts[i])
    active = []
    rnd = 0
    while waiting or active:
        while waiting and starts[waiting[0]] <= rnd:
            active.append(waiting.pop(0))
        still = []
        for idx in active:
            try:
                next(generators[idx])
                still.append(idx)
            except StopIteration as done:
                results[idx] = done.value
        active = still
        rnd += 1
    return results


def _hgrn_gates(pq, pf, lb):
    q = _silu(pq)
    forget = lb + (1.0 - lb) * jax.nn.sigmoid(pf)
    return q, 1.0 - forget, forget


def _hgrn_finish(o, pz, g):
    return (_rms_norm(o, g) * _silu(pz)).astype(BF16)


def _mlp_head(hd, u, v_n16, gate, wc_ref, bias):
    mix = jnp.dot(wc_ref[hd], v_n16, preferred_element_type=F32) + bias[:, hd:hd + 1]
    return (_gelu(u) * mix * _silu(gate)).astype(BF16)


def _prompt_kernel(x_ref, xprev_ref, ng_ref, win_ref, lbl_ref, hg_ref, lng_ref, lnb_ref, ws_ref,
                   bs_ref, wout_ref, fg_ref, y_ref, st_ref, vch_ref, proj_a, proj_b, h_ref,
                   mixed_ref, lvl_ref, wc_ref, bias_ref, wout16_ref, *, tiles_per_seq):
    g = pl.program_id(0)

    @pl.when(g == 0)
    def _():
        proj_b[...] = jnp.zeros_like(proj_b)
        _store_constants(ROWS, ws_ref, bs_ref, wout_ref, lvl_ref, wc_ref, bias_ref, wout16_ref)

    @pl.when(lax.rem(jnp.maximum(g - 1, 0), tiles_per_seq) == 0)
    def _():
        st_ref[...] = jnp.zeros_like(st_ref)

    def step(proj_next, proj_prev):
        h_ref[...] = _rms_norm(x_ref[0], ng_ref[...]).astype(BF16)
        lb = _lower_bound(lbl_ref[...])
        bits, n_levels = _row_bits(), _n_levels(ROWS)
        hg = hg_ref[...]
        lng = lng_ref[...]
        lnb = lnb_ref[...]
        bias = bias_ref[...]
        n_chunks = PROMPT_TILE // ROWS
        n_blocks = D_IN // PROJ_BLOCK

        def col(c, k, hs=slice(0, D_A)):
            return proj_prev[c * ROWS:(c + 1) * ROWS, k * D_A + hs.start:k * D_A + hs.stop]

        def head(c, hd):
            hs = slice(hd * HEAD, (hd + 1) * HEAD)

            def gates(i):
                rows = slice(c * ROWS + i * SUBLANES, c * ROWS + (i + 1) * SUBLANES)
                return _hgrn_gates(proj_prev[rows, hs],
                                   proj_prev[rows, D_A + hd * HEAD:D_A + (hd + 1) * HEAD],
                                   lb[:, hs])

            sc, q_in, k_out, total = yield from _hgrn_head(gates, lvl_ref, bits, n_levels)
            state = st_ref[0, 0, hd]
            vh = col(c, 2, hs)
            o = jnp.dot(jnp.concatenate([sc, q_in.astype(BF16)], axis=1),
                        jnp.concatenate([vh.astype(BF16), state.astype(BF16)], axis=0),
                        preferred_element_type=F32)
            yield
            st_ref[0, 0, hd] = total.T * state + jnp.dot(
                k_out.T.astype(BF16), vh.astype(BF16), preferred_element_type=F32)
            yield
            mixed_ref[c * ROWS:(c + 1) * ROWS, hs] = _hgrn_finish(o, col(c, 3, hs), hg[:, hs])

        def mlp(c):
            v_n = _layer_norm(_gelu(col(c, 5)), lng, lnb)
            if c == n_chunks - 1:
                for hd in range(N_HEADS):
                    vch_ref[0, 0, :, hd, :] = v_n[:, hd * HEAD:(hd + 1) * HEAD]
            v_n16 = v_n.astype(BF16)
            yield
            for hd in range(N_HEADS):
                hs = slice(hd * HEAD, (hd + 1) * HEAD)
                mixed_ref[c * ROWS:(c + 1) * ROWS, D_A + hd * HEAD:D_A + (hd + 1) * HEAD] = (
                    _mlp_head(hd, col(c, 4, hs), v_n16[:, hs], col(c, 6, hs), wc_ref, bias))
                yield
                yield

        def projection(rounds_per_block):
            for blk in range(n_blocks):
                cols = slice(blk * PROJ_BLOCK, (blk + 1) * PROJ_BLOCK)
                proj_next[:, cols] = jnp.dot(h_ref[...], win_ref[:, cols],
                                             preferred_element_type=F32)
                for _ in range(rounds_per_block):
                    yield

        def finish(c):
            rows = slice(c * ROWS, (c + 1) * ROWS)
            out = xprev_ref[0, rows, :] + jnp.dot(mixed_ref[rows, :], wout16_ref[...],
                                                  preferred_element_type=F32)
            yield
            y_ref[0, rows, :] = _rms_norm(out, fg)

        fg = fg_ref[...]
        items = [gen for c in range(n_chunks)
                 for gen in [head(c, hd) for hd in range(N_HEADS)] + [mlp(c), finish(c)]]
        chunk_rounds = (N_HEADS - 1) * HEAD_STAGGER + HEAD_ROUNDS
        starts = [c * CHUNK_STAGGER + k
                  for c in range(n_chunks)
                  for k in [hd * HEAD_STAGGER for hd in range(N_HEADS)] + [0, chunk_rounds]]
        rounds = (n_chunks - 1) * CHUNK_STAGGER + chunk_rounds
        _lockstep(items + [projection(max(1, rounds // n_blocks))], starts + [0])

    parity = lax.rem(g, 2)

    @pl.when(parity == 0)
    def _():
        step(proj_a, proj_b)

    @pl.when(parity == 1)
    def _():
        step(proj_b, proj_a)


def _sample_kernel(x_ref, ng_ref, win_ref, lbl_ref, hg_ref, lng_ref, lnb_ref, ws_ref, bs_ref,
                   wout_ref, fg_ref, stin_ref, y_ref, stout_ref, vch_ref,
                   qin_ref, kot_ref, tot_ref, oint_ref, lvl_ref, wc_ref, bias_ref, wout16_ref):
    @pl.when(pl.program_id(0) == 0)
    def _():
        _store_constants(DEC_SEQ, ws_ref, bs_ref, wout_ref, lvl_ref, wc_ref, bias_ref, wout16_ref)

    x = x_ref[...]
    h = _rms_norm(x, ng_ref[...]).astype(BF16)
    proj = jnp.dot(h, win_ref[...], preferred_element_type=F32)

    def col(k, hs=slice(0, D_A)):
        return proj[:, k * D_A + hs.start:k * D_A + hs.stop]

    lb = _lower_bound(lbl_ref[...])
    bits, n_levels = _row_bits(), _n_levels(DEC_SEQ)
    vi16 = col(2).astype(BF16)

    def head(hd):
        hs = slice(hd * HEAD, (hd + 1) * HEAD)

        def gates(i):
            rows = slice(i * SUBLANES, (i + 1) * SUBLANES)
            return _hgrn_gates(proj[rows, hs], proj[rows, D_A + hd * HEAD:D_A + (hd + 1) * HEAD],
                               lb[:, hs])

        sc, q_in, k_out, total = yield from _hgrn_head(gates, lvl_ref, bits, n_levels)
        qin_ref[:, hs] = q_in
        kot_ref[hd] = k_out.T
        tot_ref[hd] = total.T
        return jnp.dot(sc, vi16[:, hs], preferred_element_type=F32)

    o_intra = _lockstep([head(hd) for hd in range(N_HEADS)])

    lane = lax.broadcasted_iota(jnp.int32, (HEAD, ROWS), 1)

    def seq_head(i, hd):
        hs = slice(hd * HEAD, (hd + 1) * HEAD)
        seq_rows = slice(i * DEC_SEQ, (i + 1) * DEC_SEQ)
        state = stin_ref[0, i, hd]
        qi = qin_ref[seq_rows, hs].astype(BF16)
        oint_ref[seq_rows, hs] = jnp.dot(qi, state.astype(BF16), preferred_element_type=F32)
        yield
        in_seq = (lane >> (DEC_SEQ.bit_length() - 1)) == i
        k_seq = jnp.where(in_seq, kot_ref[hd], 0.0).astype(BF16)
        decay = tot_ref[hd, :, i * DEC_SEQ:i * DEC_SEQ + 1]
        yield
        stout_ref[0, i, hd] = decay * state + jnp.dot(
            k_seq, vi16[:, hs], preferred_element_type=F32)

    for grp in range(SEQS_PER_TILE // SEQ_GROUP):
        _lockstep([seq_head(grp * SEQ_GROUP + s, hd)
                   for s in range(SEQ_GROUP) for hd in range(N_HEADS)])

    hg = hg_ref[...]
    bias = bias_ref[...]
    v_n = _layer_norm(_gelu(col(5)), lng_ref[...], lnb_ref[...])
    for hd in range(N_HEADS):
        vch_ref[0, :, :, hd, :] = v_n[:, hd * HEAD:(hd + 1) * HEAD].reshape(
            SEQS_PER_TILE, DEC_SEQ, HEAD)
    v_n16 = v_n.astype(BF16)
    mixed = []
    for hd in range(N_HEADS):
        hs = slice(hd * HEAD, (hd + 1) * HEAD)
        mixed.append(_hgrn_finish(o_intra[hd] + oint_ref[:, hs], col(3, hs), hg[:, hs]))
    for hd in range(N_HEADS):
        hs = slice(hd * HEAD, (hd + 1) * HEAD)
        mixed.append(_mlp_head(hd, col(4, hs), v_n16[:, hs], col(6, hs), wc_ref, bias))
    out = x + jnp.dot(jnp.concatenate(mixed, axis=-1), wout16_ref[...],
                      preferred_element_type=F32)
    y_ref[...] = _rms_norm(out, fg_ref[...])


def _const_spec(shape):
    return pl.BlockSpec(shape, lambda *_: (0,) * len(shape), pipeline_mode=pl.Buffered(1))


_WEIGHT_SPECS = [
    _const_spec((1, D_MODEL)),
    _const_spec((D_MODEL, D_IN)),
    _const_spec((2, D_A)),
    _const_spec((1, D_A)),
    _const_spec((1, D_B)),
    _const_spec((1, D_B)),
    _const_spec((N_HEADS, ROWS, ROWS)),
    _const_spec((N_HEADS, ROWS)),
    _const_spec((D_MODEL, D_MODEL)),
    _const_spec((1, D_MODEL)),
]

_CONSTANT_SCRATCH = [
    pltpu.VMEM((ROWS, ROWS), jnp.int32),
    pltpu.VMEM((N_HEADS, ROWS, ROWS), BF16),
    pltpu.VMEM((ROWS, N_HEADS), F32),
    pltpu.VMEM((D_MODEL, D_MODEL), BF16),
]


def kernel(x_prompt, x_sample, state_hgrn, norm_g, w_in, lb_logits, hgrn_norm_g, sgu_ln_g,
           sgu_ln_b, w_s, b_s, w_out, final_norm_g):
    depth = norm_g.shape[0]
    assert depth == 1 and lb_logits.shape == (2, D_A)
    batch, seq, _ = x_prompt.shape
    dec_batch, dec_seq, _ = x_sample.shape
    assert seq % PROMPT_TILE == 0 and dec_seq == DEC_SEQ and dec_batch % SEQS_PER_TILE == 0

    weights = (norm_g, w_in[0].astype(BF16), lb_logits, hgrn_norm_g, sgu_ln_g, sgu_ln_b,
               w_s[0], b_s[0], w_out[0], final_norm_g.reshape(1, D_MODEL))
    params = pltpu.CompilerParams(dimension_semantics=("arbitrary",),
                                  vmem_limit_bytes=VMEM_LIMIT_BYTES)

    tiles_per_seq = seq // PROMPT_TILE
    n_tiles = batch * tiles_per_seq

    def cur_tile(g):
        t = jnp.minimum(g, n_tiles - 1)
        return t // tiles_per_seq, t % tiles_per_seq

    def prev_tile(g):
        t = jnp.maximum(g - 1, 0)
        return t // tiles_per_seq, t % tiles_per_seq

    y_p, st_p, vch_p = pl.pallas_call(
        functools.partial(_prompt_kernel, tiles_per_seq=tiles_per_seq),
        grid=(n_tiles + 1,),
        in_specs=[pl.BlockSpec((1, PROMPT_TILE, D_MODEL), lambda g: (*cur_tile(g), 0)),
                  pl.BlockSpec((1, PROMPT_TILE, D_MODEL), lambda g: (*prev_tile(g), 0))]
        + _WEIGHT_SPECS,
        out_specs=[
            pl.BlockSpec((1, PROMPT_TILE, D_MODEL), lambda g: (*prev_tile(g), 0)),
            pl.BlockSpec((1, 1, N_HEADS, HEAD, HEAD), lambda g: (0, prev_tile(g)[0], 0, 0, 0)),
            pl.BlockSpec((1, 1, ROWS, N_HEADS, HEAD), lambda g: (0, prev_tile(g)[0], 0, 0, 0)),
        ],
        out_shape=[
            jax.ShapeDtypeStruct((batch, seq, D_MODEL), F32),
            jax.ShapeDtypeStruct((1, batch, N_HEADS, HEAD, HEAD), F32),
            jax.ShapeDtypeStruct((1, batch, ROWS, N_HEADS, HEAD), F32),
        ],
        scratch_shapes=[pltpu.VMEM((PROMPT_TILE, D_IN), F32),
                        pltpu.VMEM((PROMPT_TILE, D_IN), F32),
                        pltpu.VMEM((PROMPT_TILE, D_MODEL), BF16),
                        pltpu.VMEM((PROMPT_TILE, D_MODEL), BF16)] + _CONSTANT_SCRATCH,
        compiler_params=params,
        name="prompt_layer",
    )(x_prompt, x_prompt, *weights)

    n_rows = dec_batch * DEC_SEQ
    y_s, st_s, vch_s = pl.pallas_call(
        _sample_kernel,
        grid=(n_rows // ROWS,),
        in_specs=[pl.BlockSpec((ROWS, D_MODEL), lambda i: (i, 0))]
        + _WEIGHT_SPECS
        + [pl.BlockSpec((1, SEQS_PER_TILE, N_HEADS, HEAD, HEAD), lambda i: (0, i, 0, 0, 0))],
        out_specs=[
            pl.BlockSpec((ROWS, D_MODEL), lambda i: (i, 0)),
            pl.BlockSpec((1, SEQS_PER_TILE, N_HEADS, HEAD, HEAD), lambda i: (0, i, 0, 0, 0)),
            pl.BlockSpec((1, SEQS_PER_TILE, DEC_SEQ, N_HEADS, HEAD), lambda i: (0, i, 0, 0, 0)),
        ],
        out_shape=[
            jax.ShapeDtypeStruct((n_rows, D_MODEL), F32),
            jax.ShapeDtypeStruct((1, dec_batch, N_HEADS, HEAD, HEAD), F32),
            jax.ShapeDtypeStruct((1, dec_batch, DEC_SEQ, N_HEADS, HEAD), F32),
        ],
        scratch_shapes=[pltpu.VMEM((ROWS, D_A), F32),
                        pltpu.VMEM((N_HEADS, HEAD, ROWS), F32),
                        pltpu.VMEM((N_HEADS, HEAD, ROWS), F32),
                        pltpu.VMEM((ROWS, D_A), F32)] + _CONSTANT_SCRATCH,
        compiler_params=pltpu.CompilerParams(dimension_semantics=("arbitrary",),
                                             vmem_limit_bytes=VMEM_LIMIT_BYTES),
        name="decode_layer",
    )(x_sample.reshape(n_rows, D_MODEL), *weights, state_hgrn)

    return (y_p,
            y_s.reshape(dec_batch, DEC_SEQ, D_MODEL),
            st_p,
            st_s,
            vch_p,
            vch_s)
```

```python
import functools

import jax
import jax.numpy as jnp
from jax import lax
from jax.experimental import pallas as pl
from jax.experimental.pallas import tpu as pltpu

F32 = jnp.float32
BF16 = jnp.bfloat16

D_MODEL = 1024
D_A = 512
D_B = 512
HEAD = 128
N_HEADS = 4
D_IN = 4 * D_A + 3 * D_B
ROWS = 128
SUBLANES = 8
PACKED = 2 * SUBLANES
N_TILES = ROWS // SUBLANES
SUBLANE_LEVELS = 3
EPS = 1e-6
PROMPT_TILE = 512
PROJ_BLOCK = 512
HEAD_ROUNDS = 12
CHUNK_STAGGER = 4
HEAD_STAGGER = 1
FINISH_CHUNKS = 2
DEC_SEQ = 8
SEQS_PER_TILE = ROWS // DEC_SEQ
SEQ_GROUP = 4
VMEM_LIMIT_BYTES = 56 * 1024 * 1024

_NT = (((1,), (1,)), ((), ()))


def _silu(x):
    return x * jax.nn.sigmoid(x)


def _gelu(x):
    c = -2.0 * 0.7978845608028654 * 1.4426950408889634
    return x / (1.0 + jnp.exp2(x * (c * 0.044715 * (x * x) + c)))


def _rms_norm(x, g):
    return x * lax.rsqrt(jnp.mean(x * x, axis=-1, keepdims=True) + EPS) * g


def _layer_norm(x, g, b):
    mu = jnp.mean(x, axis=-1, keepdims=True)
    xc = x - mu
    var = jnp.mean(xc * xc, axis=-1, keepdims=True)
    return xc * lax.rsqrt(var + EPS) * g + b


def _lower_bound(lbl):
    rows = [lbl[r:r + 1, :] for r in range(lbl.shape[0])]
    mx = functools.reduce(jnp.maximum, rows)
    es = [jnp.exp(r - mx) for r in rows]
    return es[0] / functools.reduce(lambda a, c: a + c, es)


def _n_levels(seg_len):
    n_levels = seg_len.bit_length() - 1
    assert seg_len == 1 << n_levels and n_levels >= SUBLANE_LEVELS
    return n_levels


def _split3(x):
    hi = x.astype(BF16)
    r1 = x - hi.astype(F32)
    mid = r1.astype(BF16)
    lo = (r1 - mid.astype(F32)).astype(BF16)
    return hi, mid, lo


def _store_constants(seg_len, ws_ref, bs_ref, wout_ref, lvl_ref, wc_ref, bias_ref, wout16_ref):
    t = lax.broadcasted_iota(jnp.int32, (ROWS, ROWS), 0)
    s = lax.broadcasted_iota(jnp.int32, (ROWS, ROWS), 1)
    n_levels = _n_levels(seg_len)
    x = t ^ s
    lvl = jnp.where(t == s, n_levels, -1)
    for j in range(n_levels):
        lvl = jnp.where(((x >> j) == 1) & (((t >> j) & 1) == 1), j, lvl)
    lvl_ref[...] = lvl
    causal = (t >= s) & ((t >> n_levels) == (s >> n_levels))
    pick = jnp.where(s == (t & (seg_len - 1)), 1.0, 0.0).astype(BF16)
    for hd in range(N_HEADS):
        w = ws_ref[hd].astype(BF16)
        if seg_len != ROWS:
            w = jnp.dot(pick, w, preferred_element_type=F32).astype(BF16)
            w = lax.dot_general(w, pick, _NT, preferred_element_type=F32).astype(BF16)
        wc_ref[hd] = jnp.where(causal, w, jnp.zeros_like(w))
    pad = jnp.zeros((PACKED - N_HEADS, ROWS), F32)
    bias = sum(lax.dot_general(pick, part, _NT, preferred_element_type=F32)
               for part in _split3(jnp.concatenate([bs_ref[...], pad], axis=0)))
    bias_ref[...] = bias[:, :N_HEADS]
    wout16_ref[...] = wout_ref[...].astype(BF16)


def _row_bits():
    rowi = lax.broadcasted_iota(jnp.int32, (SUBLANES, HEAD), 0)
    return [((rowi >> j) & 1) == 1 for j in range(SUBLANE_LEVELS)]


def _hgrn_head(gates, lvl_ref, bits, n_levels):
    one = jnp.ones((SUBLANES, HEAD), F32)
    z = [[] for _ in range(n_levels)]
    qp_t, ks_t, blk_t, diag_t = [], [], [], []
    for pair in range(N_TILES // 2):
        z_pair = [[] for _ in range(SUBLANE_LEVELS)]
        for i in (2 * pair, 2 * pair + 1):
            q, kk, f = gates(i)
            diag_t.append(jnp.sum(q * kk, axis=-1, keepdims=True))
            qp, ks, blk = q * f, kk, f
            for j in range(SUBLANE_LEVELS):
                m = 1 << j
                z_pair[j].append(jnp.where(bits[j], qp, ks))
                sib = pltpu.roll(blk, m, 0)
                if 2 * m != SUBLANES:
                    sib = jnp.where(bits[j], sib, pltpu.roll(blk, SUBLANES - m, 0))
                qp = qp * jnp.where(bits[j], sib, one)
                ks = ks * jnp.where(bits[j], one, sib)
                blk = blk * sib
            qp_t.append(qp)
            ks_t.append(ks)
            blk_t.append(blk)
        for j in range(SUBLANE_LEVELS):
            z[j].append(jnp.concatenate(z_pair[j], axis=0).astype(BF16))
    yield

    blk_levels, blk_b = [], blk_t
    for j in range(SUBLANE_LEVELS, n_levels):
        blk_levels.append(blk_b)
        blk_b = [blk_b[2 * n] * blk_b[2 * n + 1] for n in range(len(blk_b) // 2)]
    for pair in range(N_TILES // 2):
        z_pair = [[] for _ in range(SUBLANE_LEVELS, n_levels)]
        for i in (2 * pair, 2 * pair + 1):
            for j in range(SUBLANE_LEVELS, n_levels):
                block = i >> (j - SUBLANE_LEVELS)
                sib = blk_levels[j - SUBLANE_LEVELS][block ^ 1]
                if block & 1:
                    z_pair[j - SUBLANE_LEVELS].append(qp_t[i])
                    qp_t[i] = qp_t[i] * sib
                else:
                    z_pair[j - SUBLANE_LEVELS].append(ks_t[i])
                    ks_t[i] = ks_t[i] * sib
        for j in range(SUBLANE_LEVELS, n_levels):
            z[j].append(jnp.concatenate(z_pair[j - SUBLANE_LEVELS], axis=0).astype(BF16))
    q_in = jnp.concatenate(qp_t, axis=0)
    k_out = jnp.concatenate(ks_t, axis=0)
    per_tile = N_TILES // len(blk_b)
    total = jnp.concatenate([blk_b[i // per_tile] for i in range(N_TILES)], axis=0)
    yield

    p, p_row = [], []
    for j in range(n_levels):
        zj = jnp.concatenate(z[j], axis=0)
        span = 1 << max(j - SUBLANE_LEVELS, 0)
        if span * SUBLANES >= PACKED:
            upper = [i for i in range(N_TILES) if (i // span) & 1]
            lhs = jnp.concatenate([z[j][i // 2] for i in upper[::2]], axis=0)
        else:
            upper = list(range(N_TILES))
            lhs = zj
        p.append(lax.dot_general(lhs, zj, _NT, preferred_element_type=F32))
        p_row.append({i: n * SUBLANES for n, i in enumerate(upper)})
        yield
    sc_t = []
    for i in range(N_TILES):
        lvl_i = lvl_ref[i * SUBLANES:(i + 1) * SUBLANES, :]
        acc = jnp.where(lvl_i == n_levels, diag_t[i], 0.0)
        for j in range(n_levels):
            if j < SUBLANE_LEVELS or (i >> (j - SUBLANE_LEVELS)) & 1:
                r0 = p_row[j][i]
                acc = jnp.where(lvl_i == j, p[j][r0:r0 + SUBLANES, :], acc)
        sc_t.append(acc)
    sc = jnp.concatenate(sc_t, axis=0).astype(BF16)
    return sc, q_in, k_out, total


def _lockstep(generators, starts=None):
    starts = starts or [0] * len(generators)
    results = [None] * len(generators)
    waiting = sorted(range(len(generators)), key=lambda i: starts[i])
    active = []
    rnd = 0
    while waiting or active:
        while waiting and starts[waiting[0]] <= rnd:
            active.append(waiting.pop(0))
        still = []
        for idx in active:
            try:
                next(generators[idx])
                still.append(idx)
            except StopIteration as done:
                results[idx] = done.value
        active = still
        rnd += 1
    return results


def _hgrn_gates(pq, pf, lb):
    q = _silu(pq)
    forget = lb + (1.0 - lb) * jax.nn.sigmoid(pf)
    return q, 1.0 - forget, forget


def _hgrn_finish(o, pz, g):
    return (_rms_norm(o, g) * _silu(pz)).astype(BF16)


def _mlp_head(hd, u, v_n16, gate, wc_ref, bias):
    mix = jnp.dot(wc_ref[hd], v_n16, preferred_element_type=F32) + bias[:, hd:hd + 1]
    return (_gelu(u) * mix * _silu(gate)).astype(BF16)


def _prompt_kernel(x_ref, xprev_ref, ng_ref, win_ref, lbl_ref, hg_ref, lng_ref, lnb_ref, ws_ref,
                   bs_ref, wout_ref, fg_ref, y_ref, st_ref, vch_ref, proj_a, proj_b, h_ref,
                   mixed_ref, lvl_ref, wc_ref, bias_ref, wout16_ref, *, tiles_per_seq):
    g = pl.program_id(0)

    @pl.when(g == 0)
    def _():
        proj_b[...] = jnp.zeros_like(proj_b)
        _store_constants(ROWS, ws_ref, bs_ref, wout_ref, lvl_ref, wc_ref, bias_ref, wout16_ref)

    @pl.when(lax.rem(jnp.maximum(g - 1, 0), tiles_per_seq) == 0)
    def _():
        st_ref[...] = jnp.zeros_like(st_ref)

    def step(proj_next, proj_prev):
        h_ref[...] = _rms_norm(x_ref[0], ng_ref[...]).astype(BF16)
        lb = _lower_bound(lbl_ref[...])
        bits, n_levels = _row_bits(), _n_levels(ROWS)
        hg = hg_ref[...]
        lng = lng_ref[...]
        lnb = lnb_ref[...]
        bias = bias_ref[...]
        n_chunks = PROMPT_TILE // ROWS
        n_blocks = D_IN // PROJ_BLOCK

        def col(c, k, hs=slice(0, D_A)):
            return proj_prev[c * ROWS:(c + 1) * ROWS, k * D_A + hs.start:k * D_A + hs.stop]

        def head(c, hd):
            hs = slice(hd * HEAD, (hd + 1) * HEAD)

            def gates(i):
                rows = slice(c * ROWS + i * SUBLANES, c * ROWS + (i + 1) * SUBLANES)
                return _hgrn_gates(proj_prev[rows, hs],
                                   proj_prev[rows, D_A + hd * HEAD:D_A + (hd + 1) * HEAD],
                                   lb[:, hs])

            sc, q_in, k_out, total = yield from _hgrn_head(gates, lvl_ref, bits, n_levels)
            state = st_ref[0, 0, hd]
            vh = col(c, 2, hs)
            o = jnp.dot(jnp.concatenate([sc, q_in.astype(BF16)], axis=1),
                        jnp.concatenate([vh.astype(BF16), state.astype(BF16)], axis=0),
                        preferred_element_type=F32)
            yield
            st_ref[0, 0, hd] = total.T * state + jnp.dot(
                k_out.T.astype(BF16), vh.astype(BF16), preferred_element_type=F32)
            yield
            mixed_ref[c * ROWS:(c + 1) * ROWS, hs] = _hgrn_finish(o, col(c, 3, hs), hg[:, hs])

        def mlp(c):
            v_n = _layer_norm(_gelu(col(c, 5)), lng, lnb)
            if c == n_chunks - 1:
                for hd in range(N_HEADS):
                    vch_ref[0, 0, :, hd, :] = v_n[:, hd * HEAD:(hd + 1) * HEAD]
            v_n16 = v_n.astype(BF16)
            yield
            for hd in range(N_HEADS):
                hs = slice(hd * HEAD, (hd + 1) * HEAD)
                mixed_ref[c * ROWS:(c + 1) * ROWS, D_A + hd * HEAD:D_A + (hd + 1) * HEAD] = (
                    _mlp_head(hd, col(c, 4, hs), v_n16[:, hs], col(c, 6, hs), wc_ref, bias))
                yield
                yield

        def projection(rounds_per_block):
            for blk in range(n_blocks):
                cols = slice(blk * PROJ_BLOCK, (blk + 1) * PROJ_BLOCK)
                proj_next[:, cols] = jnp.dot(h_ref[...], win_ref[:, cols],
                                             preferred_element_type=F32)
                for _ in range(rounds_per_block):
                    yield

        def finish(first, n):
            rows = slice(first * ROWS, (first + n) * ROWS)
            out = xprev_ref[0, rows, :] + jnp.dot(mixed_ref[rows, :], wout16_ref[...],
                                                  preferred_element_type=F32)
            yield
            y_ref[0, rows, :] = _rms_norm(out, fg)

        fg = fg_ref[...]
        chunk_rounds = (N_HEADS - 1) * HEAD_STAGGER + HEAD_ROUNDS
        items, starts = [], []
        for c in range(n_chunks):
            items += [head(c, hd) for hd in range(N_HEADS)] + [mlp(c)]
            starts += [c * CHUNK_STAGGER + hd * HEAD_STAGGER for hd in range(N_HEADS)]
            starts += [c * CHUNK_STAGGER]
            if c % FINISH_CHUNKS == FINISH_CHUNKS - 1:
                items.append(finish(c + 1 - FINISH_CHUNKS, FINISH_CHUNKS))
                starts.append(c * CHUNK_STAGGER + chunk_rounds)
        rounds = (n_chunks - 1) * CHUNK_STAGGER + chunk_rounds
        _lockstep(items + [projection(max(1, rounds // n_blocks))], starts + [0])

    parity = lax.rem(g, 2)

    @pl.when(parity == 0)
    def _():
        step(proj_a, proj_b)

    @pl.when(parity == 1)
    def _():
        step(proj_b, proj_a)


def _sample_kernel(x_ref, ng_ref, win_ref, lbl_ref, hg_ref, lng_ref, lnb_ref, ws_ref, bs_ref,
                   wout_ref, fg_ref, stin_ref, y_ref, stout_ref, vch_ref,
                   qin_ref, kot_ref, tot_ref, oint_ref, lvl_ref, wc_ref, bias_ref, wout16_ref):
    @pl.when(pl.program_id(0) == 0)
    def _():
        _store_constants(DEC_SEQ, ws_ref, bs_ref, wout_ref, lvl_ref, wc_ref, bias_ref, wout16_ref)

    x = x_ref[...]
    h = _rms_norm(x, ng_ref[...]).astype(BF16)
    proj = jnp.dot(h, win_ref[...], preferred_element_type=F32)

    def col(k, hs=slice(0, D_A)):
        return proj[:, k * D_A + hs.start:k * D_A + hs.stop]

    lb = _lower_bound(lbl_ref[...])
    bits, n_levels = _row_bits(), _n_levels(DEC_SEQ)
    vi16 = col(2).astype(BF16)

    def head(hd):
        hs = slice(hd * HEAD, (hd + 1) * HEAD)

        def gates(i):
            rows = slice(i * SUBLANES, (i + 1) * SUBLANES)
            return _hgrn_gates(proj[rows, hs], proj[rows, D_A + hd * HEAD:D_A + (hd + 1) * HEAD],
                               lb[:, hs])

        sc, q_in, k_out, total = yield from _hgrn_head(gates, lvl_ref, bits, n_levels)
        qin_ref[:, hs] = q_in
        kot_ref[hd] = k_out.T
        tot_ref[hd] = total.T
        return jnp.dot(sc, vi16[:, hs], preferred_element_type=F32)

    o_intra = _lockstep([head(hd) for hd in range(N_HEADS)])

    lane = lax.broadcasted_iota(jnp.int32, (HEAD, ROWS), 1)

    def seq_head(i, hd):
        hs = slice(hd * HEAD, (hd + 1) * HEAD)
        seq_rows = slice(i * DEC_SEQ, (i + 1) * DEC_SEQ)
        state = stin_ref[0, i, hd]
        qi = qin_ref[seq_rows, hs].astype(BF16)
        oint_ref[seq_rows, hs] = jnp.dot(qi, state.astype(BF16), preferred_element_type=F32)
        yield
        in_seq = (lane >> (DEC_SEQ.bit_length() - 1)) == i
        k_seq = jnp.where(in_seq, kot_ref[hd], 0.0).astype(BF16)
        decay = tot_ref[hd, :, i * DEC_SEQ:i * DEC_SEQ + 1]
        yield
        stout_ref[0, i, hd] = decay * state + jnp.dot(
            k_seq, vi16[:, hs], preferred_element_type=F32)

    for grp in range(SEQS_PER_TILE // SEQ_GROUP):
        _lockstep([seq_head(grp * SEQ_GROUP + s, hd)
                   for s in range(SEQ_GROUP) for hd in range(N_HEADS)])

    hg = hg_ref[...]
    bias = bias_ref[...]
    v_n = _layer_norm(_gelu(col(5)), lng_ref[...], lnb_ref[...])
    for hd in range(N_HEADS):
        vch_ref[0, :, :, hd, :] = v_n[:, hd * HEAD:(hd + 1) * HEAD].reshape(
            SEQS_PER_TILE, DEC_SEQ, HEAD)
    v_n16 = v_n.astype(BF16)
    mixed = []
    for hd in range(N_HEADS):
        hs = slice(hd * HEAD, (hd + 1) * HEAD)
        mixed.append(_hgrn_finish(o_intra[hd] + oint_ref[:, hs], col(3, hs), hg[:, hs]))
    for hd in range(N_HEADS):
        hs = slice(hd * HEAD, (hd + 1) * HEAD)
        mixed.append(_mlp_head(hd, col(4, hs), v_n16[:, hs], col(6, hs), wc_ref, bias))
    out = x + jnp.dot(jnp.concatenate(mixed, axis=-1), wout16_ref[...],
                      preferred_element_type=F32)
    y_ref[...] = _rms_norm(out, fg_ref[...])


def _const_spec(shape):
    return pl.BlockSpec(shape, lambda *_: (0,) * len(shape), pipeline_mode=pl.Buffered(1))


_WEIGHT_SPECS = [
    _const_spec((1, D_MODEL)),
    _const_spec((D_MODEL, D_IN)),
    _const_spec((2, D_A)),
    _const_spec((1, D_A)),
    _const_spec((1, D_B)),
    _const_spec((1, D_B)),
    _const_spec((N_HEADS, ROWS, ROWS)),
    _const_spec((N_HEADS, ROWS)),
    _const_spec((D_MODEL, D_MODEL)),
    _const_spec((1, D_MODEL)),
]

_CONSTANT_SCRATCH = [
    pltpu.VMEM((ROWS, ROWS), jnp.int32),
    pltpu.VMEM((N_HEADS, ROWS, ROWS), BF16),
    pltpu.VMEM((ROWS, N_HEADS), F32),
    pltpu.VMEM((D_MODEL, D_MODEL), BF16),
]


def kernel(x_prompt, x_sample, state_hgrn, norm_g, w_in, lb_logits, hgrn_norm_g, sgu_ln_g,
           sgu_ln_b, w_s, b_s, w_out, final_norm_g):
    depth = norm_g.shape[0]
    assert depth == 1 and lb_logits.shape == (2, D_A)
    batch, seq, _ = x_prompt.shape
    dec_batch, dec_seq, _ = x_sample.shape
    assert seq % PROMPT_TILE == 0 and dec_seq == DEC_SEQ and dec_batch % SEQS_PER_TILE == 0

    weights = (norm_g, w_in[0].astype(BF16), lb_logits, hgrn_norm_g, sgu_ln_g, sgu_ln_b,
               w_s[0], b_s[0], w_out[0], final_norm_g.reshape(1, D_MODEL))
    params = pltpu.CompilerParams(dimension_semantics=("arbitrary",),
                                  vmem_limit_bytes=VMEM_LIMIT_BYTES)

    tiles_per_seq = seq // PROMPT_TILE
    n_tiles = batch * tiles_per_seq

    def cur_tile(g):
        t = jnp.minimum(g, n_tiles - 1)
        return t // tiles_per_seq, t % tiles_per_seq

    def prev_tile(g):
        t = jnp.maximum(g - 1, 0)
        return t // tiles_per_seq, t % tiles_per_seq

    y_p, st_p, vch_p = pl.pallas_call(
        functools.partial(_prompt_kernel, tiles_per_seq=tiles_per_seq),
        grid=(n_tiles + 1,),
        in_specs=[pl.BlockSpec((1, PROMPT_TILE, D_MODEL), lambda g: (*cur_tile(g), 0)),
                  pl.BlockSpec((1, PROMPT_TILE, D_MODEL), lambda g: (*prev_tile(g), 0))]
        + _WEIGHT_SPECS,
        out_specs=[
            pl.BlockSpec((1, PROMPT_TILE, D_MODEL), lambda g: (*prev_tile(g), 0)),
            pl.BlockSpec((1, 1, N_HEADS, HEAD, HEAD), lambda g: (0, prev_tile(g)[0], 0, 0, 0)),
            pl.BlockSpec((1, 1, ROWS, N_HEADS, HEAD), lambda g: (0, prev_tile(g)[0], 0, 0, 0)),
        ],
        out_shape=[
            jax.ShapeDtypeStruct((batch, seq, D_MODEL), F32),
            jax.ShapeDtypeStruct((1, batch, N_HEADS, HEAD, HEAD), F32),
            jax.ShapeDtypeStruct((1, batch, ROWS, N_HEADS, HEAD), F32),
        ],
        scratch_shapes=[pltpu.VMEM((PROMPT_TILE, D_IN), F32),
                        pltpu.VMEM((PROMPT_TILE, D_IN), F32),
                        pltpu.VMEM((PROMPT_TILE, D_MODEL), BF16),
                        pltpu.VMEM((PROMPT_TILE, D_MODEL), BF16)] + _CONSTANT_SCRATCH,
        compiler_params=params,
        name="prompt_layer",
    )(x_prompt, x_prompt, *weights)

    n_rows = dec_batch * DEC_SEQ
    y_s, st_s, vch_s = pl.pallas_call(
        _sample_kernel,
        grid=(n_rows // ROWS,),
        in_specs=[pl.BlockSpec((ROWS, D_MODEL), lambda i: (i, 0))]
        + _WEIGHT_SPECS
        + [pl.BlockSpec((1, SEQS_PER_TILE, N_HEADS, HEAD, HEAD), lambda i: (0, i, 0, 0, 0))],
        out_specs=[
            pl.BlockSpec((ROWS, D_MODEL), lambda i: (i, 0)),
            pl.BlockSpec((1, SEQS_PER_TILE, N_HEADS, HEAD, HEAD), lambda i: (0, i, 0, 0, 0)),
            pl.BlockSpec((1, SEQS_PER_TILE, DEC_SEQ, N_HEADS, HEAD), lambda i: (0, i, 0, 0, 0)),
        ],
        out_shape=[
            jax.ShapeDtypeStruct((n_rows, D_MODEL), F32),
            jax.ShapeDtypeStruct((1, dec_batch, N_HEADS, HEAD, HEAD), F32),
            jax.ShapeDtypeStruct((1, dec_batch, DEC_SEQ, N_HEADS, HEAD), F32),
        ],
        scratch_shapes=[pltpu.VMEM((ROWS, D_A), F32),
                        pltpu.VMEM((N_HEADS, HEAD, ROWS), F32),
                        pltpu.VMEM((N_HEADS, HEAD, ROWS), F32),
                        pltpu.VMEM((ROWS, D_A), F32)] + _CONSTANT_SCRATCH,
        compiler_params=pltpu.CompilerParams(dimension_semantics=("arbitrary",),
                                             vmem_limit_bytes=VMEM_LIMIT_BYTES),
        name="decode_layer",
    )(x_sample.reshape(n_rows, D_MODEL), *weights, state_hgrn)

    return (y_p,
            y_s.reshape(dec_batch, DEC_SEQ, D_MODEL),
            st_p,
            st_s,
            vch_p,
            vch_s)
```

```python
import functools

import jax
import jax.numpy as jnp
from jax import lax
from jax.experimental import pallas as pl
from jax.experimental.pallas import tpu as pltpu

F32 = jnp.float32
BF16 = jnp.bfloat16

D_MODEL = 1024
D_A = 512
D_B = 512
HEAD = 128
N_HEADS = 4
D_IN = 4 * D_A + 3 * D_B
ROWS = 128
SUBLANES = 8
PACKED = 2 * SUBLANES
N_TILES = ROWS // SUBLANES
SUBLANE_LEVELS = 3
EPS = 1e-6
PROMPT_TILE = 512
PROJ_BLOCK = 512
HEAD_ROUNDS = 13
CHUNK_STAGGER = 4
HEAD_STAGGER = 1
DEC_SEQ = 8
SEQS_PER_TILE = ROWS // DEC_SEQ
SEQ_GROUP = 4
VMEM_LIMIT_BYTES = 56 * 1024 * 1024

_NT = (((1,), (1,)), ((), ()))


def _silu(x):
    return x * jax.nn.sigmoid(x)


def _gelu(x):
    c = -2.0 * 0.7978845608028654 * 1.4426950408889634
    return x / (1.0 + jnp.exp2(x * (c * 0.044715 * (x * x) + c)))


def _rms_norm(x, g):
    return x * lax.rsqrt(jnp.mean(x * x, axis=-1, keepdims=True) + EPS) * g


def _layer_norm(x, g, b):
    mu = jnp.mean(x, axis=-1, keepdims=True)
    xc = x - mu
    var = jnp.mean(xc * xc, axis=-1, keepdims=True)
    return xc * lax.rsqrt(var + EPS) * g + b


def _lower_bound(lbl):
    rows = [lbl[r:r + 1, :] for r in range(lbl.shape[0])]
    mx = functools.reduce(jnp.maximum, rows)
    es = [jnp.exp(r - mx) for r in rows]
    return es[0] / functools.reduce(lambda a, c: a + c, es)


def _n_levels(seg_len):
    n_levels = seg_len.bit_length() - 1
    assert seg_len == 1 << n_levels and n_levels >= SUBLANE_LEVELS
    return n_levels


def _split3(x):
    hi = x.astype(BF16)
    r1 = x - hi.astype(F32)
    mid = r1.astype(BF16)
    lo = (r1 - mid.astype(F32)).astype(BF16)
    return hi, mid, lo


def _store_constants(seg_len, ws_ref, bs_ref, wout_ref, lvl_ref, wc_ref, bias_ref, wout16_ref):
    t = lax.broadcasted_iota(jnp.int32, (ROWS, ROWS), 0)
    s = lax.broadcasted_iota(jnp.int32, (ROWS, ROWS), 1)
    n_levels = _n_levels(seg_len)
    x = t ^ s
    lvl = jnp.where(t == s, n_levels, -1)
    for j in range(n_levels):
        lvl = jnp.where(((x >> j) == 1) & (((t >> j) & 1) == 1), j, lvl)
    lvl_ref[...] = lvl
    causal = (t >= s) & ((t >> n_levels) == (s >> n_levels))
    pick = jnp.where(s == (t & (seg_len - 1)), 1.0, 0.0).astype(BF16)
    for hd in range(N_HEADS):
        w = ws_ref[hd].astype(BF16)
        if seg_len != ROWS:
            w = jnp.dot(pick, w, preferred_element_type=F32).astype(BF16)
            w = lax.dot_general(w, pick, _NT, preferred_element_type=F32).astype(BF16)
        wc_ref[hd] = jnp.where(causal, w, jnp.zeros_like(w))
    pad = jnp.zeros((PACKED - N_HEADS, ROWS), F32)
    bias = sum(lax.dot_general(pick, part, _NT, preferred_element_type=F32)
               for part in _split3(jnp.concatenate([bs_ref[...], pad], axis=0)))
    bias_ref[...] = bias[:, :N_HEADS]
    wout16_ref[...] = wout_ref[...].astype(BF16)


def _row_bits():
    rowi = lax.broadcasted_iota(jnp.int32, (SUBLANES, HEAD), 0)
    return [((rowi >> j) & 1) == 1 for j in range(SUBLANE_LEVELS)]


def _hgrn_head(gates, lvl_ref, bits, n_levels):
    one = jnp.ones((SUBLANES, HEAD), F32)
    z = [[] for _ in range(n_levels)]
    qp_t, ks_t, blk_t, diag_t = [], [], [], []
    for pair in range(N_TILES // 2):
        z_pair = [[] for _ in range(SUBLANE_LEVELS)]
        for i in (2 * pair, 2 * pair + 1):
            q, kk, f = gates(i)
            diag_t.append(jnp.sum(q * kk, axis=-1, keepdims=True))
            qp, ks, blk = q * f, kk, f
            for j in range(SUBLANE_LEVELS):
                m = 1 << j
                z_pair[j].append(jnp.where(bits[j], qp, ks))
                sib = pltpu.roll(blk, m, 0)
                if 2 * m != SUBLANES:
                    sib = jnp.where(bits[j], sib, pltpu.roll(blk, SUBLANES - m, 0))
                qp = qp * jnp.where(bits[j], sib, one)
                ks = ks * jnp.where(bits[j], one, sib)
                blk = blk * sib
            qp_t.append(qp)
            ks_t.append(ks)
            blk_t.append(blk)
        for j in range(SUBLANE_LEVELS):
            z[j].append(jnp.concatenate(z_pair[j], axis=0).astype(BF16))
        if pair == N_TILES // 4 - 1:
            yield
    yield

    blk_levels, blk_b = [], blk_t
    for j in range(SUBLANE_LEVELS, n_levels):
        blk_levels.append(blk_b)
        blk_b = [blk_b[2 * n] * blk_b[2 * n + 1] for n in range(len(blk_b) // 2)]
    for pair in range(N_TILES // 2):
        z_pair = [[] for _ in range(SUBLANE_LEVELS, n_levels)]
        for i in (2 * pair, 2 * pair + 1):
            for j in range(SUBLANE_LEVELS, n_levels):
                block = i >> (j - SUBLANE_LEVELS)
                sib = blk_levels[j - SUBLANE_LEVELS][block ^ 1]
                if block & 1:
                    z_pair[j - SUBLANE_LEVELS].append(qp_t[i])
                    qp_t[i] = qp_t[i] * sib
                else:
                    z_pair[j - SUBLANE_LEVELS].append(ks_t[i])
                    ks_t[i] = ks_t[i] * sib
        for j in range(SUBLANE_LEVELS, n_levels):
            z[j].append(jnp.concatenate(z_pair[j - SUBLANE_LEVELS], axis=0).astype(BF16))
    q_in = jnp.concatenate(qp_t, axis=0)
    k_out = jnp.concatenate(ks_t, axis=0)
    per_tile = N_TILES // len(blk_b)
    total = jnp.concatenate([blk_b[i // per_tile] for i in range(N_TILES)], axis=0)
    yield

    p, p_row = [], []
    for j in range(n_levels):
        zj = jnp.concatenate(z[j], axis=0)
        span = 1 << max(j - SUBLANE_LEVELS, 0)
        if span * SUBLANES >= PACKED:
            upper = [i for i in range(N_TILES) if (i // span) & 1]
            lhs = jnp.concatenate([z[j][i // 2] for i in upper[::2]], axis=0)
        else:
            upper = list(range(N_TILES))
            lhs = zj
        p.append(lax.dot_general(lhs, zj, _NT, preferred_element_type=F32))
        p_row.append({i: n * SUBLANES for n, i in enumerate(upper)})
        yield
    sc_t = []
    for i in range(N_TILES):
        lvl_i = lvl_ref[i * SUBLANES:(i + 1) * SUBLANES, :]
        acc = jnp.where(lvl_i == n_levels, diag_t[i], 0.0)
        for j in range(n_levels):
            if j < SUBLANE_LEVELS or (i >> (j - SUBLANE_LEVELS)) & 1:
                r0 = p_row[j][i]
                acc = jnp.where(lvl_i == j, p[j][r0:r0 + SUBLANES, :], acc)
        sc_t.append(acc)
    sc = jnp.concatenate(sc_t, axis=0).astype(BF16)
    return sc, q_in, k_out, total


def _lockstep(generators, starts=None):
    starts = starts or [0] * len(generators)
    results = [None] * len(generators)
    waiting = sorted(range(len(generators)), key=lambda i: starts[i])
    active = []
    rnd = 0
    while waiting or active:
        while waiting and starts[waiting[0]] <= rnd:
            active.append(waiting.pop(0))
        still = []
        for idx in active:
            try:
                next(generators[idx])
                still.append(idx)
            except StopIteration as done:
                results[idx] = done.value
        active = still
        rnd += 1
    return results


def _hgrn_gates(pq, pf, lb):
    q = _silu(pq)
    forget = lb + (1.0 - lb) * jax.nn.sigmoid(pf)
    return q, 1.0 - forget, forget


def _hgrn_finish(o, pz, g):
    return (_rms_norm(o, g) * _silu(pz)).astype(BF16)


def _mlp_head(hd, u, v_n16, gate, wc_ref, bias):
    mix = jnp.dot(wc_ref[hd], v_n16, preferred_element_type=F32) + bias[:, hd:hd + 1]
    return (_gelu(u) * mix * _silu(gate)).astype(BF16)


def _prompt_kernel(x_ref, xprev_ref, ng_ref, win_ref, lbl_ref, hg_ref, lng_ref, lnb_ref, ws_ref,
                   bs_ref, wout_ref, fg_ref, y_ref, st_ref, vch_ref, proj_a, proj_b, h_ref,
                   mixed_ref, lvl_ref, wc_ref, bias_ref, wout16_ref, *, tiles_per_seq):
    g = pl.program_id(0)

    @pl.when(g == 0)
    def _():
        proj_b[...] = jnp.zeros_like(proj_b)
        _store_constants(ROWS, ws_ref, bs_ref, wout_ref, lvl_ref, wc_ref, bias_ref, wout16_ref)

    @pl.when(lax.rem(jnp.maximum(g - 1, 0), tiles_per_seq) == 0)
    def _():
        st_ref[...] = jnp.zeros_like(st_ref)

    def step(proj_next, proj_prev):
        h_ref[...] = _rms_norm(x_ref[0], ng_ref[...]).astype(BF16)
        lb = _lower_bound(lbl_ref[...])
        bits, n_levels = _row_bits(), _n_levels(ROWS)
        hg = hg_ref[...]
        lng = lng_ref[...]
        lnb = lnb_ref[...]
        bias = bias_ref[...]
        n_chunks = PROMPT_TILE // ROWS
        n_blocks = D_IN // PROJ_BLOCK

        def col(c, k, hs=slice(0, D_A)):
            return proj_prev[c * ROWS:(c + 1) * ROWS, k * D_A + hs.start:k * D_A + hs.stop]

        def head(c, hd):
            hs = slice(hd * HEAD, (hd + 1) * HEAD)

            def gates(i):
                rows = slice(c * ROWS + i * SUBLANES, c * ROWS + (i + 1) * SUBLANES)
                return _hgrn_gates(proj_prev[rows, hs],
                                   proj_prev[rows, D_A + hd * HEAD:D_A + (hd + 1) * HEAD],
                                   lb[:, hs])

            sc, q_in, k_out, total = yield from _hgrn_head(gates, lvl_ref, bits, n_levels)
            state = st_ref[0, 0, hd]
            vh = col(c, 2, hs)
            o = jnp.dot(jnp.concatenate([sc, q_in.astype(BF16)], axis=1),
                        jnp.concatenate([vh.astype(BF16), state.astype(BF16)], axis=0),
                        preferred_element_type=F32)
            yield
            st_ref[0, 0, hd] = total.T * state + jnp.dot(
                k_out.T.astype(BF16), vh.astype(BF16), preferred_element_type=F32)
            yield
            mixed_ref[c * ROWS:(c + 1) * ROWS, hs] = _hgrn_finish(o, col(c, 3, hs), hg[:, hs])

        def mlp(c):
            v_n = _layer_norm(_gelu(col(c, 5)), lng, lnb)
            if c == n_chunks - 1:
                for hd in range(N_HEADS):
                    vch_ref[0, 0, :, hd, :] = v_n[:, hd * HEAD:(hd + 1) * HEAD]
            v_n16 = v_n.astype(BF16)
            yield
            for hd in range(N_HEADS):
                hs = slice(hd * HEAD, (hd + 1) * HEAD)
                mixed_ref[c * ROWS:(c + 1) * ROWS, D_A + hd * HEAD:D_A + (hd + 1) * HEAD] = (
                    _mlp_head(hd, col(c, 4, hs), v_n16[:, hs], col(c, 6, hs), wc_ref, bias))
                yield
                yield

        def projection(rounds_per_block):
            for blk in range(n_blocks):
                cols = slice(blk * PROJ_BLOCK, (blk + 1) * PROJ_BLOCK)
                proj_next[:, cols] = jnp.dot(h_ref[...], win_ref[:, cols],
                                             preferred_element_type=F32)
                for _ in range(rounds_per_block):
                    yield

        items = [gen for c in range(n_chunks)
                 for gen in [head(c, hd) for hd in range(N_HEADS)] + [mlp(c)]]
        starts = [c * CHUNK_STAGGER + k * HEAD_STAGGER
                  for c in range(n_chunks) for k in list(range(N_HEADS)) + [0]]
        rounds = (n_chunks - 1) * CHUNK_STAGGER + (N_HEADS - 1) * HEAD_STAGGER + HEAD_ROUNDS
        _lockstep(items + [projection(max(1, rounds // n_blocks))], starts + [0])

        out = xprev_ref[0] + jnp.dot(mixed_ref[...], wout16_ref[...], preferred_element_type=F32)
        y_ref[0] = _rms_norm(out, fg_ref[...])

    parity = lax.rem(g, 2)

    @pl.when(parity == 0)
    def _():
        step(proj_a, proj_b)

    @pl.when(parity == 1)
    def _():
        step(proj_b, proj_a)


def _sample_kernel(x_ref, ng_ref, win_ref, lbl_ref, hg_ref, lng_ref, lnb_ref, ws_ref, bs_ref,
                   wout_ref, fg_ref, stin_ref, y_ref, stout_ref, vch_ref,
                   qin_ref, kot_ref, tot_ref, oint_ref, lvl_ref, wc_ref, bias_ref, wout16_ref):
    @pl.when(pl.program_id(0) == 0)
    def _():
        _store_constants(DEC_SEQ, ws_ref, bs_ref, wout_ref, lvl_ref, wc_ref, bias_ref, wout16_ref)

    x = x_ref[...]
    h = _rms_norm(x, ng_ref[...]).astype(BF16)
    proj = jnp.dot(h, win_ref[...], preferred_element_type=F32)

    def col(k, hs=slice(0, D_A)):
        return proj[:, k * D_A + hs.start:k * D_A + hs.stop]

    lb = _lower_bound(lbl_ref[...])
    bits, n_levels = _row_bits(), _n_levels(DEC_SEQ)
    vi16 = col(2).astype(BF16)

    def head(hd):
        hs = slice(hd * HEAD, (hd + 1) * HEAD)

        def gates(i):
            rows = slice(i * SUBLANES, (i + 1) * SUBLANES)
            return _hgrn_gates(proj[rows, hs], proj[rows, D_A + hd * HEAD:D_A + (hd + 1) * HEAD],
                               lb[:, hs])

        sc, q_in, k_out, total = yield from _hgrn_head(gates, lvl_ref, bits, n_levels)
        qin_ref[:, hs] = q_in
        kot_ref[hd] = k_out.T
        tot_ref[hd] = total.T
        return jnp.dot(sc, vi16[:, hs], preferred_element_type=F32)

    o_intra = _lockstep([head(hd) for hd in range(N_HEADS)])

    lane = lax.broadcasted_iota(jnp.int32, (HEAD, ROWS), 1)

    def seq_head(i, hd):
        hs = slice(hd * HEAD, (hd + 1) * HEAD)
        seq_rows = slice(i * DEC_SEQ, (i + 1) * DEC_SEQ)
        state = stin_ref[0, i, hd]
        qi = qin_ref[seq_rows, hs].astype(BF16)
        oint_ref[seq_rows, hs] = jnp.dot(qi, state.astype(BF16), preferred_element_type=F32)
        yield
        in_seq = (lane >> (DEC_SEQ.bit_length() - 1)) == i
        k_seq = jnp.where(in_seq, kot_ref[hd], 0.0).astype(BF16)
        decay = tot_ref[hd, :, i * DEC_SEQ:i * DEC_SEQ + 1]
        yield
        stout_ref[0, i, hd] = decay * state + jnp.dot(
            k_seq, vi16[:, hs], preferred_element_type=F32)

    for grp in range(SEQS_PER_TILE // SEQ_GROUP):
        _lockstep([seq_head(grp * SEQ_GROUP + s, hd)
                   for s in range(SEQ_GROUP) for hd in range(N_HEADS)])

    hg = hg_ref[...]
    bias = bias_ref[...]
    v_n = _layer_norm(_gelu(col(5)), lng_ref[...], lnb_ref[...])
    for hd in range(N_HEADS):
        vch_ref[0, :, :, hd, :] = v_n[:, hd * HEAD:(hd + 1) * HEAD].reshape(
            SEQS_PER_TILE, DEC_SEQ, HEAD)
    v_n16 = v_n.astype(BF16)
    mixed = []
    for hd in range(N_HEADS):
        hs = slice(hd * HEAD, (hd + 1) * HEAD)
        mixed.append(_hgrn_finish(o_intra[hd] + oint_ref[:, hs], col(3, hs), hg[:, hs]))
    for hd in range(N_HEADS):
        hs = slice(hd * HEAD, (hd + 1) * HEAD)
        mixed.append(_mlp_head(hd, col(4, hs), v_n16[:, hs], col(6, hs), wc_ref, bias))
    out = x + jnp.dot(jnp.concatenate(mixed, axis=-1), wout16_ref[...],
                      preferred_element_type=F32)
    y_ref[...] = _rms_norm(out, fg_ref[...])


def _const_spec(shape):
    return pl.BlockSpec(shape, lambda *_: (0,) * len(shape), pipeline_mode=pl.Buffered(1))


_WEIGHT_SPECS = [
    _const_spec((1, D_MODEL)),
    _const_spec((D_MODEL, D_IN)),
    _const_spec((2, D_A)),
    _const_spec((1, D_A)),
    _const_spec((1, D_B)),
    _const_spec((1, D_B)),
    _const_spec((N_HEADS, ROWS, ROWS)),
    _const_spec((N_HEADS, ROWS)),
    _const_spec((D_MODEL, D_MODEL)),
    _const_spec((1, D_MODEL)),
]

_CONSTANT_SCRATCH = [
    pltpu.VMEM((ROWS, ROWS), jnp.int32),
    pltpu.VMEM((N_HEADS, ROWS, ROWS), BF16),
    pltpu.VMEM((ROWS, N_HEADS), F32),
    pltpu.VMEM((D_MODEL, D_MODEL), BF16),
]


def kernel(x_prompt, x_sample, state_hgrn, norm_g, w_in, lb_logits, hgrn_norm_g, sgu_ln_g,
           sgu_ln_b, w_s, b_s, w_out, final_norm_g):
    depth = norm_g.shape[0]
    assert depth == 1 and lb_logits.shape == (2, D_A)
    batch, seq, _ = x_prompt.shape
    dec_batch, dec_seq, _ = x_sample.shape
    assert seq % PROMPT_TILE == 0 and dec_seq == DEC_SEQ and dec_batch % SEQS_PER_TILE == 0

    weights = (norm_g, w_in[0].astype(BF16), lb_logits, hgrn_norm_g, sgu_ln_g, sgu_ln_b,
               w_s[0], b_s[0], w_out[0], final_norm_g.reshape(1, D_MODEL))
    params = pltpu.CompilerParams(dimension_semantics=("arbitrary",),
                                  vmem_limit_bytes=VMEM_LIMIT_BYTES)

    tiles_per_seq = seq // PROMPT_TILE
    n_tiles = batch * tiles_per_seq

    def cur_tile(g):
        t = jnp.minimum(g, n_tiles - 1)
        return t // tiles_per_seq, t % tiles_per_seq

    def prev_tile(g):
        t = jnp.maximum(g - 1, 0)
        return t // tiles_per_seq, t % tiles_per_seq

    y_p, st_p, vch_p = pl.pallas_call(
        functools.partial(_prompt_kernel, tiles_per_seq=tiles_per_seq),
        grid=(n_tiles + 1,),
        in_specs=[pl.BlockSpec((1, PROMPT_TILE, D_MODEL), lambda g: (*cur_tile(g), 0)),
                  pl.BlockSpec((1, PROMPT_TILE, D_MODEL), lambda g: (*prev_tile(g), 0))]
        + _WEIGHT_SPECS,
        out_specs=[
            pl.BlockSpec((1, PROMPT_TILE, D_MODEL), lambda g: (*prev_tile(g), 0)),
            pl.BlockSpec((1, 1, N_HEADS, HEAD, HEAD), lambda g: (0, prev_tile(g)[0], 0, 0, 0)),
            pl.BlockSpec((1, 1, ROWS, N_HEADS, HEAD), lambda g: (0, prev_tile(g)[0], 0, 0, 0)),
        ],
        out_shape=[
            jax.ShapeDtypeStruct((batch, seq, D_MODEL), F32),
            jax.ShapeDtypeStruct((1, batch, N_HEADS, HEAD, HEAD), F32),
            jax.ShapeDtypeStruct((1, batch, ROWS, N_HEADS, HEAD), F32),
        ],
        scratch_shapes=[pltpu.VMEM((PROMPT_TILE, D_IN), F32),
                        pltpu.VMEM((PROMPT_TILE, D_IN), F32),
                        pltpu.VMEM((PROMPT_TILE, D_MODEL), BF16),
                        pltpu.VMEM((PROMPT_TILE, D_MODEL), BF16)] + _CONSTANT_SCRATCH,
        compiler_params=params,
        name="prompt_layer",
    )(x_prompt, x_prompt, *weights)

    n_rows = dec_batch * DEC_SEQ
    y_s, st_s, vch_s = pl.pallas_call(
        _sample_kernel,
        grid=(n_rows // ROWS,),
        in_specs=[pl.BlockSpec((ROWS, D_MODEL), lambda i: (i, 0))]
        + _WEIGHT_SPECS
        + [pl.BlockSpec((1, SEQS_PER_TILE, N_HEADS, HEAD, HEAD), lambda i: (0, i, 0, 0, 0))],
        out_specs=[
            pl.BlockSpec((ROWS, D_MODEL), lambda i: (i, 0)),
            pl.BlockSpec((1, SEQS_PER_TILE, N_HEADS, HEAD, HEAD), lambda i: (0, i, 0, 0, 0)),
            pl.BlockSpec((1, SEQS_PER_TILE, DEC_SEQ, N_HEADS, HEAD), lambda i: (0, i, 0, 0, 0)),
        ],
        out_shape=[
            jax.ShapeDtypeStruct((n_rows, D_MODEL), F32),
            jax.ShapeDtypeStruct((1, dec_batch, N_HEADS, HEAD, HEAD), F32),
            jax.ShapeDtypeStruct((1, dec_batch, DEC_SEQ, N_HEADS, HEAD), F32),
        ],
        scratch_shapes=[pltpu.VMEM((ROWS, D_A), F32),
                        pltpu.VMEM((N_HEADS, HEAD, ROWS), F32),
                        pltpu.VMEM((N_HEADS, HEAD, ROWS), F32),
                        pltpu.VMEM((ROWS, D_A), F32)] + _CONSTANT_SCRATCH,
        compiler_params=pltpu.CompilerParams(dimension_semantics=("arbitrary",),
                                             vmem_limit_bytes=VMEM_LIMIT_BYTES),
        name="decode_layer",
    )(x_sample.reshape(n_rows, D_MODEL), *weights, state_hgrn)

    return (y_p,
            y_s.reshape(dec_batch, DEC_SEQ, D_MODEL),
            st_p,
            st_s,
            vch_p,
            vch_s)
```

```python
import functools

import jax
import jax.numpy as jnp
from jax import lax
from jax.experimental import pallas as pl
from jax.experimental.pallas import tpu as pltpu

F32 = jnp.float32
BF16 = jnp.bfloat16

D_MODEL = 1024
D_A = 512
D_B = 512
HEAD = 128
N_HEADS = 4
D_IN = 4 * D_A + 3 * D_B
ROWS = 128
SUBLANES = 8
PACKED = 2 * SUBLANES
N_TILES = ROWS // SUBLANES
SUBLANE_LEVELS = 3
EPS = 1e-6
PROMPT_TILE = 512
PROJ_BLOCK = 512
HEAD_ROUNDS = 12
CHUNK_STAGGER = 4
HEAD_STAGGER = 1
DEC_SEQ = 8
SEQS_PER_TILE = ROWS // DEC_SEQ
SEQ_GROUP = 4
V7X_VMEM_BYTES = 64 * 1024 * 1024
VMEM_LIMIT_BYTES = V7X_VMEM_BYTES * 7 // 8

_NT = (((1,), (1,)), ((), ()))


def _silu(x):
    return x * jax.nn.sigmoid(x)


def _gelu(x):
    c = -2.0 * 0.7978845608028654 * 1.4426950408889634
    return x / (1.0 + jnp.exp2(x * (c * 0.044715 * (x * x) + c)))


def _rms_norm(x, g):
    return x * lax.rsqrt(jnp.mean(x * x, axis=-1, keepdims=True) + EPS) * g


def _layer_norm(x, g, b):
    mu = jnp.mean(x, axis=-1, keepdims=True)
    xc = x - mu
    var = jnp.mean(xc * xc, axis=-1, keepdims=True)
    return xc * lax.rsqrt(var + EPS) * g + b


def _lower_bound(lbl):
    rows = [lbl[r:r + 1, :] for r in range(lbl.shape[0])]
    mx = functools.reduce(jnp.maximum, rows)
    es = [jnp.exp(r - mx) for r in rows]
    return es[0] / functools.reduce(lambda a, c: a + c, es)


def _n_levels(seg_len):
    n_levels = seg_len.bit_length() - 1
    assert seg_len == 1 << n_levels and n_levels >= SUBLANE_LEVELS
    return n_levels


def _split3(x):
    hi = x.astype(BF16)
    r1 = x - hi.astype(F32)
    mid = r1.astype(BF16)
    lo = (r1 - mid.astype(F32)).astype(BF16)
    return hi, mid, lo


def _store_constants(seg_len, ws_ref, bs_ref, wout_ref, lvl_ref, wc_ref, bias_ref, wout16_ref):
    t = lax.broadcasted_iota(jnp.int32, (ROWS, ROWS), 0)
    s = lax.broadcasted_iota(jnp.int32, (ROWS, ROWS), 1)
    n_levels = _n_levels(seg_len)
    x = t ^ s
    lvl = jnp.where(t == s, n_levels, -1)
    for j in range(n_levels):
        lvl = jnp.where(((x >> j) == 1) & (((t >> j) & 1) == 1), j, lvl)
    lvl_ref[...] = lvl
    causal = (t >= s) & ((t >> n_levels) == (s >> n_levels))
    pick = jnp.where(s == (t & (seg_len - 1)), 1.0, 0.0).astype(BF16)
    for hd in range(N_HEADS):
        w = ws_ref[hd].astype(BF16)
        if seg_len != ROWS:
            w = jnp.dot(pick, w, preferred_element_type=F32).astype(BF16)
            w = lax.dot_general(w, pick, _NT, preferred_element_type=F32).astype(BF16)
        wc_ref[hd] = jnp.where(causal, w, jnp.zeros_like(w))
    pad = jnp.zeros((PACKED - N_HEADS, ROWS), F32)
    bias = sum(lax.dot_general(pick, part, _NT, preferred_element_type=F32)
               for part in _split3(jnp.concatenate([bs_ref[...], pad], axis=0)))
    bias_ref[...] = bias[:, :N_HEADS]
    wout16_ref[...] = wout_ref[...].astype(BF16)


def _row_bits():
    rowi = lax.broadcasted_iota(jnp.int32, (SUBLANES, HEAD), 0)
    return [((rowi >> j) & 1) == 1 for j in range(SUBLANE_LEVELS)]


def _hgrn_head(gates, lvl_ref, bits, n_levels):
    one = jnp.ones((SUBLANES, HEAD), F32)
    z = [[] for _ in range(n_levels)]
    qp_t, ks_t, blk_t, diag_t = [], [], [], []
    for pair in range(N_TILES // 2):
        z_pair = [[] for _ in range(SUBLANE_LEVELS)]
        for i in (2 * pair, 2 * pair + 1):
            q, kk, f = gates(i)
            diag_t.append(jnp.sum(q * kk, axis=-1, keepdims=True))
            qp, ks, blk = q * f, kk, f
            for j in range(SUBLANE_LEVELS):
                m = 1 << j
                z_pair[j].append(jnp.where(bits[j], qp, ks))
                sib = pltpu.roll(blk, m, 0)
                if 2 * m != SUBLANES:
                    sib = jnp.where(bits[j], sib, pltpu.roll(blk, SUBLANES - m, 0))
                qp = qp * jnp.where(bits[j], sib, one)
                ks = ks * jnp.where(bits[j], one, sib)
                blk = blk * sib
            qp_t.append(qp)
            ks_t.append(ks)
            blk_t.append(blk)
        for j in range(SUBLANE_LEVELS):
            z[j].append(jnp.concatenate(z_pair[j], axis=0).astype(BF16))
    yield

    blk_levels, blk_b = [], blk_t
    for j in range(SUBLANE_LEVELS, n_levels):
        blk_levels.append(blk_b)
        blk_b = [blk_b[2 * n] * blk_b[2 * n + 1] for n in range(len(blk_b) // 2)]
    for pair in range(N_TILES // 2):
        z_pair = [[] for _ in range(SUBLANE_LEVELS, n_levels)]
        for i in (2 * pair, 2 * pair + 1):
            for j in range(SUBLANE_LEVELS, n_levels):
                block = i >> (j - SUBLANE_LEVELS)
                sib = blk_levels[j - SUBLANE_LEVELS][block ^ 1]
                if block & 1:
                    z_pair[j - SUBLANE_LEVELS].append(qp_t[i])
                    qp_t[i] = qp_t[i] * sib
                else:
                    z_pair[j - SUBLANE_LEVELS].append(ks_t[i])
                    ks_t[i] = ks_t[i] * sib
        for j in range(SUBLANE_LEVELS, n_levels):
            z[j].append(jnp.concatenate(z_pair[j - SUBLANE_LEVELS], axis=0).astype(BF16))
    q_in = jnp.concatenate(qp_t, axis=0)
    k_out = jnp.concatenate(ks_t, axis=0)
    per_tile = N_TILES // len(blk_b)
    total = jnp.concatenate([blk_b[i // per_tile] for i in range(N_TILES)], axis=0)
    yield

    p, p_row = [], []
    for j in range(n_levels):
        zj = jnp.concatenate(z[j], axis=0)
        span = 1 << max(j - SUBLANE_LEVELS, 0)
        if span * SUBLANES >= PACKED:
            upper = [i for i in range(N_TILES) if (i // span) & 1]
            lhs = jnp.concatenate([z[j][i // 2] for i in upper[::2]], axis=0)
        else:
            upper = list(range(N_TILES))
            lhs = zj
        p.append(lax.dot_general(lhs, zj, _NT, preferred_element_type=F32))
        p_row.append({i: n * SUBLANES for n, i in enumerate(upper)})
        yield
    sc_t = []
    for i in range(N_TILES):
        lvl_i = lvl_ref[i * SUBLANES:(i + 1) * SUBLANES, :]
        acc = jnp.where(lvl_i == n_levels, diag_t[i], 0.0)
        for j in range(n_levels):
            if j < SUBLANE_LEVELS or (i >> (j - SUBLANE_LEVELS)) & 1:
                r0 = p_row[j][i]
                acc = jnp.where(lvl_i == j, p[j][r0:r0 + SUBLANES, :], acc)
        sc_t.append(acc)
    sc = jnp.concatenate(sc_t, axis=0).astype(BF16)
    return sc, q_in, k_out, total


def _lockstep(generators, starts=None):
    starts = starts or [0] * len(generators)
    results = [None] * len(generators)
    waiting = sorted(range(len(generators)), key=lambda i: starts[i])
    active = []
    rnd = 0
    while waiting or active:
        while waiting and starts[waiting[0]] <= rnd:
            active.append(waiting.pop(0))
        still = []
        for idx in active:
            try:
                next(generators[idx])
                still.append(idx)
            except StopIteration as done:
                results[idx] = done.value
        active = still
        rnd += 1
    return results


def _hgrn_gates(pq, pf, lb):
    q = _silu(pq)
    forget = lb + (1.0 - lb) * jax.nn.sigmoid(pf)
    return q, 1.0 - forget, forget


def _hgrn_finish(o, pz, g):
    return (_rms_norm(o, g) * _silu(pz)).astype(BF16)


def _mlp_head(hd, u, v_n16, gate, wc_ref, bias):
    mix = jnp.dot(wc_ref[hd], v_n16, preferred_element_type=F32) + bias[:, hd:hd + 1]
    return (_gelu(u) * mix * _silu(gate)).astype(BF16)


def _prompt_kernel(x_ref, xprev_ref, ng_ref, win_ref, lbl_ref, hg_ref, lng_ref, lnb_ref, ws_ref,
                   bs_ref, wout_ref, fg_ref, y_ref, st_ref, vch_ref, proj_a, proj_b, h_ref,
                   mixed_ref, lvl_ref, wc_ref, bias_ref, wout16_ref, *, tiles_per_seq):
    g = pl.program_id(0)

    @pl.when(g == 0)
    def _():
        proj_b[...] = jnp.zeros_like(proj_b)
        _store_constants(ROWS, ws_ref, bs_ref, wout_ref, lvl_ref, wc_ref, bias_ref, wout16_ref)

    @pl.when(lax.rem(jnp.maximum(g - 1, 0), tiles_per_seq) == 0)
    def _():
        st_ref[...] = jnp.zeros_like(st_ref)

    def step(proj_next, proj_prev):
        h_ref[...] = _rms_norm(x_ref[0], ng_ref[...]).astype(BF16)
        lb = _lower_bound(lbl_ref[...])
        bits, n_levels = _row_bits(), _n_levels(ROWS)
        hg = hg_ref[...]
        lng = lng_ref[...]
        lnb = lnb_ref[...]
        bias = bias_ref[...]
        n_chunks = PROMPT_TILE // ROWS
        n_blocks = D_IN // PROJ_BLOCK

        def col(c, k, hs=slice(0, D_A)):
            return proj_prev[c * ROWS:(c + 1) * ROWS, k * D_A + hs.start:k * D_A + hs.stop]

        def head(c, hd):
            hs = slice(hd * HEAD, (hd + 1) * HEAD)

            def gates(i):
                rows = slice(c * ROWS + i * SUBLANES, c * ROWS + (i + 1) * SUBLANES)
                return _hgrn_gates(proj_prev[rows, hs],
                                   proj_prev[rows, D_A + hd * HEAD:D_A + (hd + 1) * HEAD],
                                   lb[:, hs])

            sc, q_in, k_out, total = yield from _hgrn_head(gates, lvl_ref, bits, n_levels)
            state = st_ref[0, 0, hd]
            vh = col(c, 2, hs)
            o = jnp.dot(jnp.concatenate([sc, q_in.astype(BF16)], axis=1),
                        jnp.concatenate([vh.astype(BF16), state.astype(BF16)], axis=0),
                        preferred_element_type=F32)
            yield
            st_ref[0, 0, hd] = total.T * state + jnp.dot(
                k_out.T.astype(BF16), vh.astype(BF16), preferred_element_type=F32)
            yield
            mixed_ref[c * ROWS:(c + 1) * ROWS, hs] = _hgrn_finish(o, col(c, 3, hs), hg[:, hs])

        def mlp(c):
            v_n = _layer_norm(_gelu(col(c, 5)), lng, lnb)
            if c == n_chunks - 1:
                for hd in range(N_HEADS):
                    vch_ref[0, 0, :, hd, :] = v_n[:, hd * HEAD:(hd + 1) * HEAD]
            v_n16 = v_n.astype(BF16)
            yield
            for hd in range(N_HEADS):
                hs = slice(hd * HEAD, (hd + 1) * HEAD)
                mixed_ref[c * ROWS:(c + 1) * ROWS, D_A + hd * HEAD:D_A + (hd + 1) * HEAD] = (
                    _mlp_head(hd, col(c, 4, hs), v_n16[:, hs], col(c, 6, hs), wc_ref, bias))
                yield
                yield

        def projection(rounds_per_block):
            for blk in range(n_blocks):
                cols = slice(blk * PROJ_BLOCK, (blk + 1) * PROJ_BLOCK)
                proj_next[:, cols] = jnp.dot(h_ref[...], win_ref[:, cols],
                                             preferred_element_type=F32)
                for _ in range(rounds_per_block):
                    yield

        items = [gen for c in range(n_chunks)
                 for gen in [head(c, hd) for hd in range(N_HEADS)] + [mlp(c)]]
        starts = [c * CHUNK_STAGGER + k * HEAD_STAGGER
                  for c in range(n_chunks) for k in list(range(N_HEADS)) + [0]]
        rounds = (n_chunks - 1) * CHUNK_STAGGER + (N_HEADS - 1) * HEAD_STAGGER + HEAD_ROUNDS
        _lockstep(items + [projection(max(1, rounds // n_blocks))], starts + [0])

        out = xprev_ref[0] + jnp.dot(mixed_ref[...], wout16_ref[...], preferred_element_type=F32)
        y_ref[0] = _rms_norm(out, fg_ref[...])

    parity = lax.rem(g, 2)

    @pl.when(parity == 0)
    def _():
        step(proj_a, proj_b)

    @pl.when(parity == 1)
    def _():
        step(proj_b, proj_a)


def _sample_kernel(x_ref, ng_ref, win_ref, lbl_ref, hg_ref, lng_ref, lnb_ref, ws_ref, bs_ref,
                   wout_ref, fg_ref, stin_ref, y_ref, stout_ref, vch_ref,
                   qin_ref, kot_ref, tot_ref, oint_ref, lvl_ref, wc_ref, bias_ref, wout16_ref):
    @pl.when(pl.program_id(0) == 0)
    def _():
        _store_constants(DEC_SEQ, ws_ref, bs_ref, wout_ref, lvl_ref, wc_ref, bias_ref, wout16_ref)

    x = x_ref[...]
    h = _rms_norm(x, ng_ref[...]).astype(BF16)
    proj = jnp.dot(h, win_ref[...], preferred_element_type=F32)

    def col(k, hs=slice(0, D_A)):
        return proj[:, k * D_A + hs.start:k * D_A + hs.stop]

    lb = _lower_bound(lbl_ref[...])
    bits, n_levels = _row_bits(), _n_levels(DEC_SEQ)
    vi16 = col(2).astype(BF16)

    def head(hd):
        hs = slice(hd * HEAD, (hd + 1) * HEAD)

        def gates(i):
            rows = slice(i * SUBLANES, (i + 1) * SUBLANES)
            return _hgrn_gates(proj[rows, hs], proj[rows, D_A + hd * HEAD:D_A + (hd + 1) * HEAD],
                               lb[:, hs])

        sc, q_in, k_out, total = yield from _hgrn_head(gates, lvl_ref, bits, n_levels)
        qin_ref[:, hs] = q_in
        kot_ref[hd] = k_out.T
        tot_ref[hd] = total.T
        return jnp.dot(sc, vi16[:, hs], preferred_element_type=F32)

    o_intra = _lockstep([head(hd) for hd in range(N_HEADS)])

    lane = lax.broadcasted_iota(jnp.int32, (HEAD, ROWS), 1)

    def seq_head(i, hd):
        hs = slice(hd * HEAD, (hd + 1) * HEAD)
        seq_rows = slice(i * DEC_SEQ, (i + 1) * DEC_SEQ)
        state = stin_ref[0, i, hd]
        qi = qin_ref[seq_rows, hs].astype(BF16)
        oint_ref[seq_rows, hs] = jnp.dot(qi, state.astype(BF16), preferred_element_type=F32)
        yield
        in_seq = (lane >> (DEC_SEQ.bit_length() - 1)) == i
        k_seq = jnp.where(in_seq, kot_ref[hd], 0.0).astype(BF16)
        decay = tot_ref[hd, :, i * DEC_SEQ:i * DEC_SEQ + 1]
        yield
        stout_ref[0, i, hd] = decay * state + jnp.dot(
            k_seq, vi16[:, hs], preferred_element_type=F32)

    for grp in range(SEQS_PER_TILE // SEQ_GROUP):
        _lockstep([seq_head(grp * SEQ_GROUP + s, hd)
                   for s in range(SEQ_GROUP) for hd in range(N_HEADS)])

    hg = hg_ref[...]
    bias = bias_ref[...]
    v_n = _layer_norm(_gelu(col(5)), lng_ref[...], lnb_ref[...])
    for hd in range(N_HEADS):
        vch_ref[0, :, :, hd, :] = v_n[:, hd * HEAD:(hd + 1) * HEAD].reshape(
            SEQS_PER_TILE, DEC_SEQ, HEAD)
    v_n16 = v_n.astype(BF16)
    mixed = []
    for hd in range(N_HEADS):
        hs = slice(hd * HEAD, (hd + 1) * HEAD)
        mixed.append(_hgrn_finish(o_intra[hd] + oint_ref[:, hs], col(3, hs), hg[:, hs]))
    for hd in range(N_HEADS):
        hs = slice(hd * HEAD, (hd + 1) * HEAD)
        mixed.append(_mlp_head(hd, col(4, hs), v_n16[:, hs], col(6, hs), wc_ref, bias))
    out = x + jnp.dot(jnp.concatenate(mixed, axis=-1), wout16_ref[...],
                      preferred_element_type=F32)
    y_ref[...] = _rms_norm(out, fg_ref[...])


def _const_spec(shape):
    return pl.BlockSpec(shape, lambda *_: (0,) * len(shape), pipeline_mode=pl.Buffered(1))


_WEIGHT_SPECS = [
    _const_spec((1, D_MODEL)),
    _const_spec((D_MODEL, D_IN)),
    _const_spec((2, D_A)),
    _const_spec((1, D_A)),
    _const_spec((1, D_B)),
    _const_spec((1, D_B)),
    _const_spec((N_HEADS, ROWS, ROWS)),
    _const_spec((N_HEADS, ROWS)),
    _const_spec((D_MODEL, D_MODEL)),
    _const_spec((1, D_MODEL)),
]

_CONSTANT_SCRATCH = [
    pltpu.VMEM((ROWS, ROWS), jnp.int32),
    pltpu.VMEM((N_HEADS, ROWS, ROWS), BF16),
    pltpu.VMEM((ROWS, N_HEADS), F32),
    pltpu.VMEM((D_MODEL, D_MODEL), BF16),
]


def kernel(x_prompt, x_sample, state_hgrn, norm_g, w_in, lb_logits, hgrn_norm_g, sgu_ln_g,
           sgu_ln_b, w_s, b_s, w_out, final_norm_g):
    depth = norm_g.shape[0]
    assert depth == 1 and lb_logits.shape == (2, D_A)
    batch, seq, _ = x_prompt.shape
    dec_batch, dec_seq, _ = x_sample.shape
    assert seq % PROMPT_TILE == 0 and dec_seq == DEC_SEQ and dec_batch % SEQS_PER_TILE == 0

    weights = (norm_g, w_in[0].astype(BF16), lb_logits, hgrn_norm_g, sgu_ln_g, sgu_ln_b,
               w_s[0], b_s[0], w_out[0], final_norm_g.reshape(1, D_MODEL))
    params = pltpu.CompilerParams(dimension_semantics=("arbitrary",),
                                  vmem_limit_bytes=VMEM_LIMIT_BYTES)

    tiles_per_seq = seq // PROMPT_TILE
    n_tiles = batch * tiles_per_seq

    def cur_tile(g):
        t = jnp.minimum(g, n_tiles - 1)
        return t // tiles_per_seq, t % tiles_per_seq

    def prev_tile(g):
        t = jnp.maximum(g - 1, 0)
        return t // tiles_per_seq, t % tiles_per_seq

    y_p, st_p, vch_p = pl.pallas_call(
        functools.partial(_prompt_kernel, tiles_per_seq=tiles_per_seq),
        grid=(n_tiles + 1,),
        in_specs=[pl.BlockSpec((1, PROMPT_TILE, D_MODEL), lambda g: (*cur_tile(g), 0)),
                  pl.BlockSpec((1, PROMPT_TILE, D_MODEL), lambda g: (*prev_tile(g), 0))]
        + _WEIGHT_SPECS,
        out_specs=[
            pl.BlockSpec((1, PROMPT_TILE, D_MODEL), lambda g: (*prev_tile(g), 0)),
            pl.BlockSpec((1, 1, N_HEADS, HEAD, HEAD), lambda g: (0, prev_tile(g)[0], 0, 0, 0)),
            pl.BlockSpec((1, 1, ROWS, N_HEADS, HEAD), lambda g: (0, prev_tile(g)[0], 0, 0, 0)),
        ],
        out_shape=[
            jax.ShapeDtypeStruct((batch, seq, D_MODEL), F32),
            jax.ShapeDtypeStruct((1, batch, N_HEADS, HEAD, HEAD), F32),
            jax.ShapeDtypeStruct((1, batch, ROWS, N_HEADS, HEAD), F32),
        ],
        scratch_shapes=[pltpu.VMEM((PROMPT_TILE, D_IN), F32),
                        pltpu.VMEM((PROMPT_TILE, D_IN), F32),
                        pltpu.VMEM((PROMPT_TILE, D_MODEL), BF16),
                        pltpu.VMEM((PROMPT_TILE, D_MODEL), BF16)] + _CONSTANT_SCRATCH,
        compiler_params=params,
        name="prompt_layer",
    )(x_prompt, x_prompt, *weights)

    n_rows = dec_batch * DEC_SEQ
    y_s, st_s, vch_s = pl.pallas_call(
        _sample_kernel,
        grid=(n_rows // ROWS,),
        in_specs=[pl.BlockSpec((ROWS, D_MODEL), lambda i: (i, 0))]
        + _WEIGHT_SPECS
        + [pl.BlockSpec((1, SEQS_PER_TILE, N_HEADS, HEAD, HEAD), lambda i: (0, i, 0, 0, 0))],
        out_specs=[
            pl.BlockSpec((ROWS, D_MODEL), lambda i: (i, 0)),
            pl.BlockSpec((1, SEQS_PER_TILE, N_HEADS, HEAD, HEAD), lambda i: (0, i, 0, 0, 0)),
            pl.BlockSpec((1, SEQS_PER_TILE, DEC_SEQ, N_HEADS, HEAD), lambda i: (0, i, 0, 0, 0)),
        ],
        out_shape=[
            jax.ShapeDtypeStruct((n_rows, D_MODEL), F32),
            jax.ShapeDtypeStruct((1, dec_batch, N_HEADS, HEAD, HEAD), F32),
            jax.ShapeDtypeStruct((1, dec_batch, DEC_SEQ, N_HEADS, HEAD), F32),
        ],
        scratch_shapes=[pltpu.VMEM((ROWS, D_A), F32),
                        pltpu.VMEM((N_HEADS, HEAD, ROWS), F32),
                        pltpu.VMEM((N_HEADS, HEAD, ROWS), F32),
                        pltpu.VMEM((ROWS, D_A), F32)] + _CONSTANT_SCRATCH,
        compiler_params=pltpu.CompilerParams(dimension_semantics=("arbitrary",),
                                             vmem_limit_bytes=VMEM_LIMIT_BYTES),
        name="decode_layer",
    )(x_sample.reshape(n_rows, D_MODEL), *weights, state_hgrn)

    return (y_p,
            y_s.reshape(dec_batch, DEC_SEQ, D_MODEL),
            st_p,
            st_s,
            vch_p,
            vch_s)
```

```python
import functools

import jax
import jax.numpy as jnp
from jax import lax
from jax.experimental import pallas as pl
from jax.experimental.pallas import tpu as pltpu

F32 = jnp.float32
BF16 = jnp.bfloat16

D_MODEL = 1024
D_A = 512
D_B = 512
HEAD = 128
N_HEADS = 4
D_IN = 4 * D_A + 3 * D_B
ROWS = 128
SUBLANES = 8
PACKED = 2 * SUBLANES
N_TILES = ROWS // SUBLANES
SUBLANE_LEVELS = 3
EPS = 1e-6
PROMPT_TILE = 512
PROJ_BLOCK = 512
HEAD_ROUNDS = 12
CHUNK_STAGGER = 4
HEAD_STAGGER = 1
PROJ_ROUNDS = [0, 1, 2, 5, 9, 13, 17]
DEC_SEQ = 8
SEQS_PER_TILE = ROWS // DEC_SEQ
SEQ_GROUP = 4
V7X_VMEM_BYTES = 64 * 1024 * 1024
VMEM_LIMIT_BYTES = V7X_VMEM_BYTES * 7 // 8

_NT = (((1,), (1,)), ((), ()))


def _silu(x):
    return x * jax.nn.sigmoid(x)


def _gelu(x):
    c = -2.0 * 0.7978845608028654 * 1.4426950408889634
    return x / (1.0 + jnp.exp2(x * (c * 0.044715 * (x * x) + c)))


def _rms_norm(x, g):
    return x * lax.rsqrt(jnp.mean(x * x, axis=-1, keepdims=True) + EPS) * g


def _layer_norm(x, g, b):
    mu = jnp.mean(x, axis=-1, keepdims=True)
    xc = x - mu
    var = jnp.mean(xc * xc, axis=-1, keepdims=True)
    return xc * lax.rsqrt(var + EPS) * g + b


def _lower_bound(lbl):
    rows = [lbl[r:r + 1, :] for r in range(lbl.shape[0])]
    mx = functools.reduce(jnp.maximum, rows)
    es = [jnp.exp(r - mx) for r in rows]
    return es[0] / functools.reduce(lambda a, c: a + c, es)


def _n_levels(seg_len):
    n_levels = seg_len.bit_length() - 1
    assert seg_len == 1 << n_levels and n_levels >= SUBLANE_LEVELS
    return n_levels


def _split3(x):
    hi = x.astype(BF16)
    r1 = x - hi.astype(F32)
    mid = r1.astype(BF16)
    lo = (r1 - mid.astype(F32)).astype(BF16)
    return hi, mid, lo


def _store_constants(seg_len, ws_ref, bs_ref, wout_ref, lvl_ref, wc_ref, bias_ref, wout16_ref):
    t = lax.broadcasted_iota(jnp.int32, (ROWS, ROWS), 0)
    s = lax.broadcasted_iota(jnp.int32, (ROWS, ROWS), 1)
    n_levels = _n_levels(seg_len)
    x = t ^ s
    lvl = jnp.where(t == s, n_levels, -1)
    for j in range(n_levels):
        lvl = jnp.where(((x >> j) == 1) & (((t >> j) & 1) == 1), j, lvl)
    lvl_ref[...] = lvl
    causal = (t >= s) & ((t >> n_levels) == (s >> n_levels))
    pick = jnp.where(s == (t & (seg_len - 1)), 1.0, 0.0).astype(BF16)
    for hd in range(N_HEADS):
        w = ws_ref[hd].astype(BF16)
        if seg_len != ROWS:
            w = jnp.dot(pick, w, preferred_element_type=F32).astype(BF16)
            w = lax.dot_general(w, pick, _NT, preferred_element_type=F32).astype(BF16)
        wc_ref[hd] = jnp.where(causal, w, jnp.zeros_like(w))
    pad = jnp.zeros((PACKED - N_HEADS, ROWS), F32)
    bias = sum(lax.dot_general(pick, part, _NT, preferred_element_type=F32)
               for part in _split3(jnp.concatenate([bs_ref[...], pad], axis=0)))
    bias_ref[...] = bias[:, :N_HEADS]
    wout16_ref[...] = wout_ref[...].astype(BF16)


def _row_bits():
    rowi = lax.broadcasted_iota(jnp.int32, (SUBLANES, HEAD), 0)
    return [((rowi >> j) & 1) == 1 for j in range(SUBLANE_LEVELS)]


def _hgrn_head(gates, lvl_ref, bits, n_levels):
    one = jnp.ones((SUBLANES, HEAD), F32)
    z = [[] for _ in range(n_levels)]
    qp_t, ks_t, blk_t, diag_t = [], [], [], []
    for pair in range(N_TILES // 2):
        z_pair = [[] for _ in range(SUBLANE_LEVELS)]
        for i in (2 * pair, 2 * pair + 1):
            q, kk, f = gates(i)
            diag_t.append(jnp.sum(q * kk, axis=-1, keepdims=True))
            qp, ks, blk = q * f, kk, f
            for j in range(SUBLANE_LEVELS):
                m = 1 << j
                z_pair[j].append(jnp.where(bits[j], qp, ks))
                sib = pltpu.roll(blk, m, 0)
                if 2 * m != SUBLANES:
                    sib = jnp.where(bits[j], sib, pltpu.roll(blk, SUBLANES - m, 0))
                qp = qp * jnp.where(bits[j], sib, one)
                ks = ks * jnp.where(bits[j], one, sib)
                blk = blk * sib
            qp_t.append(qp)
            ks_t.append(ks)
            blk_t.append(blk)
        for j in range(SUBLANE_LEVELS):
            z[j].append(jnp.concatenate(z_pair[j], axis=0).astype(BF16))
    yield

    blk_levels, blk_b = [], blk_t
    for j in range(SUBLANE_LEVELS, n_levels):
        blk_levels.append(blk_b)
        blk_b = [blk_b[2 * n] * blk_b[2 * n + 1] for n in range(len(blk_b) // 2)]
    for pair in range(N_TILES // 2):
        z_pair = [[] for _ in range(SUBLANE_LEVELS, n_levels)]
        for i in (2 * pair, 2 * pair + 1):
            for j in range(SUBLANE_LEVELS, n_levels):
                block = i >> (j - SUBLANE_LEVELS)
                sib = blk_levels[j - SUBLANE_LEVELS][block ^ 1]
                if block & 1:
                    z_pair[j - SUBLANE_LEVELS].append(qp_t[i])
                    qp_t[i] = qp_t[i] * sib
                else:
                    z_pair[j - SUBLANE_LEVELS].append(ks_t[i])
                    ks_t[i] = ks_t[i] * sib
        for j in range(SUBLANE_LEVELS, n_levels):
            z[j].append(jnp.concatenate(z_pair[j - SUBLANE_LEVELS], axis=0).astype(BF16))
    q_in = jnp.concatenate(qp_t, axis=0)
    k_out = jnp.concatenate(ks_t, axis=0)
    per_tile = N_TILES // len(blk_b)
    total = jnp.concatenate([blk_b[i // per_tile] for i in range(N_TILES)], axis=0)
    yield

    p, p_row = [], []
    for j in range(n_levels):
        zj = jnp.concatenate(z[j], axis=0)
        span = 1 << max(j - SUBLANE_LEVELS, 0)
        if span * SUBLANES >= PACKED:
            upper = [i for i in range(N_TILES) if (i // span) & 1]
            lhs = jnp.concatenate([z[j][i // 2] for i in upper[::2]], axis=0)
        else:
            upper = list(range(N_TILES))
            lhs = zj
        p.append(lax.dot_general(lhs, zj, _NT, preferred_element_type=F32))
        p_row.append({i: n * SUBLANES for n, i in enumerate(upper)})
        yield
    sc_t = []
    for i in range(N_TILES):
        lvl_i = lvl_ref[i * SUBLANES:(i + 1) * SUBLANES, :]
        acc = jnp.where(lvl_i == n_levels, diag_t[i], 0.0)
        for j in range(n_levels):
            if j < SUBLANE_LEVELS or (i >> (j - SUBLANE_LEVELS)) & 1:
                r0 = p_row[j][i]
                acc = jnp.where(lvl_i == j, p[j][r0:r0 + SUBLANES, :], acc)
        sc_t.append(acc)
    sc = jnp.concatenate(sc_t, axis=0).astype(BF16)
    return sc, q_in, k_out, total


def _lockstep(generators, starts=None):
    starts = starts or [0] * len(generators)
    results = [None] * len(generators)
    waiting = sorted(range(len(generators)), key=lambda i: starts[i])
    active = []
    rnd = 0
    while waiting or active:
        while waiting and starts[waiting[0]] <= rnd:
            active.append(waiting.pop(0))
        still = []
        for idx in active:
            try:
                next(generators[idx])
                still.append(idx)
            except StopIteration as done:
                results[idx] = done.value
        active = still
        rnd += 1
    return results


def _hgrn_gates(pq, pf, lb):
    q = _silu(pq)
    forget = lb + (1.0 - lb) * jax.nn.sigmoid(pf)
    return q, 1.0 - forget, forget


def _hgrn_finish(o, pz, g):
    return (_rms_norm(o, g) * _silu(pz)).astype(BF16)


def _mlp_head(hd, u, v_n16, gate, wc_ref, bias):
    mix = jnp.dot(wc_ref[hd], v_n16, preferred_element_type=F32) + bias[:, hd:hd + 1]
    return (_gelu(u) * mix * _silu(gate)).astype(BF16)


def _prompt_kernel(x_ref, xprev_ref, ng_ref, win_ref, lbl_ref, hg_ref, lng_ref, lnb_ref, ws_ref,
                   bs_ref, wout_ref, fg_ref, y_ref, st_ref, vch_ref, proj_a, proj_b, h_ref,
                   mixed_ref, lvl_ref, wc_ref, bias_ref, wout16_ref, *, tiles_per_seq):
    g = pl.program_id(0)

    @pl.when(g == 0)
    def _():
        proj_b[...] = jnp.zeros_like(proj_b)
        _store_constants(ROWS, ws_ref, bs_ref, wout_ref, lvl_ref, wc_ref, bias_ref, wout16_ref)

    @pl.when(lax.rem(jnp.maximum(g - 1, 0), tiles_per_seq) == 0)
    def _():
        st_ref[...] = jnp.zeros_like(st_ref)

    def step(proj_next, proj_prev):
        h_ref[...] = _rms_norm(x_ref[0], ng_ref[...]).astype(BF16)
        lb = _lower_bound(lbl_ref[...])
        bits, n_levels = _row_bits(), _n_levels(ROWS)
        hg = hg_ref[...]
        lng = lng_ref[...]
        lnb = lnb_ref[...]
        bias = bias_ref[...]
        n_chunks = PROMPT_TILE // ROWS
        n_blocks = D_IN // PROJ_BLOCK

        def col(c, k, hs=slice(0, D_A)):
            return proj_prev[c * ROWS:(c + 1) * ROWS, k * D_A + hs.start:k * D_A + hs.stop]

        def head(c, hd):
            hs = slice(hd * HEAD, (hd + 1) * HEAD)

            def gates(i):
                rows = slice(c * ROWS + i * SUBLANES, c * ROWS + (i + 1) * SUBLANES)
                return _hgrn_gates(proj_prev[rows, hs],
                                   proj_prev[rows, D_A + hd * HEAD:D_A + (hd + 1) * HEAD],
                                   lb[:, hs])

            sc, q_in, k_out, total = yield from _hgrn_head(gates, lvl_ref, bits, n_levels)
            state = st_ref[0, 0, hd]
            vh = col(c, 2, hs)
            o = jnp.dot(jnp.concatenate([sc, q_in.astype(BF16)], axis=1),
                        jnp.concatenate([vh.astype(BF16), state.astype(BF16)], axis=0),
                        preferred_element_type=F32)
            yield
            st_ref[0, 0, hd] = total.T * state + jnp.dot(
                k_out.T.astype(BF16), vh.astype(BF16), preferred_element_type=F32)
            yield
            mixed_ref[c * ROWS:(c + 1) * ROWS, hs] = _hgrn_finish(o, col(c, 3, hs), hg[:, hs])

        def mlp(c):
            v_n = _layer_norm(_gelu(col(c, 5)), lng, lnb)
            if c == n_chunks - 1:
                for hd in range(N_HEADS):
                    vch_ref[0, 0, :, hd, :] = v_n[:, hd * HEAD:(hd + 1) * HEAD]
            v_n16 = v_n.astype(BF16)
            yield
            for hd in range(N_HEADS):
                hs = slice(hd * HEAD, (hd + 1) * HEAD)
                mixed_ref[c * ROWS:(c + 1) * ROWS, D_A + hd * HEAD:D_A + (hd + 1) * HEAD] = (
                    _mlp_head(hd, col(c, 4, hs), v_n16[:, hs], col(c, 6, hs), wc_ref, bias))
                yield
                yield

        def projection(rounds_per_block):
            at_rounds = PROJ_ROUNDS if len(PROJ_ROUNDS) == n_blocks else [
                blk * rounds_per_block for blk in range(n_blocks)]
            rnd = 0
            for blk, at in enumerate(at_rounds):
                while rnd < at:
                    yield
                    rnd += 1
                cols = slice(blk * PROJ_BLOCK, (blk + 1) * PROJ_BLOCK)
                proj_next[:, cols] = jnp.dot(h_ref[...], win_ref[:, cols],
                                             preferred_element_type=F32)

        items = [gen for c in range(n_chunks)
                 for gen in [head(c, hd) for hd in range(N_HEADS)] + [mlp(c)]]
        starts = [c * CHUNK_STAGGER + k * HEAD_STAGGER
                  for c in range(n_chunks) for k in list(range(N_HEADS)) + [0]]
        rounds = (n_chunks - 1) * CHUNK_STAGGER + (N_HEADS - 1) * HEAD_STAGGER + HEAD_ROUNDS
        _lockstep(items + [projection(max(1, rounds // n_blocks))], starts + [0])

        out = xprev_ref[0] + jnp.dot(mixed_ref[...], wout16_ref[...], preferred_element_type=F32)
        y_ref[0] = _rms_norm(out, fg_ref[...])

    parity = lax.rem(g, 2)

    @pl.when(parity == 0)
    def _():
        step(proj_a, proj_b)

    @pl.when(parity == 1)
    def _():
        step(proj_b, proj_a)


def _sample_kernel(x_ref, ng_ref, win_ref, lbl_ref, hg_ref, lng_ref, lnb_ref, ws_ref, bs_ref,
                   wout_ref, fg_ref, stin_ref, y_ref, stout_ref, vch_ref,
                   qin_ref, kot_ref, tot_ref, oint_ref, lvl_ref, wc_ref, bias_ref, wout16_ref):
    @pl.when(pl.program_id(0) == 0)
    def _():
        _store_constants(DEC_SEQ, ws_ref, bs_ref, wout_ref, lvl_ref, wc_ref, bias_ref, wout16_ref)

    x = x_ref[...]
    h = _rms_norm(x, ng_ref[...]).astype(BF16)
    proj = jnp.dot(h, win_ref[...], preferred_element_type=F32)

    def col(k, hs=slice(0, D_A)):
        return proj[:, k * D_A + hs.start:k * D_A + hs.stop]

    lb = _lower_bound(lbl_ref[...])
    bits, n_levels = _row_bits(), _n_levels(DEC_SEQ)
    vi16 = col(2).astype(BF16)

    def head(hd):
        hs = slice(hd * HEAD, (hd + 1) * HEAD)

        def gates(i):
            rows = slice(i * SUBLANES, (i + 1) * SUBLANES)
            return _hgrn_gates(proj[rows, hs], proj[rows, D_A + hd * HEAD:D_A + (hd + 1) * HEAD],
                               lb[:, hs])

        sc, q_in, k_out, total = yield from _hgrn_head(gates, lvl_ref, bits, n_levels)
        qin_ref[:, hs] = q_in
        kot_ref[hd] = k_out.T
        tot_ref[hd] = total.T
        return jnp.dot(sc, vi16[:, hs], preferred_element_type=F32)

    o_intra = _lockstep([head(hd) for hd in range(N_HEADS)])

    lane = lax.broadcasted_iota(jnp.int32, (HEAD, ROWS), 1)

    def seq_head(i, hd):
        hs = slice(hd * HEAD, (hd + 1) * HEAD)
        seq_rows = slice(i * DEC_SEQ, (i + 1) * DEC_SEQ)
        state = stin_ref[0, i, hd]
        qi = qin_ref[seq_rows, hs].astype(BF16)
        oint_ref[seq_rows, hs] = jnp.dot(qi, state.astype(BF16), preferred_element_type=F32)
        yield
        in_seq = (lane >> (DEC_SEQ.bit_length() - 1)) == i
        k_seq = jnp.where(in_seq, kot_ref[hd], 0.0).astype(BF16)
        decay = tot_ref[hd, :, i * DEC_SEQ:i * DEC_SEQ + 1]
        yield
        stout_ref[0, i, hd] = decay * state + jnp.dot(
            k_seq, vi16[:, hs], preferred_element_type=F32)

    for grp in range(SEQS_PER_TILE // SEQ_GROUP):
        _lockstep([seq_head(grp * SEQ_GROUP + s, hd)
                   for s in range(SEQ_GROUP) for hd in range(N_HEADS)])

    hg = hg_ref[...]
    bias = bias_ref[...]
    v_n = _layer_norm(_gelu(col(5)), lng_ref[...], lnb_ref[...])
    for hd in range(N_HEADS):
        vch_ref[0, :, :, hd, :] = v_n[:, hd * HEAD:(hd + 1) * HEAD].reshape(
            SEQS_PER_TILE, DEC_SEQ, HEAD)
    v_n16 = v_n.astype(BF16)
    mixed = []
    for hd in range(N_HEADS):
        hs = slice(hd * HEAD, (hd + 1) * HEAD)
        mixed.append(_hgrn_finish(o_intra[hd] + oint_ref[:, hs], col(3, hs), hg[:, hs]))
    for hd in range(N_HEADS):
        hs = slice(hd * HEAD, (hd + 1) * HEAD)
        mixed.append(_mlp_head(hd, col(4, hs), v_n16[:, hs], col(6, hs), wc_ref, bias))
    out = x + jnp.dot(jnp.concatenate(mixed, axis=-1), wout16_ref[...],
                      preferred_element_type=F32)
    y_ref[...] = _rms_norm(out, fg_ref[...])


def _const_spec(shape):
    return pl.BlockSpec(shape, lambda *_: (0,) * len(shape), pipeline_mode=pl.Buffered(1))


_WEIGHT_SPECS = [
    _const_spec((1, D_MODEL)),
    _const_spec((D_MODEL, D_IN)),
    _const_spec((2, D_A)),
    _const_spec((1, D_A)),
    _const_spec((1, D_B)),
    _const_spec((1, D_B)),
    _const_spec((N_HEADS, ROWS, ROWS)),
    _const_spec((N_HEADS, ROWS)),
    _const_spec((D_MODEL, D_MODEL)),
    _const_spec((1, D_MODEL)),
]

_CONSTANT_SCRATCH = [
    pltpu.VMEM((ROWS, ROWS), jnp.int32),
    pltpu.VMEM((N_HEADS, ROWS, ROWS), BF16),
    pltpu.VMEM((ROWS, N_HEADS), F32),
    pltpu.VMEM((D_MODEL, D_MODEL), BF16),
]


def kernel(x_prompt, x_sample, state_hgrn, norm_g, w_in, lb_logits, hgrn_norm_g, sgu_ln_g,
           sgu_ln_b, w_s, b_s, w_out, final_norm_g):
    depth = norm_g.shape[0]
    assert depth == 1 and lb_logits.shape == (2, D_A)
    batch, seq, _ = x_prompt.shape
    dec_batch, dec_seq, _ = x_sample.shape
    assert seq % PROMPT_TILE == 0 and dec_seq == DEC_SEQ and dec_batch % SEQS_PER_TILE == 0

    weights = (norm_g, w_in[0].astype(BF16), lb_logits, hgrn_norm_g, sgu_ln_g, sgu_ln_b,
               w_s[0], b_s[0], w_out[0], final_norm_g.reshape(1, D_MODEL))
    params = pltpu.CompilerParams(dimension_semantics=("arbitrary",),
                                  vmem_limit_bytes=VMEM_LIMIT_BYTES)

    tiles_per_seq = seq // PROMPT_TILE
    n_tiles = batch * tiles_per_seq

    def cur_tile(g):
        t = jnp.minimum(g, n_tiles - 1)
        return t // tiles_per_seq, t % tiles_per_seq

    def prev_tile(g):
        t = jnp.maximum(g - 1, 0)
        return t // tiles_per_seq, t % tiles_per_seq

    y_p, st_p, vch_p = pl.pallas_call(
        functools.partial(_prompt_kernel, tiles_per_seq=tiles_per_seq),
        grid=(n_tiles + 1,),
        in_specs=[pl.BlockSpec((1, PROMPT_TILE, D_MODEL), lambda g: (*cur_tile(g), 0)),
                  pl.BlockSpec((1, PROMPT_TILE, D_MODEL), lambda g: (*prev_tile(g), 0))]
        + _WEIGHT_SPECS,
        out_specs=[
            pl.BlockSpec((1, PROMPT_TILE, D_MODEL), lambda g: (*prev_tile(g), 0)),
            pl.BlockSpec((1, 1, N_HEADS, HEAD, HEAD), lambda g: (0, prev_tile(g)[0], 0, 0, 0)),
            pl.BlockSpec((1, 1, ROWS, N_HEADS, HEAD), lambda g: (0, prev_tile(g)[0], 0, 0, 0)),
        ],
        out_shape=[
            jax.ShapeDtypeStruct((batch, seq, D_MODEL), F32),
            jax.ShapeDtypeStruct((1, batch, N_HEADS, HEAD, HEAD), F32),
            jax.ShapeDtypeStruct((1, batch, ROWS, N_HEADS, HEAD), F32),
        ],
        scratch_shapes=[pltpu.VMEM((PROMPT_TILE, D_IN), F32),
                        pltpu.VMEM((PROMPT_TILE, D_IN), F32),
                        pltpu.VMEM((PROMPT_TILE, D_MODEL), BF16),
                        pltpu.VMEM((PROMPT_TILE, D_MODEL), BF16)] + _CONSTANT_SCRATCH,
        compiler_params=params,
        name="prompt_layer",
    )(x_prompt, x_prompt, *weights)

    n_rows = dec_batch * DEC_SEQ
    y_s, st_s, vch_s = pl.pallas_call(
        _sample_kernel,
        grid=(n_rows // ROWS,),
        in_specs=[pl.BlockSpec((ROWS, D_MODEL), lambda i: (i, 0))]
        + _WEIGHT_SPECS
        + [pl.BlockSpec((1, SEQS_PER_TILE, N_HEADS, HEAD, HEAD), lambda i: (0, i, 0, 0, 0))],
        out_specs=[
            pl.BlockSpec((ROWS, D_MODEL), lambda i: (i, 0)),
            pl.BlockSpec((1, SEQS_PER_TILE, N_HEADS, HEAD, HEAD), lambda i: (0, i, 0, 0, 0)),
            pl.BlockSpec((1, SEQS_PER_TILE, DEC_SEQ, N_HEADS, HEAD), lambda i: (0, i, 0, 0, 0)),
        ],
        out_shape=[
            jax.ShapeDtypeStruct((n_rows, D_MODEL), F32),
            jax.ShapeDtypeStruct((1, dec_batch, N_HEADS, HEAD, HEAD), F32),
            jax.ShapeDtypeStruct((1, dec_batch, DEC_SEQ, N_HEADS, HEAD), F32),
        ],
        scratch_shapes=[pltpu.VMEM((ROWS, D_A), F32),
                        pltpu.VMEM((N_HEADS, HEAD, ROWS), F32),
                        pltpu.VMEM((N_HEADS, HEAD, ROWS), F32),
                        pltpu.VMEM((ROWS, D_A), F32)] + _CONSTANT_SCRATCH,
        compiler_params=pltpu.CompilerParams(dimension_semantics=("arbitrary",),
                                             vmem_limit_bytes=VMEM_LIMIT_BYTES),
        name="decode_layer",
    )(x_sample.reshape(n_rows, D_MODEL), *weights, state_hgrn)

    return (y_p,
            y_s.reshape(dec_batch, DEC_SEQ, D_MODEL),
            st_p,
            st_s,
            vch_p,
            vch_s)
```

```python
import functools

import jax
import jax.numpy as jnp
from jax import lax
from jax.experimental import pallas as pl
from jax.experimental.pallas import tpu as pltpu

F32 = jnp.float32
BF16 = jnp.bfloat16

D_MODEL = 1024
D_A = 512
D_B = 512
HEAD = 128
N_HEADS = 4
D_IN = 4 * D_A + 3 * D_B
ROWS = 128
SUBLANES = 8
PACKED = 2 * SUBLANES
N_TILES = ROWS // SUBLANES
SUBLANE_LEVELS = 3
EPS = 1e-6
PROMPT_TILE = 512
PROJ_BLOCK = 512
HEAD_ROUNDS = 12
CHUNK_STAGGER = 4
HEAD_STAGGER = 1
PROJ_ROUNDS = [2, 5, 8, 11, 14, 17, 20]
DEC_SEQ = 8
SEQS_PER_TILE = ROWS // DEC_SEQ
SEQ_GROUP = 4
V7X_VMEM_BYTES = 64 * 1024 * 1024
VMEM_LIMIT_BYTES = V7X_VMEM_BYTES * 7 // 8

_NT = (((1,), (1,)), ((), ()))


def _silu(x):
    return x * jax.nn.sigmoid(x)


def _gelu(x):
    c = -2.0 * 0.7978845608028654 * 1.4426950408889634
    return x / (1.0 + jnp.exp2(x * (c * 0.044715 * (x * x) + c)))


def _rms_norm(x, g):
    return x * lax.rsqrt(jnp.mean(x * x, axis=-1, keepdims=True) + EPS) * g


def _layer_norm(x, g, b):
    mu = jnp.mean(x, axis=-1, keepdims=True)
    xc = x - mu
    var = jnp.mean(xc * xc, axis=-1, keepdims=True)
    return xc * lax.rsqrt(var + EPS) * g + b


def _lower_bound(lbl):
    rows = [lbl[r:r + 1, :] for r in range(lbl.shape[0])]
    mx = functools.reduce(jnp.maximum, rows)
    es = [jnp.exp(r - mx) for r in rows]
    return es[0] / functools.reduce(lambda a, c: a + c, es)


def _n_levels(seg_len):
    n_levels = seg_len.bit_length() - 1
    assert seg_len == 1 << n_levels and n_levels >= SUBLANE_LEVELS
    return n_levels


def _split3(x):
    hi = x.astype(BF16)
    r1 = x - hi.astype(F32)
    mid = r1.astype(BF16)
    lo = (r1 - mid.astype(F32)).astype(BF16)
    return hi, mid, lo


def _store_constants(seg_len, ws_ref, bs_ref, wout_ref, lvl_ref, wc_ref, bias_ref, wout16_ref):
    t = lax.broadcasted_iota(jnp.int32, (ROWS, ROWS), 0)
    s = lax.broadcasted_iota(jnp.int32, (ROWS, ROWS), 1)
    n_levels = _n_levels(seg_len)
    x = t ^ s
    lvl = jnp.where(t == s, n_levels, -1)
    for j in range(n_levels):
        lvl = jnp.where(((x >> j) == 1) & (((t >> j) & 1) == 1), j, lvl)
    lvl_ref[...] = lvl
    causal = (t >= s) & ((t >> n_levels) == (s >> n_levels))
    pick = jnp.where(s == (t & (seg_len - 1)), 1.0, 0.0).astype(BF16)
    for hd in range(N_HEADS):
        w = ws_ref[hd].astype(BF16)
        if seg_len != ROWS:
            w = jnp.dot(pick, w, preferred_element_type=F32).astype(BF16)
            w = lax.dot_general(w, pick, _NT, preferred_element_type=F32).astype(BF16)
        wc_ref[hd] = jnp.where(causal, w, jnp.zeros_like(w))
    pad = jnp.zeros((PACKED - N_HEADS, ROWS), F32)
    bias = sum(lax.dot_general(pick, part, _NT, preferred_element_type=F32)
               for part in _split3(jnp.concatenate([bs_ref[...], pad], axis=0)))
    bias_ref[...] = bias[:, :N_HEADS]
    wout16_ref[...] = wout_ref[...].astype(BF16)


def _row_bits():
    rowi = lax.broadcasted_iota(jnp.int32, (SUBLANES, HEAD), 0)
    return [((rowi >> j) & 1) == 1 for j in range(SUBLANE_LEVELS)]


def _hgrn_head(gates, lvl_ref, bits, n_levels):
    one = jnp.ones((SUBLANES, HEAD), F32)
    z = [[] for _ in range(n_levels)]
    qp_t, ks_t, blk_t, diag_t = [], [], [], []
    for pair in range(N_TILES // 2):
        z_pair = [[] for _ in range(SUBLANE_LEVELS)]
        for i in (2 * pair, 2 * pair + 1):
            q, kk, f = gates(i)
            diag_t.append(jnp.sum(q * kk, axis=-1, keepdims=True))
            qp, ks, blk = q * f, kk, f
            for j in range(SUBLANE_LEVELS):
                m = 1 << j
                z_pair[j].append(jnp.where(bits[j], qp, ks))
                sib = pltpu.roll(blk, m, 0)
                if 2 * m != SUBLANES:
                    sib = jnp.where(bits[j], sib, pltpu.roll(blk, SUBLANES - m, 0))
                qp = qp * jnp.where(bits[j], sib, one)
                ks = ks * jnp.where(bits[j], one, sib)
                blk = blk * sib
            qp_t.append(qp)
            ks_t.append(ks)
            blk_t.append(blk)
        for j in range(SUBLANE_LEVELS):
            z[j].append(jnp.concatenate(z_pair[j], axis=0).astype(BF16))
    yield

    blk_levels, blk_b = [], blk_t
    for j in range(SUBLANE_LEVELS, n_levels):
        blk_levels.append(blk_b)
        blk_b = [blk_b[2 * n] * blk_b[2 * n + 1] for n in range(len(blk_b) // 2)]
    for pair in range(N_TILES // 2):
        z_pair = [[] for _ in range(SUBLANE_LEVELS, n_levels)]
        for i in (2 * pair, 2 * pair + 1):
            for j in range(SUBLANE_LEVELS, n_levels):
                block = i >> (j - SUBLANE_LEVELS)
                sib = blk_levels[j - SUBLANE_LEVELS][block ^ 1]
                if block & 1:
                    z_pair[j - SUBLANE_LEVELS].append(qp_t[i])
                    qp_t[i] = qp_t[i] * sib
                else:
                    z_pair[j - SUBLANE_LEVELS].append(ks_t[i])
                    ks_t[i] = ks_t[i] * sib
        for j in range(SUBLANE_LEVELS, n_levels):
            z[j].append(jnp.concatenate(z_pair[j - SUBLANE_LEVELS], axis=0).astype(BF16))
    q_in = jnp.concatenate(qp_t, axis=0)
    k_out = jnp.concatenate(ks_t, axis=0)
    per_tile = N_TILES // len(blk_b)
    total = jnp.concatenate([blk_b[i // per_tile] for i in range(N_TILES)], axis=0)
    yield

    p, p_row = [], []
    for j in range(n_levels):
        zj = jnp.concatenate(z[j], axis=0)
        span = 1 << max(j - SUBLANE_LEVELS, 0)
        if span * SUBLANES >= PACKED:
            upper = [i for i in range(N_TILES) if (i // span) & 1]
            lhs = jnp.concatenate([z[j][i // 2] for i in upper[::2]], axis=0)
        else:
            upper = list(range(N_TILES))
            lhs = zj
        p.append(lax.dot_general(lhs, zj, _NT, preferred_element_type=F32))
        p_row.append({i: n * SUBLANES for n, i in enumerate(upper)})
        yield
    sc_t = []
    for i in range(N_TILES):
        lvl_i = lvl_ref[i * SUBLANES:(i + 1) * SUBLANES, :]
        acc = jnp.where(lvl_i == n_levels, diag_t[i], 0.0)
        for j in range(n_levels):
            if j < SUBLANE_LEVELS or (i >> (j - SUBLANE_LEVELS)) & 1:
                r0 = p_row[j][i]
                acc = jnp.where(lvl_i == j, p[j][r0:r0 + SUBLANES, :], acc)
        sc_t.append(acc)
    sc = jnp.concatenate(sc_t, axis=0).astype(BF16)
    return sc, q_in, k_out, total


def _lockstep(generators, starts=None):
    starts = starts or [0] * len(generators)
    results = [None] * len(generators)
    waiting = sorted(range(len(generators)), key=lambda i: starts[i])
    active = []
    rnd = 0
    while waiting or active:
        while waiting and starts[waiting[0]] <= rnd:
            active.append(waiting.pop(0))
        still = []
        for idx in active:
            try:
                next(generators[idx])
                still.append(idx)
            except StopIteration as done:
                results[idx] = done.value
        active = still
        rnd += 1
    return results


def _hgrn_gates(pq, pf, lb):
    q = _silu(pq)
    forget = lb + (1.0 - lb) * jax.nn.sigmoid(pf)
    return q, 1.0 - forget, forget


def _hgrn_finish(o, pz, g):
    return (_rms_norm(o, g) * _silu(pz)).astype(BF16)


def _mlp_head(hd, u, v_n16, gate, wc_ref, bias):
    mix = jnp.dot(wc_ref[hd], v_n16, preferred_element_type=F32) + bias[:, hd:hd + 1]
    return (_gelu(u) * mix * _silu(gate)).astype(BF16)


def _prompt_kernel(x_ref, xprev_ref, ng_ref, win_ref, lbl_ref, hg_ref, lng_ref, lnb_ref, ws_ref,
                   bs_ref, wout_ref, fg_ref, y_ref, st_ref, vch_ref, proj_a, proj_b, h_ref,
                   mixed_ref, lvl_ref, wc_ref, bias_ref, wout16_ref, *, tiles_per_seq):
    g = pl.program_id(0)

    @pl.when(g == 0)
    def _():
        proj_b[...] = jnp.zeros_like(proj_b)
        _store_constants(ROWS, ws_ref, bs_ref, wout_ref, lvl_ref, wc_ref, bias_ref, wout16_ref)

    @pl.when(lax.rem(jnp.maximum(g - 1, 0), tiles_per_seq) == 0)
    def _():
        st_ref[...] = jnp.zeros_like(st_ref)

    def step(proj_next, proj_prev):
        h_ref[...] = _rms_norm(x_ref[0], ng_ref[...]).astype(BF16)
        lb = _lower_bound(lbl_ref[...])
        bits, n_levels = _row_bits(), _n_levels(ROWS)
        hg = hg_ref[...]
        lng = lng_ref[...]
        lnb = lnb_ref[...]
        bias = bias_ref[...]
        n_chunks = PROMPT_TILE // ROWS
        n_blocks = D_IN // PROJ_BLOCK

        def col(c, k, hs=slice(0, D_A)):
            return proj_prev[c * ROWS:(c + 1) * ROWS, k * D_A + hs.start:k * D_A + hs.stop]

        def head(c, hd):
            hs = slice(hd * HEAD, (hd + 1) * HEAD)

            def gates(i):
                rows = slice(c * ROWS + i * SUBLANES, c * ROWS + (i + 1) * SUBLANES)
                return _hgrn_gates(proj_prev[rows, hs],
                                   proj_prev[rows, D_A + hd * HEAD:D_A + (hd + 1) * HEAD],
                                   lb[:, hs])

            sc, q_in, k_out, total = yield from _hgrn_head(gates, lvl_ref, bits, n_levels)
            state = st_ref[0, 0, hd]
            vh = col(c, 2, hs)
            o = jnp.dot(jnp.concatenate([sc, q_in.astype(BF16)], axis=1),
                        jnp.concatenate([vh.astype(BF16), state.astype(BF16)], axis=0),
                        preferred_element_type=F32)
            yield
            st_ref[0, 0, hd] = total.T * state + jnp.dot(
                k_out.T.astype(BF16), vh.astype(BF16), preferred_element_type=F32)
            yield
            mixed_ref[c * ROWS:(c + 1) * ROWS, hs] = _hgrn_finish(o, col(c, 3, hs), hg[:, hs])

        def mlp(c):
            v_n = _layer_norm(_gelu(col(c, 5)), lng, lnb)
            if c == n_chunks - 1:
                for hd in range(N_HEADS):
                    vch_ref[0, 0, :, hd, :] = v_n[:, hd * HEAD:(hd + 1) * HEAD]
            v_n16 = v_n.astype(BF16)
            yield
            for hd in range(N_HEADS):
                hs = slice(hd * HEAD, (hd + 1) * HEAD)
                mixed_ref[c * ROWS:(c + 1) * ROWS, D_A + hd * HEAD:D_A + (hd + 1) * HEAD] = (
                    _mlp_head(hd, col(c, 4, hs), v_n16[:, hs], col(c, 6, hs), wc_ref, bias))
                yield
                yield

        def projection(rounds_per_block):
            at_rounds = PROJ_ROUNDS if len(PROJ_ROUNDS) == n_blocks else [
                blk * rounds_per_block for blk in range(n_blocks)]
            rnd = 0
            for blk, at in enumerate(at_rounds):
                while rnd < at:
                    yield
                    rnd += 1
                cols = slice(blk * PROJ_BLOCK, (blk + 1) * PROJ_BLOCK)
                proj_next[:, cols] = jnp.dot(h_ref[...], win_ref[:, cols],
                                             preferred_element_type=F32)

        items = [gen for c in range(n_chunks)
                 for gen in [head(c, hd) for hd in range(N_HEADS)] + [mlp(c)]]
        starts = [c * CHUNK_STAGGER + k * HEAD_STAGGER
                  for c in range(n_chunks) for k in list(range(N_HEADS)) + [0]]
        rounds = (n_chunks - 1) * CHUNK_STAGGER + (N_HEADS - 1) * HEAD_STAGGER + HEAD_ROUNDS
        _lockstep(items + [projection(max(1, rounds // n_blocks))], starts + [0])

        out = xprev_ref[0] + jnp.dot(mixed_ref[...], wout16_ref[...], preferred_element_type=F32)
        y_ref[0] = _rms_norm(out, fg_ref[...])

    parity = lax.rem(g, 2)

    @pl.when(parity == 0)
    def _():
        step(proj_a, proj_b)

    @pl.when(parity == 1)
    def _():
        step(proj_b, proj_a)


def _sample_kernel(x_ref, ng_ref, win_ref, lbl_ref, hg_ref, lng_ref, lnb_ref, ws_ref, bs_ref,
                   wout_ref, fg_ref, stin_ref, y_ref, stout_ref, vch_ref,
                   qin_ref, kot_ref, tot_ref, oint_ref, lvl_ref, wc_ref, bias_ref, wout16_ref):
    @pl.when(pl.program_id(0) == 0)
    def _():
        _store_constants(DEC_SEQ, ws_ref, bs_ref, wout_ref, lvl_ref, wc_ref, bias_ref, wout16_ref)

    x = x_ref[...]
    h = _rms_norm(x, ng_ref[...]).astype(BF16)
    proj = jnp.dot(h, win_ref[...], preferred_element_type=F32)

    def col(k, hs=slice(0, D_A)):
        return proj[:, k * D_A + hs.start:k * D_A + hs.stop]

    lb = _lower_bound(lbl_ref[...])
    bits, n_levels = _row_bits(), _n_levels(DEC_SEQ)
    vi16 = col(2).astype(BF16)

    def head(hd):
        hs = slice(hd * HEAD, (hd + 1) * HEAD)

        def gates(i):
            rows = slice(i * SUBLANES, (i + 1) * SUBLANES)
            return _hgrn_gates(proj[rows, hs], proj[rows, D_A + hd * HEAD:D_A + (hd + 1) * HEAD],
                               lb[:, hs])

        sc, q_in, k_out, total = yield from _hgrn_head(gates, lvl_ref, bits, n_levels)
        qin_ref[:, hs] = q_in
        kot_ref[hd] = k_out.T
        tot_ref[hd] = total.T
        return jnp.dot(sc, vi16[:, hs], preferred_element_type=F32)

    o_intra = _lockstep([head(hd) for hd in range(N_HEADS)])

    lane = lax.broadcasted_iota(jnp.int32, (HEAD, ROWS), 1)

    def seq_head(i, hd):
        hs = slice(hd * HEAD, (hd + 1) * HEAD)
        seq_rows = slice(i * DEC_SEQ, (i + 1) * DEC_SEQ)
        state = stin_ref[0, i, hd]
        qi = qin_ref[seq_rows, hs].astype(BF16)
        oint_ref[seq_rows, hs] = jnp.dot(qi, state.astype(BF16), preferred_element_type=F32)
        yield
        in_seq = (lane >> (DEC_SEQ.bit_length() - 1)) == i
        k_seq = jnp.where(in_seq, kot_ref[hd], 0.0).astype(BF16)
        decay = tot_ref[hd, :, i * DEC_SEQ:i * DEC_SEQ + 1]
        yield
        stout_ref[0, i, hd] = decay * state + jnp.dot(
            k_seq, vi16[:, hs], preferred_element_type=F32)

    for grp in range(SEQS_PER_TILE // SEQ_GROUP):
        _lockstep([seq_head(grp * SEQ_GROUP + s, hd)
                   for s in range(SEQ_GROUP) for hd in range(N_HEADS)])

    hg = hg_ref[...]
    bias = bias_ref[...]
    v_n = _layer_norm(_gelu(col(5)), lng_ref[...], lnb_ref[...])
    for hd in range(N_HEADS):
        vch_ref[0, :, :, hd, :] = v_n[:, hd * HEAD:(hd + 1) * HEAD].reshape(
            SEQS_PER_TILE, DEC_SEQ, HEAD)
    v_n16 = v_n.astype(BF16)
    mixed = []
    for hd in range(N_HEADS):
        hs = slice(hd * HEAD, (hd + 1) * HEAD)
        mixed.append(_hgrn_finish(o_intra[hd] + oint_ref[:, hs], col(3, hs), hg[:, hs]))
    for hd in range(N_HEADS):
        hs = slice(hd * HEAD, (hd + 1) * HEAD)
        mixed.append(_mlp_head(hd, col(4, hs), v_n16[:, hs], col(6, hs), wc_ref, bias))
    out = x + jnp.dot(jnp.concatenate(mixed, axis=-1), wout16_ref[...],
                      preferred_element_type=F32)
    y_ref[...] = _rms_norm(out, fg_ref[...])


def _const_spec(shape):
    return pl.BlockSpec(shape, lambda *_: (0,) * len(shape), pipeline_mode=pl.Buffered(1))


_WEIGHT_SPECS = [
    _const_spec((1, D_MODEL)),
    _const_spec((D_MODEL, D_IN)),
    _const_spec((2, D_A)),
    _const_spec((1, D_A)),
    _const_spec((1, D_B)),
    _const_spec((1, D_B)),
    _const_spec((N_HEADS, ROWS, ROWS)),
    _const_spec((N_HEADS, ROWS)),
    _const_spec((D_MODEL, D_MODEL)),
    _const_spec((1, D_MODEL)),
]

_CONSTANT_SCRATCH = [
    pltpu.VMEM((ROWS, ROWS), jnp.int32),
    pltpu.VMEM((N_HEADS, ROWS, ROWS), BF16),
    pltpu.VMEM((ROWS, N_HEADS), F32),
    pltpu.VMEM((D_MODEL, D_MODEL), BF16),
]


def kernel(x_prompt, x_sample, state_hgrn, norm_g, w_in, lb_logits, hgrn_norm_g, sgu_ln_g,
           sgu_ln_b, w_s, b_s, w_out, final_norm_g):
    depth = norm_g.shape[0]
    assert depth == 1 and lb_logits.shape == (2, D_A)
    batch, seq, _ = x_prompt.shape
    dec_batch, dec_seq, _ = x_sample.shape
    assert seq % PROMPT_TILE == 0 and dec_seq == DEC_SEQ and dec_batch % SEQS_PER_TILE == 0

    weights = (norm_g, w_in[0].astype(BF16), lb_logits, hgrn_norm_g, sgu_ln_g, sgu_ln_b,
               w_s[0], b_s[0], w_out[0], final_norm_g.reshape(1, D_MODEL))
    params = pltpu.CompilerParams(dimension_semantics=("arbitrary",),
                                  vmem_limit_bytes=VMEM_LIMIT_BYTES)

    tiles_per_seq = seq // PROMPT_TILE
    n_tiles = batch * tiles_per_seq

    def cur_tile(g):
        t = jnp.minimum(g, n_tiles - 1)
        return t // tiles_per_seq, t % tiles_per_seq

    def prev_tile(g):
        t = jnp.maximum(g - 1, 0)
        return t // tiles_per_seq, t % tiles_per_seq

    y_p, st_p, vch_p = pl.pallas_call(
        functools.partial(_prompt_kernel, tiles_per_seq=tiles_per_seq),
        grid=(n_tiles + 1,),
        in_specs=[pl.BlockSpec((1, PROMPT_TILE, D_MODEL), lambda g: (*cur_tile(g), 0)),
                  pl.BlockSpec((1, PROMPT_TILE, D_MODEL), lambda g: (*prev_tile(g), 0))]
        + _WEIGHT_SPECS,
        out_specs=[
            pl.BlockSpec((1, PROMPT_TILE, D_MODEL), lambda g: (*prev_tile(g), 0)),
            pl.BlockSpec((1, 1, N_HEADS, HEAD, HEAD), lambda g: (0, prev_tile(g)[0], 0, 0, 0)),
            pl.BlockSpec((1, 1, ROWS, N_HEADS, HEAD), lambda g: (0, prev_tile(g)[0], 0, 0, 0)),
        ],
        out_shape=[
            jax.ShapeDtypeStruct((batch, seq, D_MODEL), F32),
            jax.ShapeDtypeStruct((1, batch, N_HEADS, HEAD, HEAD), F32),
            jax.ShapeDtypeStruct((1, batch, ROWS, N_HEADS, HEAD), F32),
        ],
        scratch_shapes=[pltpu.VMEM((PROMPT_TILE, D_IN), F32),
                        pltpu.VMEM((PROMPT_TILE, D_IN), F32),
                        pltpu.VMEM((PROMPT_TILE, D_MODEL), BF16),
                        pltpu.VMEM((PROMPT_TILE, D_MODEL), BF16)] + _CONSTANT_SCRATCH,
        compiler_params=params,
        name="prompt_layer",
    )(x_prompt, x_prompt, *weights)

    n_rows = dec_batch * DEC_SEQ
    y_s, st_s, vch_s = pl.pallas_call(
        _sample_kernel,
        grid=(n_rows // ROWS,),
        in_specs=[pl.BlockSpec((ROWS, D_MODEL), lambda i: (i, 0))]
        + _WEIGHT_SPECS
        + [pl.BlockSpec((1, SEQS_PER_TILE, N_HEADS, HEAD, HEAD), lambda i: (0, i, 0, 0, 0))],
        out_specs=[
            pl.BlockSpec((ROWS, D_MODEL), lambda i: (i, 0)),
            pl.BlockSpec((1, SEQS_PER_TILE, N_HEADS, HEAD, HEAD), lambda i: (0, i, 0, 0, 0)),
            pl.BlockSpec((1, SEQS_PER_TILE, DEC_SEQ, N_HEADS, HEAD), lambda i: (0, i, 0, 0, 0)),
        ],
        out_shape=[
            jax.ShapeDtypeStruct((n_rows, D_MODEL), F32),
            jax.ShapeDtypeStruct((1, dec_batch, N_HEADS, HEAD, HEAD), F32),
            jax.ShapeDtypeStruct((1, dec_batch, DEC_SEQ, N_HEADS, HEAD), F32),
        ],
        scratch_shapes=[pltpu.VMEM((ROWS, D_A), F32),
                        pltpu.VMEM((N_HEADS, HEAD, ROWS), F32),
                        pltpu.VMEM((N_HEADS, HEAD, ROWS), F32),
                        pltpu.VMEM((ROWS, D_A), F32)] + _CONSTANT_SCRATCH,
        compiler_params=pltpu.CompilerParams(dimension_semantics=("arbitrary",),
                                             vmem_limit_bytes=VMEM_LIMIT_BYTES),
        name="decode_layer",
    )(x_sample.reshape(n_rows, D_MODEL), *weights, state_hgrn)

    return (y_p,
            y_s.reshape(dec_batch, DEC_SEQ, D_MODEL),
            st_p,
            st_s,
            vch_p,
            vch_s)
```

```python
import functools

import jax
import jax.numpy as jnp
from jax import lax
from jax.experimental import pallas as pl
from jax.experimental.pallas import tpu as pltpu

F32 = jnp.float32
BF16 = jnp.bfloat16

D_MODEL = 1024
D_A = 512
D_B = 512
HEAD = 128
N_HEADS = 4
D_IN = 4 * D_A + 3 * D_B
ROWS = 128
SUBLANES = 8
PACKED = 2 * SUBLANES
N_TILES = ROWS // SUBLANES
SUBLANE_LEVELS = 3
EPS = 1e-6
PROMPT_TILE = 512
PROJ_BLOCK = 512
HEAD_ROUNDS = 12
CHUNK_STAGGER = 4
HEAD_STAGGER = 1
PROJ_ROUNDS = [0, 4, 8, 12, 16, 20, 24]
DEC_SEQ = 8
SEQS_PER_TILE = ROWS // DEC_SEQ
SEQ_GROUP = 4
V7X_VMEM_BYTES = 64 * 1024 * 1024
VMEM_LIMIT_BYTES = V7X_VMEM_BYTES * 7 // 8

_NT = (((1,), (1,)), ((), ()))


def _silu(x):
    return x * jax.nn.sigmoid(x)


def _gelu(x):
    c = -2.0 * 0.7978845608028654 * 1.4426950408889634
    return x / (1.0 + jnp.exp2(x * (c * 0.044715 * (x * x) + c)))


def _rms_norm(x, g):
    return x * lax.rsqrt(jnp.mean(x * x, axis=-1, keepdims=True) + EPS) * g


def _layer_norm(x, g, b):
    mu = jnp.mean(x, axis=-1, keepdims=True)
    xc = x - mu
    var = jnp.mean(xc * xc, axis=-1, keepdims=True)
    return xc * lax.rsqrt(var + EPS) * g + b


def _lower_bound(lbl):
    rows = [lbl[r:r + 1, :] for r in range(lbl.shape[0])]
    mx = functools.reduce(jnp.maximum, rows)
    es = [jnp.exp(r - mx) for r in rows]
    return es[0] / functools.reduce(lambda a, c: a + c, es)


def _n_levels(seg_len):
    n_levels = seg_len.bit_length() - 1
    assert seg_len == 1 << n_levels and n_levels >= SUBLANE_LEVELS
    return n_levels


def _split3(x):
    hi = x.astype(BF16)
    r1 = x - hi.astype(F32)
    mid = r1.astype(BF16)
    lo = (r1 - mid.astype(F32)).astype(BF16)
    return hi, mid, lo


def _store_constants(seg_len, ws_ref, bs_ref, wout_ref, lvl_ref, wc_ref, bias_ref, wout16_ref):
    t = lax.broadcasted_iota(jnp.int32, (ROWS, ROWS), 0)
    s = lax.broadcasted_iota(jnp.int32, (ROWS, ROWS), 1)
    n_levels = _n_levels(seg_len)
    x = t ^ s
    lvl = jnp.where(t == s, n_levels, -1)
    for j in range(n_levels):
        lvl = jnp.where(((x >> j) == 1) & (((t >> j) & 1) == 1), j, lvl)
    lvl_ref[...] = lvl
    causal = (t >= s) & ((t >> n_levels) == (s >> n_levels))
    pick = jnp.where(s == (t & (seg_len - 1)), 1.0, 0.0).astype(BF16)
    for hd in range(N_HEADS):
        w = ws_ref[hd].astype(BF16)
        if seg_len != ROWS:
            w = jnp.dot(pick, w, preferred_element_type=F32).astype(BF16)
            w = lax.dot_general(w, pick, _NT, preferred_element_type=F32).astype(BF16)
        wc_ref[hd] = jnp.where(causal, w, jnp.zeros_like(w))
    pad = jnp.zeros((PACKED - N_HEADS, ROWS), F32)
    bias = sum(lax.dot_general(pick, part, _NT, preferred_element_type=F32)
               for part in _split3(jnp.concatenate([bs_ref[...], pad], axis=0)))
    bias_ref[...] = bias[:, :N_HEADS]
    wout16_ref[...] = wout_ref[...].astype(BF16)


def _row_bits():
    rowi = lax.broadcasted_iota(jnp.int32, (SUBLANES, HEAD), 0)
    return [((rowi >> j) & 1) == 1 for j in range(SUBLANE_LEVELS)]


def _hgrn_head(gates, lvl_ref, bits, n_levels):
    one = jnp.ones((SUBLANES, HEAD), F32)
    z = [[] for _ in range(n_levels)]
    qp_t, ks_t, blk_t, diag_t = [], [], [], []
    for pair in range(N_TILES // 2):
        z_pair = [[] for _ in range(SUBLANE_LEVELS)]
        for i in (2 * pair, 2 * pair + 1):
            q, kk, f = gates(i)
            diag_t.append(jnp.sum(q * kk, axis=-1, keepdims=True))
            qp, ks, blk = q * f, kk, f
            for j in range(SUBLANE_LEVELS):
                m = 1 << j
                z_pair[j].append(jnp.where(bits[j], qp, ks))
                sib = pltpu.roll(blk, m, 0)
                if 2 * m != SUBLANES:
                    sib = jnp.where(bits[j], sib, pltpu.roll(blk, SUBLANES - m, 0))
                qp = qp * jnp.where(bits[j], sib, one)
                ks = ks * jnp.where(bits[j], one, sib)
                blk = blk * sib
            qp_t.append(qp)
            ks_t.append(ks)
            blk_t.append(blk)
        for j in range(SUBLANE_LEVELS):
            z[j].append(jnp.concatenate(z_pair[j], axis=0).astype(BF16))
    yield

    blk_levels, blk_b = [], blk_t
    for j in range(SUBLANE_LEVELS, n_levels):
        blk_levels.append(blk_b)
        blk_b = [blk_b[2 * n] * blk_b[2 * n + 1] for n in range(len(blk_b) // 2)]
    for pair in range(N_TILES // 2):
        z_pair = [[] for _ in range(SUBLANE_LEVELS, n_levels)]
        for i in (2 * pair, 2 * pair + 1):
            for j in range(SUBLANE_LEVELS, n_levels):
                block = i >> (j - SUBLANE_LEVELS)
                sib = blk_levels[j - SUBLANE_LEVELS][block ^ 1]
                if block & 1:
                    z_pair[j - SUBLANE_LEVELS].append(qp_t[i])
                    qp_t[i] = qp_t[i] * sib
                else:
                    z_pair[j - SUBLANE_LEVELS].append(ks_t[i])
                    ks_t[i] = ks_t[i] * sib
        for j in range(SUBLANE_LEVELS, n_levels):
            z[j].append(jnp.concatenate(z_pair[j - SUBLANE_LEVELS], axis=0).astype(BF16))
    q_in = jnp.concatenate(qp_t, axis=0)
    k_out = jnp.concatenate(ks_t, axis=0)
    per_tile = N_TILES // len(blk_b)
    total = jnp.concatenate([blk_b[i // per_tile] for i in range(N_TILES)], axis=0)
    yield

    p, p_row = [], []
    for j in range(n_levels):
        zj = jnp.concatenate(z[j], axis=0)
        span = 1 << max(j - SUBLANE_LEVELS, 0)
        if span * SUBLANES >= PACKED:
            upper = [i for i in range(N_TILES) if (i // span) & 1]
            lhs = jnp.concatenate([z[j][i // 2] for i in upper[::2]], axis=0)
        else:
            upper = list(range(N_TILES))
            lhs = zj
        p.append(lax.dot_general(lhs, zj, _NT, preferred_element_type=F32))
        p_row.append({i: n * SUBLANES for n, i in enumerate(upper)})
        yield
    sc_t = []
    for i in range(N_TILES):
        lvl_i = lvl_ref[i * SUBLANES:(i + 1) * SUBLANES, :]
        acc = jnp.where(lvl_i == n_levels, diag_t[i], 0.0)
        for j in range(n_levels):
            if j < SUBLANE_LEVELS or (i >> (j - SUBLANE_LEVELS)) & 1:
                r0 = p_row[j][i]
                acc = jnp.where(lvl_i == j, p[j][r0:r0 + SUBLANES, :], acc)
        sc_t.append(acc)
    sc = jnp.concatenate(sc_t, axis=0).astype(BF16)
    return sc, q_in, k_out, total


def _lockstep(generators, starts=None):
    starts = starts or [0] * len(generators)
    results = [None] * len(generators)
    waiting = sorted(range(len(generators)), key=lambda i: starts[i])
    active = []
    rnd = 0
    while waiting or active:
        while waiting and starts[waiting[0]] <= rnd:
            active.append(waiting.pop(0))
        still = []
        for idx in active:
            try:
                next(generators[idx])
                still.append(idx)
            except StopIteration as done:
                results[idx] = done.value
        active = still
        rnd += 1
    return results


def _hgrn_gates(pq, pf, lb):
    q = _silu(pq)
    forget = lb + (1.0 - lb) * jax.nn.sigmoid(pf)
    return q, 1.0 - forget, forget


def _hgrn_finish(o, pz, g):
    return (_rms_norm(o, g) * _silu(pz)).astype(BF16)


def _mlp_head(hd, u, v_n16, gate, wc_ref, bias):
    mix = jnp.dot(wc_ref[hd], v_n16, preferred_element_type=F32) + bias[:, hd:hd + 1]
    return (_gelu(u) * mix * _silu(gate)).astype(BF16)


def _prompt_kernel(x_ref, xprev_ref, ng_ref, win_ref, lbl_ref, hg_ref, lng_ref, lnb_ref, ws_ref,
                   bs_ref, wout_ref, fg_ref, y_ref, st_ref, vch_ref, proj_a, proj_b, h_ref,
                   mixed_ref, lvl_ref, wc_ref, bias_ref, wout16_ref, *, tiles_per_seq):
    g = pl.program_id(0)

    @pl.when(g == 0)
    def _():
        proj_b[...] = jnp.zeros_like(proj_b)
        _store_constants(ROWS, ws_ref, bs_ref, wout_ref, lvl_ref, wc_ref, bias_ref, wout16_ref)

    @pl.when(lax.rem(jnp.maximum(g - 1, 0), tiles_per_seq) == 0)
    def _():
        st_ref[...] = jnp.zeros_like(st_ref)

    def step(proj_next, proj_prev):
        h_ref[...] = _rms_norm(x_ref[0], ng_ref[...]).astype(BF16)
        lb = _lower_bound(lbl_ref[...])
        bits, n_levels = _row_bits(), _n_levels(ROWS)
        hg = hg_ref[...]
        lng = lng_ref[...]
        lnb = lnb_ref[...]
        bias = bias_ref[...]
        n_chunks = PROMPT_TILE // ROWS
        n_blocks = D_IN // PROJ_BLOCK

        def col(c, k, hs=slice(0, D_A)):
            return proj_prev[c * ROWS:(c + 1) * ROWS, k * D_A + hs.start:k * D_A + hs.stop]

        def head(c, hd):
            hs = slice(hd * HEAD, (hd + 1) * HEAD)

            def gates(i):
                rows = slice(c * ROWS + i * SUBLANES, c * ROWS + (i + 1) * SUBLANES)
                return _hgrn_gates(proj_prev[rows, hs],
                                   proj_prev[rows, D_A + hd * HEAD:D_A + (hd + 1) * HEAD],
                                   lb[:, hs])

            sc, q_in, k_out, total = yield from _hgrn_head(gates, lvl_ref, bits, n_levels)
            state = st_ref[0, 0, hd]
            vh = col(c, 2, hs)
            o = jnp.dot(jnp.concatenate([sc, q_in.astype(BF16)], axis=1),
                        jnp.concatenate([vh.astype(BF16), state.astype(BF16)], axis=0),
                        preferred_element_type=F32)
            yield
            st_ref[0, 0, hd] = total.T * state + jnp.dot(
                k_out.T.astype(BF16), vh.astype(BF16), preferred_element_type=F32)
            yield
            mixed_ref[c * ROWS:(c + 1) * ROWS, hs] = _hgrn_finish(o, col(c, 3, hs), hg[:, hs])

        def mlp(c):
            v_n = _layer_norm(_gelu(col(c, 5)), lng, lnb)
            if c == n_chunks - 1:
                for hd in range(N_HEADS):
                    vch_ref[0, 0, :, hd, :] = v_n[:, hd * HEAD:(hd + 1) * HEAD]
            v_n16 = v_n.astype(BF16)
            yield
            for hd in range(N_HEADS):
                hs = slice(hd * HEAD, (hd + 1) * HEAD)
                mixed_ref[c * ROWS:(c + 1) * ROWS, D_A + hd * HEAD:D_A + (hd + 1) * HEAD] = (
                    _mlp_head(hd, col(c, 4, hs), v_n16[:, hs], col(c, 6, hs), wc_ref, bias))
                yield
                yield

        def projection(rounds_per_block):
            at_rounds = PROJ_ROUNDS if len(PROJ_ROUNDS) == n_blocks else [
                blk * rounds_per_block for blk in range(n_blocks)]
            rnd = 0
            for blk, at in enumerate(at_rounds):
                while rnd < at:
                    yield
                    rnd += 1
                cols = slice(blk * PROJ_BLOCK, (blk + 1) * PROJ_BLOCK)
                proj_next[:, cols] = jnp.dot(h_ref[...], win_ref[:, cols],
                                             preferred_element_type=F32)

        items = [gen for c in range(n_chunks)
                 for gen in [head(c, hd) for hd in range(N_HEADS)] + [mlp(c)]]
        starts = [c * CHUNK_STAGGER + k * HEAD_STAGGER
                  for c in range(n_chunks) for k in list(range(N_HEADS)) + [0]]
        rounds = (n_chunks - 1) * CHUNK_STAGGER + (N_HEADS - 1) * HEAD_STAGGER + HEAD_ROUNDS
        _lockstep(items + [projection(max(1, rounds // n_blocks))], starts + [0])

        out = xprev_ref[0] + jnp.dot(mixed_ref[...], wout16_ref[...], preferred_element_type=F32)
        y_ref[0] = _rms_norm(out, fg_ref[...])

    parity = lax.rem(g, 2)

    @pl.when(parity == 0)
    def _():
        step(proj_a, proj_b)

    @pl.when(parity == 1)
    def _():
        step(proj_b, proj_a)


def _sample_kernel(x_ref, ng_ref, win_ref, lbl_ref, hg_ref, lng_ref, lnb_ref, ws_ref, bs_ref,
                   wout_ref, fg_ref, stin_ref, y_ref, stout_ref, vch_ref,
                   qin_ref, kot_ref, tot_ref, oint_ref, lvl_ref, wc_ref, bias_ref, wout16_ref):
    @pl.when(pl.program_id(0) == 0)
    def _():
        _store_constants(DEC_SEQ, ws_ref, bs_ref, wout_ref, lvl_ref, wc_ref, bias_ref, wout16_ref)

    x = x_ref[...]
    h = _rms_norm(x, ng_ref[...]).astype(BF16)
    proj = jnp.dot(h, win_ref[...], preferred_element_type=F32)

    def col(k, hs=slice(0, D_A)):
        return proj[:, k * D_A + hs.start:k * D_A + hs.stop]

    lb = _lower_bound(lbl_ref[...])
    bits, n_levels = _row_bits(), _n_levels(DEC_SEQ)
    vi16 = col(2).astype(BF16)

    def head(hd):
        hs = slice(hd * HEAD, (hd + 1) * HEAD)

        def gates(i):
            rows = slice(i * SUBLANES, (i + 1) * SUBLANES)
            return _hgrn_gates(proj[rows, hs], proj[rows, D_A + hd * HEAD:D_A + (hd + 1) * HEAD],
                               lb[:, hs])

        sc, q_in, k_out, total = yield from _hgrn_head(gates, lvl_ref, bits, n_levels)
        qin_ref[:, hs] = q_in
        kot_ref[hd] = k_out.T
        tot_ref[hd] = total.T
        return jnp.dot(sc, vi16[:, hs], preferred_element_type=F32)

    o_intra = _lockstep([head(hd) for hd in range(N_HEADS)])

    lane = lax.broadcasted_iota(jnp.int32, (HEAD, ROWS), 1)

    def seq_head(i, hd):
        hs = slice(hd * HEAD, (hd + 1) * HEAD)
        seq_rows = slice(i * DEC_SEQ, (i + 1) * DEC_SEQ)
        state = stin_ref[0, i, hd]
        qi = qin_ref[seq_rows, hs].astype(BF16)
        oint_ref[seq_rows, hs] = jnp.dot(qi, state.astype(BF16), preferred_element_type=F32)
        yield
        in_seq = (lane >> (DEC_SEQ.bit_length() - 1)) == i
        k_seq = jnp.where(in_seq, kot_ref[hd], 0.0).astype(BF16)
        decay = tot_ref[hd, :, i * DEC_SEQ:i * DEC_SEQ + 1]
        yield
        stout_ref[0, i, hd] = decay * state + jnp.dot(
            k_seq, vi16[:, hs], preferred_element_type=F32)

    for grp in range(SEQS_PER_TILE // SEQ_GROUP):
        _lockstep([seq_head(grp * SEQ_GROUP + s, hd)
                   for s in range(SEQ_GROUP) for hd in range(N_HEADS)])

    hg = hg_ref[...]
    bias = bias_ref[...]
    v_n = _layer_norm(_gelu(col(5)), lng_ref[...], lnb_ref[...])
    for hd in range(N_HEADS):
        vch_ref[0, :, :, hd, :] = v_n[:, hd * HEAD:(hd + 1) * HEAD].reshape(
            SEQS_PER_TILE, DEC_SEQ, HEAD)
    v_n16 = v_n.astype(BF16)
    mixed = []
    for hd in range(N_HEADS):
        hs = slice(hd * HEAD, (hd + 1) * HEAD)
        mixed.append(_hgrn_finish(o_intra[hd] + oint_ref[:, hs], col(3, hs), hg[:, hs]))
    for hd in range(N_HEADS):
        hs = slice(hd * HEAD, (hd + 1) * HEAD)
        mixed.append(_mlp_head(hd, col(4, hs), v_n16[:, hs], col(6, hs), wc_ref, bias))
    out = x + jnp.dot(jnp.concatenate(mixed, axis=-1), wout16_ref[...],
                      preferred_element_type=F32)
    y_ref[...] = _rms_norm(out, fg_ref[...])


def _const_spec(shape):
    return pl.BlockSpec(shape, lambda *_: (0,) * len(shape), pipeline_mode=pl.Buffered(1))


_WEIGHT_SPECS = [
    _const_spec((1, D_MODEL)),
    _const_spec((D_MODEL, D_IN)),
    _const_spec((2, D_A)),
    _const_spec((1, D_A)),
    _const_spec((1, D_B)),
    _const_spec((1, D_B)),
    _const_spec((N_HEADS, ROWS, ROWS)),
    _const_spec((N_HEADS, ROWS)),
    _const_spec((D_MODEL, D_MODEL)),
    _const_spec((1, D_MODEL)),
]

_CONSTANT_SCRATCH = [
    pltpu.VMEM((ROWS, ROWS), jnp.int32),
    pltpu.VMEM((N_HEADS, ROWS, ROWS), BF16),
    pltpu.VMEM((ROWS, N_HEADS), F32),
    pltpu.VMEM((D_MODEL, D_MODEL), BF16),
]


def kernel(x_prompt, x_sample, state_hgrn, norm_g, w_in, lb_logits, hgrn_norm_g, sgu_ln_g,
           sgu_ln_b, w_s, b_s, w_out, final_norm_g):
    depth = norm_g.shape[0]
    assert depth == 1 and lb_logits.shape == (2, D_A)
    batch, seq, _ = x_prompt.shape
    dec_batch, dec_seq, _ = x_sample.shape
    assert seq % PROMPT_TILE == 0 and dec_seq == DEC_SEQ and dec_batch % SEQS_PER_TILE == 0

    weights = (norm_g, w_in[0].astype(BF16), lb_logits, hgrn_norm_g, sgu_ln_g, sgu_ln_b,
               w_s[0], b_s[0], w_out[0], final_norm_g.reshape(1, D_MODEL))
    params = pltpu.CompilerParams(dimension_semantics=("arbitrary",),
                                  vmem_limit_bytes=VMEM_LIMIT_BYTES)

    tiles_per_seq = seq // PROMPT_TILE
    n_tiles = batch * tiles_per_seq

    def cur_tile(g):
        t = jnp.minimum(g, n_tiles - 1)
        return t // tiles_per_seq, t % tiles_per_seq

    def prev_tile(g):
        t = jnp.maximum(g - 1, 0)
        return t // tiles_per_seq, t % tiles_per_seq

    y_p, st_p, vch_p = pl.pallas_call(
        functools.partial(_prompt_kernel, tiles_per_seq=tiles_per_seq),
        grid=(n_tiles + 1,),
        in_specs=[pl.BlockSpec((1, PROMPT_TILE, D_MODEL), lambda g: (*cur_tile(g), 0)),
                  pl.BlockSpec((1, PROMPT_TILE, D_MODEL), lambda g: (*prev_tile(g), 0))]
        + _WEIGHT_SPECS,
        out_specs=[
            pl.BlockSpec((1, PROMPT_TILE, D_MODEL), lambda g: (*prev_tile(g), 0)),
            pl.BlockSpec((1, 1, N_HEADS, HEAD, HEAD), lambda g: (0, prev_tile(g)[0], 0, 0, 0)),
            pl.BlockSpec((1, 1, ROWS, N_HEADS, HEAD), lambda g: (0, prev_tile(g)[0], 0, 0, 0)),
        ],
        out_shape=[
            jax.ShapeDtypeStruct((batch, seq, D_MODEL), F32),
            jax.ShapeDtypeStruct((1, batch, N_HEADS, HEAD, HEAD), F32),
            jax.ShapeDtypeStruct((1, batch, ROWS, N_HEADS, HEAD), F32),
        ],
        scratch_shapes=[pltpu.VMEM((PROMPT_TILE, D_IN), F32),
                        pltpu.VMEM((PROMPT_TILE, D_IN), F32),
                        pltpu.VMEM((PROMPT_TILE, D_MODEL), BF16),
                        pltpu.VMEM((PROMPT_TILE, D_MODEL), BF16)] + _CONSTANT_SCRATCH,
        compiler_params=params,
        name="prompt_layer",
    )(x_prompt, x_prompt, *weights)

    n_rows = dec_batch * DEC_SEQ
    y_s, st_s, vch_s = pl.pallas_call(
        _sample_kernel,
        grid=(n_rows // ROWS,),
        in_specs=[pl.BlockSpec((ROWS, D_MODEL), lambda i: (i, 0))]
        + _WEIGHT_SPECS
        + [pl.BlockSpec((1, SEQS_PER_TILE, N_HEADS, HEAD, HEAD), lambda i: (0, i, 0, 0, 0))],
        out_specs=[
            pl.BlockSpec((ROWS, D_MODEL), lambda i: (i, 0)),
            pl.BlockSpec((1, SEQS_PER_TILE, N_HEADS, HEAD, HEAD), lambda i: (0, i, 0, 0, 0)),
            pl.BlockSpec((1, SEQS_PER_TILE, DEC_SEQ, N_HEADS, HEAD), lambda i: (0, i, 0, 0, 0)),
        ],
        out_shape=[
            jax.ShapeDtypeStruct((n_rows, D_MODEL), F32),
            jax.ShapeDtypeStruct((1, dec_batch, N_HEADS, HEAD, HEAD), F32),
            jax.ShapeDtypeStruct((1, dec_batch, DEC_SEQ, N_HEADS, HEAD), F32),
        ],
        scratch_shapes=[pltpu.VMEM((ROWS, D_A), F32),
                        pltpu.VMEM((N_HEADS, HEAD, ROWS), F32),
                        pltpu.VMEM((N_HEADS, HEAD, ROWS), F32),
                        pltpu.VMEM((ROWS, D_A), F32)] + _CONSTANT_SCRATCH,
        compiler_params=pltpu.CompilerParams(dimension_semantics=("arbitrary",),
                                             vmem_limit_bytes=VMEM_LIMIT_BYTES),
        name="decode_layer",
    )(x_sample.reshape(n_rows, D_MODEL), *weights, state_hgrn)

    return (y_p,
            y_s.reshape(dec_batch, DEC_SEQ, D_MODEL),
            st_p,
            st_s,
            vch_p,
            vch_s)
```

```python
import functools

import jax
import jax.numpy as jnp
from jax import lax
from jax.experimental import pallas as pl
from jax.experimental.pallas import tpu as pltpu

F32 = jnp.float32
BF16 = jnp.bfloat16

D_MODEL = 1024
D_A = 512
D_B = 512
HEAD = 128
N_HEADS = 4
D_IN = 4 * D_A + 3 * D_B
ROWS = 128
SUBLANES = 8
PACKED = 2 * SUBLANES
N_TILES = ROWS // SUBLANES
SUBLANE_LEVELS = 3
EPS = 1e-6
PROMPT_TILE = 512
PROJ_BLOCK = 512
HEAD_ROUNDS = 12
CHUNK_STAGGER = 4
HEAD_STAGGER = 1
PROJ_ROUNDS = [1, 5, 9, 13, 17, 21, 25]
DEC_SEQ = 8
SEQS_PER_TILE = ROWS // DEC_SEQ
SEQ_GROUP = 4
V7X_VMEM_BYTES = 64 * 1024 * 1024
VMEM_LIMIT_BYTES = V7X_VMEM_BYTES * 7 // 8

_NT = (((1,), (1,)), ((), ()))


def _silu(x):
    return x * jax.nn.sigmoid(x)


def _gelu(x):
    c = -2.0 * 0.7978845608028654 * 1.4426950408889634
    return x / (1.0 + jnp.exp2(x * (c * 0.044715 * (x * x) + c)))


def _rms_norm(x, g):
    return x * lax.rsqrt(jnp.mean(x * x, axis=-1, keepdims=True) + EPS) * g


def _layer_norm(x, g, b):
    mu = jnp.mean(x, axis=-1, keepdims=True)
    xc = x - mu
    var = jnp.mean(xc * xc, axis=-1, keepdims=True)
    return xc * lax.rsqrt(var + EPS) * g + b


def _lower_bound(lbl):
    rows = [lbl[r:r + 1, :] for r in range(lbl.shape[0])]
    mx = functools.reduce(jnp.maximum, rows)
    es = [jnp.exp(r - mx) for r in rows]
    return es[0] / functools.reduce(lambda a, c: a + c, es)


def _n_levels(seg_len):
    n_levels = seg_len.bit_length() - 1
    assert seg_len == 1 << n_levels and n_levels >= SUBLANE_LEVELS
    return n_levels


def _split3(x):
    hi = x.astype(BF16)
    r1 = x - hi.astype(F32)
    mid = r1.astype(BF16)
    lo = (r1 - mid.astype(F32)).astype(BF16)
    return hi, mid, lo


def _store_constants(seg_len, ws_ref, bs_ref, wout_ref, lvl_ref, wc_ref, bias_ref, wout16_ref):
    t = lax.broadcasted_iota(jnp.int32, (ROWS, ROWS), 0)
    s = lax.broadcasted_iota(jnp.int32, (ROWS, ROWS), 1)
    n_levels = _n_levels(seg_len)
    x = t ^ s
    lvl = jnp.where(t == s, n_levels, -1)
    for j in range(n_levels):
        lvl = jnp.where(((x >> j) == 1) & (((t >> j) & 1) == 1), j, lvl)
    lvl_ref[...] = lvl
    causal = (t >= s) & ((t >> n_levels) == (s >> n_levels))
    pick = jnp.where(s == (t & (seg_len - 1)), 1.0, 0.0).astype(BF16)
    for hd in range(N_HEADS):
        w = ws_ref[hd].astype(BF16)
        if seg_len != ROWS:
            w = jnp.dot(pick, w, preferred_element_type=F32).astype(BF16)
            w = lax.dot_general(w, pick, _NT, preferred_element_type=F32).astype(BF16)
        wc_ref[hd] = jnp.where(causal, w, jnp.zeros_like(w))
    pad = jnp.zeros((PACKED - N_HEADS, ROWS), F32)
    bias = sum(lax.dot_general(pick, part, _NT, preferred_element_type=F32)
               for part in _split3(jnp.concatenate([bs_ref[...], pad], axis=0)))
    bias_ref[...] = bias[:, :N_HEADS]
    wout16_ref[...] = wout_ref[...].astype(BF16)


def _row_bits():
    rowi = lax.broadcasted_iota(jnp.int32, (SUBLANES, HEAD), 0)
    return [((rowi >> j) & 1) == 1 for j in range(SUBLANE_LEVELS)]


def _hgrn_head(gates, lvl_ref, bits, n_levels):
    one = jnp.ones((SUBLANES, HEAD), F32)
    z = [[] for _ in range(n_levels)]
    qp_t, ks_t, blk_t, diag_t = [], [], [], []
    for pair in range(N_TILES // 2):
        z_pair = [[] for _ in range(SUBLANE_LEVELS)]
        for i in (2 * pair, 2 * pair + 1):
            q, kk, f = gates(i)
            diag_t.append(jnp.sum(q * kk, axis=-1, keepdims=True))
            qp, ks, blk = q * f, kk, f
            for j in range(SUBLANE_LEVELS):
                m = 1 << j
                z_pair[j].append(jnp.where(bits[j], qp, ks))
                sib = pltpu.roll(blk, m, 0)
                if 2 * m != SUBLANES:
                    sib = jnp.where(bits[j], sib, pltpu.roll(blk, SUBLANES - m, 0))
                qp = qp * jnp.where(bits[j], sib, one)
                ks = ks * jnp.where(bits[j], one, sib)
                blk = blk * sib
            qp_t.append(qp)
            ks_t.append(ks)
            blk_t.append(blk)
        for j in range(SUBLANE_LEVELS):
            z[j].append(jnp.concatenate(z_pair[j], axis=0).astype(BF16))
    yield

    blk_levels, blk_b = [], blk_t
    for j in range(SUBLANE_LEVELS, n_levels):
        blk_levels.append(blk_b)
        blk_b = [blk_b[2 * n] * blk_b[2 * n + 1] for n in range(len(blk_b) // 2)]
    for pair in range(N_TILES // 2):
        z_pair = [[] for _ in range(SUBLANE_LEVELS, n_levels)]
        for i in (2 * pair, 2 * pair + 1):
            for j in range(SUBLANE_LEVELS, n_levels):
                block = i >> (j - SUBLANE_LEVELS)
                sib = blk_levels[j - SUBLANE_LEVELS][block ^ 1]
                if block & 1:
                    z_pair[j - SUBLANE_LEVELS].append(qp_t[i])
                    qp_t[i] = qp_t[i] * sib
                else:
                    z_pair[j - SUBLANE_LEVELS].append(ks_t[i])
                    ks_t[i] = ks_t[i] * sib
        for j in range(SUBLANE_LEVELS, n_levels):
            z[j].append(jnp.concatenate(z_pair[j - SUBLANE_LEVELS], axis=0).astype(BF16))
    q_in = jnp.concatenate(qp_t, axis=0)
    k_out = jnp.concatenate(ks_t, axis=0)
    per_tile = N_TILES // len(blk_b)
    total = jnp.concatenate([blk_b[i // per_tile] for i in range(N_TILES)], axis=0)
    yield

    p, p_row = [], []
    for j in range(n_levels):
        zj = jnp.concatenate(z[j], axis=0)
        span = 1 << max(j - SUBLANE_LEVELS, 0)
        if span * SUBLANES >= PACKED:
            upper = [i for i in range(N_TILES) if (i // span) & 1]
            lhs = jnp.concatenate([z[j][i // 2] for i in upper[::2]], axis=0)
        else:
            upper = list(range(N_TILES))
            lhs = zj
        p.append(lax.dot_general(lhs, zj, _NT, preferred_element_type=F32))
        p_row.append({i: n * SUBLANES for n, i in enumerate(upper)})
        yield
    sc_t = []
    for i in range(N_TILES):
        lvl_i = lvl_ref[i * SUBLANES:(i + 1) * SUBLANES, :]
        acc = jnp.where(lvl_i == n_levels, diag_t[i], 0.0)
        for j in range(n_levels):
            if j < SUBLANE_LEVELS or (i >> (j - SUBLANE_LEVELS)) & 1:
                r0 = p_row[j][i]
                acc = jnp.where(lvl_i == j, p[j][r0:r0 + SUBLANES, :], acc)
        sc_t.append(acc)
    sc = jnp.concatenate(sc_t, axis=0).astype(BF16)
    return sc, q_in, k_out, total


def _lockstep(generators, starts=None):
    starts = starts or [0] * len(generators)
    results = [None] * len(generators)
    waiting = sorted(range(len(generators)), key=lambda i: starts[i])
    active = []
    rnd = 0
    while waiting or active:
        while waiting and starts[waiting[0]] <= rnd:
            active.append(waiting.pop(0))
        still = []
        for idx in active:
            try:
                next(generators[idx])
                still.append(idx)
            except StopIteration as done:
                results[idx] = done.value
        active = still
        rnd += 1
    return results


def _hgrn_gates(pq, pf, lb):
    q = _silu(pq)
    forget = lb + (1.0 - lb) * jax.nn.sigmoid(pf)
    return q, 1.0 - forget, forget


def _hgrn_finish(o, pz, g):
    return (_rms_norm(o, g) * _silu(pz)).astype(BF16)


def _mlp_head(hd, u, v_n16, gate, wc_ref, bias):
    mix = jnp.dot(wc_ref[hd], v_n16, preferred_element_type=F32) + bias[:, hd:hd + 1]
    return (_gelu(u) * mix * _silu(gate)).astype(BF16)


def _prompt_kernel(x_ref, xprev_ref, ng_ref, win_ref, lbl_ref, hg_ref, lng_ref, lnb_ref, ws_ref,
                   bs_ref, wout_ref, fg_ref, y_ref, st_ref, vch_ref, proj_a, proj_b, h_ref,
                   mixed_ref, lvl_ref, wc_ref, bias_ref, wout16_ref, *, tiles_per_seq):
    g = pl.program_id(0)

    @pl.when(g == 0)
    def _():
        proj_b[...] = jnp.zeros_like(proj_b)
        _store_constants(ROWS, ws_ref, bs_ref, wout_ref, lvl_ref, wc_ref, bias_ref, wout16_ref)

    @pl.when(lax.rem(jnp.maximum(g - 1, 0), tiles_per_seq) == 0)
    def _():
        st_ref[...] = jnp.zeros_like(st_ref)

    def step(proj_next, proj_prev):
        h_ref[...] = _rms_norm(x_ref[0], ng_ref[...]).astype(BF16)
        lb = _lower_bound(lbl_ref[...])
        bits, n_levels = _row_bits(), _n_levels(ROWS)
        hg = hg_ref[...]
        lng = lng_ref[...]
        lnb = lnb_ref[...]
        bias = bias_ref[...]
        n_chunks = PROMPT_TILE // ROWS
        n_blocks = D_IN // PROJ_BLOCK

        def col(c, k, hs=slice(0, D_A)):
            return proj_prev[c * ROWS:(c + 1) * ROWS, k * D_A + hs.start:k * D_A + hs.stop]

        def head(c, hd):
            hs = slice(hd * HEAD, (hd + 1) * HEAD)

            def gates(i):
                rows = slice(c * ROWS + i * SUBLANES, c * ROWS + (i + 1) * SUBLANES)
                return _hgrn_gates(proj_prev[rows, hs],
                                   proj_prev[rows, D_A + hd * HEAD:D_A + (hd + 1) * HEAD],
                                   lb[:, hs])

            sc, q_in, k_out, total = yield from _hgrn_head(gates, lvl_ref, bits, n_levels)
            state = st_ref[0, 0, hd]
            vh = col(c, 2, hs)
            o = jnp.dot(jnp.concatenate([sc, q_in.astype(BF16)], axis=1),
                        jnp.concatenate([vh.astype(BF16), state.astype(BF16)], axis=0),
                        preferred_element_type=F32)
            yield
            st_ref[0, 0, hd] = total.T * state + jnp.dot(
                k_out.T.astype(BF16), vh.astype(BF16), preferred_element_type=F32)
            yield
            mixed_ref[c * ROWS:(c + 1) * ROWS, hs] = _hgrn_finish(o, col(c, 3, hs), hg[:, hs])

        def mlp(c):
            v_n = _layer_norm(_gelu(col(c, 5)), lng, lnb)
            if c == n_chunks - 1:
                for hd in range(N_HEADS):
                    vch_ref[0, 0, :, hd, :] = v_n[:, hd * HEAD:(hd + 1) * HEAD]
            v_n16 = v_n.astype(BF16)
            yield
            for hd in range(N_HEADS):
                hs = slice(hd * HEAD, (hd + 1) * HEAD)
                mixed_ref[c * ROWS:(c + 1) * ROWS, D_A + hd * HEAD:D_A + (hd + 1) * HEAD] = (
                    _mlp_head(hd, col(c, 4, hs), v_n16[:, hs], col(c, 6, hs), wc_ref, bias))
                yield
                yield

        def projection(rounds_per_block):
            at_rounds = PROJ_ROUNDS if len(PROJ_ROUNDS) == n_blocks else [
                blk * rounds_per_block for blk in range(n_blocks)]
            rnd = 0
            for blk, at in enumerate(at_rounds):
                while rnd < at:
                    yield
                    rnd += 1
                cols = slice(blk * PROJ_BLOCK, (blk + 1) * PROJ_BLOCK)
                proj_next[:, cols] = jnp.dot(h_ref[...], win_ref[:, cols],
                                             preferred_element_type=F32)

        items = [gen for c in range(n_chunks)
                 for gen in [head(c, hd) for hd in range(N_HEADS)] + [mlp(c)]]
        starts = [c * CHUNK_STAGGER + k * HEAD_STAGGER
                  for c in range(n_chunks) for k in list(range(N_HEADS)) + [0]]
        rounds = (n_chunks - 1) * CHUNK_STAGGER + (N_HEADS - 1) * HEAD_STAGGER + HEAD_ROUNDS
        _lockstep(items + [projection(max(1, rounds // n_blocks))], starts + [0])

        out = xprev_ref[0] + jnp.dot(mixed_ref[...], wout16_ref[...], preferred_element_type=F32)
        y_ref[0] = _rms_norm(out, fg_ref[...])

    parity = lax.rem(g, 2)

    @pl.when(parity == 0)
    def _():
        step(proj_a, proj_b)

    @pl.when(parity == 1)
    def _():
        step(proj_b, proj_a)


def _sample_kernel(x_ref, ng_ref, win_ref, lbl_ref, hg_ref, lng_ref, lnb_ref, ws_ref, bs_ref,
                   wout_ref, fg_ref, stin_ref, y_ref, stout_ref, vch_ref,
                   qin_ref, kot_ref, tot_ref, oint_ref, lvl_ref, wc_ref, bias_ref, wout16_ref):
    @pl.when(pl.program_id(0) == 0)
    def _():
        _store_constants(DEC_SEQ, ws_ref, bs_ref, wout_ref, lvl_ref, wc_ref, bias_ref, wout16_ref)

    x = x_ref[...]
    h = _rms_norm(x, ng_ref[...]).astype(BF16)
    proj = jnp.dot(h, win_ref[...], preferred_element_type=F32)

    def col(k, hs=slice(0, D_A)):
        return proj[:, k * D_A + hs.start:k * D_A + hs.stop]

    lb = _lower_bound(lbl_ref[...])
    bits, n_levels = _row_bits(), _n_levels(DEC_SEQ)
    vi16 = col(2).astype(BF16)

    def head(hd):
        hs = slice(hd * HEAD, (hd + 1) * HEAD)

        def gates(i):
            rows = slice(i * SUBLANES, (i + 1) * SUBLANES)
            return _hgrn_gates(proj[rows, hs], proj[rows, D_A + hd * HEAD:D_A + (hd + 1) * HEAD],
                               lb[:, hs])

        sc, q_in, k_out, total = yield from _hgrn_head(gates, lvl_ref, bits, n_levels)
        qin_ref[:, hs] = q_in
        kot_ref[hd] = k_out.T
        tot_ref[hd] = total.T
        return jnp.dot(sc, vi16[:, hs], preferred_element_type=F32)

    o_intra = _lockstep([head(hd) for hd in range(N_HEADS)])

    lane = lax.broadcasted_iota(jnp.int32, (HEAD, ROWS), 1)

    def seq_head(i, hd):
        hs = slice(hd * HEAD, (hd + 1) * HEAD)
        seq_rows = slice(i * DEC_SEQ, (i + 1) * DEC_SEQ)
        state = stin_ref[0, i, hd]
        qi = qin_ref[seq_rows, hs].astype(BF16)
        oint_ref[seq_rows, hs] = jnp.dot(qi, state.astype(BF16), preferred_element_type=F32)
        yield
        in_seq = (lane >> (DEC_SEQ.bit_length() - 1)) == i
        k_seq = jnp.where(in_seq, kot_ref[hd], 0.0).astype(BF16)
        decay = tot_ref[hd, :, i * DEC_SEQ:i * DEC_SEQ + 1]
        yield
        stout_ref[0, i, hd] = decay * state + jnp.dot(
            k_seq, vi16[:, hs], preferred_element_type=F32)

    for grp in range(SEQS_PER_TILE // SEQ_GROUP):
        _lockstep([seq_head(grp * SEQ_GROUP + s, hd)
                   for s in range(SEQ_GROUP) for hd in range(N_HEADS)])

    hg = hg_ref[...]
    bias = bias_ref[...]
    v_n = _layer_norm(_gelu(col(5)), lng_ref[...], lnb_ref[...])
    for hd in range(N_HEADS):
        vch_ref[0, :, :, hd, :] = v_n[:, hd * HEAD:(hd + 1) * HEAD].reshape(
            SEQS_PER_TILE, DEC_SEQ, HEAD)
    v_n16 = v_n.astype(BF16)
    mixed = []
    for hd in range(N_HEADS):
        hs = slice(hd * HEAD, (hd + 1) * HEAD)
        mixed.append(_hgrn_finish(o_intra[hd] + oint_ref[:, hs], col(3, hs), hg[:, hs]))
    for hd in range(N_HEADS):
        hs = slice(hd * HEAD, (hd + 1) * HEAD)
        mixed.append(_mlp_head(hd, col(4, hs), v_n16[:, hs], col(6, hs), wc_ref, bias))
    out = x + jnp.dot(jnp.concatenate(mixed, axis=-1), wout16_ref[...],
                      preferred_element_type=F32)
    y_ref[...] = _rms_norm(out, fg_ref[...])


def _const_spec(shape):
    return pl.BlockSpec(shape, lambda *_: (0,) * len(shape), pipeline_mode=pl.Buffered(1))


_WEIGHT_SPECS = [
    _const_spec((1, D_MODEL)),
    _const_spec((D_MODEL, D_IN)),
    _const_spec((2, D_A)),
    _const_spec((1, D_A)),
    _const_spec((1, D_B)),
    _const_spec((1, D_B)),
    _const_spec((N_HEADS, ROWS, ROWS)),
    _const_spec((N_HEADS, ROWS)),
    _const_spec((D_MODEL, D_MODEL)),
    _const_spec((1, D_MODEL)),
]

_CONSTANT_SCRATCH = [
    pltpu.VMEM((ROWS, ROWS), jnp.int32),
    pltpu.VMEM((N_HEADS, ROWS, ROWS), BF16),
    pltpu.VMEM((ROWS, N_HEADS), F32),
    pltpu.VMEM((D_MODEL, D_MODEL), BF16),
]


def kernel(x_prompt, x_sample, state_hgrn, norm_g, w_in, lb_logits, hgrn_norm_g, sgu_ln_g,
           sgu_ln_b, w_s, b_s, w_out, final_norm_g):
    depth = norm_g.shape[0]
    assert depth == 1 and lb_logits.shape == (2, D_A)
    batch, seq, _ = x_prompt.shape
    dec_batch, dec_seq, _ = x_sample.shape
    assert seq % PROMPT_TILE == 0 and dec_seq == DEC_SEQ and dec_batch % SEQS_PER_TILE == 0

    weights = (norm_g, w_in[0].astype(BF16), lb_logits, hgrn_norm_g, sgu_ln_g, sgu_ln_b,
               w_s[0], b_s[0], w_out[0], final_norm_g.reshape(1, D_MODEL))
    params = pltpu.CompilerParams(dimension_semantics=("arbitrary",),
                                  vmem_limit_bytes=VMEM_LIMIT_BYTES)

    tiles_per_seq = seq // PROMPT_TILE
    n_tiles = batch * tiles_per_seq

    def cur_tile(g):
        t = jnp.minimum(g, n_tiles - 1)
        return t // tiles_per_seq, t % tiles_per_seq

    def prev_tile(g):
        t = jnp.maximum(g - 1, 0)
        return t // tiles_per_seq, t % tiles_per_seq

    y_p, st_p, vch_p = pl.pallas_call(
        functools.partial(_prompt_kernel, tiles_per_seq=tiles_per_seq),
        grid=(n_tiles + 1,),
        in_specs=[pl.BlockSpec((1, PROMPT_TILE, D_MODEL), lambda g: (*cur_tile(g), 0)),
                  pl.BlockSpec((1, PROMPT_TILE, D_MODEL), lambda g: (*prev_tile(g), 0))]
        + _WEIGHT_SPECS,
        out_specs=[
            pl.BlockSpec((1, PROMPT_TILE, D_MODEL), lambda g: (*prev_tile(g), 0)),
            pl.BlockSpec((1, 1, N_HEADS, HEAD, HEAD), lambda g: (0, prev_tile(g)[0], 0, 0, 0)),
            pl.BlockSpec((1, 1, ROWS, N_HEADS, HEAD), lambda g: (0, prev_tile(g)[0], 0, 0, 0)),
        ],
        out_shape=[
            jax.ShapeDtypeStruct((batch, seq, D_MODEL), F32),
            jax.ShapeDtypeStruct((1, batch, N_HEADS, HEAD, HEAD), F32),
            jax.ShapeDtypeStruct((1, batch, ROWS, N_HEADS, HEAD), F32),
        ],
        scratch_shapes=[pltpu.VMEM((PROMPT_TILE, D_IN), F32),
                        pltpu.VMEM((PROMPT_TILE, D_IN), F32),
                        pltpu.VMEM((PROMPT_TILE, D_MODEL), BF16),
                        pltpu.VMEM((PROMPT_TILE, D_MODEL), BF16)] + _CONSTANT_SCRATCH,
        compiler_params=params,
        name="prompt_layer",
    )(x_prompt, x_prompt, *weights)

    n_rows = dec_batch * DEC_SEQ
    y_s, st_s, vch_s = pl.pallas_call(
        _sample_kernel,
        grid=(n_rows // ROWS,),
        in_specs=[pl.BlockSpec((ROWS, D_MODEL), lambda i: (i, 0))]
        + _WEIGHT_SPECS
        + [pl.BlockSpec((1, SEQS_PER_TILE, N_HEADS, HEAD, HEAD), lambda i: (0, i, 0, 0, 0))],
        out_specs=[
            pl.BlockSpec((ROWS, D_MODEL), lambda i: (i, 0)),
            pl.BlockSpec((1, SEQS_PER_TILE, N_HEADS, HEAD, HEAD), lambda i: (0, i, 0, 0, 0)),
            pl.BlockSpec((1, SEQS_PER_TILE, DEC_SEQ, N_HEADS, HEAD), lambda i: (0, i, 0, 0, 0)),
        ],
        out_shape=[
            jax.ShapeDtypeStruct((n_rows, D_MODEL), F32),
            jax.ShapeDtypeStruct((1, dec_batch, N_HEADS, HEAD, HEAD), F32),
            jax.ShapeDtypeStruct((1, dec_batch, DEC_SEQ, N_HEADS, HEAD), F32),
        ],
        scratch_shapes=[pltpu.VMEM((ROWS, D_A), F32),
                        pltpu.VMEM((N_HEADS, HEAD, ROWS), F32),
                        pltpu.VMEM((N_HEADS, HEAD, ROWS), F32),
                        pltpu.VMEM((ROWS, D_A), F32)] + _CONSTANT_SCRATCH,
        compiler_params=pltpu.CompilerParams(dimension_semantics=("arbitrary",),
                                             vmem_limit_bytes=VMEM_LIMIT_BYTES),
        name="decode_layer",
    )(x_sample.reshape(n_rows, D_MODEL), *weights, state_hgrn)

    return (y_p,
            y_s.reshape(dec_batch, DEC_SEQ, D_MODEL),
            st_p,
            st_s,
            vch_p,
            vch_s)
```

```python
import functools

import jax
import jax.numpy as jnp
from jax import lax
from jax.experimental import pallas as pl
from jax.experimental.pallas import tpu as pltpu

F32 = jnp.float32
BF16 = jnp.bfloat16

D_MODEL = 1024
D_A = 512
D_B = 512
HEAD = 128
N_HEADS = 4
D_IN = 4 * D_A + 3 * D_B
ROWS = 128
SUBLANES = 8
PACKED = 2 * SUBLANES
N_TILES = ROWS // SUBLANES
SUBLANE_LEVELS = 3
EPS = 1e-6
PROMPT_TILE = 512
PROJ_BLOCK = 512
HEAD_ROUNDS = 12
CHUNK_STAGGER = 4
HEAD_STAGGER = 1
PROJ_ROUNDS = [0, 4, 9, 13, 17, 22, 26]
DEC_SEQ = 8
SEQS_PER_TILE = ROWS // DEC_SEQ
SEQ_GROUP = 4
V7X_VMEM_BYTES = 64 * 1024 * 1024
VMEM_LIMIT_BYTES = V7X_VMEM_BYTES * 7 // 8

_NT = (((1,), (1,)), ((), ()))


def _silu(x):
    return x * jax.nn.sigmoid(x)


def _gelu(x):
    c = -2.0 * 0.7978845608028654 * 1.4426950408889634
    return x / (1.0 + jnp.exp2(x * (c * 0.044715 * (x * x) + c)))


def _rms_norm(x, g):
    return x * lax.rsqrt(jnp.mean(x * x, axis=-1, keepdims=True) + EPS) * g


def _layer_norm(x, g, b):
    mu = jnp.mean(x, axis=-1, keepdims=True)
    xc = x - mu
    var = jnp.mean(xc * xc, axis=-1, keepdims=True)
    return xc * lax.rsqrt(var + EPS) * g + b


def _lower_bound(lbl):
    rows = [lbl[r:r + 1, :] for r in range(lbl.shape[0])]
    mx = functools.reduce(jnp.maximum, rows)
    es = [jnp.exp(r - mx) for r in rows]
    return es[0] / functools.reduce(lambda a, c: a + c, es)


def _n_levels(seg_len):
    n_levels = seg_len.bit_length() - 1
    assert seg_len == 1 << n_levels and n_levels >= SUBLANE_LEVELS
    return n_levels


def _split3(x):
    hi = x.astype(BF16)
    r1 = x - hi.astype(F32)
    mid = r1.astype(BF16)
    lo = (r1 - mid.astype(F32)).astype(BF16)
    return hi, mid, lo


def _store_constants(seg_len, ws_ref, bs_ref, wout_ref, lvl_ref, wc_ref, bias_ref, wout16_ref):
    t = lax.broadcasted_iota(jnp.int32, (ROWS, ROWS), 0)
    s = lax.broadcasted_iota(jnp.int32, (ROWS, ROWS), 1)
    n_levels = _n_levels(seg_len)
    x = t ^ s
    lvl = jnp.where(t == s, n_levels, -1)
    for j in range(n_levels):
        lvl = jnp.where(((x >> j) == 1) & (((t >> j) & 1) == 1), j, lvl)
    lvl_ref[...] = lvl
    causal = (t >= s) & ((t >> n_levels) == (s >> n_levels))
    pick = jnp.where(s == (t & (seg_len - 1)), 1.0, 0.0).astype(BF16)
    for hd in range(N_HEADS):
        w = ws_ref[hd].astype(BF16)
        if seg_len != ROWS:
            w = jnp.dot(pick, w, preferred_element_type=F32).astype(BF16)
            w = lax.dot_general(w, pick, _NT, preferred_element_type=F32).astype(BF16)
        wc_ref[hd] = jnp.where(causal, w, jnp.zeros_like(w))
    pad = jnp.zeros((PACKED - N_HEADS, ROWS), F32)
    bias = sum(lax.dot_general(pick, part, _NT, preferred_element_type=F32)
               for part in _split3(jnp.concatenate([bs_ref[...], pad], axis=0)))
    bias_ref[...] = bias[:, :N_HEADS]
    wout16_ref[...] = wout_ref[...].astype(BF16)


def _row_bits():
    rowi = lax.broadcasted_iota(jnp.int32, (SUBLANES, HEAD), 0)
    return [((rowi >> j) & 1) == 1 for j in range(SUBLANE_LEVELS)]


def _hgrn_head(gates, lvl_ref, bits, n_levels):
    one = jnp.ones((SUBLANES, HEAD), F32)
    z = [[] for _ in range(n_levels)]
    qp_t, ks_t, blk_t, diag_t = [], [], [], []
    for pair in range(N_TILES // 2):
        z_pair = [[] for _ in range(SUBLANE_LEVELS)]
        for i in (2 * pair, 2 * pair + 1):
            q, kk, f = gates(i)
            diag_t.append(jnp.sum(q * kk, axis=-1, keepdims=True))
            qp, ks, blk = q * f, kk, f
            for j in range(SUBLANE_LEVELS):
                m = 1 << j
                z_pair[j].append(jnp.where(bits[j], qp, ks))
                sib = pltpu.roll(blk, m, 0)
                if 2 * m != SUBLANES:
                    sib = jnp.where(bits[j], sib, pltpu.roll(blk, SUBLANES - m, 0))
                qp = qp * jnp.where(bits[j], sib, one)
                ks = ks * jnp.where(bits[j], one, sib)
                blk = blk * sib
            qp_t.append(qp)
            ks_t.append(ks)
            blk_t.append(blk)
        for j in range(SUBLANE_LEVELS):
            z[j].append(jnp.concatenate(z_pair[j], axis=0).astype(BF16))
    yield

    blk_levels, blk_b = [], blk_t
    for j in range(SUBLANE_LEVELS, n_levels):
        blk_levels.append(blk_b)
        blk_b = [blk_b[2 * n] * blk_b[2 * n + 1] for n in range(len(blk_b) // 2)]
    for pair in range(N_TILES // 2):
        z_pair = [[] for _ in range(SUBLANE_LEVELS, n_levels)]
        for i in (2 * pair, 2 * pair + 1):
            for j in range(SUBLANE_LEVELS, n_levels):
                block = i >> (j - SUBLANE_LEVELS)
                sib = blk_levels[j - SUBLANE_LEVELS][block ^ 1]
                if block & 1:
                    z_pair[j - SUBLANE_LEVELS].append(qp_t[i])
                    qp_t[i] = qp_t[i] * sib
                else:
                    z_pair[j - SUBLANE_LEVELS].append(ks_t[i])
                    ks_t[i] = ks_t[i] * sib
        for j in range(SUBLANE_LEVELS, n_levels):
            z[j].append(jnp.concatenate(z_pair[j - SUBLANE_LEVELS], axis=0).astype(BF16))
    q_in = jnp.concatenate(qp_t, axis=0)
    k_out = jnp.concatenate(ks_t, axis=0)
    per_tile = N_TILES // len(blk_b)
    total = jnp.concatenate([blk_b[i // per_tile] for i in range(N_TILES)], axis=0)
    yield

    p, p_row = [], []
    for j in range(n_levels):
        zj = jnp.concatenate(z[j], axis=0)
        span = 1 << max(j - SUBLANE_LEVELS, 0)
        if span * SUBLANES >= PACKED:
            upper = [i for i in range(N_TILES) if (i // span) & 1]
            lhs = jnp.concatenate([z[j][i // 2] for i in upper[::2]], axis=0)
        else:
            upper = list(range(N_TILES))
            lhs = zj
        p.append(lax.dot_general(lhs, zj, _NT, preferred_element_type=F32))
        p_row.append({i: n * SUBLANES for n, i in enumerate(upper)})
        yield
    sc_t = []
    for i in range(N_TILES):
        lvl_i = lvl_ref[i * SUBLANES:(i + 1) * SUBLANES, :]
        acc = jnp.where(lvl_i == n_levels, diag_t[i], 0.0)
        for j in range(n_levels):
            if j < SUBLANE_LEVELS or (i >> (j - SUBLANE_LEVELS)) & 1:
                r0 = p_row[j][i]
                acc = jnp.where(lvl_i == j, p[j][r0:r0 + SUBLANES, :], acc)
        sc_t.append(acc)
    sc = jnp.concatenate(sc_t, axis=0).astype(BF16)
    return sc, q_in, k_out, total


def _lockstep(generators, starts=None):
    starts = starts or [0] * len(generators)
    results = [None] * len(generators)
    waiting = sorted(range(len(generators)), key=lambda i: starts[i])
    active = []
    rnd = 0
    while waiting or active:
        while waiting and starts[waiting[0]] <= rnd:
            active.append(waiting.pop(0))
        still = []
        for idx in active:
            try:
                next(generators[idx])
                still.append(idx)
            except StopIteration as done:
                results[idx] = done.value
        active = still
        rnd += 1
    return results


def _hgrn_gates(pq, pf, lb):
    q = _silu(pq)
    forget = lb + (1.0 - lb) * jax.nn.sigmoid(pf)
    return q, 1.0 - forget, forget


def _hgrn_finish(o, pz, g):
    return (_rms_norm(o, g) * _silu(pz)).astype(BF16)


def _mlp_head(hd, u, v_n16, gate, wc_ref, bias):
    mix = jnp.dot(wc_ref[hd], v_n16, preferred_element_type=F32) + bias[:, hd:hd + 1]
    return (_gelu(u) * mix * _silu(gate)).astype(BF16)


def _prompt_kernel(x_ref, xprev_ref, ng_ref, win_ref, lbl_ref, hg_ref, lng_ref, lnb_ref, ws_ref,
                   bs_ref, wout_ref, fg_ref, y_ref, st_ref, vch_ref, proj_a, proj_b, h_ref,
                   mixed_ref, lvl_ref, wc_ref, bias_ref, wout16_ref, *, tiles_per_seq):
    g = pl.program_id(0)

    @pl.when(g == 0)
    def _():
        proj_b[...] = jnp.zeros_like(proj_b)
        _store_constants(ROWS, ws_ref, bs_ref, wout_ref, lvl_ref, wc_ref, bias_ref, wout16_ref)

    @pl.when(lax.rem(jnp.maximum(g - 1, 0), tiles_per_seq) == 0)
    def _():
        st_ref[...] = jnp.zeros_like(st_ref)

    def step(proj_next, proj_prev):
        h_ref[...] = _rms_norm(x_ref[0], ng_ref[...]).astype(BF16)
        lb = _lower_bound(lbl_ref[...])
        bits, n_levels = _row_bits(), _n_levels(ROWS)
        hg = hg_ref[...]
        lng = lng_ref[...]
        lnb = lnb_ref[...]
        bias = bias_ref[...]
        n_chunks = PROMPT_TILE // ROWS
        n_blocks = D_IN // PROJ_BLOCK

        def col(c, k, hs=slice(0, D_A)):
            return proj_prev[c * ROWS:(c + 1) * ROWS, k * D_A + hs.start:k * D_A + hs.stop]

        def head(c, hd):
            hs = slice(hd * HEAD, (hd + 1) * HEAD)

            def gates(i):
                rows = slice(c * ROWS + i * SUBLANES, c * ROWS + (i + 1) * SUBLANES)
                return _hgrn_gates(proj_prev[rows, hs],
                                   proj_prev[rows, D_A + hd * HEAD:D_A + (hd + 1) * HEAD],
                                   lb[:, hs])

            sc, q_in, k_out, total = yield from _hgrn_head(gates, lvl_ref, bits, n_levels)
            state = st_ref[0, 0, hd]
            vh = col(c, 2, hs)
            o = jnp.dot(jnp.concatenate([sc, q_in.astype(BF16)], axis=1),
                        jnp.concatenate([vh.astype(BF16), state.astype(BF16)], axis=0),
                        preferred_element_type=F32)
            yield
            st_ref[0, 0, hd] = total.T * state + jnp.dot(
                k_out.T.astype(BF16), vh.astype(BF16), preferred_element_type=F32)
            yield
            mixed_ref[c * ROWS:(c + 1) * ROWS, hs] = _hgrn_finish(o, col(c, 3, hs), hg[:, hs])

        def mlp(c):
            v_n = _layer_norm(_gelu(col(c, 5)), lng, lnb)
            if c == n_chunks - 1:
                for hd in range(N_HEADS):
                    vch_ref[0, 0, :, hd, :] = v_n[:, hd * HEAD:(hd + 1) * HEAD]
            v_n16 = v_n.astype(BF16)
            yield
            for hd in range(N_HEADS):
                hs = slice(hd * HEAD, (hd + 1) * HEAD)
                mixed_ref[c * ROWS:(c + 1) * ROWS, D_A + hd * HEAD:D_A + (hd + 1) * HEAD] = (
                    _mlp_head(hd, col(c, 4, hs), v_n16[:, hs], col(c, 6, hs), wc_ref, bias))
                yield
                yield

        def projection(rounds_per_block):
            at_rounds = PROJ_ROUNDS if len(PROJ_ROUNDS) == n_blocks else [
                blk * rounds_per_block for blk in range(n_blocks)]
            rnd = 0
            for blk, at in enumerate(at_rounds):
                while rnd < at:
                    yield
                    rnd += 1
                cols = slice(blk * PROJ_BLOCK, (blk + 1) * PROJ_BLOCK)
                proj_next[:, cols] = jnp.dot(h_ref[...], win_ref[:, cols],
                                             preferred_element_type=F32)

        items = [gen for c in range(n_chunks)
                 for gen in [head(c, hd) for hd in range(N_HEADS)] + [mlp(c)]]
        starts = [c * CHUNK_STAGGER + k * HEAD_STAGGER
                  for c in range(n_chunks) for k in list(range(N_HEADS)) + [0]]
        rounds = (n_chunks - 1) * CHUNK_STAGGER + (N_HEADS - 1) * HEAD_STAGGER + HEAD_ROUNDS
        _lockstep(items + [projection(max(1, rounds // n_blocks))], starts + [0])

        out = xprev_ref[0] + jnp.dot(mixed_ref[...], wout16_ref[...], preferred_element_type=F32)
        y_ref[0] = _rms_norm(out, fg_ref[...])

    parity = lax.rem(g, 2)

    @pl.when(parity == 0)
    def _():
        step(proj_a, proj_b)

    @pl.when(parity == 1)
    def _():
        step(proj_b, proj_a)


def _sample_kernel(x_ref, ng_ref, win_ref, lbl_ref, hg_ref, lng_ref, lnb_ref, ws_ref, bs_ref,
                   wout_ref, fg_ref, stin_ref, y_ref, stout_ref, vch_ref,
                   qin_ref, kot_ref, tot_ref, oint_ref, lvl_ref, wc_ref, bias_ref, wout16_ref):
    @pl.when(pl.program_id(0) == 0)
    def _():
        _store_constants(DEC_SEQ, ws_ref, bs_ref, wout_ref, lvl_ref, wc_ref, bias_ref, wout16_ref)

    x = x_ref[...]
    h = _rms_norm(x, ng_ref[...]).astype(BF16)
    proj = jnp.dot(h, win_ref[...], preferred_element_type=F32)

    def col(k, hs=slice(0, D_A)):
        return proj[:, k * D_A + hs.start:k * D_A + hs.stop]

    lb = _lower_bound(lbl_ref[...])
    bits, n_levels = _row_bits(), _n_levels(DEC_SEQ)
    vi16 = col(2).astype(BF16)

    def head(hd):
        hs = slice(hd * HEAD, (hd + 1) * HEAD)

        def gates(i):
            rows = slice(i * SUBLANES, (i + 1) * SUBLANES)
            return _hgrn_gates(proj[rows, hs], proj[rows, D_A + hd * HEAD:D_A + (hd + 1) * HEAD],
                               lb[:, hs])

        sc, q_in, k_out, total = yield from _hgrn_head(gates, lvl_ref, bits, n_levels)
        qin_ref[:, hs] = q_in
        kot_ref[hd] = k_out.T
        tot_ref[hd] = total.T
        return jnp.dot(sc, vi16[:, hs], preferred_element_type=F32)

    o_intra = _lockstep([head(hd) for hd in range(N_HEADS)])

    lane = lax.broadcasted_iota(jnp.int32, (HEAD, ROWS), 1)

    def seq_head(i, hd):
        hs = slice(hd * HEAD, (hd + 1) * HEAD)
        seq_rows = slice(i * DEC_SEQ, (i + 1) * DEC_SEQ)
        state = stin_ref[0, i, hd]
        qi = qin_ref[seq_rows, hs].astype(BF16)
        oint_ref[seq_rows, hs] = jnp.dot(qi, state.astype(BF16), preferred_element_type=F32)
        yield
        in_seq = (lane >> (DEC_SEQ.bit_length() - 1)) == i
        k_seq = jnp.where(in_seq, kot_ref[hd], 0.0).astype(BF16)
        decay = tot_ref[hd, :, i * DEC_SEQ:i * DEC_SEQ + 1]
        yield
        stout_ref[0, i, hd] = decay * state + jnp.dot(
            k_seq, vi16[:, hs], preferred_element_type=F32)

    for grp in range(SEQS_PER_TILE // SEQ_GROUP):
        _lockstep([seq_head(grp * SEQ_GROUP + s, hd)
                   for s in range(SEQ_GROUP) for hd in range(N_HEADS)])

    hg = hg_ref[...]
    bias = bias_ref[...]
    v_n = _layer_norm(_gelu(col(5)), lng_ref[...], lnb_ref[...])
    for hd in range(N_HEADS):
        vch_ref[0, :, :, hd, :] = v_n[:, hd * HEAD:(hd + 1) * HEAD].reshape(
            SEQS_PER_TILE, DEC_SEQ, HEAD)
    v_n16 = v_n.astype(BF16)
    mixed = []
    for hd in range(N_HEADS):
        hs = slice(hd * HEAD, (hd + 1) * HEAD)
        mixed.append(_hgrn_finish(o_intra[hd] + oint_ref[:, hs], col(3, hs), hg[:, hs]))
    for hd in range(N_HEADS):
        hs = slice(hd * HEAD, (hd + 1) * HEAD)
        mixed.append(_mlp_head(hd, col(4, hs), v_n16[:, hs], col(6, hs), wc_ref, bias))
    out = x + jnp.dot(jnp.concatenate(mixed, axis=-1), wout16_ref[...],
                      preferred_element_type=F32)
    y_ref[...] = _rms_norm(out, fg_ref[...])


def _const_spec(shape):
    return pl.BlockSpec(shape, lambda *_: (0,) * len(shape), pipeline_mode=pl.Buffered(1))


_WEIGHT_SPECS = [
    _const_spec((1, D_MODEL)),
    _const_spec((D_MODEL, D_IN)),
    _const_spec((2, D_A)),
    _const_spec((1, D_A)),
    _const_spec((1, D_B)),
    _const_spec((1, D_B)),
    _const_spec((N_HEADS, ROWS, ROWS)),
    _const_spec((N_HEADS, ROWS)),
    _const_spec((D_MODEL, D_MODEL)),
    _const_spec((1, D_MODEL)),
]

_CONSTANT_SCRATCH = [
    pltpu.VMEM((ROWS, ROWS), jnp.int32),
    pltpu.VMEM((N_HEADS, ROWS, ROWS), BF16),
    pltpu.VMEM((ROWS, N_HEADS), F32),
    pltpu.VMEM((D_MODEL, D_MODEL), BF16),
]


def kernel(x_prompt, x_sample, state_hgrn, norm_g, w_in, lb_logits, hgrn_norm_g, sgu_ln_g,
           sgu_ln_b, w_s, b_s, w_out, final_norm_g):
    depth = norm_g.shape[0]
    assert depth == 1 and lb_logits.shape == (2, D_A)
    batch, seq, _ = x_prompt.shape
    dec_batch, dec_seq, _ = x_sample.shape
    assert seq % PROMPT_TILE == 0 and dec_seq == DEC_SEQ and dec_batch % SEQS_PER_TILE == 0

    weights = (norm_g, w_in[0].astype(BF16), lb_logits, hgrn_norm_g, sgu_ln_g, sgu_ln_b,
               w_s[0], b_s[0], w_out[0], final_norm_g.reshape(1, D_MODEL))
    params = pltpu.CompilerParams(dimension_semantics=("arbitrary",),
                                  vmem_limit_bytes=VMEM_LIMIT_BYTES)

    tiles_per_seq = seq // PROMPT_TILE
    n_tiles = batch * tiles_per_seq

    def cur_tile(g):
        t = jnp.minimum(g, n_tiles - 1)
        return t // tiles_per_seq, t % tiles_per_seq

    def prev_tile(g):
        t = jnp.maximum(g - 1, 0)
        return t // tiles_per_seq, t % tiles_per_seq

    y_p, st_p, vch_p = pl.pallas_call(
        functools.partial(_prompt_kernel, tiles_per_seq=tiles_per_seq),
        grid=(n_tiles + 1,),
        in_specs=[pl.BlockSpec((1, PROMPT_TILE, D_MODEL), lambda g: (*cur_tile(g), 0)),
                  pl.BlockSpec((1, PROMPT_TILE, D_MODEL), lambda g: (*prev_tile(g), 0))]
        + _WEIGHT_SPECS,
        out_specs=[
            pl.BlockSpec((1, PROMPT_TILE, D_MODEL), lambda g: (*prev_tile(g), 0)),
            pl.BlockSpec((1, 1, N_HEADS, HEAD, HEAD), lambda g: (0, prev_tile(g)[0], 0, 0, 0)),
            pl.BlockSpec((1, 1, ROWS, N_HEADS, HEAD), lambda g: (0, prev_tile(g)[0], 0, 0, 0)),
        ],
        out_shape=[
            jax.ShapeDtypeStruct((batch, seq, D_MODEL), F32),
            jax.ShapeDtypeStruct((1, batch, N_HEADS, HEAD, HEAD), F32),
            jax.ShapeDtypeStruct((1, batch, ROWS, N_HEADS, HEAD), F32),
        ],
        scratch_shapes=[pltpu.VMEM((PROMPT_TILE, D_IN), F32),
                        pltpu.VMEM((PROMPT_TILE, D_IN), F32),
                        pltpu.VMEM((PROMPT_TILE, D_MODEL), BF16),
                        pltpu.VMEM((PROMPT_TILE, D_MODEL), BF16)] + _CONSTANT_SCRATCH,
        compiler_params=params,
        name="prompt_layer",
    )(x_prompt, x_prompt, *weights)

    n_rows = dec_batch * DEC_SEQ
    y_s, st_s, vch_s = pl.pallas_call(
        _sample_kernel,
        grid=(n_rows // ROWS,),
        in_specs=[pl.BlockSpec((ROWS, D_MODEL), lambda i: (i, 0))]
        + _WEIGHT_SPECS
        + [pl.BlockSpec((1, SEQS_PER_TILE, N_HEADS, HEAD, HEAD), lambda i: (0, i, 0, 0, 0))],
        out_specs=[
            pl.BlockSpec((ROWS, D_MODEL), lambda i: (i, 0)),
            pl.BlockSpec((1, SEQS_PER_TILE, N_HEADS, HEAD, HEAD), lambda i: (0, i, 0, 0, 0)),
            pl.BlockSpec((1, SEQS_PER_TILE, DEC_SEQ, N_HEADS, HEAD), lambda i: (0, i, 0, 0, 0)),
        ],
        out_shape=[
            jax.ShapeDtypeStruct((n_rows, D_MODEL), F32),
            jax.ShapeDtypeStruct((1, dec_batch, N_HEADS, HEAD, HEAD), F32),
            jax.ShapeDtypeStruct((1, dec_batch, DEC_SEQ, N_HEADS, HEAD), F32),
        ],
        scratch_shapes=[pltpu.VMEM((ROWS, D_A), F32),
                        pltpu.VMEM((N_HEADS, HEAD, ROWS), F32),
                        pltpu.VMEM((N_HEADS, HEAD, ROWS), F32),
                        pltpu.VMEM((ROWS, D_A), F32)] + _CONSTANT_SCRATCH,
        compiler_params=pltpu.CompilerParams(dimension_semantics=("arbitrary",),
                                             vmem_limit_bytes=VMEM_LIMIT_BYTES),
        name="decode_layer",
    )(x_sample.reshape(n_rows, D_MODEL), *weights, state_hgrn)

    return (y_p,
            y_s.reshape(dec_batch, DEC_SEQ, D_MODEL),
            st_p,
            st_s,
            vch_p,
            vch_s)
```

```python
import functools

import jax
import jax.numpy as jnp
from jax import lax
from jax.experimental import pallas as pl
from jax.experimental.pallas import tpu as pltpu

F32 = jnp.float32
BF16 = jnp.bfloat16

D_MODEL = 1024
D_A = 512
D_B = 512
HEAD = 128
N_HEADS = 4
D_IN = 4 * D_A + 3 * D_B
ROWS = 128
SUBLANES = 8
PACKED = 2 * SUBLANES
N_TILES = ROWS // SUBLANES
SUBLANE_LEVELS = 3
EPS = 1e-6
PROMPT_TILE = 512
PROJ_BLOCK = 512
HEAD_ROUNDS = 12
CHUNK_STAGGER = 4
HEAD_STAGGER = 1
PROJ_ROUNDS = [0, 4, 8, 12, 16, 20, 24]
DEC_SEQ = 8
SEQS_PER_TILE = ROWS // DEC_SEQ
SEQ_GROUP = 4
DEC_HALVES = 2
DEC_TILE = DEC_HALVES * ROWS
V7X_VMEM_BYTES = 64 * 1024 * 1024
VMEM_LIMIT_BYTES = V7X_VMEM_BYTES * 7 // 8

_NT = (((1,), (1,)), ((), ()))


def _silu(x):
    return x * jax.nn.sigmoid(x)


def _gelu(x):
    c = -2.0 * 0.7978845608028654 * 1.4426950408889634
    return x / (1.0 + jnp.exp2(x * (c * 0.044715 * (x * x) + c)))


def _rms_norm(x, g):
    return x * lax.rsqrt(jnp.mean(x * x, axis=-1, keepdims=True) + EPS) * g


def _layer_norm(x, g, b):
    mu = jnp.mean(x, axis=-1, keepdims=True)
    xc = x - mu
    var = jnp.mean(xc * xc, axis=-1, keepdims=True)
    return xc * lax.rsqrt(var + EPS) * g + b


def _lower_bound(lbl):
    rows = [lbl[r:r + 1, :] for r in range(lbl.shape[0])]
    mx = functools.reduce(jnp.maximum, rows)
    es = [jnp.exp(r - mx) for r in rows]
    return es[0] / functools.reduce(lambda a, c: a + c, es)


def _n_levels(seg_len):
    n_levels = seg_len.bit_length() - 1
    assert seg_len == 1 << n_levels and n_levels >= SUBLANE_LEVELS
    return n_levels


def _split3(x):
    hi = x.astype(BF16)
    r1 = x - hi.astype(F32)
    mid = r1.astype(BF16)
    lo = (r1 - mid.astype(F32)).astype(BF16)
    return hi, mid, lo


def _store_constants(seg_len, ws_ref, bs_ref, wout_ref, lvl_ref, wc_ref, bias_ref, wout16_ref):
    t = lax.broadcasted_iota(jnp.int32, (ROWS, ROWS), 0)
    s = lax.broadcasted_iota(jnp.int32, (ROWS, ROWS), 1)
    n_levels = _n_levels(seg_len)
    x = t ^ s
    lvl = jnp.where(t == s, n_levels, -1)
    for j in range(n_levels):
        lvl = jnp.where(((x >> j) == 1) & (((t >> j) & 1) == 1), j, lvl)
    lvl_ref[...] = lvl
    causal = (t >= s) & ((t >> n_levels) == (s >> n_levels))
    pick = jnp.where(s == (t & (seg_len - 1)), 1.0, 0.0).astype(BF16)
    for hd in range(N_HEADS):
        w = ws_ref[hd].astype(BF16)
        if seg_len != ROWS:
            w = jnp.dot(pick, w, preferred_element_type=F32).astype(BF16)
            w = lax.dot_general(w, pick, _NT, preferred_element_type=F32).astype(BF16)
        wc_ref[hd] = jnp.where(causal, w, jnp.zeros_like(w))
    pad = jnp.zeros((PACKED - N_HEADS, ROWS), F32)
    bias = sum(lax.dot_general(pick, part, _NT, preferred_element_type=F32)
               for part in _split3(jnp.concatenate([bs_ref[...], pad], axis=0)))
    bias_ref[...] = bias[:, :N_HEADS]
    wout16_ref[...] = wout_ref[...].astype(BF16)


def _row_bits():
    rowi = lax.broadcasted_iota(jnp.int32, (SUBLANES, HEAD), 0)
    return [((rowi >> j) & 1) == 1 for j in range(SUBLANE_LEVELS)]


def _hgrn_head(gates, lvl_ref, bits, n_levels):
    one = jnp.ones((SUBLANES, HEAD), F32)
    z = [[] for _ in range(n_levels)]
    qp_t, ks_t, blk_t, diag_t = [], [], [], []
    for pair in range(N_TILES // 2):
        z_pair = [[] for _ in range(SUBLANE_LEVELS)]
        for i in (2 * pair, 2 * pair + 1):
            q, kk, f = gates(i)
            diag_t.append(jnp.sum(q * kk, axis=-1, keepdims=True))
            qp, ks, blk = q * f, kk, f
            for j in range(SUBLANE_LEVELS):
                m = 1 << j
                z_pair[j].append(jnp.where(bits[j], qp, ks))
                sib = pltpu.roll(blk, m, 0)
                if 2 * m != SUBLANES:
                    sib = jnp.where(bits[j], sib, pltpu.roll(blk, SUBLANES - m, 0))
                qp = qp * jnp.where(bits[j], sib, one)
                ks = ks * jnp.where(bits[j], one, sib)
                blk = blk * sib
            qp_t.append(qp)
            ks_t.append(ks)
            blk_t.append(blk)
        for j in range(SUBLANE_LEVELS):
            z[j].append(jnp.concatenate(z_pair[j], axis=0).astype(BF16))
    yield

    blk_levels, blk_b = [], blk_t
    for j in range(SUBLANE_LEVELS, n_levels):
        blk_levels.append(blk_b)
        blk_b = [blk_b[2 * n] * blk_b[2 * n + 1] for n in range(len(blk_b) // 2)]
    for pair in range(N_TILES // 2):
        z_pair = [[] for _ in range(SUBLANE_LEVELS, n_levels)]
        for i in (2 * pair, 2 * pair + 1):
            for j in range(SUBLANE_LEVELS, n_levels):
                block = i >> (j - SUBLANE_LEVELS)
                sib = blk_levels[j - SUBLANE_LEVELS][block ^ 1]
                if block & 1:
                    z_pair[j - SUBLANE_LEVELS].append(qp_t[i])
                    qp_t[i] = qp_t[i] * sib
                else:
                    z_pair[j - SUBLANE_LEVELS].append(ks_t[i])
                    ks_t[i] = ks_t[i] * sib
        for j in range(SUBLANE_LEVELS, n_levels):
            z[j].append(jnp.concatenate(z_pair[j - SUBLANE_LEVELS], axis=0).astype(BF16))
    q_in = jnp.concatenate(qp_t, axis=0)
    k_out = jnp.concatenate(ks_t, axis=0)
    per_tile = N_TILES // len(blk_b)
    total = jnp.concatenate([blk_b[i // per_tile] for i in range(N_TILES)], axis=0)
    yield

    p, p_row = [], []
    for j in range(n_levels):
        zj = jnp.concatenate(z[j], axis=0)
        span = 1 << max(j - SUBLANE_LEVELS, 0)
        if span * SUBLANES >= PACKED:
            upper = [i for i in range(N_TILES) if (i // span) & 1]
            lhs = jnp.concatenate([z[j][i // 2] for i in upper[::2]], axis=0)
        else:
            upper = list(range(N_TILES))
            lhs = zj
        p.append(lax.dot_general(lhs, zj, _NT, preferred_element_type=F32))
        p_row.append({i: n * SUBLANES for n, i in enumerate(upper)})
        yield
    sc_t = []
    for i in range(N_TILES):
        lvl_i = lvl_ref[i * SUBLANES:(i + 1) * SUBLANES, :]
        acc = jnp.where(lvl_i == n_levels, diag_t[i], 0.0)
        for j in range(n_levels):
            if j < SUBLANE_LEVELS or (i >> (j - SUBLANE_LEVELS)) & 1:
                r0 = p_row[j][i]
                acc = jnp.where(lvl_i == j, p[j][r0:r0 + SUBLANES, :], acc)
        sc_t.append(acc)
    sc = jnp.concatenate(sc_t, axis=0).astype(BF16)
    return sc, q_in, k_out, total


def _lockstep(generators, starts=None):
    starts = starts or [0] * len(generators)
    results = [None] * len(generators)
    waiting = sorted(range(len(generators)), key=lambda i: starts[i])
    active = []
    rnd = 0
    while waiting or active:
        while waiting and starts[waiting[0]] <= rnd:
            active.append(waiting.pop(0))
        still = []
        for idx in active:
            try:
                next(generators[idx])
                still.append(idx)
            except StopIteration as done:
                results[idx] = done.value
        active = still
        rnd += 1
    return results


def _hgrn_gates(pq, pf, lb):
    q = _silu(pq)
    forget = lb + (1.0 - lb) * jax.nn.sigmoid(pf)
    return q, 1.0 - forget, forget


def _hgrn_finish(o, pz, g):
    return (_rms_norm(o, g) * _silu(pz)).astype(BF16)


def _mlp_head(hd, u, v_n16, gate, wc_ref, bias):
    mix = jnp.dot(wc_ref[hd], v_n16, preferred_element_type=F32) + bias[:, hd:hd + 1]
    return (_gelu(u) * mix * _silu(gate)).astype(BF16)


def _prompt_kernel(x_ref, xprev_ref, ng_ref, win_ref, lbl_ref, hg_ref, lng_ref, lnb_ref, ws_ref,
                   bs_ref, wout_ref, fg_ref, y_ref, st_ref, vch_ref, proj_a, proj_b, h_ref,
                   mixed_ref, lvl_ref, wc_ref, bias_ref, wout16_ref, *, tiles_per_seq):
    g = pl.program_id(0)

    @pl.when(g == 0)
    def _():
        proj_b[...] = jnp.zeros_like(proj_b)
        _store_constants(ROWS, ws_ref, bs_ref, wout_ref, lvl_ref, wc_ref, bias_ref, wout16_ref)

    @pl.when(lax.rem(jnp.maximum(g - 1, 0), tiles_per_seq) == 0)
    def _():
        st_ref[...] = jnp.zeros_like(st_ref)

    def step(proj_next, proj_prev):
        h_ref[...] = _rms_norm(x_ref[0], ng_ref[...]).astype(BF16)
        lb = _lower_bound(lbl_ref[...])
        bits, n_levels = _row_bits(), _n_levels(ROWS)
        hg = hg_ref[...]
        lng = lng_ref[...]
        lnb = lnb_ref[...]
        bias = bias_ref[...]
        n_chunks = PROMPT_TILE // ROWS
        n_blocks = D_IN // PROJ_BLOCK

        def col(c, k, hs=slice(0, D_A)):
            return proj_prev[c * ROWS:(c + 1) * ROWS, k * D_A + hs.start:k * D_A + hs.stop]

        def head(c, hd):
            hs = slice(hd * HEAD, (hd + 1) * HEAD)

            def gates(i):
                rows = slice(c * ROWS + i * SUBLANES, c * ROWS + (i + 1) * SUBLANES)
                return _hgrn_gates(proj_prev[rows, hs],
                                   proj_prev[rows, D_A + hd * HEAD:D_A + (hd + 1) * HEAD],
                                   lb[:, hs])

            sc, q_in, k_out, total = yield from _hgrn_head(gates, lvl_ref, bits, n_levels)
            state = st_ref[0, 0, hd]
            vh = col(c, 2, hs)
            o = jnp.dot(jnp.concatenate([sc, q_in.astype(BF16)], axis=1),
                        jnp.concatenate([vh.astype(BF16), state.astype(BF16)], axis=0),
                        preferred_element_type=F32)
            yield
            st_ref[0, 0, hd] = total.T * state + jnp.dot(
                k_out.T.astype(BF16), vh.astype(BF16), preferred_element_type=F32)
            yield
            mixed_ref[c * ROWS:(c + 1) * ROWS, hs] = _hgrn_finish(o, col(c, 3, hs), hg[:, hs])

        def mlp(c):
            v_n = _layer_norm(_gelu(col(c, 5)), lng, lnb)
            if c == n_chunks - 1:
                for hd in range(N_HEADS):
                    vch_ref[0, 0, :, hd, :] = v_n[:, hd * HEAD:(hd + 1) * HEAD]
            v_n16 = v_n.astype(BF16)
            yield
            for hd in range(N_HEADS):
                hs = slice(hd * HEAD, (hd + 1) * HEAD)
                mixed_ref[c * ROWS:(c + 1) * ROWS, D_A + hd * HEAD:D_A + (hd + 1) * HEAD] = (
                    _mlp_head(hd, col(c, 4, hs), v_n16[:, hs], col(c, 6, hs), wc_ref, bias))
                yield
                yield

        def projection(rounds_per_block):
            at_rounds = PROJ_ROUNDS if len(PROJ_ROUNDS) == n_blocks else [
                blk * rounds_per_block for blk in range(n_blocks)]
            rnd = 0
            for blk, at in enumerate(at_rounds):
                while rnd < at:
                    yield
                    rnd += 1
                cols = slice(blk * PROJ_BLOCK, (blk + 1) * PROJ_BLOCK)
                proj_next[:, cols] = jnp.dot(h_ref[...], win_ref[:, cols],
                                             preferred_element_type=F32)

        items = [gen for c in range(n_chunks)
                 for gen in [head(c, hd) for hd in range(N_HEADS)] + [mlp(c)]]
        starts = [c * CHUNK_STAGGER + k * HEAD_STAGGER
                  for c in range(n_chunks) for k in list(range(N_HEADS)) + [0]]
        rounds = (n_chunks - 1) * CHUNK_STAGGER + (N_HEADS - 1) * HEAD_STAGGER + HEAD_ROUNDS
        _lockstep(items + [projection(max(1, rounds // n_blocks))], starts + [0])

        out = xprev_ref[0] + jnp.dot(mixed_ref[...], wout16_ref[...], preferred_element_type=F32)
        y_ref[0] = _rms_norm(out, fg_ref[...])

    parity = lax.rem(g, 2)

    @pl.when(parity == 0)
    def _():
        step(proj_a, proj_b)

    @pl.when(parity == 1)
    def _():
        step(proj_b, proj_a)


def _sample_kernel(x_ref, ng_ref, win_ref, lbl_ref, hg_ref, lng_ref, lnb_ref, ws_ref, bs_ref,
                   wout_ref, fg_ref, stin_ref, y_ref, stout_ref, vch_ref,
                   qin_ref, kot_ref, tot_ref, oint_ref, lvl_ref, wc_ref, bias_ref, wout16_ref):
    @pl.when(pl.program_id(0) == 0)
    def _():
        _store_constants(DEC_SEQ, ws_ref, bs_ref, wout_ref, lvl_ref, wc_ref, bias_ref, wout16_ref)

    x = x_ref[...]
    h = _rms_norm(x, ng_ref[...]).astype(BF16)
    proj = jnp.dot(h, win_ref[...], preferred_element_type=F32)

    def col(half, k, hs=slice(0, D_A)):
        return proj[half * ROWS:(half + 1) * ROWS, k * D_A + hs.start:k * D_A + hs.stop]

    lb = _lower_bound(lbl_ref[...])
    bits, n_levels = _row_bits(), _n_levels(DEC_SEQ)
    vi16 = [col(half, 2).astype(BF16) for half in range(DEC_HALVES)]

    def head(half, hd):
        hs = slice(hd * HEAD, (hd + 1) * HEAD)
        r0 = half * ROWS

        def gates(i):
            rows = slice(r0 + i * SUBLANES, r0 + (i + 1) * SUBLANES)
            return _hgrn_gates(proj[rows, hs], proj[rows, D_A + hd * HEAD:D_A + (hd + 1) * HEAD],
                               lb[:, hs])

        sc, q_in, k_out, total = yield from _hgrn_head(gates, lvl_ref, bits, n_levels)
        qin_ref[r0:r0 + ROWS, hs] = q_in
        kot_ref[half, hd] = k_out.T
        tot_ref[half, hd] = total.T
        return jnp.dot(sc, vi16[half][:, hs], preferred_element_type=F32)

    pairs = [(half, hd) for half in range(DEC_HALVES) for hd in range(N_HEADS)]
    o_intra = _lockstep([head(half, hd) for half, hd in pairs], list(range(len(pairs))))

    lane = lax.broadcasted_iota(jnp.int32, (HEAD, ROWS), 1)

    def seq_head(half, i, hd):
        hs = slice(hd * HEAD, (hd + 1) * HEAD)
        seq_rows = slice(half * ROWS + i * DEC_SEQ, half * ROWS + (i + 1) * DEC_SEQ)
        state = stin_ref[0, half * SEQS_PER_TILE + i, hd]
        qi = qin_ref[seq_rows, hs].astype(BF16)
        oint_ref[seq_rows, hs] = jnp.dot(qi, state.astype(BF16), preferred_element_type=F32)
        yield
        in_seq = (lane >> (DEC_SEQ.bit_length() - 1)) == i
        k_seq = jnp.where(in_seq, kot_ref[half, hd], 0.0).astype(BF16)
        decay = tot_ref[half, hd, :, i * DEC_SEQ:i * DEC_SEQ + 1]
        yield
        stout_ref[0, half * SEQS_PER_TILE + i, hd] = decay * state + jnp.dot(
            k_seq, vi16[half][:, hs], preferred_element_type=F32)

    for half in range(DEC_HALVES):
        for grp in range(SEQS_PER_TILE // SEQ_GROUP):
            _lockstep([seq_head(half, grp * SEQ_GROUP + s, hd)
                       for s in range(SEQ_GROUP) for hd in range(N_HEADS)])

    hg = hg_ref[...]
    bias = bias_ref[...]
    mixed_rows = []
    for half in range(DEC_HALVES):
        v_n = _layer_norm(_gelu(col(half, 5)), lng_ref[...], lnb_ref[...])
        for hd in range(N_HEADS):
            vch_ref[0, half * SEQS_PER_TILE:(half + 1) * SEQS_PER_TILE, :, hd, :] = (
                v_n[:, hd * HEAD:(hd + 1) * HEAD].reshape(SEQS_PER_TILE, DEC_SEQ, HEAD))
        v_n16 = v_n.astype(BF16)
        mixed = []
        for hd in range(N_HEADS):
            hs = slice(hd * HEAD, (hd + 1) * HEAD)
            mixed.append(_hgrn_finish(
                o_intra[half * N_HEADS + hd] + oint_ref[half * ROWS:(half + 1) * ROWS, hs],
                col(half, 3, hs), hg[:, hs]))
        for hd in range(N_HEADS):
            hs = slice(hd * HEAD, (hd + 1) * HEAD)
            mixed.append(_mlp_head(hd, col(half, 4, hs), v_n16[:, hs], col(half, 6, hs), wc_ref,
                                   bias))
        mixed_rows.append(jnp.concatenate(mixed, axis=-1))
    out = x + jnp.dot(jnp.concatenate(mixed_rows, axis=0), wout16_ref[...],
                      preferred_element_type=F32)
    y_ref[...] = _rms_norm(out, fg_ref[...])


def _const_spec(shape):
    return pl.BlockSpec(shape, lambda *_: (0,) * len(shape), pipeline_mode=pl.Buffered(1))


_WEIGHT_SPECS = [
    _const_spec((1, D_MODEL)),
    _const_spec((D_MODEL, D_IN)),
    _const_spec((2, D_A)),
    _const_spec((1, D_A)),
    _const_spec((1, D_B)),
    _const_spec((1, D_B)),
    _const_spec((N_HEADS, ROWS, ROWS)),
    _const_spec((N_HEADS, ROWS)),
    _const_spec((D_MODEL, D_MODEL)),
    _const_spec((1, D_MODEL)),
]

_CONSTANT_SCRATCH = [
    pltpu.VMEM((ROWS, ROWS), jnp.int32),
    pltpu.VMEM((N_HEADS, ROWS, ROWS), BF16),
    pltpu.VMEM((ROWS, N_HEADS), F32),
    pltpu.VMEM((D_MODEL, D_MODEL), BF16),
]


def kernel(x_prompt, x_sample, state_hgrn, norm_g, w_in, lb_logits, hgrn_norm_g, sgu_ln_g,
           sgu_ln_b, w_s, b_s, w_out, final_norm_g):
    depth = norm_g.shape[0]
    assert depth == 1 and lb_logits.shape == (2, D_A)
    batch, seq, _ = x_prompt.shape
    dec_batch, dec_seq, _ = x_sample.shape
    assert seq % PROMPT_TILE == 0 and dec_seq == DEC_SEQ
    assert dec_batch % (DEC_HALVES * SEQS_PER_TILE) == 0

    weights = (norm_g, w_in[0].astype(BF16), lb_logits, hgrn_norm_g, sgu_ln_g, sgu_ln_b,
               w_s[0], b_s[0], w_out[0], final_norm_g.reshape(1, D_MODEL))
    params = pltpu.CompilerParams(dimension_semantics=("arbitrary",),
                                  vmem_limit_bytes=VMEM_LIMIT_BYTES)

    tiles_per_seq = seq // PROMPT_TILE
    n_tiles = batch * tiles_per_seq

    def cur_tile(g):
        t = jnp.minimum(g, n_tiles - 1)
        return t // tiles_per_seq, t % tiles_per_seq

    def prev_tile(g):
        t = jnp.maximum(g - 1, 0)
        return t // tiles_per_seq, t % tiles_per_seq

    y_p, st_p, vch_p = pl.pallas_call(
        functools.partial(_prompt_kernel, tiles_per_seq=tiles_per_seq),
        grid=(n_tiles + 1,),
        in_specs=[pl.BlockSpec((1, PROMPT_TILE, D_MODEL), lambda g: (*cur_tile(g), 0)),
                  pl.BlockSpec((1, PROMPT_TILE, D_MODEL), lambda g: (*prev_tile(g), 0))]
        + _WEIGHT_SPECS,
        out_specs=[
            pl.BlockSpec((1, PROMPT_TILE, D_MODEL), lambda g: (*prev_tile(g), 0)),
            pl.BlockSpec((1, 1, N_HEADS, HEAD, HEAD), lambda g: (0, prev_tile(g)[0], 0, 0, 0)),
            pl.BlockSpec((1, 1, ROWS, N_HEADS, HEAD), lambda g: (0, prev_tile(g)[0], 0, 0, 0)),
        ],
        out_shape=[
            jax.ShapeDtypeStruct((batch, seq, D_MODEL), F32),
            jax.ShapeDtypeStruct((1, batch, N_HEADS, HEAD, HEAD), F32),
            jax.ShapeDtypeStruct((1, batch, ROWS, N_HEADS, HEAD), F32),
        ],
        scratch_shapes=[pltpu.VMEM((PROMPT_TILE, D_IN), F32),
                        pltpu.VMEM((PROMPT_TILE, D_IN), F32),
                        pltpu.VMEM((PROMPT_TILE, D_MODEL), BF16),
                        pltpu.VMEM((PROMPT_TILE, D_MODEL), BF16)] + _CONSTANT_SCRATCH,
        compiler_params=params,
        name="prompt_layer",
    )(x_prompt, x_prompt, *weights)

    n_rows = dec_batch * DEC_SEQ
    y_s, st_s, vch_s = pl.pallas_call(
        _sample_kernel,
        grid=(n_rows // DEC_TILE,),
        in_specs=[pl.BlockSpec((DEC_TILE, D_MODEL), lambda i: (i, 0))]
        + _WEIGHT_SPECS
        + [pl.BlockSpec((1, DEC_HALVES * SEQS_PER_TILE, N_HEADS, HEAD, HEAD),
                        lambda i: (0, i, 0, 0, 0))],
        out_specs=[
            pl.BlockSpec((DEC_TILE, D_MODEL), lambda i: (i, 0)),
            pl.BlockSpec((1, DEC_HALVES * SEQS_PER_TILE, N_HEADS, HEAD, HEAD),
                         lambda i: (0, i, 0, 0, 0)),
            pl.BlockSpec((1, DEC_HALVES * SEQS_PER_TILE, DEC_SEQ, N_HEADS, HEAD),
                         lambda i: (0, i, 0, 0, 0)),
        ],
        out_shape=[
            jax.ShapeDtypeStruct((n_rows, D_MODEL), F32),
            jax.ShapeDtypeStruct((1, dec_batch, N_HEADS, HEAD, HEAD), F32),
            jax.ShapeDtypeStruct((1, dec_batch, DEC_SEQ, N_HEADS, HEAD), F32),
        ],
        scratch_shapes=[pltpu.VMEM((DEC_TILE, D_A), F32),
                        pltpu.VMEM((DEC_HALVES, N_HEADS, HEAD, ROWS), F32),
                        pltpu.VMEM((DEC_HALVES, N_HEADS, HEAD, ROWS), F32),
                        pltpu.VMEM((DEC_TILE, D_A), F32)] + _CONSTANT_SCRATCH,
        compiler_params=pltpu.CompilerParams(dimension_semantics=("arbitrary",),
                                             vmem_limit_bytes=V7X_VMEM_BYTES * 31 // 32),
        name="decode_layer",
    )(x_sample.reshape(n_rows, D_MODEL), *weights, state_hgrn)

    return (y_p,
            y_s.reshape(dec_batch, DEC_SEQ, D_MODEL),
            st_p,
            st_s,
            vch_p,
            vch_s)
```
